```python
import math
import jax
import jax.numpy as jnp
from jax import lax
import numpy as np

D_MODEL = 1024
BATCH = 16
SEQ = 4096
DEPTH = 2
DEC_BATCH = 16
DEC_SEQ = 64
PAST_LEN = 4096

CHUNK = 64
Q_BLOCK = 128
HEAD_DIM = 64
H_A = 8
W_A = H_A * HEAD_DIM
H_B = 8
HKV_B = 2
G_B = H_B // HKV_B
W_B = H_B * HEAD_DIM
HI_B = 4
DI_B = 64
TOPK_MAX = 256
H_C = 4
DK_C = 128
W_C = H_C * DK_C
N_BUCKETS = 32
MAX_DISTANCE = 128
D_FF = 2816
N_EXPERTS = 8
TOP_K_EXPERTS = 2
D_FF_EXPERT = 3584
N_BRANCH = 3
IN_SPLITS = (W_A, W_A, W_A, H_A, W_B, HKV_B * HEAD_DIM, HKV_B * HEAD_DIM, HI_B * DI_B, DI_B, HI_B, W_C, W_C, W_C, W_C)
N_IN = sum(IN_SPLITS)
DEEPNORM_ALPHA = (2.0 * DEPTH) ** 0.25
DEEPNORM_BETA = (8.0 * DEPTH) ** -0.25
LN_EPS = 1e-5

kernel_name = 'hybrid_stream_fox_dsa_hgrn2_step'


def _layer_norm(x, g, b):
    xf = x.astype(jnp.float32)
    xc = xf - jnp.mean(xf, axis=-1, keepdims=True)
    var = jnp.mean(xc * xc, axis=-1, keepdims=True)
    return (xc * lax.rsqrt(var + LN_EPS) * g + b).astype(x.dtype)


def _to_blocks(a, n, size):
    return jnp.swapaxes(a.reshape((a.shape[0], n, size) + a.shape[2:]), 0, 1)


def _from_blocks(a):
    a = jnp.swapaxes(a, 0, 1)
    return a.reshape((a.shape[0], a.shape[1] * a.shape[2]) + a.shape[3:])


def _t5_bucket(rel):
    nb = N_BUCKETS // 2
    max_exact = nb // 2
    ret = jnp.where(rel > 0, nb, 0)
    n = jnp.abs(rel)
    nf = jnp.maximum(n, 1).astype(jnp.float32)
    large = max_exact + (jnp.log(nf / max_exact) / math.log(MAX_DISTANCE / max_exact) * (nb - max_exact)).astype(jnp.int32)
    large = jnp.minimum(large, nb - 1)
    return ret + jnp.where(n < max_exact, n, large)


def _fox_attention(q, k, v, cq, ck, q_pos, k_pos):
    B, Tq, H, dh = q.shape
    qb = min(Q_BLOCK, Tq)
    n = Tq // qb
    ck_t = jnp.swapaxes(ck, 1, 2)
    scale = dh ** -0.5

    def block(args):
        qx, cx, px = args
        s = jnp.einsum('bqhd,bkhd->bhqk', qx, k).astype(jnp.float32) * scale
        s = s + jnp.swapaxes(cx, 1, 2)[..., None] - ck_t[:, :, None, :]
        s = jnp.where(k_pos[None, :] <= px[:, None], s, -jnp.inf)
        p = jax.nn.softmax(s, axis=-1).astype(v.dtype)
        return jnp.einsum('bhqk,bkhd->bqhd', p, v)

    out = lax.map(block, (_to_blocks(q, n, qb), _to_blocks(cq, n, qb), q_pos.reshape(n, qb)))
    return _from_blocks(out)


def _dsa_attention(q, k, v, qi, wi, ki, q_pos, k_pos, rel_bias, topk):
    B, Tq, H, dh = q.shape
    qb = min(Q_BLOCK, Tq)
    n = Tq // qb
    scale = dh ** -0.5
    iscale = DI_B ** -0.5
    k_chunk = k_pos // CHUNK
    take_rows = jax.vmap(lambda a, idx: jnp.take(a, idx, axis=0))

    def block(args):
        qx, qix, wix, px = args
        sc = jax.nn.relu(jnp.einsum('bqhd,bkd->bqhk', qix, ki).astype(jnp.float32) * iscale)
        score = jnp.einsum('bqhk,bqh->bqk', sc, wix.astype(jnp.float32))
        adm = k_chunk[None, :] <= (px // CHUNK)[:, None]
        score = jnp.where(adm[None], score, -jnp.inf)
        vals, idx = lax.top_k(score, topk)
        valid = jnp.isfinite(vals)
        kg = take_rows(k, idx)
        vg = take_rows(v, idx)
        qg = qx.reshape(B, qb, HKV_B, G_B, dh)
        s = jnp.einsum('bqhgd,bqkhd->bqhgk', qg, kg).astype(jnp.float32) * scale
        rel = jnp.take(k_pos, idx) - px[None, :, None]
        bias = jnp.take(rel_bias, _t5_bucket(rel), axis=0).astype(jnp.float32)
        bias = bias.reshape(B, qb, topk, HKV_B, G_B).transpose(0, 1, 3, 4, 2)
        s = jnp.where(valid[:, :, None, None, :], s + bias, -jnp.inf)
        p = jax.nn.softmax(s, axis=-1).astype(v.dtype)
        o = jnp.einsum('bqhgk,bqkhd->bqhgd', p, vg)
        return o.reshape(B, qb, H, dh)

    out = lax.map(block, (_to_blocks(q, n, qb), _to_blocks(qi, n, qb), _to_blocks(wi, n, qb), q_pos.reshape(n, qb)))
    return _from_blocks(out)


def _hgrn2_recurrence(q, logf, kk, iv, s0):
    B, T, H, dk = q.shape
    c = min(CHUNK, T)
    n = T // c
    tril = jnp.tril(jnp.ones((c, c), dtype=bool))

    def step(S, inp):
        qc, gc, kc, vc = inp
        b = jnp.cumsum(gc, axis=1)
        diff = b[:, :, None] - b[:, None, :]
        dec = jnp.exp(jnp.where(tril[None, :, :, None, None], diff, -jnp.inf))
        a = jnp.einsum('bthd,btshd,bshd->bhts', qc, dec, kc)
        o = jnp.einsum('bhts,bshv->bthv', a, vc) + jnp.einsum('bthd,bhdv->bthv', qc * jnp.exp(b), S)
        bl = b[:, -1]
        S = jnp.exp(bl)[..., None] * S + jnp.einsum('bshd,bshv->bhdv', kc * jnp.exp(bl[:, None] - b), vc)
        return S, o

    s_fin, o = lax.scan(step, s0, (_to_blocks(q, n, c), _to_blocks(logf, n, c), _to_blocks(kk, n, c), _to_blocks(iv, n, c)))
    return _from_blocks(o), s_fin


def _mixer(x, l, w_in, b_in, w_gate, b_gate, w_o_fox, w_o_dsa, w_o_hgrn, w_out, hgrn_lb, hgrn_norm_g, rel_bias, past):
    B, T, D = x.shape
    z = x @ w_in[l] + b_in[l]
    (q_a, k_a, v_a, f_a, q_b, k_b, v_b, qi_b, ki_b, wi_b, f_c, i_c, q_c, g_c) = jnp.split(z, np.cumsum(IN_SPLITS)[:-1].tolist(), axis=-1)

    def heads(a, h):
        return a.reshape(B, T, h, -1)

    q_a, k_a, v_a = heads(q_a, H_A), heads(k_a, H_A), heads(v_a, H_A)
    logf_a = jax.nn.log_sigmoid(f_a.astype(jnp.float32))
    q_b, k_b, v_b, qi_b = heads(q_b, H_B), heads(k_b, HKV_B), heads(v_b, HKV_B), heads(qi_b, HI_B)
    wi_b = wi_b * (HI_B ** -0.5)

    if past is None:
        P = 0
        ka_all, va_all, lf_all, kb_all, vb_all, ki_all = k_a, v_a, logf_a, k_b, v_b, ki_b
        s0 = jnp.zeros((B, H_C, DK_C, DK_C), jnp.float32)
    else:
        c_ka, c_va, c_lf, c_kb, c_vb, c_ki, c_s = past
        P = c_ka.shape[1]

        def cat(c, new):
            return jnp.concatenate([c.astype(new.dtype), new], axis=1)

        ka_all, va_all, lf_all = cat(c_ka, k_a), cat(c_va, v_a), cat(c_lf, logf_a)
        kb_all, vb_all, ki_all = cat(c_kb, k_b), cat(c_vb, v_b), cat(c_ki, ki_b)
        s0 = c_s.astype(jnp.float32)
    q_pos = P + jnp.arange(T, dtype=jnp.int32)
    k_pos = jnp.arange(P + T, dtype=jnp.int32)

    cum = jnp.cumsum(lf_all, axis=1)
    o_a = _fox_attention(q_a, ka_all, va_all, cum[:, P:], cum, q_pos, k_pos)

    topk = min(TOPK_MAX, (P + T) // 4)
    o_b = _dsa_attention(q_b, kb_all, vb_all, qi_b, wi_b, ki_all, q_pos, k_pos, rel_bias, topk)

    lb_cum = jnp.cumsum(jax.nn.softmax(hgrn_lb.astype(jnp.float32), axis=0), axis=0)
    lb = (lb_cum[l] - lb_cum[0]).reshape(H_C, DK_C)
    zf = heads(f_c, H_C).astype(jnp.float32)
    logf_c = jnp.logaddexp(jnp.log(lb), jnp.log1p(-lb) + jax.nn.log_sigmoid(zf))
    k_c = (1.0 - lb) * jax.nn.sigmoid(-zf)
    o_c, s_new = _hgrn2_recurrence(heads(q_c, H_C).astype(jnp.float32), logf_c, k_c, heads(i_c, H_C).astype(jnp.float32), s0)
    o_c = o_c * lax.rsqrt(jnp.mean(o_c * o_c, axis=-1, keepdims=True) + LN_EPS) * hgrn_norm_g[l].reshape(H_C, DK_C)
    o_c = o_c.reshape(B, T, W_C).astype(x.dtype) * jax.nn.silu(g_c)

    gates = jax.nn.sigmoid(x @ w_gate[l] + b_gate[l]).reshape(B, T, N_BRANCH, D)
    mix = (gates[:, :, 0] * (o_a.reshape(B, T, W_A) @ w_o_fox[l])
           + gates[:, :, 1] * (o_b.reshape(B, T, W_B) @ w_o_dsa[l])
           + gates[:, :, 2] * (o_c @ w_o_hgrn[l]))
    out = mix @ w_out[l]
    return out, (k_a, v_a, logf_a, k_b, v_b, ki_b, s_new)


def _swiglu(x, w1, w3, w2):
    return (jax.nn.silu(x @ w1) * (x @ w3)) @ w2


def _moe(x, router, w1, w3, w2):
    B, T, D = x.shape
    xf = x.reshape(-1, D)
    logits = (xf @ router).astype(jnp.float32)
    vals, idx = lax.top_k(logits, TOP_K_EXPERTS)
    gates = jax.nn.softmax(vals, axis=-1)
    comb = jnp.sum(jax.nn.one_hot(idx, N_EXPERTS, dtype=jnp.float32) * gates[..., None], axis=1).astype(x.dtype)
    out = jnp.zeros_like(xf)
    for e in range(N_EXPERTS):
        out = out + comb[:, e:e + 1] * _swiglu(xf, w1[e], w3[e], w2[e])
    return out.reshape(B, T, D)


def setup_inputs(seed: int = 0) -> dict:
    key = jax.random.key(seed)
    ks = iter(jax.random.split(key, 48))

    def nrm(shape, scale=1.0):
        return jax.random.normal(next(ks), shape, jnp.float32) * scale

    n_dense = (DEPTH + 1) // 2
    n_moe = DEPTH // 2
    return {
        'x_prompt': nrm((BATCH, SEQ, D_MODEL)),
        'x_sample': nrm((DEC_BATCH, DEC_SEQ, D_MODEL)),
        'cache_fox_k': nrm((DEPTH, DEC_BATCH, PAST_LEN, H_A, HEAD_DIM)),
        'cache_fox_v': nrm((DEPTH, DEC_BATCH, PAST_LEN, H_A, HEAD_DIM)),
        'cache_fox_logf': jax.nn.log_sigmoid(nrm((DEPTH, DEC_BATCH, PAST_LEN, H_A)) + 1.0),
        'cache_dsa_k': nrm((DEPTH, DEC_BATCH, PAST_LEN, HKV_B, HEAD_DIM)),
        'cache_dsa_v': nrm((DEPTH, DEC_BATCH, PAST_LEN, HKV_B, HEAD_DIM)),
        'cache_dsa_kidx': nrm((DEPTH, DEC_BATCH, PAST_LEN, DI_B)),
        'state_hgrn': nrm((DEPTH, DEC_BATCH, H_C, DK_C, DK_C), 0.5),
        'w_in': nrm((DEPTH, D_MODEL, N_IN), D_MODEL ** -0.5),
        'b_in': nrm((DEPTH, N_IN), 0.02),
        'w_gate': nrm((DEPTH, D_MODEL, N_BRANCH * D_MODEL), D_MODEL ** -0.5),
        'b_gate': nrm((DEPTH, N_BRANCH * D_MODEL), 0.02),
        'w_o_fox': nrm((DEPTH, W_A, D_MODEL), W_A ** -0.5 * DEEPNORM_BETA),
        'w_o_dsa': nrm((DEPTH, W_B, D_MODEL), W_B ** -0.5 * DEEPNORM_BETA),
        'w_o_hgrn': nrm((DEPTH, W_C, D_MODEL), W_C ** -0.5 * DEEPNORM_BETA),
        'w_out': nrm((DEPTH, D_MODEL, D_MODEL), D_MODEL ** -0.5 * DEEPNORM_BETA),
        'hgrn_lb': nrm((DEPTH, W_C), 0.1),
        'hgrn_norm_g': 1.0 + nrm((DEPTH, W_C), 0.02),
        'rel_bias': nrm((N_BUCKETS, H_B), 0.5),
        'ln1_g': 1.0 + nrm((DEPTH, D_MODEL), 0.02),
        'ln1_b': nrm((DEPTH, D_MODEL), 0.02),
        'ln2_g': 1.0 + nrm((DEPTH, D_MODEL), 0.02),
        'ln2_b': nrm((DEPTH, D_MODEL), 0.02),
        'ffn_w1': nrm((n_dense, D_MODEL, D_FF), D_MODEL ** -0.5),
        'ffn_w3': nrm((n_dense, D_MODEL, D_FF), D_MODEL ** -0.5),
        'ffn_w2': nrm((n_dense, D_FF, D_MODEL), D_FF ** -0.5 * DEEPNORM_BETA),
        'moe_router': nrm((n_moe, D_MODEL, N_EXPERTS), D_MODEL ** -0.5),
        'moe_w1': nrm((n_moe, N_EXPERTS, D_MODEL, D_FF_EXPERT), D_MODEL ** -0.5),
        'moe_w3': nrm((n_moe, N_EXPERTS, D_MODEL, D_FF_EXPERT), D_MODEL ** -0.5),
        'moe_w2': nrm((n_moe, N_EXPERTS, D_FF_EXPERT, D_MODEL), D_FF_EXPERT ** -0.5 * DEEPNORM_BETA),
    }


def reference(x_prompt, x_sample, cache_fox_k, cache_fox_v, cache_fox_logf, cache_dsa_k, cache_dsa_v, cache_dsa_kidx, state_hgrn,
              w_in, b_in, w_gate, b_gate, w_o_fox, w_o_dsa, w_o_hgrn, w_out, hgrn_lb, hgrn_norm_g, rel_bias,
              ln1_g, ln1_b, ln2_g, ln2_b, ffn_w1, ffn_w3, ffn_w2, moe_router, moe_w1, moe_w3, moe_w2):
    def run(x, with_cache):
        states = []
        for l in range(DEPTH):
            past = None
            if with_cache:
                past = (cache_fox_k[l], cache_fox_v[l], cache_fox_logf[l], cache_dsa_k[l], cache_dsa_v[l], cache_dsa_kidx[l], state_hgrn[l])
            h, st = _mixer(x, l, w_in, b_in, w_gate, b_gate, w_o_fox, w_o_dsa, w_o_hgrn, w_out, hgrn_lb, hgrn_norm_g, rel_bias, past)
            x = _layer_norm(DEEPNORM_ALPHA * x + h, ln1_g[l], ln1_b[l])
            if l % 2 == 0:
                f = _swiglu(x, ffn_w1[l // 2], ffn_w3[l // 2], ffn_w2[l // 2])
            else:
                f = _moe(x, moe_router[l // 2], moe_w1[l // 2], moe_w3[l // 2], moe_w2[l // 2])
            x = _layer_norm(DEEPNORM_ALPHA * x + f, ln2_g[l], ln2_b[l])
            states.append(st)
        stacked = [jnp.stack([s[i] for s in states], axis=0) for i in range(7)]
        return x, stacked

    y_prompt, sp = run(x_prompt, False)
    y_sample, ss = run(x_sample, True)
    return (y_prompt, y_sample, sp[0], sp[1], sp[2], sp[3], sp[4], sp[5], sp[6], ss[0], ss[1], ss[2], ss[3], ss[4], ss[5], ss[6])
```

```python
import functools
import math

import jax
import jax.numpy as jnp
import numpy as np
from jax import lax
from jax.experimental import pallas as pl
from jax.experimental.pallas import tpu as pltpu

F32 = jnp.float32
BF16 = jnp.bfloat16
NEG_INF = float("-inf")

D_MODEL = 1024
DEPTH = 2
CHUNK = 64
HEAD_DIM = 64
H_A = 8
H_B = 8
HKV_B = 2
G_B = H_B // HKV_B
HI_B = 4
DI_B = 64
TOPK_MAX = 256
H_C = 4
DK_C = 128
W_C = H_C * DK_C
N_BUCKETS = 32
MAX_DISTANCE = 128
N_EXPERTS = 8
DEEPNORM_ALPHA = (2.0 * DEPTH) ** 0.25
LN_EPS = 1e-5

LANES = 128
SUBLANES = 8
VMEM_LIMIT = 56 * 1024 * 1024

C_QA, C_KA, C_VA = 0, 512, 1024
C_QB, C_KB, C_VB, C_QI = 1536, 2048, 2176, 2304
C_FC, C_IC, C_QC, C_GC = 2560, 3072, 3584, 4096
C_KI, C_FA, C_WI = 4608, 4672, 4680
N_IN = 4684
N_IN_PAD = 4864
PROJ_TN = 256


def _params(sem, vmem=VMEM_LIMIT):
    return pltpu.CompilerParams(dimension_semantics=sem, vmem_limit_bytes=vmem)


def _log_sigmoid(z):
    return jnp.minimum(z, 0.0) - jnp.log1p(jnp.exp(-jnp.abs(z)))


def _sigmoid(z):
    return 1.0 / (1.0 + jnp.exp(-z))


def _dot_nt(a, b):
    return lax.dot_general(a, b, (((1,), (1,)), ((), ())), preferred_element_type=F32)


def _dot_tn(a, b):
    return lax.dot_general(a, b, (((0,), (0,)), ((), ())), preferred_element_type=F32)


def _dot(a, b):
    return jnp.dot(a, b, preferred_element_type=F32)


def _split3(x):
    hi = x.astype(BF16)
    r1 = x - hi.astype(F32)
    mid = r1.astype(BF16)
    lo = (r1 - mid.astype(F32)).astype(BF16)
    return hi, mid, lo


def _layer_norm_rows(r, g, b):
    mu = jnp.mean(r, axis=-1, keepdims=True)
    rc = r - mu
    var = jnp.mean(rc * rc, axis=-1, keepdims=True)
    return rc * lax.rsqrt(var + LN_EPS) * g + b


def _proj_kernel(x_ref, w_ref, b_ref, flag_ref, o_ref):
    z = _dot(x_ref[...], w_ref[...]) + b_ref[...]
    last = pl.num_programs(1) - 1

    @pl.when(pl.program_id(1) != last)
    def _():
        o_ref[...] = z

    @pl.when(pl.program_id(1) == last)
    def _():
        o_ref[...] = jnp.where(flag_ref[...] > 0.0, _log_sigmoid(z), z)


def _proj(xb, w, b, flag, tm):
    n, k = xb.shape
    m = w.shape[1]
    tn = PROJ_TN
    return pl.pallas_call(
        _proj_kernel,
        grid=(n // tm, m // tn),
        in_specs=[
            pl.BlockSpec((tm, k), lambda i, j: (i, 0)),
            pl.BlockSpec((k, tn), lambda i, j: (0, j)),
            pl.BlockSpec((1, tn), lambda i, j: (0, j)),
            pl.BlockSpec((1, tn), lambda i, j: (0, j)),
        ],
        out_specs=pl.BlockSpec((tm, tn), lambda i, j: (i, j)),
        out_shape=jax.ShapeDtypeStruct((n, m), F32),
        compiler_params=_params(("parallel", "arbitrary")),
        name="proj",
    )(xb, w, b, flag)


def _cumsum_kernel(x_ref, o_ref):
    c = x_ref[0]
    length = c.shape[-1]
    lane = lax.broadcasted_iota(jnp.int32, c.shape, 1)
    s = 1
    while s < length:
        c = c + jnp.where(lane >= s, pltpu.roll(c, s, 1), 0.0)
        s *= 2
    o_ref[0] = c


def _cumsum(x):
    b, r, length = x.shape
    return pl.pallas_call(
        _cumsum_kernel,
        grid=(b,),
        in_specs=[pl.BlockSpec((1, r, length), lambda i: (i, 0, 0))],
        out_specs=pl.BlockSpec((1, r, length), lambda i: (i, 0, 0)),
        out_shape=jax.ShapeDtypeStruct(x.shape, F32),
        compiler_params=_params(("parallel",)),
        name="cumsum",
    )(x)


def _fox_kernel(q_ref, k_ref, v_ref, cq_ref, ck_ref, o_ref, m_ref, l_ref, acc_ref, *, tq, tk, q_off, hps):
    i = pl.program_id(2)
    q0 = q_off + i * tq
    nk = (q0 + tq + tk - 1) // tk
    qpos = q0 + lax.broadcasted_iota(jnp.int32, (tq, tk), 0)
    kcol = lax.broadcasted_iota(jnp.int32, (tq, tk), 1)
    outs = []
    for h in range(hps):
        q = q_ref[0, h]
        cq = cq_ref[0, h]
        m_ref[...] = jnp.full((tq, 1), NEG_INF, F32)
        l_ref[...] = jnp.zeros((tq, 1), F32)
        acc_ref[...] = jnp.zeros((tq, HEAD_DIM), F32)

        def body(j, carry, h=h, q=q, cq=cq):
            ks = pl.multiple_of(j * tk, tk)
            k = k_ref[0, h, pl.ds(ks, tk), :]
            v = v_ref[0, h, pl.ds(ks, tk), :]
            s = _dot_nt(q, k) + cq - ck_ref[0, h, :, pl.ds(ks, tk)]
            s = jnp.where(kcol + ks <= qpos, s, NEG_INF)
            m_old = m_ref[...]
            m_new = jnp.maximum(m_old, jnp.max(s, axis=-1, keepdims=True))
            alpha = jnp.exp(m_old - m_new)
            p = jnp.exp(s - m_new)
            l_ref[...] = alpha * l_ref[...] + jnp.sum(p, axis=-1, keepdims=True)
            acc_ref[...] = alpha * acc_ref[...] + _dot(p.astype(BF16), v)
            m_ref[...] = m_new
            return carry

        lax.fori_loop(0, nk, body, 0)
        outs.append(acc_ref[...] / l_ref[...])
    o_ref[0] = jnp.concatenate(outs, axis=-1).astype(o_ref.dtype)


def _fox(q, k, v, cq, ck, *, tq, tk, q_off):
    b, h, t, dh = q.shape
    length = k.shape[2]
    hps = 2
    kern = functools.partial(_fox_kernel, tq=tq, tk=tk, q_off=q_off, hps=hps)
    return pl.pallas_call(
        kern,
        grid=(b, h // hps, t // tq),
        in_specs=[
            pl.BlockSpec((1, hps, tq, dh), lambda bi, hi, i: (bi, hi, i, 0)),
            pl.BlockSpec((1, hps, length, dh), lambda bi, hi, i: (bi, hi, 0, 0)),
            pl.BlockSpec((1, hps, length, dh), lambda bi, hi, i: (bi, hi, 0, 0)),
            pl.BlockSpec((1, hps, tq, 1), lambda bi, hi, i: (bi, hi, i, 0)),
            pl.BlockSpec((1, hps, 1, length), lambda bi, hi, i: (bi, hi, 0, 0)),
        ],
        out_specs=pl.BlockSpec((1, tq, hps * dh), lambda bi, hi, i: (bi, i, hi)),
        out_shape=jax.ShapeDtypeStruct((b, t, h * dh), BF16),
        scratch_shapes=[
            pltpu.VMEM((tq, 1), F32),
            pltpu.VMEM((tq, 1), F32),
            pltpu.VMEM((tq, dh), F32),
        ],
        compiler_params=_params(("parallel", "parallel", "arbitrary")),
        name="fox",
    )(q, k, v, cq, ck)


INT_MIN = -(2 ** 31)
KEY_NEG_INF = int(np.int32(np.uint32(0xFF800000) ^ np.uint32(0x7FFFFFFF)))


def _order_key(x):
    bits = lax.bitcast_convert_type(x, jnp.int32)
    return bits ^ ((bits >> 31) & jnp.int32(0x7FFFFFFF))


def _dsa_kernel(qi_ref, wi_ref, ki_ref, q_ref, k_ref, v_ref, nb_ref, o_ref,
                sc_ref, m_ref, l_ref, acc_ref, *, tq, q_off, topk):
    tk = LANES
    i = pl.program_id(1)
    q0 = q_off + i * tq
    i_abs = q0 // tk
    nk = i_abs + 1
    rows = G_B * tq

    w = wi_ref[0]
    wcols = [w[:, hh:hh + 1] for hh in range(HI_B)]
    q_chunk = (q0 + lax.broadcasted_iota(jnp.int32, (tq, tk), 0)) >> 6
    kcol = lax.broadcasted_iota(jnp.int32, (tq, tk), 1)

    def score_tile(j, carry):
        ks = pl.multiple_of(j * tk, tk)
        ki = ki_ref[0, pl.ds(ks, tk), :]
        tot = jnp.zeros((tq, tk), F32)
        for hh in range(HI_B):
            tot = tot + wcols[hh] * jnp.maximum(_dot_nt(qi_ref[0, hh], ki), 0.0)
        tot = jnp.where(((kcol + ks) >> 6) <= q_chunk, tot, NEG_INF)
        sc_ref[j] = _order_key(tot)
        return carry

    lax.fori_loop(0, nk, score_tile, 0)

    kf = float(topk)

    def count(pred_fn):
        def cb(j, accv):
            return accv + jnp.where(pred_fn(sc_ref[j]), 1.0, 0.0)
        accv = lax.fori_loop(0, nk, cb, jnp.zeros((tq, tk), F32))
        return jnp.sum(accv, axis=-1, keepdims=True)

    zero_b = jnp.zeros((tq, tk), jnp.int32)
    c0 = count(lambda key: key >= zero_b)
    thr = jnp.where(c0 >= kf, jnp.int32(0), jnp.int32(INT_MIN))
    for bit in range(30, -1, -1):
        cand = thr + jnp.int32(1 << bit)
        cand_b = jnp.broadcast_to(cand, (tq, tk))
        c = count(lambda key, cand_b=cand_b: key >= cand_b)
        thr = jnp.where(c >= kf, cand, thr)

    thr_b = jnp.broadcast_to(thr, (tq, tk))
    cnt_gt = count(lambda key: key > thr_b)
    short = thr <= jnp.int32(KEY_NEG_INF)
    thr_b = jnp.broadcast_to(jnp.maximum(thr, jnp.int32(KEY_NEG_INF)), (tq, tk))
    need = jnp.where(short, 0.0, kf - cnt_gt)

    tri = jnp.where(lax.broadcasted_iota(jnp.int32, (tk, tk), 0) <= lax.broadcasted_iota(jnp.int32, (tk, tk), 1),
                    1.0, 0.0).astype(BF16)
    m_ref[...] = jnp.full((HKV_B, rows, 1), NEG_INF, F32)
    l_ref[...] = jnp.zeros((HKV_B, rows, 1), F32)
    acc_ref[...] = jnp.zeros((HKV_B, rows, HEAD_DIM), F32)

    def attend_tile(j, eqc):
        ks = pl.multiple_of(j * tk, tk)
        key = sc_ref[j]
        gt = key > thr_b
        eq = key == thr_b
        eqf = jnp.where(eq, 1.0, 0.0)
        rank = _dot(eqf.astype(BF16), tri) + eqc
        pen = jnp.where(gt, 0.0, jnp.where(eq, rank, 1e9))
        madd = jnp.where(pen <= need, 0.0, NEG_INF)
        madd = jnp.concatenate([madd] * G_B, axis=0)
        sel = jnp.clip(j - i_abs + 2, 0, 2)
        for g in range(HKV_B):
            k = k_ref[0, g, pl.ds(ks, tk), :]
            v = v_ref[0, g, pl.ds(ks, tk), :]
            s = _dot_nt(q_ref[0, g], k) + nb_ref[sel, g] + madd
            m_old = m_ref[g]
            m_new = jnp.maximum(m_old, jnp.max(s, axis=-1, keepdims=True))
            m_safe = jnp.where(m_new == NEG_INF, 0.0, m_new)
            alpha = jnp.exp(m_old - m_safe)
            p = jnp.exp(s - m_safe)
            l_ref[g] = alpha * l_ref[g] + jnp.sum(p, axis=-1, keepdims=True)
            acc_ref[g] = alpha * acc_ref[g] + _dot(p.astype(BF16), v)
            m_ref[g] = m_new
        return eqc + jnp.sum(eqf, axis=-1, keepdims=True)

    lax.fori_loop(0, nk, attend_tile, jnp.zeros((tq, 1), F32))
    for g in range(HKV_B):
        o_ref[0, g] = (acc_ref[g] / l_ref[g]).astype(o_ref.dtype)


def _dsa(qi, wi, ki, q, k, v, nb, *, tq, q_off, topk):
    b, _, t, _ = qi.shape
    length = ki.shape[1]
    rows = G_B * tq
    assert q_off % LANES == 0 and tq <= LANES and length % LANES == 0 and length >= q_off + t
    assert tq % CHUNK == 0 and (tq == LANES or t == tq)
    kern = functools.partial(_dsa_kernel, tq=tq, q_off=q_off, topk=topk)
    return pl.pallas_call(
        kern,
        grid=(b, t // tq),
        in_specs=[
            pl.BlockSpec((1, HI_B, tq, DI_B), lambda bi, i: (bi, 0, i, 0)),
            pl.BlockSpec((1, tq, HI_B), lambda bi, i: (bi, i, 0)),
            pl.BlockSpec((1, length, DI_B), lambda bi, i: (bi, 0, 0)),
            pl.BlockSpec((1, HKV_B, rows, HEAD_DIM), lambda bi, i: (bi, 0, i, 0)),
            pl.BlockSpec((1, HKV_B, length, HEAD_DIM), lambda bi, i: (bi, 0, 0, 0)),
            pl.BlockSpec((1, HKV_B, length, HEAD_DIM), lambda bi, i: (bi, 0, 0, 0)),
            pl.BlockSpec((3, HKV_B, rows, LANES), lambda bi, i: (0, 0, 0, 0)),
        ],
        out_specs=pl.BlockSpec((1, HKV_B, rows, HEAD_DIM), lambda bi, i: (bi, 0, i, 0)),
        out_shape=jax.ShapeDtypeStruct(q.shape, BF16),
        scratch_shapes=[
            pltpu.VMEM((length // LANES, tq, LANES), jnp.int32),
            pltpu.VMEM((HKV_B, rows, 1), F32),
            pltpu.VMEM((HKV_B, rows, 1), F32),
            pltpu.VMEM((HKV_B, rows, HEAD_DIM), F32),
        ],
        compiler_params=_params(("parallel", "arbitrary")),
        name="dsa",
    )(qi, wi, ki, q, k, v, nb)


HG_SUB = SUBLANES


def _hgrn_kernel(f_ref, i_ref, q_ref, g_ref, la_ref, l1_ref, oml_ref, ng_ref, s0_ref,
                 o_ref, sfin_ref, st_ref):
    c = pl.program_id(1)
    n_c = pl.num_programs(1)
    ct = CHUNK

    @pl.when(c == 0)
    def _():
        for hh in range(H_C):
            st_ref[hh] = s0_ref[0, hh].T

    zf = f_ref[0]
    a = la_ref[...]
    cc = l1_ref[...] + _log_sigmoid(zf)
    logf = jnp.maximum(a, cc) + jnp.log1p(jnp.exp(-jnp.abs(a - cc)))
    kk = oml_ref[...] * _sigmoid(-zf)

    r_i = lax.broadcasted_iota(jnp.int32, (ct, ct), 0)
    c_i = lax.broadcasted_iota(jnp.int32, (ct, ct), 1)
    low = jnp.where(c_i <= r_i, 1.0, 0.0).astype(BF16)
    hi, mid, lo = _split3(logf)
    bcum = _dot(low, hi) + _dot(low, mid) + _dot(low, lo)

    sub_row = lax.broadcasted_iota(jnp.int32, (HG_SUB, DK_C), 0)
    a_col = lax.broadcasted_iota(jnp.int32, (HG_SUB, ct), 1)
    n_sub = ct // HG_SUB

    for hh in range(H_C):
        cs = slice(hh * DK_C, (hh + 1) * DK_C)
        qh = q_ref[0, :, cs]
        kh = kk[:, cs]
        vh = i_ref[0, :, cs]
        bh = bcum[:, cs]
        a_rows = []
        for bi in range(n_sub):
            r0 = bi * HG_SUB
            q_blk = qh[r0:r0 + HG_SUB]
            b_blk = bh[r0:r0 + HG_SUB]
            k_blk = kh[r0:r0 + HG_SUB]
            if bi > 0:
                ref = bh[r0:r0 + 1]
                qp = q_blk * jnp.exp(b_blk - ref)
                kp = kh[:r0] * jnp.exp(ref - bh[:r0])
                kp = jnp.concatenate([kp, jnp.zeros((ct - r0, DK_C), F32)], axis=0)
                a_row = _dot_nt(qp.astype(BF16), kp.astype(BF16))
            else:
                a_row = jnp.zeros((HG_SUB, ct), F32)
            for s in range(HG_SUB):
                d = jnp.where(sub_row >= s, b_blk - b_blk[s:s + 1], NEG_INF)
                x = q_blk * jnp.exp(d) * k_blk[s:s + 1]
                col = jnp.sum(x, axis=-1, keepdims=True)
                a_row = jnp.where(a_col == r0 + s, col, a_row)
            a_rows.append(a_row)
        a_mat = jnp.concatenate(a_rows, axis=0)
        st = st_ref[hh]
        o = _dot(a_mat.astype(BF16), vh.astype(BF16)) + _dot_nt((qh * jnp.exp(bh)).astype(BF16), st.astype(BF16))
        bl = bh[ct - 1:ct]
        kdec = kh * jnp.exp(bl - bh)
        st_ref[hh] = st * jnp.exp(bl) + _dot_tn(vh.astype(BF16), kdec.astype(BF16))
        o = o * lax.rsqrt(jnp.mean(o * o, axis=-1, keepdims=True) + LN_EPS) * ng_ref[:, cs]
        gh = g_ref[0, :, cs]
        o_ref[0, :, cs] = (o * (gh * _sigmoid(gh))).astype(o_ref.dtype)

    @pl.when(c == n_c - 1)
    def _():
        for hh in range(H_C):
            sfin_ref[0, hh] = st_ref[hh].T


def _hgrn(z3, la, l1, oml, ng, s0):
    b, t, _ = z3.shape
    n_c = t // CHUNK

    def zspec(col):
        blk = col // W_C
        return pl.BlockSpec((1, CHUNK, W_C), lambda bi, ci: (bi, ci, blk))

    vec = pl.BlockSpec((1, W_C), lambda bi, ci: (0, 0))
    return pl.pallas_call(
        _hgrn_kernel,
        grid=(b, n_c),
        in_specs=[zspec(C_FC), zspec(C_IC), zspec(C_QC), zspec(C_GC), vec, vec, vec, vec,
                  pl.BlockSpec((1, H_C, DK_C, DK_C), lambda bi, ci: (bi, 0, 0, 0))],
        out_specs=[pl.BlockSpec((1, CHUNK, W_C), lambda bi, ci: (bi, ci, 0)),
                   pl.BlockSpec((1, H_C, DK_C, DK_C), lambda bi, ci: (bi, 0, 0, 0))],
        out_shape=[jax.ShapeDtypeStruct((b, t, W_C), BF16),
                   jax.ShapeDtypeStruct((b, H_C, DK_C, DK_C), F32)],
        scratch_shapes=[pltpu.VMEM((H_C, DK_C, DK_C), F32)],
        compiler_params=_params(("parallel", "arbitrary")),
        name="hgrn",
    )(z3, z3, z3, z3, la, l1, oml, ng, s0)


def _merge_kernel(x_ref, xb_ref, oa_ref, ob_ref, oc_ref, wg_ref, bg_ref, wa_ref, wb_ref, wc_ref, wo_ref,
                  g_ref, b_ref, y_ref, yb_ref):
    xb = xb_ref[...]
    d = D_MODEL
    mix = None
    for br, (o_ref, w_ref) in enumerate(((oa_ref, wa_ref), (ob_ref, wb_ref), (oc_ref, wc_ref))):
        gate = _sigmoid(_dot(xb, wg_ref[:, br * d:(br + 1) * d]) + bg_ref[:, br * d:(br + 1) * d])
        term = gate * _dot(o_ref[...], w_ref[...])
        mix = term if mix is None else mix + term
    out = _dot(mix.astype(BF16), wo_ref[...])
    y = _layer_norm_rows(DEEPNORM_ALPHA * x_ref[...] + out, g_ref[...], b_ref[...])
    y_ref[...] = y
    yb_ref[...] = y.astype(BF16)


def _merge(x, xb, oa, ob, oc, wg, bg, wa, wb, wc, wo, g, b, tm):
    n, d = x.shape

    def row(width):
        return pl.BlockSpec((tm, width), lambda i: (i, 0))

    def full(arr):
        return pl.BlockSpec(arr.shape, lambda i: (0, 0))

    return pl.pallas_call(
        _merge_kernel,
        grid=(n // tm,),
        in_specs=[row(d), row(d), row(oa.shape[1]), row(ob.shape[1]), row(oc.shape[1]),
                  full(wg), full(bg), full(wa), full(wb), full(wc), full(wo), full(g), full(b)],
        out_specs=[row(d), row(d)],
        out_shape=[jax.ShapeDtypeStruct((n, d), F32), jax.ShapeDtypeStruct((n, d), BF16)],
        compiler_params=_params(("parallel",)),
        name="merge",
    )(x, xb, oa, ob, oc, wg, bg, wa, wb, wc, wo, g, b)


def _router_kernel(x_ref, wh_ref, wl_ref, o_ref):
    x = x_ref[...]
    xh = x.astype(BF16)
    xl = (x - xh.astype(F32)).astype(BF16)
    logits = _dot(xh, wh_ref[...]) + _dot(xh, wl_ref[...]) + _dot(xl, wh_ref[...])
    lane = lax.broadcasted_iota(jnp.int32, logits.shape, 1)
    logits = jnp.where(lane < N_EXPERTS, logits, NEG_INF)
    big = jnp.int32(LANES)
    m1 = jnp.max(logits, axis=-1, keepdims=True)
    i1 = jnp.min(jnp.where(logits == m1, lane, big), axis=-1, keepdims=True)
    rest = jnp.where(lane == i1, NEG_INF, logits)
    m2 = jnp.max(rest, axis=-1, keepdims=True)
    i2 = jnp.min(jnp.where(rest == m2, lane, big), axis=-1, keepdims=True)
    e2 = jnp.exp(m2 - m1)
    g1 = 1.0 / (1.0 + e2)
    g2 = e2 / (1.0 + e2)
    o_ref[...] = jnp.where(lane == i1, g1, 0.0) + jnp.where(lane == i2, g2, 0.0)


def _router(x, wh, wl, tm):
    n, d = x.shape
    return pl.pallas_call(
        _router_kernel,
        grid=(n // tm,),
        in_specs=[pl.BlockSpec((tm, d), lambda i: (i, 0)),
                  pl.BlockSpec(wh.shape, lambda i: (0, 0)),
                  pl.BlockSpec(wl.shape, lambda i: (0, 0))],
        out_specs=pl.BlockSpec((tm, LANES), lambda i: (i, 0)),
        out_shape=jax.ShapeDtypeStruct((n, LANES), F32),
        compiler_params=_params(("parallel",)),
        name="router",
    )(x, wh, wl)


def _ffn_kernel(x_ref, xb_ref, comb_ref, w1_ref, w3_ref, w2_ref, g_ref, b_ref, y_ref, yb_ref, acc_ref, *, use_comb):
    e = pl.program_id(1)
    f = pl.program_id(2)

    @pl.when((e == 0) & (f == 0))
    def _():
        acc_ref[...] = jnp.zeros_like(acc_ref)

    xb = xb_ref[...]
    h1 = _dot(xb, w1_ref[0])
    h = (h1 * _sigmoid(h1)) * _dot(xb, w3_ref[0])
    part = _dot(h.astype(BF16), w2_ref[0])
    if use_comb:
        comb = comb_ref[...]
        lane = lax.broadcasted_iota(jnp.int32, comb.shape, 1)
        part = part * jnp.sum(jnp.where(lane == e, comb, 0.0), axis=-1, keepdims=True)
    acc_ref[...] += part

    @pl.when((e == pl.num_programs(1) - 1) & (f == pl.num_programs(2) - 1))
    def _():
        y = _layer_norm_rows(DEEPNORM_ALPHA * x_ref[...] + acc_ref[...], g_ref[...], b_ref[...])
        y_ref[...] = y
        yb_ref[...] = y.astype(BF16)


def _ffn(x, xb, comb, w1, w3, w2, g, b, *, tm, tf, use_comb):
    n, d = x.shape
    n_e, _, ff = w1.shape
    kern = functools.partial(_ffn_kernel, use_comb=use_comb)
    return pl.pallas_call(
        kern,
        grid=(n // tm, n_e, ff // tf),
        in_specs=[
            pl.BlockSpec((tm, d), lambda i, e, f: (i, 0)),
            pl.BlockSpec((tm, d), lambda i, e, f: (i, 0)),
            pl.BlockSpec((tm, LANES), lambda i, e, f: (i, 0)),
            pl.BlockSpec((1, d, tf), lambda i, e, f: (e, 0, f)),
            pl.BlockSpec((1, d, tf), lambda i, e, f: (e, 0, f)),
            pl.BlockSpec((1, tf, d), lambda i, e, f: (e, f, 0)),
            pl.BlockSpec((1, d), lambda i, e, f: (0, 0)),
            pl.BlockSpec((1, d), lambda i, e, f: (0, 0)),
        ],
        out_specs=[pl.BlockSpec((tm, d), lambda i, e, f: (i, 0)),
                   pl.BlockSpec((tm, d), lambda i, e, f: (i, 0))],
        out_shape=[jax.ShapeDtypeStruct((n, d), F32), jax.ShapeDtypeStruct((n, d), BF16)],
        scratch_shapes=[pltpu.VMEM((tm, d), F32)],
        compiler_params=_params(("parallel", "arbitrary", "arbitrary")),
        name="ffn",
    )(x, xb, comb, w1, w3, w2, g, b)


def _rel_bucket(rel):
    nb = N_BUCKETS // 2
    max_exact = nb // 2
    ret = jnp.where(rel > 0, nb, 0)
    n = jnp.abs(rel)
    nf = jnp.maximum(n, 1).astype(jnp.float32)
    large = max_exact + (jnp.log(nf / max_exact) / math.log(MAX_DISTANCE / max_exact) * (nb - max_exact)).astype(jnp.int32)
    large = jnp.minimum(large, nb - 1)
    return ret + jnp.where(n < max_exact, n, large)


def _bias_tiles(rel_bias, tq):
    r = jnp.arange(tq, dtype=jnp.int32)[:, None]
    c = jnp.arange(LANES, dtype=jnp.int32)[None, :]
    near = jnp.stack([c - r - LANES, c - r], axis=0)
    far = jnp.full((1, tq, LANES), -(2 * MAX_DISTANCE + 2 * LANES), jnp.int32)
    rel = jnp.concatenate([far, near], axis=0)
    vals = jnp.take(rel_bias.astype(F32), _rel_bucket(rel), axis=0)
    vals = vals.transpose(0, 3, 1, 2).reshape(3, HKV_B, G_B * tq, LANES)
    return vals


def _prep_w_in(w_in_l, b_in_l):
    def cols(a):
        parts = [a[..., 0:1536], a[..., 1544:2568], a[..., 2636:4684], a[..., 2568:2632], a[..., 1536:1544], a[..., 2632:2636]]
        pad = jnp.zeros(a.shape[:-1] + (N_IN_PAD - N_IN,), a.dtype)
        return jnp.concatenate(parts + [pad], axis=-1)

    return cols(w_in_l).astype(BF16), cols(b_in_l[None, :]).astype(F32)


def _pad_time(a, length):
    pad = length - a.shape[1]
    if pad == 0:
        return a
    return jnp.concatenate([a, jnp.zeros((a.shape[0], pad) + a.shape[2:], a.dtype)], axis=1)


def _heads_first(a, n_heads, scale=None):
    b, t, _ = a.shape
    a = a.reshape(b, t, n_heads, -1)
    if scale is not None:
        a = a * scale
    return a.transpose(0, 2, 1, 3).astype(BF16)


def _round_up(x, m):
    return (x + m - 1) // m * m


def _run_group(x, caches, wts):
    b, t, d = x.shape
    n = b * t
    p_len = 0 if caches is None else caches[0].shape[2]
    tm = min(512, n)
    xf = x.reshape(n, d).astype(F32)
    xb = xf.astype(BF16)
    flag = jnp.zeros((1, N_IN_PAD), F32).at[:, C_FA:C_FA + H_A].set(1.0)

    fox_tq = min(512, t)
    fox_tk = 512
    l_fox = _round_up(p_len + t, fox_tk)
    dsa_tq = min(LANES, t)
    l_dsa = _round_up(p_len + t, LANES)
    topk = min(TOPK_MAX, (p_len + t) // 4)
    nb = _bias_tiles(wts["rel_bias"], dsa_tq)

    lb_cum = jnp.cumsum(jax.nn.softmax(wts["hgrn_lb"].astype(F32), axis=0), axis=0)

    states = []
    for l in range(DEPTH):
        w_in_p, b_in_p = _prep_w_in(wts["w_in"][l], wts["b_in"][l])
        z = _proj(xb, w_in_p, b_in_p, flag, tm)
        z3 = z.reshape(b, t, N_IN_PAD)

        k_a = z3[..., C_KA:C_KA + 512]
        v_a = z3[..., C_VA:C_VA + 512]
        logf_a = z3[..., C_FA:C_FA + H_A]
        k_b = z3[..., C_KB:C_KB + 128]
        v_b = z3[..., C_VB:C_VB + 128]
        ki_b = z3[..., C_KI:C_KI + DI_B]
        if caches is None:
            ka_all, va_all, lf_all, kb_all, vb_all, ki_all = k_a, v_a, logf_a, k_b, v_b, ki_b
            s0 = jnp.zeros((b, H_C, DK_C, DK_C), F32)
        else:
            c_ka, c_va, c_lf, c_kb, c_vb, c_ki, c_s = [c[l] for c in caches]

            def cat(c, new):
                return jnp.concatenate([c.reshape(b, p_len, -1).astype(F32), new], axis=1)

            ka_all, va_all, lf_all = cat(c_ka, k_a), cat(c_va, v_a), cat(c_lf, logf_a)
            kb_all, vb_all, ki_all = cat(c_kb, k_b), cat(c_vb, v_b), cat(c_ki, ki_b)
            s0 = c_s.astype(F32)

        cum = _cumsum(_pad_time(lf_all, l_fox).transpose(0, 2, 1))
        cq = cum[:, :, p_len:p_len + t, None]
        ck = cum[:, :, None, :]
        o_a = _fox(_heads_first(z3[..., C_QA:C_QA + 512], H_A, HEAD_DIM ** -0.5),
                   _heads_first(_pad_time(ka_all, l_fox), H_A), _heads_first(_pad_time(va_all, l_fox), H_A),
                   cq, ck, tq=fox_tq, tk=fox_tk, q_off=p_len)

        n_t = t // dsa_tq
        q_b = (z3[..., C_QB:C_QB + 512] * HEAD_DIM ** -0.5).reshape(b, n_t, dsa_tq, HKV_B, G_B, HEAD_DIM)
        q_b = q_b.transpose(0, 3, 1, 4, 2, 5).reshape(b, HKV_B, n_t * G_B * dsa_tq, HEAD_DIM).astype(BF16)
        o_b = _dsa(_heads_first(z3[..., C_QI:C_QI + HI_B * DI_B], HI_B, DI_B ** -0.5),
                   z3[..., C_WI:C_WI + HI_B] * HI_B ** -0.5,
                   _pad_time(ki_all, l_dsa).astype(BF16),
                   q_b, _heads_first(_pad_time(kb_all, l_dsa), HKV_B), _heads_first(_pad_time(vb_all, l_dsa), HKV_B),
                   nb, tq=dsa_tq, q_off=p_len, topk=topk)
        o_b = o_b.reshape(b, HKV_B, n_t, G_B, dsa_tq, HEAD_DIM).transpose(0, 2, 4, 1, 3, 5).reshape(n, H_B * HEAD_DIM)

        lb = (lb_cum[l] - lb_cum[0]).reshape(1, W_C)
        o_c, s_new = _hgrn(z3, jnp.log(lb), jnp.log1p(-lb), 1.0 - lb,
                           wts["hgrn_norm_g"][l].reshape(1, W_C).astype(F32), s0)

        x1, x1b = _merge(xf, xb, o_a.reshape(n, -1), o_b, o_c.reshape(n, -1),
                         wts["w_gate"][l].astype(BF16), wts["b_gate"][l][None, :].astype(F32),
                         wts["w_o_fox"][l].astype(BF16), wts["w_o_dsa"][l].astype(BF16), wts["w_o_hgrn"][l].astype(BF16),
                         wts["w_out"][l].astype(BF16), wts["ln1_g"][l][None, :], wts["ln1_b"][l][None, :], min(256, n))

        g2, b2 = wts["ln2_g"][l][None, :], wts["ln2_b"][l][None, :]
        if l % 2 == 0:
            j = l // 2
            comb = jnp.zeros((n, LANES), F32)
            xf, xb = _ffn(x1, x1b, comb, wts["ffn_w1"][j][None].astype(BF16), wts["ffn_w3"][j][None].astype(BF16),
                          wts["ffn_w2"][j][None].astype(BF16), g2, b2, tm=tm, tf=1408, use_comb=False)
        else:
            j = l // 2
            rw = jnp.zeros((d, LANES), F32).at[:, :N_EXPERTS].set(wts["moe_router"][j].astype(F32))
            rwh = rw.astype(BF16)
            rwl = (rw - rwh.astype(F32)).astype(BF16)
            comb = _router(x1, rwh, rwl, tm)
            xf, xb = _ffn(x1, x1b, comb, wts["moe_w1"][j].astype(BF16), wts["moe_w3"][j].astype(BF16),
                          wts["moe_w2"][j].astype(BF16), g2, b2, tm=tm, tf=896, use_comb=True)

        states.append((k_a.reshape(b, t, H_A, HEAD_DIM), v_a.reshape(b, t, H_A, HEAD_DIM), logf_a,
                       k_b.reshape(b, t, HKV_B, HEAD_DIM), v_b.reshape(b, t, HKV_B, HEAD_DIM), ki_b, s_new))
    stacked = [jnp.stack([s[i] for s in states], axis=0) for i in range(7)]
    return xf.reshape(b, t, d), stacked


def kernel(x_prompt, x_sample, cache_fox_k, cache_fox_v, cache_fox_logf, cache_dsa_k, cache_dsa_v, cache_dsa_kidx, state_hgrn,
           w_in, b_in, w_gate, b_gate, w_o_fox, w_o_dsa, w_o_hgrn, w_out, hgrn_lb, hgrn_norm_g, rel_bias,
           ln1_g, ln1_b, ln2_g, ln2_b, ffn_w1, ffn_w3, ffn_w2, moe_router, moe_w1, moe_w3, moe_w2):
    wts = dict(w_in=w_in, b_in=b_in, w_gate=w_gate, b_gate=b_gate, w_o_fox=w_o_fox, w_o_dsa=w_o_dsa, w_o_hgrn=w_o_hgrn,
               w_out=w_out, hgrn_lb=hgrn_lb, hgrn_norm_g=hgrn_norm_g, rel_bias=rel_bias, ln1_g=ln1_g, ln1_b=ln1_b,
               ln2_g=ln2_g, ln2_b=ln2_b, ffn_w1=ffn_w1, ffn_w3=ffn_w3, ffn_w2=ffn_w2, moe_router=moe_router,
               moe_w1=moe_w1, moe_w3=moe_w3, moe_w2=moe_w2)
    caches = (cache_fox_k, cache_fox_v, cache_fox_logf, cache_dsa_k, cache_dsa_v, cache_dsa_kidx, state_hgrn)
    y_prompt, sp = _run_group(x_prompt, None, wts)
    y_sample, ss = _run_group(x_sample, caches, wts)
    return (y_prompt, y_sample, sp[0], sp[1], sp[2], sp[3], sp[4], sp[5], sp[6],
            ss[0], ss[1], ss[2], ss[3], ss[4], ss[5], ss[6])
```

```python
import functools
import math

import jax
import jax.numpy as jnp
import numpy as np
from jax import lax
from jax.experimental import pallas as pl
from jax.experimental.pallas import tpu as pltpu

F32 = jnp.float32
BF16 = jnp.bfloat16
NEG_INF = float("-inf")

D_MODEL = 1024
DEPTH = 2
CHUNK = 64
HEAD_DIM = 64
H_A = 8
H_B = 8
HKV_B = 2
G_B = H_B // HKV_B
HI_B = 4
DI_B = 64
TOPK_MAX = 256
H_C = 4
DK_C = 128
W_C = H_C * DK_C
N_BUCKETS = 32
MAX_DISTANCE = 128
N_EXPERTS = 8
DEEPNORM_ALPHA = (2.0 * DEPTH) ** 0.25
LN_EPS = 1e-5

LANES = 128
SUBLANES = 8
VMEM_LIMIT = 56 * 1024 * 1024

C_QA, C_KA, C_VA = 0, 512, 1024
C_QB, C_KB, C_VB, C_QI = 1536, 2048, 2176, 2304
C_FC, C_IC, C_QC, C_GC = 2560, 3072, 3584, 4096
C_KI, C_FA, C_WI = 4608, 4672, 4680
N_IN = 4684
N_IN_PAD = 4864
PROJ_TN = 256


def _params(sem, vmem=VMEM_LIMIT):
    return pltpu.CompilerParams(dimension_semantics=sem, vmem_limit_bytes=vmem)


def _log_sigmoid(z):
    return jnp.minimum(z, 0.0) - jnp.log1p(jnp.exp(-jnp.abs(z)))


def _sigmoid(z):
    return 1.0 / (1.0 + jnp.exp(-z))


def _dot_nt(a, b):
    return lax.dot_general(a, b, (((1,), (1,)), ((), ())), preferred_element_type=F32)


def _dot_tn(a, b):
    return lax.dot_general(a, b, (((0,), (0,)), ((), ())), preferred_element_type=F32)


def _dot(a, b):
    return jnp.dot(a, b, preferred_element_type=F32)


def _split3(x):
    hi = x.astype(BF16)
    r1 = x - hi.astype(F32)
    mid = r1.astype(BF16)
    lo = (r1 - mid.astype(F32)).astype(BF16)
    return hi, mid, lo


def _layer_norm_rows(r, g, b):
    mu = jnp.mean(r, axis=-1, keepdims=True)
    rc = r - mu
    var = jnp.mean(rc * rc, axis=-1, keepdims=True)
    return rc * lax.rsqrt(var + LN_EPS) * g + b


def _proj_kernel(x_ref, w_ref, b_ref, flag_ref, o_ref):
    z = _dot(x_ref[...], w_ref[...]) + b_ref[...]
    last = pl.num_programs(1) - 1

    @pl.when(pl.program_id(1) != last)
    def _():
        o_ref[...] = z

    @pl.when(pl.program_id(1) == last)
    def _():
        o_ref[...] = jnp.where(flag_ref[...] > 0.0, _log_sigmoid(z), z)


def _proj(xb, w, b, flag, tm):
    n, k = xb.shape
    m = w.shape[1]
    tn = PROJ_TN
    return pl.pallas_call(
        _proj_kernel,
        grid=(n // tm, m // tn),
        in_specs=[
            pl.BlockSpec((tm, k), lambda i, j: (i, 0)),
            pl.BlockSpec((k, tn), lambda i, j: (0, j)),
            pl.BlockSpec((1, tn), lambda i, j: (0, j)),
            pl.BlockSpec((1, tn), lambda i, j: (0, j)),
        ],
        out_specs=pl.BlockSpec((tm, tn), lambda i, j: (i, j)),
        out_shape=jax.ShapeDtypeStruct((n, m), F32),
        compiler_params=_params(("parallel", "arbitrary")),
        name="proj",
    )(xb, w, b, flag)


def _cumsum_kernel(x_ref, hi_ref, mid_ref, lo_ref):
    c = x_ref[0]
    length = c.shape[-1]
    lane = lax.broadcasted_iota(jnp.int32, c.shape, 1)
    s = 1
    while s < length:
        c = c + jnp.where(lane >= s, pltpu.roll(c, s, 1), 0.0)
        s *= 2
    hi, mid, lo = _split3(c)
    hi_ref[0] = hi.astype(F32)
    mid_ref[0] = mid.astype(F32)
    lo_ref[0] = lo.astype(F32)


def _cumsum(x):
    b, r, length = x.shape
    spec = pl.BlockSpec((1, r, length), lambda i: (i, 0, 0))
    return pl.pallas_call(
        _cumsum_kernel,
        grid=(b,),
        in_specs=[spec],
        out_specs=[spec, spec, spec],
        out_shape=[jax.ShapeDtypeStruct(x.shape, F32)] * 3,
        compiler_params=_params(("parallel",)),
        name="cumsum",
    )(x)


FOX_KDIM = 128
FOX_VROWS = 80


def _fox_kernel(qt_ref, k_ref, vt_ref, o_ref, m_ref, acc_ref, *, tq, tk, q_off, hps):
    i = pl.program_id(2)
    q0 = q_off + i * tq
    nk = (q0 + tq + tk - 1) // tk
    n_full = (q0 + 1) // tk
    krow = lax.broadcasted_iota(jnp.int32, (tk, tq), 0)
    qcol = q0 + lax.broadcasted_iota(jnp.int32, (tk, tq), 1)
    outs = []
    for h in range(hps):
        qt = qt_ref[0, h]
        m_ref[...] = jnp.full((1, tq), NEG_INF, F32)
        acc_ref[...] = jnp.zeros((FOX_VROWS, tq), F32)

        def tile(j, masked, h=h, qt=qt):
            ks = pl.multiple_of(j * tk, tk)
            s = _dot(k_ref[0, h, pl.ds(ks, tk), :], qt)
            if masked:
                s = jnp.where(krow + ks <= qcol, s, NEG_INF)
            m_old = m_ref[...]
            m_new = jnp.maximum(m_old, jnp.max(s, axis=0, keepdims=True))
            p = jnp.exp(s - m_new)
            acc_ref[...] = jnp.exp(m_old - m_new) * acc_ref[...] + _dot(vt_ref[0, h, :, pl.ds(ks, tk)], p.astype(BF16))
            m_ref[...] = m_new

        def full_tile(j, carry):
            tile(j, False)
            return carry

        def edge_tile(j, carry):
            tile(j, True)
            return carry

        lax.fori_loop(0, n_full, full_tile, 0)
        lax.fori_loop(n_full, nk, edge_tile, 0)
        acc = acc_ref[...]
        outs.append((acc[:HEAD_DIM] / acc[HEAD_DIM:HEAD_DIM + 1]).T)
    o_ref[0] = jnp.concatenate(outs, axis=-1).astype(o_ref.dtype)


def _fox(qt, k, vt, *, tq, tk, q_off):
    b, h, _, t = qt.shape
    length = k.shape[2]
    hps = 2
    kern = functools.partial(_fox_kernel, tq=tq, tk=tk, q_off=q_off, hps=hps)
    return pl.pallas_call(
        kern,
        grid=(b, h // hps, t // tq),
        in_specs=[
            pl.BlockSpec((1, hps, FOX_KDIM, tq), lambda bi, hi, i: (bi, hi, 0, i)),
            pl.BlockSpec((1, hps, length, FOX_KDIM), lambda bi, hi, i: (bi, hi, 0, 0)),
            pl.BlockSpec((1, hps, FOX_VROWS, length), lambda bi, hi, i: (bi, hi, 0, 0)),
        ],
        out_specs=pl.BlockSpec((1, tq, hps * HEAD_DIM), lambda bi, hi, i: (bi, i, hi)),
        out_shape=jax.ShapeDtypeStruct((b, t, h * HEAD_DIM), BF16),
        scratch_shapes=[
            pltpu.VMEM((1, tq), F32),
            pltpu.VMEM((FOX_VROWS, tq), F32),
        ],
        compiler_params=_params(("parallel", "parallel", "arbitrary")),
        name="fox",
    )(qt, k, vt)


def _fox_operands(q, k_all, v_all, cum3, p_len, length):
    b, t, _ = q.shape
    n_keys = k_all.shape[1]
    ones3 = jnp.ones((b, H_A, n_keys, 3), BF16)
    ck = jnp.stack([c[:, :, :n_keys].astype(BF16) for c in cum3], axis=-1)
    kh = k_all.reshape(b, n_keys, H_A, HEAD_DIM).transpose(0, 2, 1, 3).astype(BF16)
    k_ext = jnp.concatenate([kh, ones3, ck, jnp.zeros((b, H_A, n_keys, FOX_KDIM - HEAD_DIM - 6), BF16)], axis=-1)
    k_ext = jnp.concatenate([k_ext, jnp.zeros((b, H_A, length - n_keys, FOX_KDIM), BF16)], axis=2)
    cq = jnp.stack([c[:, :, p_len:p_len + t].astype(BF16) for c in cum3], axis=2)
    qh = (q.reshape(b, t, H_A, HEAD_DIM) * HEAD_DIM ** -0.5).transpose(0, 2, 3, 1).astype(BF16)
    qt_ext = jnp.concatenate([qh, cq, -jnp.ones((b, H_A, 3, t), BF16),
                              jnp.zeros((b, H_A, FOX_KDIM - HEAD_DIM - 6, t), BF16)], axis=2)
    vh = v_all.reshape(b, n_keys, H_A, HEAD_DIM).transpose(0, 2, 3, 1).astype(BF16)
    vt_ext = jnp.concatenate([vh, jnp.ones((b, H_A, FOX_VROWS - HEAD_DIM, n_keys), BF16)], axis=2)
    vt_ext = jnp.concatenate([vt_ext, jnp.zeros((b, H_A, FOX_VROWS, length - n_keys), BF16)], axis=3)
    return qt_ext, k_ext, vt_ext


INT_MIN = -(2 ** 31)
KEY_NEG_INF = int(np.int32(np.uint32(0xFF800000) ^ np.uint32(0x7FFFFFFF)))
DSA_VROWS = 80


def _order_key(x):
    bits = lax.bitcast_convert_type(x, jnp.int32)
    return bits ^ ((bits >> 31) & jnp.int32(0x7FFFFFFF))


def _dsa_kernel(qit_ref, wit_ref, ki_ref, qt_ref, k_ref, vt_ref, nbt_ref, o_ref,
                sc_ref, m_ref, acc_ref, *, tq, q_off, topk):
    tk = LANES
    i = pl.program_id(1)
    q0 = q_off + i * tq
    i_abs = q0 // tk
    nk = i_abs + 1
    cols = G_B * tq

    wit = wit_ref[0]
    wrows = [wit[hh:hh + 1, :] for hh in range(HI_B)]
    q_chunk = (q0 + lax.broadcasted_iota(jnp.int32, (tk, tq), 1)) >> 6
    krow = lax.broadcasted_iota(jnp.int32, (tk, tq), 0)

    def score_tile(j, carry):
        ks = pl.multiple_of(j * tk, tk)
        ki = ki_ref[0, pl.ds(ks, tk), :]
        tot = jnp.zeros((tk, tq), F32)
        for hh in range(HI_B):
            tot = tot + wrows[hh] * jnp.maximum(_dot(ki, qit_ref[0, hh]), 0.0)
        tot = jnp.where(((krow + ks) >> 6) <= q_chunk, tot, NEG_INF)
        sc_ref[j] = _order_key(tot)
        return carry

    lax.fori_loop(0, nk, score_tile, 0)

    kf = float(topk)

    def count(pred_fn):
        def cb(j, accv):
            return accv + jnp.where(pred_fn(sc_ref[j]), 1.0, 0.0)
        accv = lax.fori_loop(0, nk, cb, jnp.zeros((tk, tq), F32))
        return jnp.sum(accv, axis=0, keepdims=True)

    zero_b = jnp.zeros((tk, tq), jnp.int32)
    c0 = count(lambda key: key >= zero_b)
    thr = jnp.where(c0 >= kf, jnp.int32(0), jnp.int32(INT_MIN))
    for bit in range(30, -1, -1):
        cand = thr + jnp.int32(1 << bit)
        cand_b = jnp.broadcast_to(cand, (tk, tq))
        c = count(lambda key, cand_b=cand_b: key >= cand_b)
        thr = jnp.where(c >= kf, cand, thr)

    thr_b = jnp.broadcast_to(thr, (tk, tq))
    cnt_gt = count(lambda key: key > thr_b)
    short = thr <= jnp.int32(KEY_NEG_INF)
    thr_b = jnp.broadcast_to(jnp.maximum(thr, jnp.int32(KEY_NEG_INF)), (tk, tq))
    need = jnp.where(short, 0.0, kf - cnt_gt)

    tri = jnp.where(lax.broadcasted_iota(jnp.int32, (tk, tk), 1) <= lax.broadcasted_iota(jnp.int32, (tk, tk), 0),
                    1.0, 0.0).astype(BF16)
    m_ref[...] = jnp.full((HKV_B, 1, cols), NEG_INF, F32)
    acc_ref[...] = jnp.zeros((HKV_B, DSA_VROWS, cols), F32)

    def attend_tile(j, eqc):
        ks = pl.multiple_of(j * tk, tk)
        key = sc_ref[j]
        gt = key > thr_b
        eq = key == thr_b
        eqf = jnp.where(eq, 1.0, 0.0)
        rank = _dot(tri, eqf.astype(BF16)) + eqc
        pen = jnp.where(gt, 0.0, jnp.where(eq, rank, 1e9))
        madd = jnp.where(pen <= need, 0.0, NEG_INF)
        madd = jnp.concatenate([madd] * G_B, axis=1)
        sel = jnp.clip(j - i_abs + 2, 0, 2)
        for g in range(HKV_B):
            s = _dot(k_ref[0, g, pl.ds(ks, tk), :], qt_ref[0, g]) + nbt_ref[sel, g] + madd
            m_old = m_ref[g]
            m_new = jnp.maximum(m_old, jnp.max(s, axis=0, keepdims=True))
            m_safe = jnp.where(m_new == NEG_INF, 0.0, m_new)
            p = jnp.exp(s - m_safe)
            acc_ref[g] = jnp.exp(m_old - m_safe) * acc_ref[g] + _dot(vt_ref[0, g, :, pl.ds(ks, tk)], p.astype(BF16))
            m_ref[g] = m_new
        return eqc + jnp.sum(eqf, axis=0, keepdims=True)

    lax.fori_loop(0, nk, attend_tile, jnp.zeros((1, tq), F32))
    for g in range(HKV_B):
        acc = acc_ref[g]
        o_ref[0, g] = (acc[:HEAD_DIM] / acc[HEAD_DIM:HEAD_DIM + 1]).astype(o_ref.dtype)


def _dsa(qit, wit, ki, qt, k, vt, nbt, *, tq, q_off, topk):
    b, _, _, t = qit.shape
    length = ki.shape[1]
    cols = G_B * tq
    assert q_off % LANES == 0 and tq <= LANES and length % LANES == 0 and length >= q_off + t
    assert tq % CHUNK == 0 and (tq == LANES or t == tq)
    kern = functools.partial(_dsa_kernel, tq=tq, q_off=q_off, topk=topk)
    return pl.pallas_call(
        kern,
        grid=(b, t // tq),
        in_specs=[
            pl.BlockSpec((1, HI_B, DI_B, tq), lambda bi, i: (bi, 0, 0, i)),
            pl.BlockSpec((1, HI_B, tq), lambda bi, i: (bi, 0, i)),
            pl.BlockSpec((1, length, DI_B), lambda bi, i: (bi, 0, 0)),
            pl.BlockSpec((1, HKV_B, HEAD_DIM, cols), lambda bi, i: (bi, 0, 0, i)),
            pl.BlockSpec((1, HKV_B, length, HEAD_DIM), lambda bi, i: (bi, 0, 0, 0)),
            pl.BlockSpec((1, HKV_B, DSA_VROWS, length), lambda bi, i: (bi, 0, 0, 0)),
            pl.BlockSpec((3, HKV_B, LANES, cols), lambda bi, i: (0, 0, 0, 0)),
        ],
        out_specs=pl.BlockSpec((1, HKV_B, HEAD_DIM, cols), lambda bi, i: (bi, 0, 0, i)),
        out_shape=jax.ShapeDtypeStruct(qt.shape, BF16),
        scratch_shapes=[
            pltpu.VMEM((length // LANES, LANES, tq), jnp.int32),
            pltpu.VMEM((HKV_B, 1, cols), F32),
            pltpu.VMEM((HKV_B, DSA_VROWS, cols), F32),
        ],
        compiler_params=_params(("parallel", "arbitrary")),
        name="dsa",
    )(qit, wit, ki, qt, k, vt, nbt)


HG_SUB = SUBLANES


def _hgrn_kernel(f_ref, i_ref, q_ref, g_ref, la_ref, l1_ref, oml_ref, ng_ref, s0_ref,
                 o_ref, sfin_ref, st_ref):
    c = pl.program_id(1)
    n_c = pl.num_programs(1)
    ct = CHUNK

    @pl.when(c == 0)
    def _():
        for hh in range(H_C):
            st_ref[hh] = s0_ref[0, hh].T

    zf = f_ref[0]
    a = la_ref[...]
    cc = l1_ref[...] + _log_sigmoid(zf)
    logf = jnp.maximum(a, cc) + jnp.log1p(jnp.exp(-jnp.abs(a - cc)))
    kk = oml_ref[...] * _sigmoid(-zf)

    r_i = lax.broadcasted_iota(jnp.int32, (ct, ct), 0)
    c_i = lax.broadcasted_iota(jnp.int32, (ct, ct), 1)
    low = jnp.where(c_i <= r_i, 1.0, 0.0).astype(BF16)
    hi, mid, lo = _split3(logf)
    bcum = _dot(low, hi) + _dot(low, mid) + _dot(low, lo)

    sub_row = lax.broadcasted_iota(jnp.int32, (HG_SUB, DK_C), 0)
    a_col = lax.broadcasted_iota(jnp.int32, (HG_SUB, ct), 1)
    n_sub = ct // HG_SUB

    for hh in range(H_C):
        cs = slice(hh * DK_C, (hh + 1) * DK_C)
        qh = q_ref[0, :, cs]
        kh = kk[:, cs]
        vh = i_ref[0, :, cs]
        bh = bcum[:, cs]
        a_rows = []
        for bi in range(n_sub):
            r0 = bi * HG_SUB
            q_blk = qh[r0:r0 + HG_SUB]
            b_blk = bh[r0:r0 + HG_SUB]
            k_blk = kh[r0:r0 + HG_SUB]
            if bi > 0:
                ref = bh[r0:r0 + 1]
                qp = q_blk * jnp.exp(b_blk - ref)
                kp = kh[:r0] * jnp.exp(ref - bh[:r0])
                kp = jnp.concatenate([kp, jnp.zeros((ct - r0, DK_C), F32)], axis=0)
                a_row = _dot_nt(qp.astype(BF16), kp.astype(BF16))
            else:
                a_row = jnp.zeros((HG_SUB, ct), F32)
            for s in range(HG_SUB):
                d = jnp.where(sub_row >= s, b_blk - b_blk[s:s + 1], NEG_INF)
                x = q_blk * jnp.exp(d) * k_blk[s:s + 1]
                col = jnp.sum(x, axis=-1, keepdims=True)
                a_row = jnp.where(a_col == r0 + s, col, a_row)
            a_rows.append(a_row)
        a_mat = jnp.concatenate(a_rows, axis=0)
        st = st_ref[hh]
        o = _dot(a_mat.astype(BF16), vh.astype(BF16)) + _dot_nt((qh * jnp.exp(bh)).astype(BF16), st.astype(BF16))
        bl = bh[ct - 1:ct]
        kdec = kh * jnp.exp(bl - bh)
        st_ref[hh] = st * jnp.exp(bl) + _dot_tn(vh.astype(BF16), kdec.astype(BF16))
        o = o * lax.rsqrt(jnp.mean(o * o, axis=-1, keepdims=True) + LN_EPS) * ng_ref[:, cs]
        gh = g_ref[0, :, cs]
        o_ref[0, :, cs] = (o * (gh * _sigmoid(gh))).astype(o_ref.dtype)

    @pl.when(c == n_c - 1)
    def _():
        for hh in range(H_C):
            sfin_ref[0, hh] = st_ref[hh].T


def _hgrn(z3, la, l1, oml, ng, s0):
    b, t, _ = z3.shape
    n_c = t // CHUNK

    def zspec(col):
        blk = col // W_C
        return pl.BlockSpec((1, CHUNK, W_C), lambda bi, ci: (bi, ci, blk))

    vec = pl.BlockSpec((1, W_C), lambda bi, ci: (0, 0))
    return pl.pallas_call(
        _hgrn_kernel,
        grid=(b, n_c),
        in_specs=[zspec(C_FC), zspec(C_IC), zspec(C_QC), zspec(C_GC), vec, vec, vec, vec,
                  pl.BlockSpec((1, H_C, DK_C, DK_C), lambda bi, ci: (bi, 0, 0, 0))],
        out_specs=[pl.BlockSpec((1, CHUNK, W_C), lambda bi, ci: (bi, ci, 0)),
                   pl.BlockSpec((1, H_C, DK_C, DK_C), lambda bi, ci: (bi, 0, 0, 0))],
        out_shape=[jax.ShapeDtypeStruct((b, t, W_C), BF16),
                   jax.ShapeDtypeStruct((b, H_C, DK_C, DK_C), F32)],
        scratch_shapes=[pltpu.VMEM((H_C, DK_C, DK_C), F32)],
        compiler_params=_params(("parallel", "arbitrary")),
        name="hgrn",
    )(z3, z3, z3, z3, la, l1, oml, ng, s0)


def _merge_kernel(x_ref, xb_ref, oa_ref, ob_ref, oc_ref, wg_ref, bg_ref, wa_ref, wb_ref, wc_ref, wo_ref,
                  g_ref, b_ref, y_ref, yb_ref):
    xb = xb_ref[...]
    d = D_MODEL
    mix = None
    for br, (o_ref, w_ref) in enumerate(((oa_ref, wa_ref), (ob_ref, wb_ref), (oc_ref, wc_ref))):
        gate = _sigmoid(_dot(xb, wg_ref[:, br * d:(br + 1) * d]) + bg_ref[:, br * d:(br + 1) * d])
        term = gate * _dot(o_ref[...], w_ref[...])
        mix = term if mix is None else mix + term
    out = _dot(mix.astype(BF16), wo_ref[...])
    y = _layer_norm_rows(DEEPNORM_ALPHA * x_ref[...] + out, g_ref[...], b_ref[...])
    y_ref[...] = y
    yb_ref[...] = y.astype(BF16)


def _merge(x, xb, oa, ob, oc, wg, bg, wa, wb, wc, wo, g, b, tm):
    n, d = x.shape

    def row(width):
        return pl.BlockSpec((tm, width), lambda i: (i, 0))

    def full(arr):
        return pl.BlockSpec(arr.shape, lambda i: (0, 0))

    return pl.pallas_call(
        _merge_kernel,
        grid=(n // tm,),
        in_specs=[row(d), row(d), row(oa.shape[1]), row(ob.shape[1]), row(oc.shape[1]),
                  full(wg), full(bg), full(wa), full(wb), full(wc), full(wo), full(g), full(b)],
        out_specs=[row(d), row(d)],
        out_shape=[jax.ShapeDtypeStruct((n, d), F32), jax.ShapeDtypeStruct((n, d), BF16)],
        compiler_params=_params(("parallel",)),
        name="merge",
    )(x, xb, oa, ob, oc, wg, bg, wa, wb, wc, wo, g, b)


def _router_kernel(x_ref, w_ref, o_ref):
    x = x_ref[...]
    xh = x.astype(BF16)
    xl = (x - xh.astype(F32)).astype(BF16)
    w = w_ref[...]
    wh = w.astype(BF16)
    wl = (w - wh.astype(F32)).astype(BF16)
    logits = _dot(xh, wh) + _dot(xh, wl) + _dot(xl, wh)
    lane = lax.broadcasted_iota(jnp.int32, logits.shape, 1)
    logits = jnp.where(lane < N_EXPERTS, logits, NEG_INF)
    big = jnp.int32(LANES)
    m1 = jnp.max(logits, axis=-1, keepdims=True)
    i1 = jnp.min(jnp.where(logits == m1, lane, big), axis=-1, keepdims=True)
    rest = jnp.where(lane == i1, NEG_INF, logits)
    m2 = jnp.max(rest, axis=-1, keepdims=True)
    i2 = jnp.min(jnp.where(rest == m2, lane, big), axis=-1, keepdims=True)
    e2 = jnp.exp(m2 - m1)
    g1 = 1.0 / (1.0 + e2)
    g2 = e2 / (1.0 + e2)
    o_ref[...] = jnp.where(lane == i1, g1, 0.0) + jnp.where(lane == i2, g2, 0.0)


def _router(x, w, tm):
    n, d = x.shape
    return pl.pallas_call(
        _router_kernel,
        grid=(n // tm,),
        in_specs=[pl.BlockSpec((tm, d), lambda i: (i, 0)),
                  pl.BlockSpec(w.shape, lambda i: (0, 0))],
        out_specs=pl.BlockSpec((tm, LANES), lambda i: (i, 0)),
        out_shape=jax.ShapeDtypeStruct((n, LANES), F32),
        compiler_params=_params(("parallel",)),
        name="router",
    )(x, w)


def _ffn_kernel(x_ref, xb_ref, comb_ref, w1_ref, w3_ref, w2_ref, g_ref, b_ref, y_ref, yb_ref, acc_ref, *, use_comb):
    e = pl.program_id(1)
    f = pl.program_id(2)

    @pl.when((e == 0) & (f == 0))
    def _():
        acc_ref[...] = jnp.zeros_like(acc_ref)

    xb = xb_ref[...]
    h1 = _dot(xb, w1_ref[0])
    h = (h1 * _sigmoid(h1)) * _dot(xb, w3_ref[0])
    part = _dot(h.astype(BF16), w2_ref[0])
    if use_comb:
        comb = comb_ref[...]
        lane = lax.broadcasted_iota(jnp.int32, comb.shape, 1)
        part = part * jnp.sum(jnp.where(lane == e, comb, 0.0), axis=-1, keepdims=True)
    acc_ref[...] += part

    @pl.when((e == pl.num_programs(1) - 1) & (f == pl.num_programs(2) - 1))
    def _():
        y = _layer_norm_rows(DEEPNORM_ALPHA * x_ref[...] + acc_ref[...], g_ref[...], b_ref[...])
        y_ref[...] = y
        yb_ref[...] = y.astype(BF16)


def _ffn(x, xb, comb, w1, w3, w2, g, b, *, tm, tf, use_comb):
    n, d = x.shape
    n_e, _, ff = w1.shape
    kern = functools.partial(_ffn_kernel, use_comb=use_comb)
    return pl.pallas_call(
        kern,
        grid=(n // tm, n_e, ff // tf),
        in_specs=[
            pl.BlockSpec((tm, d), lambda i, e, f: (i, 0)),
            pl.BlockSpec((tm, d), lambda i, e, f: (i, 0)),
            pl.BlockSpec((tm, LANES), lambda i, e, f: (i, 0)),
            pl.BlockSpec((1, d, tf), lambda i, e, f: (e, 0, f)),
            pl.BlockSpec((1, d, tf), lambda i, e, f: (e, 0, f)),
            pl.BlockSpec((1, tf, d), lambda i, e, f: (e, f, 0)),
            pl.BlockSpec((1, d), lambda i, e, f: (0, 0)),
            pl.BlockSpec((1, d), lambda i, e, f: (0, 0)),
        ],
        out_specs=[pl.BlockSpec((tm, d), lambda i, e, f: (i, 0)),
                   pl.BlockSpec((tm, d), lambda i, e, f: (i, 0))],
        out_shape=[jax.ShapeDtypeStruct((n, d), F32), jax.ShapeDtypeStruct((n, d), BF16)],
        scratch_shapes=[pltpu.VMEM((tm, d), F32)],
        compiler_params=_params(("parallel", "arbitrary", "arbitrary")),
        name="ffn",
    )(x, xb, comb, w1, w3, w2, g, b)


def _rel_bucket(rel):
    nb = N_BUCKETS // 2
    max_exact = nb // 2
    ret = jnp.where(rel > 0, nb, 0)
    n = jnp.abs(rel)
    nf = jnp.maximum(n, 1).astype(jnp.float32)
    large = max_exact + (jnp.log(nf / max_exact) / math.log(MAX_DISTANCE / max_exact) * (nb - max_exact)).astype(jnp.int32)
    large = jnp.minimum(large, nb - 1)
    return ret + jnp.where(n < max_exact, n, large)


def _bias_tiles(rel_bias, tq):
    r = jnp.arange(tq, dtype=jnp.int32)[:, None]
    c = jnp.arange(LANES, dtype=jnp.int32)[None, :]
    near = jnp.stack([c - r - LANES, c - r], axis=0)
    far = jnp.full((1, tq, LANES), -(2 * MAX_DISTANCE + 2 * LANES), jnp.int32)
    rel = jnp.concatenate([far, near], axis=0)
    vals = jnp.take(rel_bias.astype(F32), _rel_bucket(rel), axis=0)
    vals = vals.transpose(0, 3, 2, 1).reshape(3, HKV_B, G_B, LANES, tq)
    return vals.transpose(0, 1, 3, 2, 4).reshape(3, HKV_B, LANES, G_B * tq)


def _prep_w_in(w_in_l, b_in_l):
    def cols(a):
        parts = [a[..., 0:1536], a[..., 1544:2568], a[..., 2636:4684], a[..., 2568:2632], a[..., 1536:1544], a[..., 2632:2636]]
        pad = jnp.zeros(a.shape[:-1] + (N_IN_PAD - N_IN,), a.dtype)
        return jnp.concatenate(parts + [pad], axis=-1)

    return cols(w_in_l).astype(BF16), cols(b_in_l[None, :]).astype(F32)


def _pad_time(a, length):
    pad = length - a.shape[1]
    if pad == 0:
        return a
    return jnp.concatenate([a, jnp.zeros((a.shape[0], pad) + a.shape[2:], a.dtype)], axis=1)


def _heads_first(a, n_heads, scale=None):
    b, t, _ = a.shape
    a = a.reshape(b, t, n_heads, -1)
    if scale is not None:
        a = a * scale
    return a.transpose(0, 2, 1, 3).astype(BF16)


def _round_up(x, m):
    return (x + m - 1) // m * m


def _run_group(x, caches, wts):
    b, t, d = x.shape
    n = b * t
    p_len = 0 if caches is None else caches[0].shape[2]
    tm = min(512, n)
    xf = x.reshape(n, d).astype(F32)
    xb = xf.astype(BF16)
    flag = jnp.zeros((1, N_IN_PAD), F32).at[:, C_FA:C_FA + H_A].set(1.0)

    fox_tq = min(512, t)
    fox_tk = 512
    l_fox = _round_up(p_len + t, fox_tk)
    dsa_tq = min(LANES, t)
    l_dsa = _round_up(p_len + t, LANES)
    topk = min(TOPK_MAX, (p_len + t) // 4)
    nb = _bias_tiles(wts["rel_bias"], dsa_tq)

    lb_cum = jnp.cumsum(jax.nn.softmax(wts["hgrn_lb"].astype(F32), axis=0), axis=0)

    states = []
    for l in range(DEPTH):
        w_in_p, b_in_p = _prep_w_in(wts["w_in"][l], wts["b_in"][l])
        z = _proj(xb, w_in_p, b_in_p, flag, tm)
        z3 = z.reshape(b, t, N_IN_PAD)

        k_a = z3[..., C_KA:C_KA + 512]
        v_a = z3[..., C_VA:C_VA + 512]
        logf_a = z3[..., C_FA:C_FA + H_A]
        k_b = z3[..., C_KB:C_KB + 128]
        v_b = z3[..., C_VB:C_VB + 128]
        ki_b = z3[..., C_KI:C_KI + DI_B]
        if caches is None:
            ka_all, va_all, lf_all, kb_all, vb_all, ki_all = k_a, v_a, logf_a, k_b, v_b, ki_b
            s0 = jnp.zeros((b, H_C, DK_C, DK_C), F32)
        else:
            c_ka, c_va, c_lf, c_kb, c_vb, c_ki, c_s = [c[l] for c in caches]

            def cat(c, new):
                return jnp.concatenate([c.reshape(b, p_len, -1).astype(F32), new], axis=1)

            ka_all, va_all, lf_all = cat(c_ka, k_a), cat(c_va, v_a), cat(c_lf, logf_a)
            kb_all, vb_all, ki_all = cat(c_kb, k_b), cat(c_vb, v_b), cat(c_ki, ki_b)
            s0 = c_s.astype(F32)

        cum3 = _cumsum(_pad_time(lf_all, l_fox).transpose(0, 2, 1))
        qt_a, k_a_ext, vt_a = _fox_operands(z3[..., C_QA:C_QA + 512], ka_all, va_all, cum3, p_len, l_fox)
        o_a = _fox(qt_a, k_a_ext, vt_a, tq=fox_tq, tk=fox_tk, q_off=p_len)

        n_t = t // dsa_tq
        q_b = (z3[..., C_QB:C_QB + 512] * HEAD_DIM ** -0.5).reshape(b, n_t, dsa_tq, HKV_B, G_B, HEAD_DIM)
        qt_b = q_b.transpose(0, 3, 5, 1, 4, 2).reshape(b, HKV_B, HEAD_DIM, n_t * G_B * dsa_tq).astype(BF16)
        qit_b = (z3[..., C_QI:C_QI + HI_B * DI_B] * DI_B ** -0.5).reshape(b, t, HI_B, DI_B).transpose(0, 2, 3, 1).astype(BF16)
        wit_b = (z3[..., C_WI:C_WI + HI_B] * HI_B ** -0.5).transpose(0, 2, 1)
        vt_b = _pad_time(vb_all, l_dsa).reshape(b, l_dsa, HKV_B, HEAD_DIM).transpose(0, 2, 3, 1).astype(BF16)
        vt_b = jnp.concatenate([vt_b, jnp.ones((b, HKV_B, DSA_VROWS - HEAD_DIM, l_dsa), BF16)], axis=2)
        o_b = _dsa(qit_b, wit_b, _pad_time(ki_all, l_dsa).astype(BF16),
                   qt_b, _heads_first(_pad_time(kb_all, l_dsa), HKV_B), vt_b,
                   nb, tq=dsa_tq, q_off=p_len, topk=topk)
        o_b = o_b.reshape(b, HKV_B, HEAD_DIM, n_t, G_B, dsa_tq).transpose(0, 3, 5, 1, 4, 2).reshape(n, H_B * HEAD_DIM)

        lb = (lb_cum[l] - lb_cum[0]).reshape(1, W_C)
        o_c, s_new = _hgrn(z3, jnp.log(lb), jnp.log1p(-lb), 1.0 - lb,
                           wts["hgrn_norm_g"][l].reshape(1, W_C).astype(F32), s0)

        x1, x1b = _merge(xf, xb, o_a.reshape(n, -1), o_b, o_c.reshape(n, -1),
                         wts["w_gate"][l].astype(BF16), wts["b_gate"][l][None, :].astype(F32),
                         wts["w_o_fox"][l].astype(BF16), wts["w_o_dsa"][l].astype(BF16), wts["w_o_hgrn"][l].astype(BF16),
                         wts["w_out"][l].astype(BF16), wts["ln1_g"][l][None, :], wts["ln1_b"][l][None, :], min(256, n))

        g2, b2 = wts["ln2_g"][l][None, :], wts["ln2_b"][l][None, :]
        if l % 2 == 0:
            j = l // 2
            comb = jnp.zeros((n, LANES), F32)
            xf, xb = _ffn(x1, x1b, comb, wts["ffn_w1"][j][None].astype(BF16), wts["ffn_w3"][j][None].astype(BF16),
                          wts["ffn_w2"][j][None].astype(BF16), g2, b2, tm=tm, tf=1408, use_comb=False)
        else:
            j = l // 2
            rw = jnp.zeros((d, LANES), F32).at[:, :N_EXPERTS].set(wts["moe_router"][j].astype(F32))
            comb = _router(x1, rw, tm)
            xf, xb = _ffn(x1, x1b, comb, wts["moe_w1"][j].astype(BF16), wts["moe_w3"][j].astype(BF16),
                          wts["moe_w2"][j].astype(BF16), g2, b2, tm=tm, tf=896, use_comb=True)

        states.append((k_a.reshape(b, t, H_A, HEAD_DIM), v_a.reshape(b, t, H_A, HEAD_DIM), logf_a,
                       k_b.reshape(b, t, HKV_B, HEAD_DIM), v_b.reshape(b, t, HKV_B, HEAD_DIM), ki_b, s_new))
    stacked = [jnp.stack([s[i] for s in states], axis=0) for i in range(7)]
    return xf.reshape(b, t, d), stacked


def kernel(x_prompt, x_sample, cache_fox_k, cache_fox_v, cache_fox_logf, cache_dsa_k, cache_dsa_v, cache_dsa_kidx, state_hgrn,
           w_in, b_in, w_gate, b_gate, w_o_fox, w_o_dsa, w_o_hgrn, w_out, hgrn_lb, hgrn_norm_g, rel_bias,
           ln1_g, ln1_b, ln2_g, ln2_b, ffn_w1, ffn_w3, ffn_w2, moe_router, moe_w1, moe_w3, moe_w2):
    wts = dict(w_in=w_in, b_in=b_in, w_gate=w_gate, b_gate=b_gate, w_o_fox=w_o_fox, w_o_dsa=w_o_dsa, w_o_hgrn=w_o_hgrn,
               w_out=w_out, hgrn_lb=hgrn_lb, hgrn_norm_g=hgrn_norm_g, rel_bias=rel_bias, ln1_g=ln1_g, ln1_b=ln1_b,
               ln2_g=ln2_g, ln2_b=ln2_b, ffn_w1=ffn_w1, ffn_w3=ffn_w3, ffn_w2=ffn_w2, moe_router=moe_router,
               moe_w1=moe_w1, moe_w3=moe_w3, moe_w2=moe_w2)
    caches = (cache_fox_k, cache_fox_v, cache_fox_logf, cache_dsa_k, cache_dsa_v, cache_dsa_kidx, state_hgrn)
    y_prompt, sp = _run_group(x_prompt, None, wts)
    y_sample, ss = _run_group(x_sample, caches, wts)
    return (y_prompt, y_sample, sp[0], sp[1], sp[2], sp[3], sp[4], sp[5], sp[6],
            ss[0], ss[1], ss[2], ss[3], ss[4], ss[5], ss[6])
```

```python
import functools
import math

import jax
import jax.numpy as jnp
import numpy as np
from jax import lax
from jax.experimental import pallas as pl
from jax.experimental.pallas import tpu as pltpu

F32 = jnp.float32
BF16 = jnp.bfloat16
NEG_INF = float("-inf")

D_MODEL = 1024
DEPTH = 2
CHUNK = 64
HEAD_DIM = 64
H_A = 8
H_B = 8
HKV_B = 2
G_B = H_B // HKV_B
HI_B = 4
DI_B = 64
TOPK_MAX = 256
H_C = 4
DK_C = 128
W_C = H_C * DK_C
N_BUCKETS = 32
MAX_DISTANCE = 128
N_EXPERTS = 8
DEEPNORM_ALPHA = (2.0 * DEPTH) ** 0.25
LN_EPS = 1e-5

LANES = 128
SUBLANES = 8
VMEM_LIMIT = 56 * 1024 * 1024

C_QA, C_KA, C_VA = 0, 512, 1024
C_QB, C_KB, C_VB, C_QI = 1536, 2048, 2176, 2304
C_FC, C_IC, C_QC, C_GC = 2560, 3072, 3584, 4096
C_KI, C_FA, C_WI = 4608, 4672, 4680
N_IN = 4684
N_IN_PAD = 4864
PROJ_TN = 256


def _params(sem, vmem=VMEM_LIMIT):
    return pltpu.CompilerParams(dimension_semantics=sem, vmem_limit_bytes=vmem)


def _log_sigmoid(z):
    return jnp.minimum(z, 0.0) - jnp.log1p(jnp.exp(-jnp.abs(z)))


def _sigmoid(z):
    return 1.0 / (1.0 + jnp.exp(-z))


def _dot_nt(a, b):
    return lax.dot_general(a, b, (((1,), (1,)), ((), ())), preferred_element_type=F32)


def _dot_tn(a, b):
    return lax.dot_general(a, b, (((0,), (0,)), ((), ())), preferred_element_type=F32)


def _dot(a, b):
    return jnp.dot(a, b, preferred_element_type=F32)


def _split3(x):
    hi = x.astype(BF16)
    r1 = x - hi.astype(F32)
    mid = r1.astype(BF16)
    lo = (r1 - mid.astype(F32)).astype(BF16)
    return hi, mid, lo


def _layer_norm_rows(r, g, b):
    mu = jnp.mean(r, axis=-1, keepdims=True)
    rc = r - mu
    var = jnp.mean(rc * rc, axis=-1, keepdims=True)
    return rc * lax.rsqrt(var + LN_EPS) * g + b


def _proj_kernel(x_ref, w_ref, b_ref, o_ref):
    x = x_ref[...]
    o_ref[:, :C_KI] = _dot(x, w_ref[:, :C_KI]) + b_ref[:, :C_KI]
    z = _dot(x, w_ref[:, C_KI:]) + b_ref[:, C_KI:]
    col = lax.broadcasted_iota(jnp.int32, z.shape, 1) + C_KI
    o_ref[:, C_KI:] = jnp.where(col >= C_FA, jnp.where(col < C_FA + H_A, _log_sigmoid(z), z), z)


def _proj(xb, w, b, tm):
    n, k = xb.shape
    m = w.shape[1]
    return pl.pallas_call(
        _proj_kernel,
        grid=(n // tm,),
        in_specs=[
            pl.BlockSpec((tm, k), lambda i: (i, 0)),
            pl.BlockSpec((k, m), lambda i: (0, 0), pipeline_mode=pl.Buffered(1)),
            pl.BlockSpec((1, m), lambda i: (0, 0), pipeline_mode=pl.Buffered(1)),
        ],
        out_specs=pl.BlockSpec((tm, m), lambda i: (i, 0)),
        out_shape=jax.ShapeDtypeStruct((n, m), F32),
        compiler_params=_params(("parallel",)),
        name="proj",
    )(xb, w, b)


def _cumsum_kernel(x_ref, hi_ref, mid_ref, lo_ref):
    c = x_ref[0]
    length = c.shape[-1]
    lane = lax.broadcasted_iota(jnp.int32, c.shape, 1)
    s = 1
    while s < length:
        c = c + jnp.where(lane >= s, pltpu.roll(c, s, 1), 0.0)
        s *= 2
    hi, mid, lo = _split3(c)
    hi_ref[0] = hi.astype(F32)
    mid_ref[0] = mid.astype(F32)
    lo_ref[0] = lo.astype(F32)


def _cumsum(x):
    b, r, length = x.shape
    spec = pl.BlockSpec((1, r, length), lambda i: (i, 0, 0))
    return pl.pallas_call(
        _cumsum_kernel,
        grid=(b,),
        in_specs=[spec],
        out_specs=[spec, spec, spec],
        out_shape=[jax.ShapeDtypeStruct(x.shape, F32)] * 3,
        compiler_params=_params(("parallel",)),
        name="cumsum",
    )(x)


FOX_HPS = 2
FOX_CK_TERMS = 3


def _fox_kernel(qt_ref, k_ref, ckx_ref, vt_ref, o_ref, m_ref, l_ref, acc_ref, *, tq, tk, q_off):
    hps = FOX_HPS
    i = pl.program_id(2)
    q0 = q_off + i * tq
    nk = (q0 + tq + tk - 1) // tk
    n_full = (q0 + 1) // tk
    krow = lax.broadcasted_iota(jnp.int32, (tk, tq), 0)
    qcol = q0 + lax.broadcasted_iota(jnp.int32, (tk, tq), 1)
    ck_row = lax.broadcasted_iota(jnp.int32, (LANES, tq), 0)
    zeros = jnp.zeros((HEAD_DIM, tq), BF16)
    outs = []
    for h in range(hps):
        d = ck_row - FOX_CK_TERMS * (pl.program_id(1) * hps + h)
        minus_one = jnp.where(d >= 0, jnp.where(d < FOX_CK_TERMS, -1.0, 0.0), 0.0).astype(BF16)
        q_parts = [zeros] * hps
        q_parts[h] = qt_ref[0, h]
        qt = jnp.concatenate(q_parts + [minus_one], axis=0)
        m_ref[...] = jnp.full((1, tq), NEG_INF, F32)
        l_ref[...] = jnp.zeros((1, tq), F32)
        acc_ref[...] = jnp.zeros((HEAD_DIM, tq), F32)

        def tile(j, masked, h=h, qt=qt):
            ks = pl.multiple_of(j * tk, tk)
            kk = jnp.concatenate([k_ref[0, pl.ds(ks, tk), :], ckx_ref[0, pl.ds(ks, tk), :]], axis=1)
            s = _dot(kk, qt)
            if masked:
                s = jnp.where(krow + ks <= qcol, s, NEG_INF)
            m_old = m_ref[...]
            m_new = jnp.maximum(m_old, jnp.max(s, axis=0, keepdims=True))
            p = jnp.exp(s - m_new)
            alpha = jnp.exp(m_old - m_new)
            l_ref[...] = alpha * l_ref[...] + jnp.sum(p, axis=0, keepdims=True)
            acc_ref[...] = alpha * acc_ref[...] + _dot(vt_ref[0, h, :, pl.ds(ks, tk)], p.astype(BF16))
            m_ref[...] = m_new

        def full_tile(j, carry):
            tile(j, False)
            return carry

        def edge_tile(j, carry):
            tile(j, True)
            return carry

        lax.fori_loop(0, n_full, full_tile, 0)
        lax.fori_loop(n_full, nk, edge_tile, 0)
        outs.append((acc_ref[...] / l_ref[...]).T)
    o_ref[0] = jnp.concatenate(outs, axis=-1).astype(o_ref.dtype)


def _fox(qt, k, ckx, vt, *, tq, tk, q_off):
    b, h, dh, t = qt.shape
    length = k.shape[1]
    hps = FOX_HPS
    assert hps * dh == LANES and FOX_CK_TERMS * h <= LANES and length % tk == 0
    kern = functools.partial(_fox_kernel, tq=tq, tk=tk, q_off=q_off)
    return pl.pallas_call(
        kern,
        grid=(b, h // hps, t // tq),
        in_specs=[
            pl.BlockSpec((1, hps, dh, tq), lambda bi, hi, i: (bi, hi, 0, i)),
            pl.BlockSpec((1, length, LANES), lambda bi, hi, i: (bi, 0, hi)),
            pl.BlockSpec((1, length, LANES), lambda bi, hi, i: (bi, 0, 0)),
            pl.BlockSpec((1, hps, dh, length), lambda bi, hi, i: (bi, hi, 0, 0)),
        ],
        out_specs=pl.BlockSpec((1, tq, hps * dh), lambda bi, hi, i: (bi, i, hi)),
        out_shape=jax.ShapeDtypeStruct((b, t, h * dh), BF16),
        scratch_shapes=[
            pltpu.VMEM((1, tq), F32),
            pltpu.VMEM((1, tq), F32),
            pltpu.VMEM((dh, tq), F32),
        ],
        compiler_params=_params(("parallel", "parallel", "arbitrary")),
        name="fox",
    )(qt, k, ckx, vt)


def _fox_operands(q, k_all, v_all, cum3, length):
    b, t, _ = q.shape
    n_keys = k_all.shape[1]
    ck = jnp.stack(cum3, axis=-1).transpose(0, 2, 1, 3).reshape(b, length, H_A * FOX_CK_TERMS)
    ckx = jnp.concatenate([ck, jnp.zeros((b, length, LANES - H_A * FOX_CK_TERMS), F32)], axis=-1).astype(BF16)
    qt = (q.reshape(b, t, H_A, HEAD_DIM) * HEAD_DIM ** -0.5).transpose(0, 2, 3, 1).astype(BF16)
    vt = _pad_time(v_all, length).reshape(b, length, H_A, HEAD_DIM).transpose(0, 2, 3, 1).astype(BF16)
    return qt, _pad_time(k_all, length).astype(BF16), ckx, vt


INT_MIN = -(2 ** 31)
KEY_NEG_INF = int(np.int32(np.uint32(0xFF800000) ^ np.uint32(0x7FFFFFFF)))
DSA_VROWS = 80


def _order_key(x):
    bits = lax.bitcast_convert_type(x, jnp.int32)
    return bits ^ ((bits >> 31) & jnp.int32(0x7FFFFFFF))


def _dsa_kernel(qit_ref, wit_ref, ki_ref, qt_ref, k_ref, vt_ref, nbt_ref, o_ref,
                sc_ref, m_ref, acc_ref, *, tq, q_off, topk):
    tk = LANES
    i = pl.program_id(1)
    q0 = q_off + i * tq
    i_abs = q0 // tk
    q_tiles = -(-tq // tk)
    nk = i_abs + q_tiles
    cols = G_B * tq

    wit = wit_ref[0]
    wrows = [wit[hh:hh + 1, :] for hh in range(HI_B)]
    q_chunk = (q0 + lax.broadcasted_iota(jnp.int32, (tk, tq), 1)) >> 6
    krow = lax.broadcasted_iota(jnp.int32, (tk, tq), 0)

    def score_tile(j, carry):
        ks = pl.multiple_of(j * tk, tk)
        ki = ki_ref[0, pl.ds(ks, tk), :]
        tot = jnp.zeros((tk, tq), F32)
        for hh in range(HI_B):
            tot = tot + wrows[hh] * jnp.maximum(_dot(ki, qit_ref[0, hh]), 0.0)
        tot = jnp.where(((krow + ks) >> 6) <= q_chunk, tot, NEG_INF)
        sc_ref[j] = _order_key(tot)
        return carry

    lax.fori_loop(0, nk, score_tile, 0)

    kf = float(topk)

    def count(pred_fn):
        def cb(j, accv):
            return accv + jnp.where(pred_fn(sc_ref[j]), 1.0, 0.0)
        accv = lax.fori_loop(0, nk, cb, jnp.zeros((tk, tq), F32))
        return jnp.sum(accv, axis=0, keepdims=True)

    zero_b = jnp.zeros((tk, tq), jnp.int32)
    c0 = count(lambda key: key >= zero_b)
    thr = jnp.where(c0 >= kf, jnp.int32(0), jnp.int32(INT_MIN))
    for bit in range(30, -1, -1):
        cand = thr + jnp.int32(1 << bit)
        cand_b = jnp.broadcast_to(cand, (tk, tq))
        c = count(lambda key, cand_b=cand_b: key >= cand_b)
        thr = jnp.where(c >= kf, cand, thr)

    thr_b = jnp.broadcast_to(thr, (tk, tq))
    cnt_gt = count(lambda key: key > thr_b)
    short = thr <= jnp.int32(KEY_NEG_INF)
    thr_b = jnp.broadcast_to(jnp.maximum(thr, jnp.int32(KEY_NEG_INF)), (tk, tq))
    need = jnp.where(short, 0.0, kf - cnt_gt)

    tri = jnp.where(lax.broadcasted_iota(jnp.int32, (tk, tk), 1) <= lax.broadcasted_iota(jnp.int32, (tk, tk), 0),
                    1.0, 0.0).astype(BF16)
    m_ref[...] = jnp.full((HKV_B, 1, cols), NEG_INF, F32)
    acc_ref[...] = jnp.zeros((HKV_B, DSA_VROWS, cols), F32)
    q_zero = jnp.zeros((HEAD_DIM, cols), BF16)
    q_ext = []
    for g in range(HKV_B):
        parts = [q_zero] * HKV_B
        parts[g] = qt_ref[0, g]
        q_ext.append(jnp.concatenate(parts, axis=0))

    def attend_tile(j, eqc):
        ks = pl.multiple_of(j * tk, tk)
        key = sc_ref[j]
        gt = key > thr_b
        eq = key == thr_b
        eqf = jnp.where(eq, 1.0, 0.0)
        rank = _dot(tri, eqf.astype(BF16)) + eqc
        pen = jnp.where(gt, 0.0, jnp.where(eq, rank, 1e9))
        madd = jnp.where(pen <= need, 0.0, NEG_INF)
        madd = jnp.concatenate([madd] * G_B, axis=1)
        sel = jnp.clip(j - i_abs + 2, 0, q_tiles + 1)
        k_tile = k_ref[0, pl.ds(ks, tk), :]
        for g in range(HKV_B):
            s = _dot(k_tile, q_ext[g]) + nbt_ref[sel, g] + madd
            m_old = m_ref[g]
            m_new = jnp.maximum(m_old, jnp.max(s, axis=0, keepdims=True))
            m_safe = jnp.where(m_new == NEG_INF, 0.0, m_new)
            p = jnp.exp(s - m_safe)
            acc_ref[g] = jnp.exp(m_old - m_safe) * acc_ref[g] + _dot(vt_ref[0, g, :, pl.ds(ks, tk)], p.astype(BF16))
            m_ref[g] = m_new
        return eqc + jnp.sum(eqf, axis=0, keepdims=True)

    lax.fori_loop(0, nk, attend_tile, jnp.zeros((1, tq), F32))
    for g in range(HKV_B):
        acc = acc_ref[g]
        o_ref[0, g] = (acc[:HEAD_DIM] / acc[HEAD_DIM:HEAD_DIM + 1]).astype(o_ref.dtype)


def _dsa(qit, wit, ki, qt, k, vt, nbt, *, tq, q_off, topk):
    b, _, _, t = qit.shape
    length = ki.shape[1]
    cols = G_B * tq
    n_bias = nbt.shape[0]
    assert q_off % LANES == 0 and length % LANES == 0 and length >= q_off + t and HKV_B * HEAD_DIM == LANES
    assert tq % CHUNK == 0 and (tq % LANES == 0 or t == tq) and n_bias == 2 + -(-tq // LANES)
    kern = functools.partial(_dsa_kernel, tq=tq, q_off=q_off, topk=topk)
    return pl.pallas_call(
        kern,
        grid=(b, t // tq),
        in_specs=[
            pl.BlockSpec((1, HI_B, DI_B, tq), lambda bi, i: (bi, 0, 0, i)),
            pl.BlockSpec((1, HI_B, tq), lambda bi, i: (bi, 0, i)),
            pl.BlockSpec((1, length, DI_B), lambda bi, i: (bi, 0, 0)),
            pl.BlockSpec((1, HKV_B, HEAD_DIM, cols), lambda bi, i: (bi, 0, 0, i)),
            pl.BlockSpec((1, length, LANES), lambda bi, i: (bi, 0, 0)),
            pl.BlockSpec((1, HKV_B, DSA_VROWS, length), lambda bi, i: (bi, 0, 0, 0)),
            pl.BlockSpec((n_bias, HKV_B, LANES, cols), lambda bi, i: (0, 0, 0, 0)),
        ],
        out_specs=pl.BlockSpec((1, HKV_B, HEAD_DIM, cols), lambda bi, i: (bi, 0, 0, i)),
        out_shape=jax.ShapeDtypeStruct(qt.shape, BF16),
        scratch_shapes=[
            pltpu.VMEM((length // LANES, LANES, tq), jnp.int32),
            pltpu.VMEM((HKV_B, 1, cols), F32),
            pltpu.VMEM((HKV_B, DSA_VROWS, cols), F32),
        ],
        compiler_params=_params(("parallel", "arbitrary")),
        name="dsa",
    )(qit, wit, ki, qt, k, vt, nbt)


HG_SUB = SUBLANES


def _hgrn_kernel(f_ref, i_ref, q_ref, g_ref, la_ref, l1_ref, oml_ref, ng_ref, s0_ref,
                 o_ref, sfin_ref, st_ref):
    c = pl.program_id(1)
    n_c = pl.num_programs(1)
    ct = CHUNK

    @pl.when(c == 0)
    def _():
        for hh in range(H_C):
            st_ref[hh] = s0_ref[0, hh].T

    zf = f_ref[0]
    a = la_ref[...]
    cc = l1_ref[...] + _log_sigmoid(zf)
    logf = jnp.maximum(a, cc) + jnp.log1p(jnp.exp(-jnp.abs(a - cc)))
    kk = oml_ref[...] * _sigmoid(-zf)

    r_i = lax.broadcasted_iota(jnp.int32, (ct, ct), 0)
    c_i = lax.broadcasted_iota(jnp.int32, (ct, ct), 1)
    low = jnp.where(c_i <= r_i, 1.0, 0.0).astype(BF16)
    hi, mid, lo = _split3(logf)
    bcum = _dot(low, hi) + _dot(low, mid) + _dot(low, lo)

    sub_row = lax.broadcasted_iota(jnp.int32, (HG_SUB, DK_C), 0)
    a_col = lax.broadcasted_iota(jnp.int32, (HG_SUB, ct), 1)
    n_sub = ct // HG_SUB

    for hh in range(H_C):
        cs = slice(hh * DK_C, (hh + 1) * DK_C)
        qh = q_ref[0, :, cs]
        kh = kk[:, cs]
        vh = i_ref[0, :, cs]
        bh = bcum[:, cs]
        a_rows = []
        for bi in range(n_sub):
            r0 = bi * HG_SUB
            q_blk = qh[r0:r0 + HG_SUB]
            b_blk = bh[r0:r0 + HG_SUB]
            k_blk = kh[r0:r0 + HG_SUB]
            if bi > 0:
                ref = bh[r0:r0 + 1]
                qp = q_blk * jnp.exp(b_blk - ref)
                kp = kh[:r0] * jnp.exp(ref - bh[:r0])
                kp = jnp.concatenate([kp, jnp.zeros((ct - r0, DK_C), F32)], axis=0)
                a_row = _dot_nt(qp.astype(BF16), kp.astype(BF16))
            else:
                a_row = jnp.zeros((HG_SUB, ct), F32)
            for s in range(HG_SUB):
                d = jnp.where(sub_row >= s, b_blk - b_blk[s:s + 1], NEG_INF)
                x = q_blk * jnp.exp(d) * k_blk[s:s + 1]
                col = jnp.sum(x, axis=-1, keepdims=True)
                a_row = jnp.where(a_col == r0 + s, col, a_row)
            a_rows.append(a_row)
        a_mat = jnp.concatenate(a_rows, axis=0)
        st = st_ref[hh]
        o = _dot(a_mat.astype(BF16), vh.astype(BF16)) + _dot_nt((qh * jnp.exp(bh)).astype(BF16), st.astype(BF16))
        bl = bh[ct - 1:ct]
        kdec = kh * jnp.exp(bl - bh)
        st_ref[hh] = st * jnp.exp(bl) + _dot_tn(vh.astype(BF16), kdec.astype(BF16))
        o = o * lax.rsqrt(jnp.mean(o * o, axis=-1, keepdims=True) + LN_EPS) * ng_ref[:, cs]
        gh = g_ref[0, :, cs]
        o_ref[0, :, cs] = (o * (gh * _sigmoid(gh))).astype(o_ref.dtype)

    @pl.when(c == n_c - 1)
    def _():
        for hh in range(H_C):
            sfin_ref[0, hh] = st_ref[hh].T


def _hgrn(z3, la, l1, oml, ng, s0):
    b, t, _ = z3.shape
    n_c = t // CHUNK

    def zspec(col):
        blk = col // W_C
        return pl.BlockSpec((1, CHUNK, W_C), lambda bi, ci: (bi, ci, blk))

    vec = pl.BlockSpec((1, W_C), lambda bi, ci: (0, 0))
    return pl.pallas_call(
        _hgrn_kernel,
        grid=(b, n_c),
        in_specs=[zspec(C_FC), zspec(C_IC), zspec(C_QC), zspec(C_GC), vec, vec, vec, vec,
                  pl.BlockSpec((1, H_C, DK_C, DK_C), lambda bi, ci: (bi, 0, 0, 0))],
        out_specs=[pl.BlockSpec((1, CHUNK, W_C), lambda bi, ci: (bi, ci, 0)),
                   pl.BlockSpec((1, H_C, DK_C, DK_C), lambda bi, ci: (bi, 0, 0, 0))],
        out_shape=[jax.ShapeDtypeStruct((b, t, W_C), BF16),
                   jax.ShapeDtypeStruct((b, H_C, DK_C, DK_C), F32)],
        scratch_shapes=[pltpu.VMEM((H_C, DK_C, DK_C), F32)],
        compiler_params=_params(("parallel", "arbitrary")),
        name="hgrn",
    )(z3, z3, z3, z3, la, l1, oml, ng, s0)


def _merge_kernel(x_ref, xb_ref, oa_ref, ob_ref, oc_ref, wg_ref, bg_ref, wa_ref, wb_ref, wc_ref, wo_ref,
                  g_ref, b_ref, y_ref, yb_ref):
    xb = xb_ref[...]
    d = D_MODEL
    mix = None
    for br, (o_ref, w_ref) in enumerate(((oa_ref, wa_ref), (ob_ref, wb_ref), (oc_ref, wc_ref))):
        gate = _sigmoid(_dot(xb, wg_ref[:, br * d:(br + 1) * d]) + bg_ref[:, br * d:(br + 1) * d])
        term = gate * _dot(o_ref[...], w_ref[...])
        mix = term if mix is None else mix + term
    out = _dot(mix.astype(BF16), wo_ref[...])
    y = _layer_norm_rows(DEEPNORM_ALPHA * x_ref[...] + out, g_ref[...], b_ref[...])
    y_ref[...] = y
    yb_ref[...] = y.astype(BF16)


def _merge(x, xb, oa, ob, oc, wg, bg, wa, wb, wc, wo, g, b, tm):
    n, d = x.shape

    def row(width):
        return pl.BlockSpec((tm, width), lambda i: (i, 0))

    def full(arr):
        return pl.BlockSpec(arr.shape, lambda i: (0, 0))

    return pl.pallas_call(
        _merge_kernel,
        grid=(n // tm,),
        in_specs=[row(d), row(d), row(oa.shape[1]), row(ob.shape[1]), row(oc.shape[1]),
                  full(wg), full(bg), full(wa), full(wb), full(wc), full(wo), full(g), full(b)],
        out_specs=[row(d), row(d)],
        out_shape=[jax.ShapeDtypeStruct((n, d), F32), jax.ShapeDtypeStruct((n, d), BF16)],
        compiler_params=_params(("parallel",)),
        name="merge",
    )(x, xb, oa, ob, oc, wg, bg, wa, wb, wc, wo, g, b)


def _router_kernel(x_ref, w_ref, o_ref):
    x = x_ref[...]
    xh = x.astype(BF16)
    xl = (x - xh.astype(F32)).astype(BF16)
    w = w_ref[...]
    wh = w.astype(BF16)
    wl = (w - wh.astype(F32)).astype(BF16)
    logits = _dot(xh, wh) + _dot(xh, wl) + _dot(xl, wh)
    lane = lax.broadcasted_iota(jnp.int32, logits.shape, 1)
    logits = jnp.where(lane < N_EXPERTS, logits, NEG_INF)
    big = jnp.int32(LANES)
    m1 = jnp.max(logits, axis=-1, keepdims=True)
    i1 = jnp.min(jnp.where(logits == m1, lane, big), axis=-1, keepdims=True)
    rest = jnp.where(lane == i1, NEG_INF, logits)
    m2 = jnp.max(rest, axis=-1, keepdims=True)
    i2 = jnp.min(jnp.where(rest == m2, lane, big), axis=-1, keepdims=True)
    e2 = jnp.exp(m2 - m1)
    g1 = 1.0 / (1.0 + e2)
    g2 = e2 / (1.0 + e2)
    o_ref[...] = jnp.where(lane == i1, g1, 0.0) + jnp.where(lane == i2, g2, 0.0)


def _router(x, w, tm):
    n, d = x.shape
    return pl.pallas_call(
        _router_kernel,
        grid=(n // tm,),
        in_specs=[pl.BlockSpec((tm, d), lambda i: (i, 0)),
                  pl.BlockSpec(w.shape, lambda i: (0, 0))],
        out_specs=pl.BlockSpec((tm, LANES), lambda i: (i, 0)),
        out_shape=jax.ShapeDtypeStruct((n, LANES), F32),
        compiler_params=_params(("parallel",)),
        name="router",
    )(x, w)


def _ffn_kernel(x_ref, xb_ref, comb_ref, w1_ref, w3_ref, w2_ref, g_ref, b_ref, y_ref, yb_ref, acc_ref, *, use_comb):
    e = pl.program_id(1)
    f = pl.program_id(2)

    @pl.when((e == 0) & (f == 0))
    def _():
        acc_ref[...] = jnp.zeros_like(acc_ref)

    xb = xb_ref[...]
    h1 = _dot(xb, w1_ref[0])
    h = (h1 * _sigmoid(h1)) * _dot(xb, w3_ref[0])
    part = _dot(h.astype(BF16), w2_ref[0])
    if use_comb:
        comb = comb_ref[...]
        lane = lax.broadcasted_iota(jnp.int32, comb.shape, 1)
        part = part * jnp.sum(jnp.where(lane == e, comb, 0.0), axis=-1, keepdims=True)
    acc_ref[...] += part

    @pl.when((e == pl.num_programs(1) - 1) & (f == pl.num_programs(2) - 1))
    def _():
        y = _layer_norm_rows(DEEPNORM_ALPHA * x_ref[...] + acc_ref[...], g_ref[...], b_ref[...])
        y_ref[...] = y
        yb_ref[...] = y.astype(BF16)


def _ffn(x, xb, comb, w1, w3, w2, g, b, *, tm, tf, use_comb):
    n, d = x.shape
    n_e, _, ff = w1.shape
    kern = functools.partial(_ffn_kernel, use_comb=use_comb)
    return pl.pallas_call(
        kern,
        grid=(n // tm, n_e, ff // tf),
        in_specs=[
            pl.BlockSpec((tm, d), lambda i, e, f: (i, 0)),
            pl.BlockSpec((tm, d), lambda i, e, f: (i, 0)),
            pl.BlockSpec((tm, LANES), lambda i, e, f: (i, 0)),
            pl.BlockSpec((1, d, tf), lambda i, e, f: (e, 0, f)),
            pl.BlockSpec((1, d, tf), lambda i, e, f: (e, 0, f)),
            pl.BlockSpec((1, tf, d), lambda i, e, f: (e, f, 0)),
            pl.BlockSpec((1, d), lambda i, e, f: (0, 0)),
            pl.BlockSpec((1, d), lambda i, e, f: (0, 0)),
        ],
        out_specs=[pl.BlockSpec((tm, d), lambda i, e, f: (i, 0)),
                   pl.BlockSpec((tm, d), lambda i, e, f: (i, 0))],
        out_shape=[jax.ShapeDtypeStruct((n, d), F32), jax.ShapeDtypeStruct((n, d), BF16)],
        scratch_shapes=[pltpu.VMEM((tm, d), F32)],
        compiler_params=_params(("parallel", "arbitrary", "arbitrary")),
        name="ffn",
    )(x, xb, comb, w1, w3, w2, g, b)


def _rel_bucket(rel):
    nb = N_BUCKETS // 2
    max_exact = nb // 2
    ret = jnp.where(rel > 0, nb, 0)
    n = jnp.abs(rel)
    nf = jnp.maximum(n, 1).astype(jnp.float32)
    large = max_exact + (jnp.log(nf / max_exact) / math.log(MAX_DISTANCE / max_exact) * (nb - max_exact)).astype(jnp.int32)
    large = jnp.minimum(large, nb - 1)
    return ret + jnp.where(n < max_exact, n, large)


def _bias_tiles(rel_bias, tq):
    r = jnp.arange(tq, dtype=jnp.int32)[:, None]
    c = jnp.arange(LANES, dtype=jnp.int32)[None, :]
    n_near = 1 + -(-tq // LANES)
    near = jnp.stack([c - r + (s - 1) * LANES for s in range(n_near)], axis=0)
    far = jnp.full((1, tq, LANES), -(2 * MAX_DISTANCE + tq + 2 * LANES), jnp.int32)
    rel = jnp.concatenate([far, near], axis=0)
    n = n_near + 1
    vals = jnp.take(rel_bias.astype(F32), _rel_bucket(rel), axis=0)
    vals = vals.transpose(0, 3, 2, 1).reshape(n, HKV_B, G_B, LANES, tq)
    return vals.transpose(0, 1, 3, 2, 4).reshape(n, HKV_B, LANES, G_B * tq)


def _prep_w_in(w_in_l, b_in_l):
    def cols(a):
        parts = [a[..., 0:1536], a[..., 1544:2568], a[..., 2636:4684], a[..., 2568:2632], a[..., 1536:1544], a[..., 2632:2636]]
        pad = jnp.zeros(a.shape[:-1] + (N_IN_PAD - N_IN,), a.dtype)
        return jnp.concatenate(parts + [pad], axis=-1)

    return cols(w_in_l).astype(BF16), cols(b_in_l[None, :]).astype(F32)


def _pad_time(a, length):
    pad = length - a.shape[1]
    if pad == 0:
        return a
    return jnp.concatenate([a, jnp.zeros((a.shape[0], pad) + a.shape[2:], a.dtype)], axis=1)


def _heads_first(a, n_heads, scale=None):
    b, t, _ = a.shape
    a = a.reshape(b, t, n_heads, -1)
    if scale is not None:
        a = a * scale
    return a.transpose(0, 2, 1, 3).astype(BF16)


def _round_up(x, m):
    return (x + m - 1) // m * m


def _run_group(x, caches, wts):
    b, t, d = x.shape
    n = b * t
    p_len = 0 if caches is None else caches[0].shape[2]
    tm = min(512, n)
    xf = x.reshape(n, d).astype(F32)
    xb = xf.astype(BF16)

    fox_tq = min(512, t)
    fox_tk = 512
    l_fox = _round_up(p_len + t, fox_tk)
    dsa_tq = min(2 * LANES, t)
    l_dsa = _round_up(p_len + t, LANES)
    topk = min(TOPK_MAX, (p_len + t) // 4)
    nb = _bias_tiles(wts["rel_bias"], dsa_tq)

    lb_cum = jnp.cumsum(jax.nn.softmax(wts["hgrn_lb"].astype(F32), axis=0), axis=0)

    states = []
    for l in range(DEPTH):
        w_in_p, b_in_p = _prep_w_in(wts["w_in"][l], wts["b_in"][l])
        z = _proj(xb, w_in_p, b_in_p, tm)
        z3 = z.reshape(b, t, N_IN_PAD)

        k_a = z3[..., C_KA:C_KA + 512]
        v_a = z3[..., C_VA:C_VA + 512]
        logf_a = z3[..., C_FA:C_FA + H_A]
        k_b = z3[..., C_KB:C_KB + 128]
        v_b = z3[..., C_VB:C_VB + 128]
        ki_b = z3[..., C_KI:C_KI + DI_B]
        if caches is None:
            ka_all, va_all, lf_all, kb_all, vb_all, ki_all = k_a, v_a, logf_a, k_b, v_b, ki_b
            s0 = jnp.zeros((b, H_C, DK_C, DK_C), F32)
        else:
            c_ka, c_va, c_lf, c_kb, c_vb, c_ki, c_s = [c[l] for c in caches]

            def cat(c, new):
                return jnp.concatenate([c.reshape(b, p_len, -1).astype(F32), new], axis=1)

            ka_all, va_all, lf_all = cat(c_ka, k_a), cat(c_va, v_a), cat(c_lf, logf_a)
            kb_all, vb_all, ki_all = cat(c_kb, k_b), cat(c_vb, v_b), cat(c_ki, ki_b)
            s0 = c_s.astype(F32)

        cum3 = _cumsum(_pad_time(lf_all, l_fox).transpose(0, 2, 1))
        qt_a, kt_a, ckx_a, vt_a = _fox_operands(z3[..., C_QA:C_QA + 512], ka_all, va_all, cum3, l_fox)
        o_a = _fox(qt_a, kt_a, ckx_a, vt_a, tq=fox_tq, tk=fox_tk, q_off=p_len)

        n_t = t // dsa_tq
        q_b = (z3[..., C_QB:C_QB + 512] * HEAD_DIM ** -0.5).reshape(b, n_t, dsa_tq, HKV_B, G_B, HEAD_DIM)
        qt_b = q_b.transpose(0, 3, 5, 1, 4, 2).reshape(b, HKV_B, HEAD_DIM, n_t * G_B * dsa_tq).astype(BF16)
        qit_b = (z3[..., C_QI:C_QI + HI_B * DI_B] * DI_B ** -0.5).reshape(b, t, HI_B, DI_B).transpose(0, 2, 3, 1).astype(BF16)
        wit_b = (z3[..., C_WI:C_WI + HI_B] * HI_B ** -0.5).transpose(0, 2, 1)
        vt_b = _pad_time(vb_all, l_dsa).reshape(b, l_dsa, HKV_B, HEAD_DIM).transpose(0, 2, 3, 1).astype(BF16)
        vt_b = jnp.concatenate([vt_b, jnp.ones((b, HKV_B, DSA_VROWS - HEAD_DIM, l_dsa), BF16)], axis=2)
        o_b = _dsa(qit_b, wit_b, _pad_time(ki_all, l_dsa).astype(BF16),
                   qt_b, _pad_time(kb_all, l_dsa).astype(BF16), vt_b,
                   nb, tq=dsa_tq, q_off=p_len, topk=topk)
        o_b = o_b.reshape(b, HKV_B, HEAD_DIM, n_t, G_B, dsa_tq).transpose(0, 3, 5, 1, 4, 2).reshape(n, H_B * HEAD_DIM)

        lb = (lb_cum[l] - lb_cum[0]).reshape(1, W_C)
        o_c, s_new = _hgrn(z3, jnp.log(lb), jnp.log1p(-lb), 1.0 - lb,
                           wts["hgrn_norm_g"][l].reshape(1, W_C).astype(F32), s0)

        x1, x1b = _merge(xf, xb, o_a.reshape(n, -1), o_b, o_c.reshape(n, -1),
                         wts["w_gate"][l].astype(BF16), wts["b_gate"][l][None, :].astype(F32),
                         wts["w_o_fox"][l].astype(BF16), wts["w_o_dsa"][l].astype(BF16), wts["w_o_hgrn"][l].astype(BF16),
                         wts["w_out"][l].astype(BF16), wts["ln1_g"][l][None, :], wts["ln1_b"][l][None, :], min(256, n))

        g2, b2 = wts["ln2_g"][l][None, :], wts["ln2_b"][l][None, :]
        if l % 2 == 0:
            j = l // 2
            comb = jnp.zeros((n, LANES), F32)
            xf, xb = _ffn(x1, x1b, comb, wts["ffn_w1"][j][None].astype(BF16), wts["ffn_w3"][j][None].astype(BF16),
                          wts["ffn_w2"][j][None].astype(BF16), g2, b2, tm=tm, tf=1408, use_comb=False)
        else:
            j = l // 2
            rw = jnp.zeros((d, LANES), F32).at[:, :N_EXPERTS].set(wts["moe_router"][j].astype(F32))
            comb = _router(x1, rw, tm)
            xf, xb = _ffn(x1, x1b, comb, wts["moe_w1"][j].astype(BF16), wts["moe_w3"][j].astype(BF16),
                          wts["moe_w2"][j].astype(BF16), g2, b2, tm=tm, tf=896, use_comb=True)

        states.append((k_a.reshape(b, t, H_A, HEAD_DIM), v_a.reshape(b, t, H_A, HEAD_DIM), logf_a,
                       k_b.reshape(b, t, HKV_B, HEAD_DIM), v_b.reshape(b, t, HKV_B, HEAD_DIM), ki_b, s_new))
    stacked = [jnp.stack([s[i] for s in states], axis=0) for i in range(7)]
    return xf.reshape(b, t, d), stacked


def kernel(x_prompt, x_sample, cache_fox_k, cache_fox_v, cache_fox_logf, cache_dsa_k, cache_dsa_v, cache_dsa_kidx, state_hgrn,
           w_in, b_in, w_gate, b_gate, w_o_fox, w_o_dsa, w_o_hgrn, w_out, hgrn_lb, hgrn_norm_g, rel_bias,
           ln1_g, ln1_b, ln2_g, ln2_b, ffn_w1, ffn_w3, ffn_w2, moe_router, moe_w1, moe_w3, moe_w2):
    wts = dict(w_in=w_in, b_in=b_in, w_gate=w_gate, b_gate=b_gate, w_o_fox=w_o_fox, w_o_dsa=w_o_dsa, w_o_hgrn=w_o_hgrn,
               w_out=w_out, hgrn_lb=hgrn_lb, hgrn_norm_g=hgrn_norm_g, rel_bias=rel_bias, ln1_g=ln1_g, ln1_b=ln1_b,
               ln2_g=ln2_g, ln2_b=ln2_b, ffn_w1=ffn_w1, ffn_w3=ffn_w3, ffn_w2=ffn_w2, moe_router=moe_router,
               moe_w1=moe_w1, moe_w3=moe_w3, moe_w2=moe_w2)
    caches = (cache_fox_k, cache_fox_v, cache_fox_logf, cache_dsa_k, cache_dsa_v, cache_dsa_kidx, state_hgrn)
    y_prompt, sp = _run_group(x_prompt, None, wts)
    y_sample, ss = _run_group(x_sample, caches, wts)
    return (y_prompt, y_sample, sp[0], sp[1], sp[2], sp[3], sp[4], sp[5], sp[6],
            ss[0], ss[1], ss[2], ss[3], ss[4], ss[5], ss[6])
```

```python
import functools
import math

import jax
import jax.numpy as jnp
import numpy as np
from jax import lax
from jax.experimental import pallas as pl
from jax.experimental.pallas import tpu as pltpu

F32 = jnp.float32
BF16 = jnp.bfloat16
NEG_INF = float("-inf")

D_MODEL = 1024
DEPTH = 2
CHUNK = 64
HEAD_DIM = 64
H_A = 8
H_B = 8
HKV_B = 2
G_B = H_B // HKV_B
HI_B = 4
DI_B = 64
TOPK_MAX = 256
H_C = 4
DK_C = 128
W_C = H_C * DK_C
N_BUCKETS = 32
MAX_DISTANCE = 128
N_EXPERTS = 8
DEEPNORM_ALPHA = (2.0 * DEPTH) ** 0.25
LN_EPS = 1e-5

LANES = 128
SUBLANES = 8
VMEM_LIMIT = 56 * 1024 * 1024

C_QA, C_KA, C_VA = 0, 512, 1024
C_QB, C_KB, C_VB, C_QI = 1536, 2048, 2176, 2304
C_FC, C_IC, C_QC, C_GC = 2560, 3072, 3584, 4096
C_KI, C_FA, C_WI = 4608, 4672, 4680
N_IN = 4684
N_IN_PAD = 4864
PROJ_TN = 256


def _params(sem, vmem=VMEM_LIMIT):
    return pltpu.CompilerParams(dimension_semantics=sem, vmem_limit_bytes=vmem)


def _log_sigmoid(z):
    return jnp.minimum(z, 0.0) - jnp.log1p(jnp.exp(-jnp.abs(z)))


def _sigmoid(z):
    return 1.0 / (1.0 + jnp.exp(-z))


def _dot_nt(a, b):
    return lax.dot_general(a, b, (((1,), (1,)), ((), ())), preferred_element_type=F32)


def _dot_tn(a, b):
    return lax.dot_general(a, b, (((0,), (0,)), ((), ())), preferred_element_type=F32)


def _dot(a, b):
    return jnp.dot(a, b, preferred_element_type=F32)


def _split3(x):
    hi = x.astype(BF16)
    r1 = x - hi.astype(F32)
    mid = r1.astype(BF16)
    lo = (r1 - mid.astype(F32)).astype(BF16)
    return hi, mid, lo


def _layer_norm_rows(r, g, b):
    mu = jnp.mean(r, axis=-1, keepdims=True)
    rc = r - mu
    var = jnp.mean(rc * rc, axis=-1, keepdims=True)
    return rc * lax.rsqrt(var + LN_EPS) * g + b


def _proj_kernel(x_ref, w_ref, b_ref, o_ref):
    x = x_ref[...]
    o_ref[:, :C_KI] = _dot(x, w_ref[:, :C_KI]) + b_ref[:, :C_KI]
    z = _dot(x, w_ref[:, C_KI:]) + b_ref[:, C_KI:]
    col = lax.broadcasted_iota(jnp.int32, z.shape, 1) + C_KI
    o_ref[:, C_KI:] = jnp.where(col >= C_FA, jnp.where(col < C_FA + H_A, _log_sigmoid(z), z), z)


def _proj(xb, w, b, tm):
    n, k = xb.shape
    m = w.shape[1]
    return pl.pallas_call(
        _proj_kernel,
        grid=(n // tm,),
        in_specs=[
            pl.BlockSpec((tm, k), lambda i: (i, 0)),
            pl.BlockSpec((k, m), lambda i: (0, 0), pipeline_mode=pl.Buffered(1)),
            pl.BlockSpec((1, m), lambda i: (0, 0), pipeline_mode=pl.Buffered(1)),
        ],
        out_specs=pl.BlockSpec((tm, m), lambda i: (i, 0)),
        out_shape=jax.ShapeDtypeStruct((n, m), F32),
        compiler_params=_params(("parallel",)),
        name="proj",
    )(xb, w, b)


def _cumsum_kernel(x_ref, hi_ref, mid_ref, lo_ref):
    c = x_ref[0]
    length = c.shape[-1]
    lane = lax.broadcasted_iota(jnp.int32, c.shape, 1)
    s = 1
    while s < length:
        c = c + jnp.where(lane >= s, pltpu.roll(c, s, 1), 0.0)
        s *= 2
    hi, mid, lo = _split3(c)
    hi_ref[0] = hi.astype(F32)
    mid_ref[0] = mid.astype(F32)
    lo_ref[0] = lo.astype(F32)


def _cumsum(x):
    b, r, length = x.shape
    spec = pl.BlockSpec((1, r, length), lambda i: (i, 0, 0))
    return pl.pallas_call(
        _cumsum_kernel,
        grid=(b,),
        in_specs=[spec],
        out_specs=[spec, spec, spec],
        out_shape=[jax.ShapeDtypeStruct(x.shape, F32)] * 3,
        compiler_params=_params(("parallel",)),
        name="cumsum",
    )(x)


FOX_HPS = 2
FOX_CK_TERMS = 3


def _fox_kernel(qt_ref, k_ref, ckx_ref, vt_ref, o_ref, m_ref, l_ref, acc_ref, *, tq, tk, q_off):
    hps = FOX_HPS
    i = pl.program_id(2)
    q0 = q_off + i * tq
    nk = (q0 + tq + tk - 1) // tk
    n_full = (q0 + 1) // tk
    krow = lax.broadcasted_iota(jnp.int32, (tk, tq), 0)
    qcol = q0 + lax.broadcasted_iota(jnp.int32, (tk, tq), 1)
    ck_row = lax.broadcasted_iota(jnp.int32, (LANES, tq), 0)
    zeros = jnp.zeros((HEAD_DIM, tq), BF16)
    outs = []
    for h in range(hps):
        d = ck_row - FOX_CK_TERMS * (pl.program_id(1) * hps + h)
        minus_one = jnp.where(d >= 0, jnp.where(d < FOX_CK_TERMS, -1.0, 0.0), 0.0).astype(BF16)
        q_parts = [zeros] * hps
        q_parts[h] = qt_ref[0, h]
        qt = jnp.concatenate(q_parts + [minus_one], axis=0)
        m_ref[...] = jnp.full((1, tq), NEG_INF, F32)
        l_ref[...] = jnp.zeros((1, tq), F32)
        acc_ref[...] = jnp.zeros((HEAD_DIM, tq), F32)

        def tile(j, masked, h=h, qt=qt):
            ks = pl.multiple_of(j * tk, tk)
            kk = jnp.concatenate([k_ref[0, pl.ds(ks, tk), :], ckx_ref[0, pl.ds(ks, tk), :]], axis=1)
            s = _dot(kk, qt)
            if masked:
                s = jnp.where(krow + ks <= qcol, s, NEG_INF)
            m_old = m_ref[...]
            m_new = jnp.maximum(m_old, jnp.max(s, axis=0, keepdims=True))
            p = jnp.exp(s - m_new)
            alpha = jnp.exp(m_old - m_new)
            l_ref[...] = alpha * l_ref[...] + jnp.sum(p, axis=0, keepdims=True)
            acc_ref[...] = alpha * acc_ref[...] + _dot(vt_ref[0, h, :, pl.ds(ks, tk)], p.astype(BF16))
            m_ref[...] = m_new

        def full_tile(j, carry):
            tile(j, False)
            return carry

        def edge_tile(j, carry):
            tile(j, True)
            return carry

        lax.fori_loop(0, n_full, full_tile, 0)
        lax.fori_loop(n_full, nk, edge_tile, 0)
        outs.append((acc_ref[...] / l_ref[...]).T)
    o_ref[0] = jnp.concatenate(outs, axis=-1).astype(o_ref.dtype)


def _fox(qt, k, ckx, vt, *, tq, tk, q_off):
    b, h, dh, t = qt.shape
    length = k.shape[1]
    hps = FOX_HPS
    assert hps * dh == LANES and FOX_CK_TERMS * h <= LANES and length % tk == 0
    kern = functools.partial(_fox_kernel, tq=tq, tk=tk, q_off=q_off)
    return pl.pallas_call(
        kern,
        grid=(b, h // hps, t // tq),
        in_specs=[
            pl.BlockSpec((1, hps, dh, tq), lambda bi, hi, i: (bi, hi, 0, i)),
            pl.BlockSpec((1, length, LANES), lambda bi, hi, i: (bi, 0, hi)),
            pl.BlockSpec((1, length, LANES), lambda bi, hi, i: (bi, 0, 0)),
            pl.BlockSpec((1, hps, dh, length), lambda bi, hi, i: (bi, hi, 0, 0)),
        ],
        out_specs=pl.BlockSpec((1, tq, hps * dh), lambda bi, hi, i: (bi, i, hi)),
        out_shape=jax.ShapeDtypeStruct((b, t, h * dh), BF16),
        scratch_shapes=[
            pltpu.VMEM((1, tq), F32),
            pltpu.VMEM((1, tq), F32),
            pltpu.VMEM((dh, tq), F32),
        ],
        compiler_params=_params(("parallel", "parallel", "arbitrary")),
        name="fox",
    )(qt, k, ckx, vt)


def _fox_operands(q, k_all, v_all, cum3, length):
    b, t, _ = q.shape
    n_keys = k_all.shape[1]
    ck = jnp.stack(cum3, axis=-1).transpose(0, 2, 1, 3).reshape(b, length, H_A * FOX_CK_TERMS)
    ckx = jnp.concatenate([ck, jnp.zeros((b, length, LANES - H_A * FOX_CK_TERMS), F32)], axis=-1).astype(BF16)
    qt = (q.reshape(b, t, H_A, HEAD_DIM) * HEAD_DIM ** -0.5).transpose(0, 2, 3, 1).astype(BF16)
    vt = _pad_time(v_all, length).reshape(b, length, H_A, HEAD_DIM).transpose(0, 2, 3, 1).astype(BF16)
    return qt, _pad_time(k_all, length).astype(BF16), ckx, vt


INT_MIN = -(2 ** 31)
KEY_NEG_INF = int(np.int32(np.uint32(0xFF800000) ^ np.uint32(0x7FFFFFFF)))
DSA_VROWS = 80


def _order_key(x):
    bits = lax.bitcast_convert_type(x, jnp.int32)
    return bits ^ ((bits >> 31) & jnp.int32(0x7FFFFFFF))


def _dsa_kernel(qit_ref, wit_ref, ki_ref, qt_ref, k_ref, vt_ref, nbt_ref, o_ref,
                sc_ref, m_ref, acc_ref, *, tq, q_off, topk):
    tk = LANES
    i = pl.program_id(1)
    q0 = q_off + i * tq
    i_abs = q0 // tk
    q_tiles = -(-tq // tk)
    nk = i_abs + q_tiles
    cols = G_B * tq

    wit = wit_ref[0]
    wrows = [wit[hh:hh + 1, :] for hh in range(HI_B)]
    q_chunk = (q0 + lax.broadcasted_iota(jnp.int32, (tk, tq), 1)) >> 6
    krow = lax.broadcasted_iota(jnp.int32, (tk, tq), 0)

    def score_tile(j, carry):
        ks = pl.multiple_of(j * tk, tk)
        ki = ki_ref[0, pl.ds(ks, tk), :]
        tot = jnp.zeros((tk, tq), F32)
        for hh in range(HI_B):
            tot = tot + wrows[hh] * jnp.maximum(_dot(ki, qit_ref[0, hh]), 0.0)
        tot = jnp.where(((krow + ks) >> 6) <= q_chunk, tot, NEG_INF)
        sc_ref[j] = _order_key(tot)
        return carry

    lax.fori_loop(0, nk, score_tile, 0)

    kf = float(topk)

    def count(pred_fn):
        def cb(j, accv):
            return accv + jnp.where(pred_fn(sc_ref[j]), 1.0, 0.0)
        accv = lax.fori_loop(0, nk, cb, jnp.zeros((tk, tq), F32))
        return jnp.sum(accv, axis=0, keepdims=True)

    zero_b = jnp.zeros((tk, tq), jnp.int32)
    c0 = count(lambda key: key >= zero_b)
    thr = jnp.where(c0 >= kf, jnp.int32(0), jnp.int32(INT_MIN))
    for bit in range(30, -1, -1):
        cand = thr + jnp.int32(1 << bit)
        cand_b = jnp.broadcast_to(cand, (tk, tq))
        c = count(lambda key, cand_b=cand_b: key >= cand_b)
        thr = jnp.where(c >= kf, cand, thr)

    thr_b = jnp.broadcast_to(thr, (tk, tq))
    cnt_gt = count(lambda key: key > thr_b)
    short = thr <= jnp.int32(KEY_NEG_INF)
    thr_b = jnp.broadcast_to(jnp.maximum(thr, jnp.int32(KEY_NEG_INF)), (tk, tq))
    need = jnp.where(short, 0.0, kf - cnt_gt)

    tri = jnp.where(lax.broadcasted_iota(jnp.int32, (tk, tk), 1) <= lax.broadcasted_iota(jnp.int32, (tk, tk), 0),
                    1.0, 0.0).astype(BF16)
    m_ref[...] = jnp.full((HKV_B, 1, cols), NEG_INF, F32)
    acc_ref[...] = jnp.zeros((HKV_B, DSA_VROWS, cols), F32)
    q_zero = jnp.zeros((HEAD_DIM, cols), BF16)
    q_ext = []
    for g in range(HKV_B):
        parts = [q_zero] * HKV_B
        parts[g] = qt_ref[0, g]
        q_ext.append(jnp.concatenate(parts, axis=0))

    def attend_tile(j, eqc):
        ks = pl.multiple_of(j * tk, tk)
        key = sc_ref[j]
        gt = key > thr_b
        eq = key == thr_b
        eqf = jnp.where(eq, 1.0, 0.0)
        rank = _dot(tri, eqf.astype(BF16)) + eqc
        pen = jnp.where(gt, 0.0, jnp.where(eq, rank, 1e9))
        madd = jnp.where(pen <= need, 0.0, NEG_INF)
        madd = jnp.concatenate([madd] * G_B, axis=1)
        sel = jnp.clip(j - i_abs + 2, 0, q_tiles + 1)
        k_tile = k_ref[0, pl.ds(ks, tk), :]
        for g in range(HKV_B):
            s = _dot(k_tile, q_ext[g]) + nbt_ref[sel, g] + madd
            m_old = m_ref[g]
            m_new = jnp.maximum(m_old, jnp.max(s, axis=0, keepdims=True))
            m_safe = jnp.where(m_new == NEG_INF, 0.0, m_new)
            p = jnp.exp(s - m_safe)
            acc_ref[g] = jnp.exp(m_old - m_safe) * acc_ref[g] + _dot(vt_ref[0, g, :, pl.ds(ks, tk)], p.astype(BF16))
            m_ref[g] = m_new
        return eqc + jnp.sum(eqf, axis=0, keepdims=True)

    lax.fori_loop(0, nk, attend_tile, jnp.zeros((1, tq), F32))
    for g in range(HKV_B):
        acc = acc_ref[g]
        o_ref[0, g] = (acc[:HEAD_DIM] / acc[HEAD_DIM:HEAD_DIM + 1]).astype(o_ref.dtype)


def _dsa(qit, wit, ki, qt, k, vt, nbt, *, tq, q_off, topk):
    b, _, _, t = qit.shape
    length = ki.shape[1]
    cols = G_B * tq
    n_bias = nbt.shape[0]
    assert q_off % LANES == 0 and length % LANES == 0 and length >= q_off + t and HKV_B * HEAD_DIM == LANES
    assert tq % CHUNK == 0 and (tq % LANES == 0 or t == tq) and n_bias == 2 + -(-tq // LANES)
    kern = functools.partial(_dsa_kernel, tq=tq, q_off=q_off, topk=topk)
    return pl.pallas_call(
        kern,
        grid=(b, t // tq),
        in_specs=[
            pl.BlockSpec((1, HI_B, DI_B, tq), lambda bi, i: (bi, 0, 0, i)),
            pl.BlockSpec((1, HI_B, tq), lambda bi, i: (bi, 0, i)),
            pl.BlockSpec((1, length, DI_B), lambda bi, i: (bi, 0, 0)),
            pl.BlockSpec((1, HKV_B, HEAD_DIM, cols), lambda bi, i: (bi, 0, 0, i)),
            pl.BlockSpec((1, length, LANES), lambda bi, i: (bi, 0, 0)),
            pl.BlockSpec((1, HKV_B, DSA_VROWS, length), lambda bi, i: (bi, 0, 0, 0)),
            pl.BlockSpec((n_bias, HKV_B, LANES, cols), lambda bi, i: (0, 0, 0, 0)),
        ],
        out_specs=pl.BlockSpec((1, HKV_B, HEAD_DIM, cols), lambda bi, i: (bi, 0, 0, i)),
        out_shape=jax.ShapeDtypeStruct(qt.shape, BF16),
        scratch_shapes=[
            pltpu.VMEM((length // LANES, LANES, tq), jnp.int32),
            pltpu.VMEM((HKV_B, 1, cols), F32),
            pltpu.VMEM((HKV_B, DSA_VROWS, cols), F32),
        ],
        compiler_params=_params(("parallel", "arbitrary")),
        name="dsa",
    )(qit, wit, ki, qt, k, vt, nbt)


HG_SUB = SUBLANES


def _hgrn_kernel(f_ref, i_ref, q_ref, g_ref, la_ref, l1_ref, oml_ref, ng_ref, s0_ref,
                 o_ref, sfin_ref, st_ref):
    c = pl.program_id(1)
    n_c = pl.num_programs(1)
    ct = CHUNK

    @pl.when(c == 0)
    def _():
        for hh in range(H_C):
            st_ref[hh] = s0_ref[0, hh].T

    zf = f_ref[0]
    a = la_ref[...]
    cc = l1_ref[...] + _log_sigmoid(zf)
    logf = jnp.maximum(a, cc) + jnp.log1p(jnp.exp(-jnp.abs(a - cc)))
    kk = oml_ref[...] * _sigmoid(-zf)

    r_i = lax.broadcasted_iota(jnp.int32, (ct, ct), 0)
    c_i = lax.broadcasted_iota(jnp.int32, (ct, ct), 1)
    low = jnp.where(c_i <= r_i, 1.0, 0.0).astype(BF16)
    hi, mid, lo = _split3(logf)
    bcum = _dot(low, hi) + _dot(low, mid) + _dot(low, lo)

    sub_row = lax.broadcasted_iota(jnp.int32, (HG_SUB, DK_C), 0)
    a_col = lax.broadcasted_iota(jnp.int32, (HG_SUB, ct), 1)
    n_sub = ct // HG_SUB

    for hh in range(H_C):
        cs = slice(hh * DK_C, (hh + 1) * DK_C)
        qh = q_ref[0, :, cs]
        kh = kk[:, cs]
        vh = i_ref[0, :, cs]
        bh = bcum[:, cs]
        a_rows = []
        for bi in range(n_sub):
            r0 = bi * HG_SUB
            q_blk = qh[r0:r0 + HG_SUB]
            b_blk = bh[r0:r0 + HG_SUB]
            k_blk = kh[r0:r0 + HG_SUB]
            if bi > 0:
                ref = bh[r0:r0 + 1]
                qp = q_blk * jnp.exp(b_blk - ref)
                kp = kh[:r0] * jnp.exp(ref - bh[:r0])
                kp = jnp.concatenate([kp, jnp.zeros((ct - r0, DK_C), F32)], axis=0)
                a_row = _dot_nt(qp.astype(BF16), kp.astype(BF16))
            else:
                a_row = jnp.zeros((HG_SUB, ct), F32)
            for s in range(HG_SUB):
                d = jnp.where(sub_row >= s, b_blk - b_blk[s:s + 1], NEG_INF)
                x = q_blk * jnp.exp(d) * k_blk[s:s + 1]
                col = jnp.sum(x, axis=-1, keepdims=True)
                a_row = jnp.where(a_col == r0 + s, col, a_row)
            a_rows.append(a_row)
        a_mat = jnp.concatenate(a_rows, axis=0)
        st = st_ref[hh]
        o = _dot(a_mat.astype(BF16), vh.astype(BF16)) + _dot_nt((qh * jnp.exp(bh)).astype(BF16), st.astype(BF16))
        bl = bh[ct - 1:ct]
        kdec = kh * jnp.exp(bl - bh)
        st_ref[hh] = st * jnp.exp(bl) + _dot_tn(vh.astype(BF16), kdec.astype(BF16))
        o = o * lax.rsqrt(jnp.mean(o * o, axis=-1, keepdims=True) + LN_EPS) * ng_ref[:, cs]
        gh = g_ref[0, :, cs]
        o_ref[0, :, cs] = (o * (gh * _sigmoid(gh))).astype(o_ref.dtype)

    @pl.when(c == n_c - 1)
    def _():
        for hh in range(H_C):
            sfin_ref[0, hh] = st_ref[hh].T


def _hgrn(z3, la, l1, oml, ng, s0):
    b, t, _ = z3.shape
    n_c = t // CHUNK

    def zspec(col):
        blk = col // W_C
        return pl.BlockSpec((1, CHUNK, W_C), lambda bi, ci: (bi, ci, blk))

    vec = pl.BlockSpec((1, W_C), lambda bi, ci: (0, 0))
    return pl.pallas_call(
        _hgrn_kernel,
        grid=(b, n_c),
        in_specs=[zspec(C_FC), zspec(C_IC), zspec(C_QC), zspec(C_GC), vec, vec, vec, vec,
                  pl.BlockSpec((1, H_C, DK_C, DK_C), lambda bi, ci: (bi, 0, 0, 0))],
        out_specs=[pl.BlockSpec((1, CHUNK, W_C), lambda bi, ci: (bi, ci, 0)),
                   pl.BlockSpec((1, H_C, DK_C, DK_C), lambda bi, ci: (bi, 0, 0, 0))],
        out_shape=[jax.ShapeDtypeStruct((b, t, W_C), BF16),
                   jax.ShapeDtypeStruct((b, H_C, DK_C, DK_C), F32)],
        scratch_shapes=[pltpu.VMEM((H_C, DK_C, DK_C), F32)],
        compiler_params=_params(("parallel", "arbitrary")),
        name="hgrn",
    )(z3, z3, z3, z3, la, l1, oml, ng, s0)


def _merge_kernel(x_ref, xb_ref, oa_ref, ob_ref, oc_ref, wg_ref, bg_ref, wa_ref, wb_ref, wc_ref, wo_ref,
                  g_ref, b_ref, y_ref, yb_ref):
    xb = xb_ref[...]
    d = D_MODEL
    mix = None
    for br, (o_ref, w_ref) in enumerate(((oa_ref, wa_ref), (ob_ref, wb_ref), (oc_ref, wc_ref))):
        gate = _sigmoid(_dot(xb, wg_ref[:, br * d:(br + 1) * d]) + bg_ref[:, br * d:(br + 1) * d])
        term = gate * _dot(o_ref[...], w_ref[...])
        mix = term if mix is None else mix + term
    out = _dot(mix.astype(BF16), wo_ref[...])
    y = _layer_norm_rows(DEEPNORM_ALPHA * x_ref[...] + out, g_ref[...], b_ref[...])
    y_ref[...] = y
    yb_ref[...] = y.astype(BF16)


def _merge(x, xb, oa, ob, oc, wg, bg, wa, wb, wc, wo, g, b, tm):
    n, d = x.shape

    def row(width):
        return pl.BlockSpec((tm, width), lambda i: (i, 0))

    def full(arr):
        return pl.BlockSpec(arr.shape, lambda i: (0, 0))

    return pl.pallas_call(
        _merge_kernel,
        grid=(n // tm,),
        in_specs=[row(d), row(d), row(oa.shape[1]), row(ob.shape[1]), row(oc.shape[1]),
                  full(wg), full(bg), full(wa), full(wb), full(wc), full(wo), full(g), full(b)],
        out_specs=[row(d), row(d)],
        out_shape=[jax.ShapeDtypeStruct((n, d), F32), jax.ShapeDtypeStruct((n, d), BF16)],
        compiler_params=_params(("parallel",)),
        name="merge",
    )(x, xb, oa, ob, oc, wg, bg, wa, wb, wc, wo, g, b)


def _router_kernel(x_ref, w_ref, o_ref):
    x = x_ref[...]
    xh = x.astype(BF16)
    xl = (x - xh.astype(F32)).astype(BF16)
    w = w_ref[...]
    wh = w.astype(BF16)
    wl = (w - wh.astype(F32)).astype(BF16)
    logits = _dot(xh, wh) + _dot(xh, wl) + _dot(xl, wh)
    lane = lax.broadcasted_iota(jnp.int32, logits.shape, 1)
    logits = jnp.where(lane < N_EXPERTS, logits, NEG_INF)
    big = jnp.int32(LANES)
    m1 = jnp.max(logits, axis=-1, keepdims=True)
    i1 = jnp.min(jnp.where(logits == m1, lane, big), axis=-1, keepdims=True)
    rest = jnp.where(lane == i1, NEG_INF, logits)
    m2 = jnp.max(rest, axis=-1, keepdims=True)
    i2 = jnp.min(jnp.where(rest == m2, lane, big), axis=-1, keepdims=True)
    e2 = jnp.exp(m2 - m1)
    g1 = 1.0 / (1.0 + e2)
    g2 = e2 / (1.0 + e2)
    o_ref[...] = jnp.where(lane == i1, g1, 0.0) + jnp.where(lane == i2, g2, 0.0)


def _router(x, w, tm):
    n, d = x.shape
    return pl.pallas_call(
        _router_kernel,
        grid=(n // tm,),
        in_specs=[pl.BlockSpec((tm, d), lambda i: (i, 0)),
                  pl.BlockSpec(w.shape, lambda i: (0, 0))],
        out_specs=pl.BlockSpec((tm, LANES), lambda i: (i, 0)),
        out_shape=jax.ShapeDtypeStruct((n, LANES), F32),
        compiler_params=_params(("parallel",)),
        name="router",
    )(x, w)


def _ffn_kernel(x_ref, xb_ref, w1_ref, w3_ref, w2_ref, g_ref, b_ref, y_ref, yb_ref, acc_ref):
    f = pl.program_id(1)

    @pl.when(f == 0)
    def _():
        acc_ref[...] = jnp.zeros_like(acc_ref)

    xb = xb_ref[...]
    h1 = _dot(xb, w1_ref[...])
    h = (h1 * _sigmoid(h1)) * _dot(xb, w3_ref[...])
    acc_ref[...] += _dot(h.astype(BF16), w2_ref[...])

    @pl.when(f == pl.num_programs(1) - 1)
    def _():
        y = _layer_norm_rows(DEEPNORM_ALPHA * x_ref[...] + acc_ref[...], g_ref[...], b_ref[...])
        y_ref[...] = y
        yb_ref[...] = y.astype(BF16)


def _ffn(x, xb, w1, w3, w2, g, b, *, tm, tf):
    n, d = x.shape
    ff = w1.shape[1]
    return pl.pallas_call(
        _ffn_kernel,
        grid=(n // tm, ff // tf),
        in_specs=[
            pl.BlockSpec((tm, d), lambda i, f: (i, 0)),
            pl.BlockSpec((tm, d), lambda i, f: (i, 0)),
            pl.BlockSpec((d, tf), lambda i, f: (0, f)),
            pl.BlockSpec((d, tf), lambda i, f: (0, f)),
            pl.BlockSpec((tf, d), lambda i, f: (f, 0)),
            pl.BlockSpec((1, d), lambda i, f: (0, 0)),
            pl.BlockSpec((1, d), lambda i, f: (0, 0)),
        ],
        out_specs=[pl.BlockSpec((tm, d), lambda i, f: (i, 0)),
                   pl.BlockSpec((tm, d), lambda i, f: (i, 0))],
        out_shape=[jax.ShapeDtypeStruct((n, d), F32), jax.ShapeDtypeStruct((n, d), BF16)],
        scratch_shapes=[pltpu.VMEM((tm, d), F32)],
        compiler_params=_params(("parallel", "arbitrary")),
        name="ffn",
    )(x, xb, w1, w3, w2, g, b)


MOE_BS = 128


def _moe_kernel(x_ref, xb_ref, comb_ref, w1_ref, w3_ref, w2_ref, g_ref, b_ref, y_ref, yb_ref,
                rank_ref, rankt_ref, xg_ref, yacc_ref, out_ref, *, tm):
    e = pl.program_id(1)
    f = pl.program_id(2)
    n_f = pl.num_programs(2)
    bs = MOE_BS

    @pl.when((e == 0) & (f == 0))
    def _():
        comb = comb_ref[...]
        routed = comb > 0.0
        r_i = lax.broadcasted_iota(jnp.int32, (tm, tm), 0)
        c_i = lax.broadcasted_iota(jnp.int32, (tm, tm), 1)
        strict_low = jnp.where(c_i < r_i, 1.0, 0.0).astype(BF16)
        rank = _dot(strict_low, jnp.where(routed, 1.0, 0.0).astype(BF16))
        rank = jnp.where(routed, rank, -1.0)
        rank_ref[...] = rank
        rankt_ref[...] = rank.T[:N_EXPERTS]
        out_ref[...] = jnp.zeros_like(out_ref)

    rrow = rankt_ref[pl.ds(e, 1), :]
    n_e = jnp.sum(jnp.where(rrow >= 0.0, 1.0, 0.0)).astype(jnp.int32)
    n_blk = (n_e + bs - 1) // bs

    @pl.when(f == 0)
    def _():
        slot = lax.broadcasted_iota(jnp.int32, (bs, tm), 0).astype(F32)

        def gather(blk, carry):
            r0 = pl.multiple_of(blk * bs, bs)
            pick = jnp.where(rrow - (blk * bs).astype(F32) == slot, 1.0, 0.0).astype(BF16)
            xg_ref[pl.ds(r0, bs), :] = _dot(pick, xb_ref[...]).astype(BF16)
            yacc_ref[pl.ds(r0, bs), :] = jnp.zeros((bs, yacc_ref.shape[1]), F32)
            return carry

        lax.fori_loop(0, n_blk, gather, 0)

    def expert(blk, carry):
        r0 = pl.multiple_of(blk * bs, bs)
        xg = xg_ref[pl.ds(r0, bs), :]
        h1 = _dot(xg, w1_ref[0])
        h = (h1 * _sigmoid(h1)) * _dot(xg, w3_ref[0])
        yacc_ref[pl.ds(r0, bs), :] += _dot(h.astype(BF16), w2_ref[0])
        return carry

    lax.fori_loop(0, n_blk, expert, 0)

    @pl.when(f == n_f - 1)
    def _():
        lane = lax.broadcasted_iota(jnp.int32, (tm, LANES), 1)
        rcol = jnp.sum(jnp.where(lane == e, rank_ref[...], 0.0), axis=-1, keepdims=True)
        gcol = jnp.sum(jnp.where(lane == e, comb_ref[...], 0.0), axis=-1, keepdims=True)
        slot = lax.broadcasted_iota(jnp.int32, (tm, bs), 1).astype(F32)

        def scatter(blk, carry):
            r0 = pl.multiple_of(blk * bs, bs)
            put = jnp.where(rcol - (blk * bs).astype(F32) == slot, 1.0, 0.0).astype(BF16)
            out_ref[...] += gcol * _dot(put, yacc_ref[pl.ds(r0, bs), :].astype(BF16))
            return carry

        lax.fori_loop(0, n_blk, scatter, 0)

    @pl.when((e == pl.num_programs(1) - 1) & (f == n_f - 1))
    def _():
        y = _layer_norm_rows(DEEPNORM_ALPHA * x_ref[...] + out_ref[...], g_ref[...], b_ref[...])
        y_ref[...] = y
        yb_ref[...] = y.astype(BF16)


def _moe(x, xb, comb, w1, w3, w2, g, b, *, tm, tf):
    n, d = x.shape
    n_e, _, ff = w1.shape
    assert tm % MOE_BS == 0 and n % tm == 0 and ff % tf == 0 and n_e == N_EXPERTS
    kern = functools.partial(_moe_kernel, tm=tm)
    one = pl.Buffered(1)
    return pl.pallas_call(
        kern,
        grid=(n // tm, n_e, ff // tf),
        in_specs=[
            pl.BlockSpec((tm, d), lambda i, e, f: (i, 0), pipeline_mode=one),
            pl.BlockSpec((tm, d), lambda i, e, f: (i, 0), pipeline_mode=one),
            pl.BlockSpec((tm, LANES), lambda i, e, f: (i, 0), pipeline_mode=one),
            pl.BlockSpec((1, d, tf), lambda i, e, f: (e, 0, f)),
            pl.BlockSpec((1, d, tf), lambda i, e, f: (e, 0, f)),
            pl.BlockSpec((1, tf, d), lambda i, e, f: (e, f, 0)),
            pl.BlockSpec((1, d), lambda i, e, f: (0, 0)),
            pl.BlockSpec((1, d), lambda i, e, f: (0, 0)),
        ],
        out_specs=[pl.BlockSpec((tm, d), lambda i, e, f: (i, 0)),
                   pl.BlockSpec((tm, d), lambda i, e, f: (i, 0))],
        out_shape=[jax.ShapeDtypeStruct((n, d), F32), jax.ShapeDtypeStruct((n, d), BF16)],
        scratch_shapes=[pltpu.VMEM((tm, LANES), F32), pltpu.VMEM((N_EXPERTS, tm), F32),
                        pltpu.VMEM((tm, d), BF16), pltpu.VMEM((tm, d), F32), pltpu.VMEM((tm, d), F32)],
        compiler_params=_params(("parallel", "arbitrary", "arbitrary")),
        name="moe",
    )(x, xb, comb, w1, w3, w2, g, b)


def _rel_bucket(rel):
    nb = N_BUCKETS // 2
    max_exact = nb // 2
    ret = jnp.where(rel > 0, nb, 0)
    n = jnp.abs(rel)
    nf = jnp.maximum(n, 1).astype(jnp.float32)
    large = max_exact + (jnp.log(nf / max_exact) / math.log(MAX_DISTANCE / max_exact) * (nb - max_exact)).astype(jnp.int32)
    large = jnp.minimum(large, nb - 1)
    return ret + jnp.where(n < max_exact, n, large)


def _bias_tiles(rel_bias, tq):
    r = jnp.arange(tq, dtype=jnp.int32)[:, None]
    c = jnp.arange(LANES, dtype=jnp.int32)[None, :]
    n_near = 1 + -(-tq // LANES)
    near = jnp.stack([c - r + (s - 1) * LANES for s in range(n_near)], axis=0)
    far = jnp.full((1, tq, LANES), -(2 * MAX_DISTANCE + tq + 2 * LANES), jnp.int32)
    rel = jnp.concatenate([far, near], axis=0)
    n = n_near + 1
    vals = jnp.take(rel_bias.astype(F32), _rel_bucket(rel), axis=0)
    vals = vals.transpose(0, 3, 2, 1).reshape(n, HKV_B, G_B, LANES, tq)
    return vals.transpose(0, 1, 3, 2, 4).reshape(n, HKV_B, LANES, G_B * tq)


def _prep_w_in(w_in_l, b_in_l):
    def cols(a):
        parts = [a[..., 0:1536], a[..., 1544:2568], a[..., 2636:4684], a[..., 2568:2632], a[..., 1536:1544], a[..., 2632:2636]]
        pad = jnp.zeros(a.shape[:-1] + (N_IN_PAD - N_IN,), a.dtype)
        return jnp.concatenate(parts + [pad], axis=-1)

    return cols(w_in_l).astype(BF16), cols(b_in_l[None, :]).astype(F32)


def _pad_time(a, length):
    pad = length - a.shape[1]
    if pad == 0:
        return a
    return jnp.concatenate([a, jnp.zeros((a.shape[0], pad) + a.shape[2:], a.dtype)], axis=1)


def _heads_first(a, n_heads, scale=None):
    b, t, _ = a.shape
    a = a.reshape(b, t, n_heads, -1)
    if scale is not None:
        a = a * scale
    return a.transpose(0, 2, 1, 3).astype(BF16)


def _round_up(x, m):
    return (x + m - 1) // m * m


def _run_group(x, caches, wts):
    b, t, d = x.shape
    n = b * t
    p_len = 0 if caches is None else caches[0].shape[2]
    tm = min(512, n)
    xf = x.reshape(n, d).astype(F32)
    xb = xf.astype(BF16)

    fox_tq = min(512, t)
    fox_tk = 512
    l_fox = _round_up(p_len + t, fox_tk)
    dsa_tq = min(2 * LANES, t)
    l_dsa = _round_up(p_len + t, LANES)
    topk = min(TOPK_MAX, (p_len + t) // 4)
    nb = _bias_tiles(wts["rel_bias"], dsa_tq)

    lb_cum = jnp.cumsum(jax.nn.softmax(wts["hgrn_lb"].astype(F32), axis=0), axis=0)

    states = []
    for l in range(DEPTH):
        w_in_p, b_in_p = _prep_w_in(wts["w_in"][l], wts["b_in"][l])
        z = _proj(xb, w_in_p, b_in_p, tm)
        z3 = z.reshape(b, t, N_IN_PAD)

        k_a = z3[..., C_KA:C_KA + 512]
        v_a = z3[..., C_VA:C_VA + 512]
        logf_a = z3[..., C_FA:C_FA + H_A]
        k_b = z3[..., C_KB:C_KB + 128]
        v_b = z3[..., C_VB:C_VB + 128]
        ki_b = z3[..., C_KI:C_KI + DI_B]
        if caches is None:
            ka_all, va_all, lf_all, kb_all, vb_all, ki_all = k_a, v_a, logf_a, k_b, v_b, ki_b
            s0 = jnp.zeros((b, H_C, DK_C, DK_C), F32)
        else:
            c_ka, c_va, c_lf, c_kb, c_vb, c_ki, c_s = [c[l] for c in caches]

            def cat(c, new):
                return jnp.concatenate([c.reshape(b, p_len, -1).astype(F32), new], axis=1)

            ka_all, va_all, lf_all = cat(c_ka, k_a), cat(c_va, v_a), cat(c_lf, logf_a)
            kb_all, vb_all, ki_all = cat(c_kb, k_b), cat(c_vb, v_b), cat(c_ki, ki_b)
            s0 = c_s.astype(F32)

        cum3 = _cumsum(_pad_time(lf_all, l_fox).transpose(0, 2, 1))
        qt_a, kt_a, ckx_a, vt_a = _fox_operands(z3[..., C_QA:C_QA + 512], ka_all, va_all, cum3, l_fox)
        o_a = _fox(qt_a, kt_a, ckx_a, vt_a, tq=fox_tq, tk=fox_tk, q_off=p_len)

        n_t = t // dsa_tq
        q_b = (z3[..., C_QB:C_QB + 512] * HEAD_DIM ** -0.5).reshape(b, n_t, dsa_tq, HKV_B, G_B, HEAD_DIM)
        qt_b = q_b.transpose(0, 3, 5, 1, 4, 2).reshape(b, HKV_B, HEAD_DIM, n_t * G_B * dsa_tq).astype(BF16)
        qit_b = (z3[..., C_QI:C_QI + HI_B * DI_B] * DI_B ** -0.5).reshape(b, t, HI_B, DI_B).transpose(0, 2, 3, 1).astype(BF16)
        wit_b = (z3[..., C_WI:C_WI + HI_B] * HI_B ** -0.5).transpose(0, 2, 1)
        vt_b = _pad_time(vb_all, l_dsa).reshape(b, l_dsa, HKV_B, HEAD_DIM).transpose(0, 2, 3, 1).astype(BF16)
        vt_b = jnp.concatenate([vt_b, jnp.ones((b, HKV_B, DSA_VROWS - HEAD_DIM, l_dsa), BF16)], axis=2)
        o_b = _dsa(qit_b, wit_b, _pad_time(ki_all, l_dsa).astype(BF16),
                   qt_b, _pad_time(kb_all, l_dsa).astype(BF16), vt_b,
                   nb, tq=dsa_tq, q_off=p_len, topk=topk)
        o_b = o_b.reshape(b, HKV_B, HEAD_DIM, n_t, G_B, dsa_tq).transpose(0, 3, 5, 1, 4, 2).reshape(n, H_B * HEAD_DIM)

        lb = (lb_cum[l] - lb_cum[0]).reshape(1, W_C)
        o_c, s_new = _hgrn(z3, jnp.log(lb), jnp.log1p(-lb), 1.0 - lb,
                           wts["hgrn_norm_g"][l].reshape(1, W_C).astype(F32), s0)

        x1, x1b = _merge(xf, xb, o_a.reshape(n, -1), o_b, o_c.reshape(n, -1),
                         wts["w_gate"][l].astype(BF16), wts["b_gate"][l][None, :].astype(F32),
                         wts["w_o_fox"][l].astype(BF16), wts["w_o_dsa"][l].astype(BF16), wts["w_o_hgrn"][l].astype(BF16),
                         wts["w_out"][l].astype(BF16), wts["ln1_g"][l][None, :], wts["ln1_b"][l][None, :], min(256, n))

        g2, b2 = wts["ln2_g"][l][None, :], wts["ln2_b"][l][None, :]
        if l % 2 == 0:
            j = l // 2
            xf, xb = _ffn(x1, x1b, wts["ffn_w1"][j].astype(BF16), wts["ffn_w3"][j].astype(BF16),
                          wts["ffn_w2"][j].astype(BF16), g2, b2, tm=tm, tf=1408)
        else:
            j = l // 2
            rw = jnp.zeros((d, LANES), F32).at[:, :N_EXPERTS].set(wts["moe_router"][j].astype(F32))
            comb = _router(x1, rw, tm)
            xf, xb = _moe(x1, x1b, comb, wts["moe_w1"][j].astype(BF16), wts["moe_w3"][j].astype(BF16),
                          wts["moe_w2"][j].astype(BF16), g2, b2, tm=min(1024, n), tf=896)

        states.append((k_a.reshape(b, t, H_A, HEAD_DIM), v_a.reshape(b, t, H_A, HEAD_DIM), logf_a,
                       k_b.reshape(b, t, HKV_B, HEAD_DIM), v_b.reshape(b, t, HKV_B, HEAD_DIM), ki_b, s_new))
    stacked = [jnp.stack([s[i] for s in states], axis=0) for i in range(7)]
    return xf.reshape(b, t, d), stacked


def kernel(x_prompt, x_sample, cache_fox_k, cache_fox_v, cache_fox_logf, cache_dsa_k, cache_dsa_v, cache_dsa_kidx, state_hgrn,
           w_in, b_in, w_gate, b_gate, w_o_fox, w_o_dsa, w_o_hgrn, w_out, hgrn_lb, hgrn_norm_g, rel_bias,
           ln1_g, ln1_b, ln2_g, ln2_b, ffn_w1, ffn_w3, ffn_w2, moe_router, moe_w1, moe_w3, moe_w2):
    wts = dict(w_in=w_in, b_in=b_in, w_gate=w_gate, b_gate=b_gate, w_o_fox=w_o_fox, w_o_dsa=w_o_dsa, w_o_hgrn=w_o_hgrn,
               w_out=w_out, hgrn_lb=hgrn_lb, hgrn_norm_g=hgrn_norm_g, rel_bias=rel_bias, ln1_g=ln1_g, ln1_b=ln1_b,
               ln2_g=ln2_g, ln2_b=ln2_b, ffn_w1=ffn_w1, ffn_w3=ffn_w3, ffn_w2=ffn_w2, moe_router=moe_router,
               moe_w1=moe_w1, moe_w3=moe_w3, moe_w2=moe_w2)
    caches = (cache_fox_k, cache_fox_v, cache_fox_logf, cache_dsa_k, cache_dsa_v, cache_dsa_kidx, state_hgrn)
    y_prompt, sp = _run_group(x_prompt, None, wts)
    y_sample, ss = _run_group(x_sample, caches, wts)
    return (y_prompt, y_sample, sp[0], sp[1], sp[2], sp[3], sp[4], sp[5], sp[6],
            ss[0], ss[1], ss[2], ss[3], ss[4], ss[5], ss[6])
```

```python
import functools
import math

import jax
import jax.numpy as jnp
import numpy as np
from jax import lax
from jax.experimental import pallas as pl
from jax.experimental.pallas import tpu as pltpu

F32 = jnp.float32
BF16 = jnp.bfloat16
NEG_INF = float("-inf")

D_MODEL = 1024
DEPTH = 2
CHUNK = 64
HEAD_DIM = 64
H_A = 8
H_B = 8
HKV_B = 2
G_B = H_B // HKV_B
HI_B = 4
DI_B = 64
TOPK_MAX = 256
H_C = 4
DK_C = 128
W_C = H_C * DK_C
N_BUCKETS = 32
MAX_DISTANCE = 128
N_EXPERTS = 8
DEEPNORM_ALPHA = (2.0 * DEPTH) ** 0.25
LN_EPS = 1e-5

LANES = 128
SUBLANES = 8
VMEM_LIMIT = 56 * 1024 * 1024

C_QA, C_KA, C_VA = 0, 512, 1024
C_QB, C_KB, C_VB, C_QI = 1536, 2048, 2176, 2304
C_FC, C_IC, C_QC, C_GC = 2560, 3072, 3584, 4096
C_KI, C_FA, C_WI = 4608, 4672, 4680
N_IN = 4684
N_IN_PAD = 4864
PROJ_TN = 256


def _params(sem, vmem=VMEM_LIMIT):
    return pltpu.CompilerParams(dimension_semantics=sem, vmem_limit_bytes=vmem)


def _log_sigmoid(z):
    return jnp.minimum(z, 0.0) - jnp.log1p(jnp.exp(-jnp.abs(z)))


def _sigmoid(z):
    return 1.0 / (1.0 + jnp.exp(-z))


def _dot_nt(a, b):
    return lax.dot_general(a, b, (((1,), (1,)), ((), ())), preferred_element_type=F32)


def _dot_tn(a, b):
    return lax.dot_general(a, b, (((0,), (0,)), ((), ())), preferred_element_type=F32)


def _dot(a, b):
    return jnp.dot(a, b, preferred_element_type=F32)


def _split3(x):
    hi = x.astype(BF16)
    r1 = x - hi.astype(F32)
    mid = r1.astype(BF16)
    lo = (r1 - mid.astype(F32)).astype(BF16)
    return hi, mid, lo


def _layer_norm_rows(r, g, b):
    mu = jnp.mean(r, axis=-1, keepdims=True)
    rc = r - mu
    var = jnp.mean(rc * rc, axis=-1, keepdims=True)
    return rc * lax.rsqrt(var + LN_EPS) * g + b


def _proj_kernel(x_ref, w_ref, b_ref, o_ref):
    x = x_ref[...]
    o_ref[:, :C_KI] = _dot(x, w_ref[:, :C_KI]) + b_ref[:, :C_KI]
    z = _dot(x, w_ref[:, C_KI:]) + b_ref[:, C_KI:]
    col = lax.broadcasted_iota(jnp.int32, z.shape, 1) + C_KI
    o_ref[:, C_KI:] = jnp.where(col >= C_FA, jnp.where(col < C_FA + H_A, _log_sigmoid(z), z), z)


def _proj(xb, w, b, tm):
    n, k = xb.shape
    m = w.shape[1]
    return pl.pallas_call(
        _proj_kernel,
        grid=(n // tm,),
        in_specs=[
            pl.BlockSpec((tm, k), lambda i: (i, 0)),
            pl.BlockSpec((k, m), lambda i: (0, 0), pipeline_mode=pl.Buffered(1)),
            pl.BlockSpec((1, m), lambda i: (0, 0), pipeline_mode=pl.Buffered(1)),
        ],
        out_specs=pl.BlockSpec((tm, m), lambda i: (i, 0)),
        out_shape=jax.ShapeDtypeStruct((n, m), F32),
        compiler_params=_params(("parallel",)),
        name="proj",
    )(xb, w, b)


def _cumsum_kernel(x_ref, hi_ref, mid_ref, lo_ref):
    c = x_ref[0]
    length = c.shape[-1]
    lane = lax.broadcasted_iota(jnp.int32, c.shape, 1)
    s = 1
    while s < length:
        c = c + jnp.where(lane >= s, pltpu.roll(c, s, 1), 0.0)
        s *= 2
    hi, mid, lo = _split3(c)
    hi_ref[0] = hi.astype(F32)
    mid_ref[0] = mid.astype(F32)
    lo_ref[0] = lo.astype(F32)


def _cumsum(x):
    b, r, length = x.shape
    spec = pl.BlockSpec((1, r, length), lambda i: (i, 0, 0))
    return pl.pallas_call(
        _cumsum_kernel,
        grid=(b,),
        in_specs=[spec],
        out_specs=[spec, spec, spec],
        out_shape=[jax.ShapeDtypeStruct(x.shape, F32)] * 3,
        compiler_params=_params(("parallel",)),
        name="cumsum",
    )(x)


FOX_HPS = 2
FOX_CK_TERMS = 3


def _fox_kernel(qt_ref, k_ref, ckx_ref, vt_ref, o_ref, m_ref, l_ref, acc_ref, *, tq, tk, q_off):
    hps = FOX_HPS
    i = pl.program_id(2)
    q0 = q_off + i * tq
    nk = (q0 + tq + tk - 1) // tk
    n_full = (q0 + 1) // tk
    ck_row = lax.broadcasted_iota(jnp.int32, (LANES, tq), 0)
    zeros = jnp.zeros((HEAD_DIM, tq), BF16)
    q_cols = []
    for h in range(hps):
        d = ck_row - FOX_CK_TERMS * (pl.program_id(1) * hps + h)
        minus_one = jnp.where(d >= 0, jnp.where(d < FOX_CK_TERMS, -1.0, 0.0), 0.0).astype(BF16)
        q_parts = [zeros] * hps
        q_parts[h] = qt_ref[0, h]
        q_cols.append(jnp.concatenate(q_parts + [minus_one], axis=0))
    qt = jnp.concatenate(q_cols, axis=1)
    krow = lax.broadcasted_iota(jnp.int32, (tk, hps * tq), 0)
    qcol = q0 + (lax.broadcasted_iota(jnp.int32, (tk, hps * tq), 1) & (tq - 1))
    m_ref[...] = jnp.full((1, hps * tq), NEG_INF, F32)
    l_ref[...] = jnp.zeros((1, hps * tq), F32)
    acc_ref[...] = jnp.zeros((HEAD_DIM, hps * tq), F32)

    def tile(j, masked):
        ks = pl.multiple_of(j * tk, tk)
        kk = jnp.concatenate([k_ref[0, pl.ds(ks, tk), :], ckx_ref[0, pl.ds(ks, tk), :]], axis=1)
        s = _dot(kk, qt)
        if masked:
            s = jnp.where(krow + ks <= qcol, s, NEG_INF)
        m_old = m_ref[...]
        m_new = jnp.maximum(m_old, jnp.max(s, axis=0, keepdims=True))
        p = jnp.exp(s - m_new)
        alpha = jnp.exp(m_old - m_new)
        l_ref[...] = alpha * l_ref[...] + jnp.sum(p, axis=0, keepdims=True)
        p = p.astype(BF16)
        pv = [_dot(vt_ref[0, h, :, pl.ds(ks, tk)], p[:, h * tq:(h + 1) * tq]) for h in range(hps)]
        acc_ref[...] = alpha * acc_ref[...] + jnp.concatenate(pv, axis=1)
        m_ref[...] = m_new

    def full_tile(j, carry):
        tile(j, False)
        return carry

    def edge_tile(j, carry):
        tile(j, True)
        return carry

    lax.fori_loop(0, n_full, full_tile, 0)
    lax.fori_loop(n_full, nk, edge_tile, 0)
    o = acc_ref[...] / l_ref[...]
    o_ref[0] = jnp.concatenate([o[:, h * tq:(h + 1) * tq].T for h in range(hps)], axis=-1).astype(o_ref.dtype)


def _fox(qt, k, ckx, vt, *, tq, tk, q_off):
    b, h, dh, t = qt.shape
    length = k.shape[1]
    hps = FOX_HPS
    assert hps * dh == LANES and FOX_CK_TERMS * h <= LANES and length % tk == 0 and tq & (tq - 1) == 0
    kern = functools.partial(_fox_kernel, tq=tq, tk=tk, q_off=q_off)
    return pl.pallas_call(
        kern,
        grid=(b, h // hps, t // tq),
        in_specs=[
            pl.BlockSpec((1, hps, dh, tq), lambda bi, hi, i: (bi, hi, 0, i)),
            pl.BlockSpec((1, length, LANES), lambda bi, hi, i: (bi, 0, hi)),
            pl.BlockSpec((1, length, LANES), lambda bi, hi, i: (bi, 0, 0)),
            pl.BlockSpec((1, hps, dh, length), lambda bi, hi, i: (bi, hi, 0, 0)),
        ],
        out_specs=pl.BlockSpec((1, tq, hps * dh), lambda bi, hi, i: (bi, i, hi)),
        out_shape=jax.ShapeDtypeStruct((b, t, h * dh), BF16),
        scratch_shapes=[
            pltpu.VMEM((1, hps * tq), F32),
            pltpu.VMEM((1, hps * tq), F32),
            pltpu.VMEM((dh, hps * tq), F32),
        ],
        compiler_params=_params(("parallel", "parallel", "arbitrary")),
        name="fox",
    )(qt, k, ckx, vt)


def _fox_operands(q, k_all, v_all, cum3, length):
    b, t, _ = q.shape
    n_keys = k_all.shape[1]
    ck = jnp.stack(cum3, axis=-1).transpose(0, 2, 1, 3).reshape(b, length, H_A * FOX_CK_TERMS)
    ckx = jnp.concatenate([ck, jnp.zeros((b, length, LANES - H_A * FOX_CK_TERMS), F32)], axis=-1).astype(BF16)
    qt = (q.reshape(b, t, H_A, HEAD_DIM) * HEAD_DIM ** -0.5).transpose(0, 2, 3, 1).astype(BF16)
    vt = _pad_time(v_all, length).reshape(b, length, H_A, HEAD_DIM).transpose(0, 2, 3, 1).astype(BF16)
    return qt, _pad_time(k_all, length).astype(BF16), ckx, vt


INT_MIN = -(2 ** 31)
KEY_NEG_INF = int(np.int32(np.uint32(0xFF800000) ^ np.uint32(0x7FFFFFFF)))
DSA_VROWS = 80
DSA_TK = 256


def _order_key(x):
    bits = lax.bitcast_convert_type(x, jnp.int32)
    return bits ^ ((bits >> 31) & jnp.int32(0x7FFFFFFF))


def _dsa_kernel(qit_ref, wit_ref, ki_ref, qt_ref, k_ref, vt_ref, nbt_ref, o_ref,
                sc_ref, m_ref, acc_ref, *, tq, q_off, topk):
    tk = DSA_TK
    i = pl.program_id(1)
    q0 = q_off + i * tq
    i_abs = q0 // tk
    q_tiles = -(-tq // tk)
    nk = i_abs + q_tiles
    cols = G_B * tq

    wit = wit_ref[0]
    wrows = [wit[hh:hh + 1, :] for hh in range(HI_B)]
    q_chunk = (q0 + lax.broadcasted_iota(jnp.int32, (tk, tq), 1)) >> 6
    krow = lax.broadcasted_iota(jnp.int32, (tk, tq), 0)

    def score_tile(j, carry):
        ks = pl.multiple_of(j * tk, tk)
        ki = ki_ref[0, pl.ds(ks, tk), :]
        tot = jnp.zeros((tk, tq), F32)
        for hh in range(HI_B):
            tot = tot + wrows[hh] * jnp.maximum(_dot(ki, qit_ref[0, hh]), 0.0)
        tot = jnp.where(((krow + ks) >> 6) <= q_chunk, tot, NEG_INF)
        sc_ref[j] = _order_key(tot)
        return carry

    lax.fori_loop(0, nk, score_tile, 0)

    kf = float(topk)

    def count(pred_fn):
        def tile_hits(j):
            hit = jnp.where(pred_fn(sc_ref[j]), 1.0, 0.0)
            parts = [hit[r:r + SUBLANES] for r in range(0, tk, SUBLANES)]
            while len(parts) > 1:
                parts = [parts[a] + parts[a + 1] for a in range(0, len(parts), 2)]
            return parts[0]

        acc8 = lax.fori_loop(0, nk, lambda j, acc8: acc8 + tile_hits(j), jnp.zeros((SUBLANES, tq), F32))
        return jnp.sum(acc8, axis=0, keepdims=True)

    c0 = count(lambda key: key >= 0)
    thr = jnp.where(c0 >= kf, jnp.int32(0), jnp.int32(INT_MIN))
    for bit in range(30, -1, -1):
        cand = thr + jnp.int32(1 << bit)
        c = count(lambda key, cand=cand: key >= cand)
        thr = jnp.where(c >= kf, cand, thr)

    cnt_gt = count(lambda key: key > thr)
    short = thr <= jnp.int32(KEY_NEG_INF)
    thr_b = jnp.broadcast_to(jnp.maximum(thr, jnp.int32(KEY_NEG_INF)), (tk, tq))
    need = jnp.where(short, 0.0, kf - cnt_gt)

    tri = jnp.where(lax.broadcasted_iota(jnp.int32, (tk, tk), 1) <= lax.broadcasted_iota(jnp.int32, (tk, tk), 0),
                    1.0, 0.0).astype(BF16)
    m_ref[...] = jnp.full((HKV_B, 1, cols), NEG_INF, F32)
    acc_ref[...] = jnp.zeros((HKV_B, DSA_VROWS, cols), F32)
    q_zero = jnp.zeros((HEAD_DIM, cols), BF16)
    q_ext = []
    for g in range(HKV_B):
        parts = [q_zero] * HKV_B
        parts[g] = qt_ref[0, g]
        q_ext.append(jnp.concatenate(parts, axis=0))

    n_far = jnp.maximum(i_abs - 1, 0)

    def attend_tile(j, eqc, near):
        ks = pl.multiple_of(j * tk, tk)
        key = sc_ref[j]
        gt = key > thr_b
        eq = key == thr_b
        eqf = jnp.where(eq, 1.0, 0.0)
        rank = _dot(tri, eqf.astype(BF16)) + eqc
        pen = jnp.where(gt, 0.0, jnp.where(eq, rank, 1e9))
        madd = jnp.where(pen <= need, 0.0, NEG_INF)
        madd = jnp.concatenate([madd] * G_B, axis=1)
        k_tile = k_ref[0, pl.ds(ks, tk), :]
        for g in range(HKV_B):
            s = _dot(k_tile, q_ext[g]) + madd
            if near:
                s = s + nbt_ref[j - i_abs + 1, g]
            m_old = m_ref[g]
            m_new = jnp.maximum(m_old, jnp.max(s, axis=0, keepdims=True))
            m_safe = jnp.where(m_new == NEG_INF, 0.0, m_new)
            p = jnp.exp(s - m_safe)
            acc_ref[g] = jnp.exp(m_old - m_safe) * acc_ref[g] + _dot(vt_ref[0, g, :, pl.ds(ks, tk)], p.astype(BF16))
            m_ref[g] = m_new
        return eqc + jnp.sum(eqf, axis=0, keepdims=True)

    eqc = lax.fori_loop(0, n_far, functools.partial(attend_tile, near=False), jnp.zeros((1, tq), F32))
    lax.fori_loop(n_far, nk, functools.partial(attend_tile, near=True), eqc)
    for g in range(HKV_B):
        acc = acc_ref[g]
        o_ref[0, g] = (acc[:HEAD_DIM] / acc[HEAD_DIM:HEAD_DIM + 1]).astype(o_ref.dtype)


def _dsa(qit, wit, ki, qt, k, vt, nbt, *, tq, q_off, topk):
    b, _, _, t = qit.shape
    length = ki.shape[1]
    cols = G_B * tq
    n_bias = nbt.shape[0]
    tk = DSA_TK
    assert tk >= MAX_DISTANCE and q_off % tk == 0 and length % tk == 0 and length >= q_off + t
    assert HKV_B * HEAD_DIM == LANES and tq % CHUNK == 0 and (tq % tk == 0 or t == tq) and n_bias == 1 + -(-tq // tk)
    kern = functools.partial(_dsa_kernel, tq=tq, q_off=q_off, topk=topk)
    return pl.pallas_call(
        kern,
        grid=(b, t // tq),
        in_specs=[
            pl.BlockSpec((1, HI_B, DI_B, tq), lambda bi, i: (bi, 0, 0, i)),
            pl.BlockSpec((1, HI_B, tq), lambda bi, i: (bi, 0, i)),
            pl.BlockSpec((1, length, DI_B), lambda bi, i: (bi, 0, 0)),
            pl.BlockSpec((1, HKV_B, HEAD_DIM, cols), lambda bi, i: (bi, 0, 0, i)),
            pl.BlockSpec((1, length, LANES), lambda bi, i: (bi, 0, 0)),
            pl.BlockSpec((1, HKV_B, DSA_VROWS, length), lambda bi, i: (bi, 0, 0, 0)),
            pl.BlockSpec((n_bias, HKV_B, tk, cols), lambda bi, i: (0, 0, 0, 0), pipeline_mode=pl.Buffered(1)),
        ],
        out_specs=pl.BlockSpec((1, HKV_B, HEAD_DIM, cols), lambda bi, i: (bi, 0, 0, i)),
        out_shape=jax.ShapeDtypeStruct(qt.shape, BF16),
        scratch_shapes=[
            pltpu.VMEM((length // tk, tk, tq), jnp.int32),
            pltpu.VMEM((HKV_B, 1, cols), F32),
            pltpu.VMEM((HKV_B, DSA_VROWS, cols), F32),
        ],
        compiler_params=_params(("parallel", "arbitrary")),
        name="dsa",
    )(qit, wit, ki, qt, k, vt, nbt)


HG_SUB = SUBLANES


def _hgrn_kernel(f_ref, i_ref, q_ref, g_ref, la_ref, l1_ref, oml_ref, ng_ref, s0_ref,
                 o_ref, sfin_ref, st_ref):
    c = pl.program_id(1)
    n_c = pl.num_programs(1)
    ct = CHUNK

    @pl.when(c == 0)
    def _():
        for hh in range(H_C):
            st_ref[hh] = s0_ref[0, hh].T

    zf = f_ref[0]
    a = la_ref[...]
    cc = l1_ref[...] + _log_sigmoid(zf)
    logf = jnp.maximum(a, cc) + jnp.log1p(jnp.exp(-jnp.abs(a - cc)))
    kk = oml_ref[...] * _sigmoid(-zf)

    r_i = lax.broadcasted_iota(jnp.int32, (ct, ct), 0)
    c_i = lax.broadcasted_iota(jnp.int32, (ct, ct), 1)
    low = jnp.where(c_i <= r_i, 1.0, 0.0).astype(BF16)
    hi, mid, lo = _split3(logf)
    bcum = _dot(low, hi) + _dot(low, mid) + _dot(low, lo)

    sub_row = lax.broadcasted_iota(jnp.int32, (HG_SUB, DK_C), 0)
    a_col = lax.broadcasted_iota(jnp.int32, (HG_SUB, ct), 1)
    n_sub = ct // HG_SUB

    for hh in range(H_C):
        cs = slice(hh * DK_C, (hh + 1) * DK_C)
        qh = q_ref[0, :, cs]
        kh = kk[:, cs]
        vh = i_ref[0, :, cs]
        bh = bcum[:, cs]
        a_rows = []
        for bi in range(n_sub):
            r0 = bi * HG_SUB
            q_blk = qh[r0:r0 + HG_SUB]
            b_blk = bh[r0:r0 + HG_SUB]
            k_blk = kh[r0:r0 + HG_SUB]
            if bi > 0:
                ref = bh[r0:r0 + 1]
                qp = q_blk * jnp.exp(b_blk - ref)
                kp = kh[:r0] * jnp.exp(ref - bh[:r0])
                kp = jnp.concatenate([kp, jnp.zeros((ct - r0, DK_C), F32)], axis=0)
                a_row = _dot_nt(qp.astype(BF16), kp.astype(BF16))
            else:
                a_row = jnp.zeros((HG_SUB, ct), F32)
            for s in range(HG_SUB):
                d = jnp.where(sub_row >= s, b_blk - b_blk[s:s + 1], NEG_INF)
                x = q_blk * jnp.exp(d) * k_blk[s:s + 1]
                col = jnp.sum(x, axis=-1, keepdims=True)
                a_row = jnp.where(a_col == r0 + s, col, a_row)
            a_rows.append(a_row)
        a_mat = jnp.concatenate(a_rows, axis=0)
        st = st_ref[hh]
        o = _dot(a_mat.astype(BF16), vh.astype(BF16)) + _dot_nt((qh * jnp.exp(bh)).astype(BF16), st.astype(BF16))
        bl = bh[ct - 1:ct]
        kdec = kh * jnp.exp(bl - bh)
        st_ref[hh] = st * jnp.exp(bl) + _dot_tn(vh.astype(BF16), kdec.astype(BF16))
        o = o * lax.rsqrt(jnp.mean(o * o, axis=-1, keepdims=True) + LN_EPS) * ng_ref[:, cs]
        gh = g_ref[0, :, cs]
        o_ref[0, :, cs] = (o * (gh * _sigmoid(gh))).astype(o_ref.dtype)

    @pl.when(c == n_c - 1)
    def _():
        for hh in range(H_C):
            sfin_ref[0, hh] = st_ref[hh].T


def _hgrn(z3, la, l1, oml, ng, s0):
    b, t, _ = z3.shape
    n_c = t // CHUNK

    def zspec(col):
        blk = col // W_C
        return pl.BlockSpec((1, CHUNK, W_C), lambda bi, ci: (bi, ci, blk))

    vec = pl.BlockSpec((1, W_C), lambda bi, ci: (0, 0))
    return pl.pallas_call(
        _hgrn_kernel,
        grid=(b, n_c),
        in_specs=[zspec(C_FC), zspec(C_IC), zspec(C_QC), zspec(C_GC), vec, vec, vec, vec,
                  pl.BlockSpec((1, H_C, DK_C, DK_C), lambda bi, ci: (bi, 0, 0, 0))],
        out_specs=[pl.BlockSpec((1, CHUNK, W_C), lambda bi, ci: (bi, ci, 0)),
                   pl.BlockSpec((1, H_C, DK_C, DK_C), lambda bi, ci: (bi, 0, 0, 0))],
        out_shape=[jax.ShapeDtypeStruct((b, t, W_C), BF16),
                   jax.ShapeDtypeStruct((b, H_C, DK_C, DK_C), F32)],
        scratch_shapes=[pltpu.VMEM((H_C, DK_C, DK_C), F32)],
        compiler_params=_params(("parallel", "arbitrary")),
        name="hgrn",
    )(z3, z3, z3, z3, la, l1, oml, ng, s0)


def _merge_kernel(x_ref, xb_ref, oa_ref, ob_ref, oc_ref, wg_ref, bg_ref, wa_ref, wb_ref, wc_ref, wo_ref,
                  g_ref, b_ref, y_ref, yb_ref):
    xb = xb_ref[...]
    d = D_MODEL
    mix = None
    for br, (o_ref, w_ref) in enumerate(((oa_ref, wa_ref), (ob_ref, wb_ref), (oc_ref, wc_ref))):
        gate = _sigmoid(_dot(xb, wg_ref[:, br * d:(br + 1) * d]) + bg_ref[:, br * d:(br + 1) * d])
        term = gate * _dot(o_ref[...], w_ref[...])
        mix = term if mix is None else mix + term
    out = _dot(mix.astype(BF16), wo_ref[...])
    y = _layer_norm_rows(DEEPNORM_ALPHA * x_ref[...] + out, g_ref[...], b_ref[...])
    y_ref[...] = y
    yb_ref[...] = y.astype(BF16)


def _merge(x, xb, oa, ob, oc, wg, bg, wa, wb, wc, wo, g, b, tm):
    n, d = x.shape

    def row(width):
        return pl.BlockSpec((tm, width), lambda i: (i, 0))

    def full(arr):
        return pl.BlockSpec(arr.shape, lambda i: (0, 0))

    return pl.pallas_call(
        _merge_kernel,
        grid=(n // tm,),
        in_specs=[row(d), row(d), row(oa.shape[1]), row(ob.shape[1]), row(oc.shape[1]),
                  full(wg), full(bg), full(wa), full(wb), full(wc), full(wo), full(g), full(b)],
        out_specs=[row(d), row(d)],
        out_shape=[jax.ShapeDtypeStruct((n, d), F32), jax.ShapeDtypeStruct((n, d), BF16)],
        compiler_params=_params(("parallel",)),
        name="merge",
    )(x, xb, oa, ob, oc, wg, bg, wa, wb, wc, wo, g, b)


def _router_kernel(x_ref, w_ref, o_ref):
    x = x_ref[...]
    xh = x.astype(BF16)
    xl = (x - xh.astype(F32)).astype(BF16)
    w = w_ref[...]
    wh = w.astype(BF16)
    wl = (w - wh.astype(F32)).astype(BF16)
    logits = _dot(xh, wh) + _dot(xh, wl) + _dot(xl, wh)
    lane = lax.broadcasted_iota(jnp.int32, logits.shape, 1)
    logits = jnp.where(lane < N_EXPERTS, logits, NEG_INF)
    big = jnp.int32(LANES)
    m1 = jnp.max(logits, axis=-1, keepdims=True)
    i1 = jnp.min(jnp.where(logits == m1, lane, big), axis=-1, keepdims=True)
    rest = jnp.where(lane == i1, NEG_INF, logits)
    m2 = jnp.max(rest, axis=-1, keepdims=True)
    i2 = jnp.min(jnp.where(rest == m2, lane, big), axis=-1, keepdims=True)
    e2 = jnp.exp(m2 - m1)
    g1 = 1.0 / (1.0 + e2)
    g2 = e2 / (1.0 + e2)
    o_ref[...] = jnp.where(lane == i1, g1, 0.0) + jnp.where(lane == i2, g2, 0.0)


def _router(x, w, tm):
    n, d = x.shape
    return pl.pallas_call(
        _router_kernel,
        grid=(n // tm,),
        in_specs=[pl.BlockSpec((tm, d), lambda i: (i, 0)),
                  pl.BlockSpec(w.shape, lambda i: (0, 0))],
        out_specs=pl.BlockSpec((tm, LANES), lambda i: (i, 0)),
        out_shape=jax.ShapeDtypeStruct((n, LANES), F32),
        compiler_params=_params(("parallel",)),
        name="router",
    )(x, w)


def _ffn_kernel(x_ref, xb_ref, w1_ref, w3_ref, w2_ref, g_ref, b_ref, y_ref, yb_ref, acc_ref):
    f = pl.program_id(1)

    @pl.when(f == 0)
    def _():
        acc_ref[...] = jnp.zeros_like(acc_ref)

    xb = xb_ref[...]
    h1 = _dot(xb, w1_ref[...])
    h = (h1 * _sigmoid(h1)) * _dot(xb, w3_ref[...])
    acc_ref[...] += _dot(h.astype(BF16), w2_ref[...])

    @pl.when(f == pl.num_programs(1) - 1)
    def _():
        y = _layer_norm_rows(DEEPNORM_ALPHA * x_ref[...] + acc_ref[...], g_ref[...], b_ref[...])
        y_ref[...] = y
        yb_ref[...] = y.astype(BF16)


def _ffn(x, xb, w1, w3, w2, g, b, *, tm, tf):
    n, d = x.shape
    ff = w1.shape[1]
    return pl.pallas_call(
        _ffn_kernel,
        grid=(n // tm, ff // tf),
        in_specs=[
            pl.BlockSpec((tm, d), lambda i, f: (i, 0)),
            pl.BlockSpec((tm, d), lambda i, f: (i, 0)),
            pl.BlockSpec((d, tf), lambda i, f: (0, f)),
            pl.BlockSpec((d, tf), lambda i, f: (0, f)),
            pl.BlockSpec((tf, d), lambda i, f: (f, 0)),
            pl.BlockSpec((1, d), lambda i, f: (0, 0)),
            pl.BlockSpec((1, d), lambda i, f: (0, 0)),
        ],
        out_specs=[pl.BlockSpec((tm, d), lambda i, f: (i, 0)),
                   pl.BlockSpec((tm, d), lambda i, f: (i, 0))],
        out_shape=[jax.ShapeDtypeStruct((n, d), F32), jax.ShapeDtypeStruct((n, d), BF16)],
        scratch_shapes=[pltpu.VMEM((tm, d), F32)],
        compiler_params=_params(("parallel", "arbitrary")),
        name="ffn",
    )(x, xb, w1, w3, w2, g, b)


MOE_BS = 128


def _moe_kernel(x_ref, xb_ref, comb_ref, w1_ref, w3_ref, w2_ref, g_ref, b_ref, y_ref, yb_ref,
                rank_ref, rankt_ref, xg_ref, yacc_ref, out_ref, *, tm):
    e = pl.program_id(1)
    f = pl.program_id(2)
    n_f = pl.num_programs(2)
    bs = MOE_BS

    @pl.when((e == 0) & (f == 0))
    def _():
        comb = comb_ref[...]
        routed = comb > 0.0
        r_i = lax.broadcasted_iota(jnp.int32, (tm, tm), 0)
        c_i = lax.broadcasted_iota(jnp.int32, (tm, tm), 1)
        strict_low = jnp.where(c_i < r_i, 1.0, 0.0).astype(BF16)
        rank = _dot(strict_low, jnp.where(routed, 1.0, 0.0).astype(BF16))
        rank = jnp.where(routed, rank, -1.0)
        rank_ref[...] = rank
        rankt_ref[...] = rank.T[:N_EXPERTS]
        out_ref[...] = jnp.zeros_like(out_ref)

    rrow = rankt_ref[pl.ds(e, 1), :]
    n_e = jnp.sum(jnp.where(rrow >= 0.0, 1.0, 0.0)).astype(jnp.int32)
    n_blk = (n_e + bs - 1) // bs

    @pl.when(f == 0)
    def _():
        slot = lax.broadcasted_iota(jnp.int32, (bs, tm), 0).astype(F32)

        def gather(blk, carry):
            r0 = pl.multiple_of(blk * bs, bs)
            pick = jnp.where(rrow - (blk * bs).astype(F32) == slot, 1.0, 0.0).astype(BF16)
            xg_ref[pl.ds(r0, bs), :] = _dot(pick, xb_ref[...]).astype(BF16)
            yacc_ref[pl.ds(r0, bs), :] = jnp.zeros((bs, yacc_ref.shape[1]), F32)
            return carry

        lax.fori_loop(0, n_blk, gather, 0)

    def expert_rows(r0, rows):
        xg = xg_ref[pl.ds(r0, rows), :]
        h1 = _dot(xg, w1_ref[0])
        h = (h1 * _sigmoid(h1)) * _dot(xg, w3_ref[0])
        yacc_ref[pl.ds(r0, rows), :] += _dot(h.astype(BF16), w2_ref[0])

    def expert_pair(pair, carry):
        expert_rows(pl.multiple_of(pair * (2 * bs), 2 * bs), 2 * bs)
        return carry

    n_pair = n_blk // 2
    lax.fori_loop(0, n_pair, expert_pair, 0)

    @pl.when(n_blk % 2 == 1)
    def _():
        expert_rows(pl.multiple_of(n_pair * (2 * bs), bs), bs)

    @pl.when(f == n_f - 1)
    def _():
        lane = lax.broadcasted_iota(jnp.int32, (tm, LANES), 1)
        rcol = jnp.sum(jnp.where(lane == e, rank_ref[...], 0.0), axis=-1, keepdims=True)
        gcol = jnp.sum(jnp.where(lane == e, comb_ref[...], 0.0), axis=-1, keepdims=True)
        slot = lax.broadcasted_iota(jnp.int32, (tm, bs), 1).astype(F32)

        def scatter(blk, carry):
            r0 = pl.multiple_of(blk * bs, bs)
            put = jnp.where(rcol - (blk * bs).astype(F32) == slot, 1.0, 0.0).astype(BF16)
            out_ref[...] += gcol * _dot(put, yacc_ref[pl.ds(r0, bs), :].astype(BF16))
            return carry

        lax.fori_loop(0, n_blk, scatter, 0)

    @pl.when((e == pl.num_programs(1) - 1) & (f == n_f - 1))
    def _():
        y = _layer_norm_rows(DEEPNORM_ALPHA * x_ref[...] + out_ref[...], g_ref[...], b_ref[...])
        y_ref[...] = y
        yb_ref[...] = y.astype(BF16)


def _moe(x, xb, comb, w1, w3, w2, g, b, *, tm, tf):
    n, d = x.shape
    n_e, _, ff = w1.shape
    assert tm % MOE_BS == 0 and n % tm == 0 and ff % tf == 0 and n_e == N_EXPERTS
    kern = functools.partial(_moe_kernel, tm=tm)
    one = pl.Buffered(1)
    return pl.pallas_call(
        kern,
        grid=(n // tm, n_e, ff // tf),
        in_specs=[
            pl.BlockSpec((tm, d), lambda i, e, f: (i, 0), pipeline_mode=one),
            pl.BlockSpec((tm, d), lambda i, e, f: (i, 0), pipeline_mode=one),
            pl.BlockSpec((tm, LANES), lambda i, e, f: (i, 0), pipeline_mode=one),
            pl.BlockSpec((1, d, tf), lambda i, e, f: (e, 0, f)),
            pl.BlockSpec((1, d, tf), lambda i, e, f: (e, 0, f)),
            pl.BlockSpec((1, tf, d), lambda i, e, f: (e, f, 0)),
            pl.BlockSpec((1, d), lambda i, e, f: (0, 0)),
            pl.BlockSpec((1, d), lambda i, e, f: (0, 0)),
        ],
        out_specs=[pl.BlockSpec((tm, d), lambda i, e, f: (i, 0)),
                   pl.BlockSpec((tm, d), lambda i, e, f: (i, 0))],
        out_shape=[jax.ShapeDtypeStruct((n, d), F32), jax.ShapeDtypeStruct((n, d), BF16)],
        scratch_shapes=[pltpu.VMEM((tm, LANES), F32), pltpu.VMEM((N_EXPERTS, tm), F32),
                        pltpu.VMEM((tm, d), BF16), pltpu.VMEM((tm, d), F32), pltpu.VMEM((tm, d), F32)],
        compiler_params=_params(("parallel", "arbitrary", "arbitrary")),
        name="moe",
    )(x, xb, comb, w1, w3, w2, g, b)


def _rel_bucket(rel):
    nb = N_BUCKETS // 2
    max_exact = nb // 2
    ret = jnp.where(rel > 0, nb, 0)
    n = jnp.abs(rel)
    nf = jnp.maximum(n, 1).astype(jnp.float32)
    large = max_exact + (jnp.log(nf / max_exact) / math.log(MAX_DISTANCE / max_exact) * (nb - max_exact)).astype(jnp.int32)
    large = jnp.minimum(large, nb - 1)
    return ret + jnp.where(n < max_exact, n, large)


def _bias_tiles(rel_bias, tq):
    tk = DSA_TK
    r = jnp.arange(tq, dtype=jnp.int32)[:, None]
    c = jnp.arange(tk, dtype=jnp.int32)[None, :]
    n = 1 + -(-tq // tk)
    rel = jnp.stack([c - r + (s - 1) * tk for s in range(n)], axis=0)
    table = rel_bias.astype(F32)
    onehot = _rel_bucket(rel)[..., None] == jnp.arange(N_BUCKETS, dtype=jnp.int32)
    vals = jnp.sum(jnp.where(onehot[..., None], table, 0.0), axis=-2)
    far = table[_rel_bucket(jnp.int32(-(2 * MAX_DISTANCE)))]
    vals = (vals - far).transpose(0, 3, 2, 1).reshape(n, HKV_B, G_B, tk, tq)
    return vals.transpose(0, 1, 3, 2, 4).reshape(n, HKV_B, tk, G_B * tq)


def _prep_w_in(w_in_l, b_in_l):
    def cols(a):
        parts = [a[..., 0:1536], a[..., 1544:2568], a[..., 2636:4684], a[..., 2568:2632], a[..., 1536:1544], a[..., 2632:2636]]
        pad = jnp.zeros(a.shape[:-1] + (N_IN_PAD - N_IN,), a.dtype)
        return jnp.concatenate(parts + [pad], axis=-1)

    return cols(w_in_l).astype(BF16), cols(b_in_l[None, :]).astype(F32)


def _pad_time(a, length):
    pad = length - a.shape[1]
    if pad == 0:
        return a
    return jnp.concatenate([a, jnp.zeros((a.shape[0], pad) + a.shape[2:], a.dtype)], axis=1)


def _heads_first(a, n_heads, scale=None):
    b, t, _ = a.shape
    a = a.reshape(b, t, n_heads, -1)
    if scale is not None:
        a = a * scale
    return a.transpose(0, 2, 1, 3).astype(BF16)


def _round_up(x, m):
    return (x + m - 1) // m * m


def _run_group(x, caches, wts):
    b, t, d = x.shape
    n = b * t
    p_len = 0 if caches is None else caches[0].shape[2]
    tm = min(512, n)
    xf = x.reshape(n, d).astype(F32)
    xb = xf.astype(BF16)

    fox_tq = min(512, t)
    fox_tk = 512
    l_fox = _round_up(p_len + t, fox_tk)
    dsa_tq = min(2 * LANES, t)
    l_dsa = _round_up(p_len + t, DSA_TK)
    topk = min(TOPK_MAX, (p_len + t) // 4)
    nb = _bias_tiles(wts["rel_bias"], dsa_tq)

    lb_cum = jnp.cumsum(jax.nn.softmax(wts["hgrn_lb"].astype(F32), axis=0), axis=0)

    states = []
    for l in range(DEPTH):
        w_in_p, b_in_p = _prep_w_in(wts["w_in"][l], wts["b_in"][l])
        z = _proj(xb, w_in_p, b_in_p, tm)
        z3 = z.reshape(b, t, N_IN_PAD)

        k_a = z3[..., C_KA:C_KA + 512]
        v_a = z3[..., C_VA:C_VA + 512]
        logf_a = z3[..., C_FA:C_FA + H_A]
        k_b = z3[..., C_KB:C_KB + 128]
        v_b = z3[..., C_VB:C_VB + 128]
        ki_b = z3[..., C_KI:C_KI + DI_B]
        if caches is None:
            ka_all, va_all, lf_all, kb_all, vb_all, ki_all = k_a, v_a, logf_a, k_b, v_b, ki_b
            s0 = jnp.zeros((b, H_C, DK_C, DK_C), F32)
        else:
            c_ka, c_va, c_lf, c_kb, c_vb, c_ki, c_s = [c[l] for c in caches]

            def cat(c, new):
                return jnp.concatenate([c.reshape(b, p_len, -1).astype(F32), new], axis=1)

            ka_all, va_all, lf_all = cat(c_ka, k_a), cat(c_va, v_a), cat(c_lf, logf_a)
            kb_all, vb_all, ki_all = cat(c_kb, k_b), cat(c_vb, v_b), cat(c_ki, ki_b)
            s0 = c_s.astype(F32)

        cum3 = _cumsum(_pad_time(lf_all, l_fox).transpose(0, 2, 1))
        qt_a, kt_a, ckx_a, vt_a = _fox_operands(z3[..., C_QA:C_QA + 512], ka_all, va_all, cum3, l_fox)
        o_a = _fox(qt_a, kt_a, ckx_a, vt_a, tq=fox_tq, tk=fox_tk, q_off=p_len)

        n_t = t // dsa_tq
        q_b = (z3[..., C_QB:C_QB + 512] * HEAD_DIM ** -0.5).reshape(b, n_t, dsa_tq, HKV_B, G_B, HEAD_DIM)
        qt_b = q_b.transpose(0, 3, 5, 1, 4, 2).reshape(b, HKV_B, HEAD_DIM, n_t * G_B * dsa_tq).astype(BF16)
        qit_b = (z3[..., C_QI:C_QI + HI_B * DI_B] * DI_B ** -0.5).reshape(b, t, HI_B, DI_B).transpose(0, 2, 3, 1).astype(BF16)
        wit_b = (z3[..., C_WI:C_WI + HI_B] * HI_B ** -0.5).transpose(0, 2, 1)
        vt_b = _pad_time(vb_all, l_dsa).reshape(b, l_dsa, HKV_B, HEAD_DIM).transpose(0, 2, 3, 1).astype(BF16)
        vt_b = jnp.concatenate([vt_b, jnp.ones((b, HKV_B, DSA_VROWS - HEAD_DIM, l_dsa), BF16)], axis=2)
        o_b = _dsa(qit_b, wit_b, _pad_time(ki_all, l_dsa).astype(BF16),
                   qt_b, _pad_time(kb_all, l_dsa).astype(BF16), vt_b,
                   nb, tq=dsa_tq, q_off=p_len, topk=topk)
        o_b = o_b.reshape(b, HKV_B, HEAD_DIM, n_t, G_B, dsa_tq).transpose(0, 3, 5, 1, 4, 2).reshape(n, H_B * HEAD_DIM)

        lb = (lb_cum[l] - lb_cum[0]).reshape(1, W_C)
        o_c, s_new = _hgrn(z3, jnp.log(lb), jnp.log1p(-lb), 1.0 - lb,
                           wts["hgrn_norm_g"][l].reshape(1, W_C).astype(F32), s0)

        x1, x1b = _merge(xf, xb, o_a.reshape(n, -1), o_b, o_c.reshape(n, -1),
                         wts["w_gate"][l].astype(BF16), wts["b_gate"][l][None, :].astype(F32),
                         wts["w_o_fox"][l].astype(BF16), wts["w_o_dsa"][l].astype(BF16), wts["w_o_hgrn"][l].astype(BF16),
                         wts["w_out"][l].astype(BF16), wts["ln1_g"][l][None, :], wts["ln1_b"][l][None, :], min(256, n))

        g2, b2 = wts["ln2_g"][l][None, :], wts["ln2_b"][l][None, :]
        if l % 2 == 0:
            j = l // 2
            xf, xb = _ffn(x1, x1b, wts["ffn_w1"][j].astype(BF16), wts["ffn_w3"][j].astype(BF16),
                          wts["ffn_w2"][j].astype(BF16), g2, b2, tm=tm, tf=1408)
        else:
            j = l // 2
            rw = jnp.zeros((d, LANES), F32).at[:, :N_EXPERTS].set(wts["moe_router"][j].astype(F32))
            comb = _router(x1, rw, tm)
            xf, xb = _moe(x1, x1b, comb, wts["moe_w1"][j].astype(BF16), wts["moe_w3"][j].astype(BF16),
                          wts["moe_w2"][j].astype(BF16), g2, b2, tm=min(1024, n), tf=896)

        states.append((k_a.reshape(b, t, H_A, HEAD_DIM), v_a.reshape(b, t, H_A, HEAD_DIM), logf_a,
                       k_b.reshape(b, t, HKV_B, HEAD_DIM), v_b.reshape(b, t, HKV_B, HEAD_DIM), ki_b, s_new))
    stacked = [jnp.stack([s[i] for s in states], axis=0) for i in range(7)]
    return xf.reshape(b, t, d), stacked


def kernel(x_prompt, x_sample, cache_fox_k, cache_fox_v, cache_fox_logf, cache_dsa_k, cache_dsa_v, cache_dsa_kidx, state_hgrn,
           w_in, b_in, w_gate, b_gate, w_o_fox, w_o_dsa, w_o_hgrn, w_out, hgrn_lb, hgrn_norm_g, rel_bias,
           ln1_g, ln1_b, ln2_g, ln2_b, ffn_w1, ffn_w3, ffn_w2, moe_router, moe_w1, moe_w3, moe_w2):
    wts = dict(w_in=w_in, b_in=b_in, w_gate=w_gate, b_gate=b_gate, w_o_fox=w_o_fox, w_o_dsa=w_o_dsa, w_o_hgrn=w_o_hgrn,
               w_out=w_out, hgrn_lb=hgrn_lb, hgrn_norm_g=hgrn_norm_g, rel_bias=rel_bias, ln1_g=ln1_g, ln1_b=ln1_b,
               ln2_g=ln2_g, ln2_b=ln2_b, ffn_w1=ffn_w1, ffn_w3=ffn_w3, ffn_w2=ffn_w2, moe_router=moe_router,
               moe_w1=moe_w1, moe_w3=moe_w3, moe_w2=moe_w2)
    caches = (cache_fox_k, cache_fox_v, cache_fox_logf, cache_dsa_k, cache_dsa_v, cache_dsa_kidx, state_hgrn)
    y_prompt, sp = _run_group(x_prompt, None, wts)
    y_sample, ss = _run_group(x_sample, caches, wts)
    return (y_prompt, y_sample, sp[0], sp[1], sp[2], sp[3], sp[4], sp[5], sp[6],
            ss[0], ss[1], ss[2], ss[3], ss[4], ss[5], ss[6])
```

```python
import functools
import math

import jax
import jax.numpy as jnp
import numpy as np
from jax import lax
from jax.experimental import pallas as pl
from jax.experimental.pallas import tpu as pltpu

F32 = jnp.float32
BF16 = jnp.bfloat16
NEG_INF = float("-inf")

D_MODEL = 1024
DEPTH = 2
CHUNK = 64
HEAD_DIM = 64
H_A = 8
H_B = 8
HKV_B = 2
G_B = H_B // HKV_B
HI_B = 4
DI_B = 64
TOPK_MAX = 256
H_C = 4
DK_C = 128
W_C = H_C * DK_C
N_BUCKETS = 32
MAX_DISTANCE = 128
N_EXPERTS = 8
DEEPNORM_ALPHA = (2.0 * DEPTH) ** 0.25
LN_EPS = 1e-5

LANES = 128
SUBLANES = 8
VMEM_LIMIT = 56 * 1024 * 1024

W_A = H_A * HEAD_DIM
W_B = H_B * HEAD_DIM
W_KV_B = HKV_B * HEAD_DIM

IN_QA, IN_KA, IN_VA, IN_FA = 0, 512, 1024, 1536
IN_QB, IN_KB, IN_VB, IN_QI, IN_KI, IN_WI = 1544, 2056, 2184, 2312, 2568, 2632
IN_FC, IN_END = 2636, 4684
Z_KA, Z_VA = 0, 512
Z_FC, Z_IC, Z_QC, Z_GC = 1024, 1536, 2048, 2560
Z_KB, Z_VB, Z_KI, Z_FA = 3072, 3200, 3328, 3392
N_Z = 3456
T_QA, T_VA, T_QB, T_QI, T_VB, T_WI = 0, 512, 1024, 1536, 1792, 1920
T_ROWS = 1928


def _params(sem, vmem=VMEM_LIMIT):
    return pltpu.CompilerParams(dimension_semantics=sem, vmem_limit_bytes=vmem)


def _log_sigmoid(z):
    return jnp.minimum(z, 0.0) - jnp.log1p(jnp.exp(-jnp.abs(z)))


def _sigmoid(z):
    return 1.0 / (1.0 + jnp.exp(-z))


def _dot_nt(a, b):
    return lax.dot_general(a, b, (((1,), (1,)), ((), ())), preferred_element_type=F32)


def _dot_tn(a, b):
    return lax.dot_general(a, b, (((0,), (0,)), ((), ())), preferred_element_type=F32)


def _dot(a, b):
    return jnp.dot(a, b, preferred_element_type=F32)


def _split3(x):
    hi = x.astype(BF16)
    r1 = x - hi.astype(F32)
    mid = r1.astype(BF16)
    lo = (r1 - mid.astype(F32)).astype(BF16)
    return hi, mid, lo


def _layer_norm_rows(r, g, b):
    mu = jnp.mean(r, axis=-1, keepdims=True)
    rc = r - mu
    var = jnp.mean(rc * rc, axis=-1, keepdims=True)
    return rc * lax.rsqrt(var + LN_EPS) * g + b


def _proj_kernel(x_ref, w_ref, b_ref, wt_ref, bt_ref,
                 z_ref, ka_ref, kb_ref, ki_ref, qa_ref, va_ref, qb_ref, qi_ref, vb_ref, wi_ref):
    x = x_ref[...]
    z_ref[:, :Z_KI] = _dot(x, w_ref[:, :Z_KI]) + b_ref[:, :Z_KI]
    tail = _dot(x, w_ref[:, Z_KI:]) + b_ref[:, Z_KI:]
    col = lax.broadcasted_iota(jnp.int32, tail.shape, 1) + Z_KI
    tail = jnp.where(col >= Z_FA, jnp.where(col < Z_FA + H_A, _log_sigmoid(tail), tail), tail)
    z_ref[:, Z_KI:] = tail
    ka_ref[...] = z_ref[:, Z_KA:Z_KA + W_A].astype(BF16)
    kb_ref[...] = z_ref[:, Z_KB:Z_KB + W_KV_B].astype(BF16)
    ki_ref[...] = tail[:, :DI_B].astype(BF16)
    zt = _dot_nt(wt_ref[...], x) + bt_ref[...]
    qa_ref[0] = zt[T_QA:T_QA + W_A].astype(BF16)
    va_ref[0] = zt[T_VA:T_VA + W_A].astype(BF16)
    qb_ref[0] = zt[T_QB:T_QB + W_B].astype(BF16)
    qi_ref[0] = zt[T_QI:T_QI + HI_B * DI_B].astype(BF16)
    vb_ref[0] = zt[T_VB:T_VB + W_KV_B].astype(BF16)
    wi_ref[0] = zt[T_WI:T_WI + SUBLANES]


def _proj(xb, w, b, wt, bt, batch, tm):
    n, k = xb.shape
    t = n // batch
    assert t % tm == 0
    tpb = t // tm
    full = pl.Buffered(1)

    def tok(width):
        return pl.BlockSpec((tm, width), lambda i: (i, 0))

    def feat(rows):
        return pl.BlockSpec((1, rows, tm), lambda i: (i // tpb, 0, i % tpb))

    def feat_shape(rows, dtype):
        return jax.ShapeDtypeStruct((batch, rows, t), dtype)

    return pl.pallas_call(
        _proj_kernel,
        grid=(n // tm,),
        in_specs=[
            tok(k),
            pl.BlockSpec(w.shape, lambda i: (0, 0), pipeline_mode=full),
            pl.BlockSpec(b.shape, lambda i: (0, 0), pipeline_mode=full),
            pl.BlockSpec(wt.shape, lambda i: (0, 0), pipeline_mode=full),
            pl.BlockSpec(bt.shape, lambda i: (0, 0), pipeline_mode=full),
        ],
        out_specs=[tok(N_Z), tok(W_A), tok(W_KV_B), tok(DI_B),
                   feat(W_A), feat(W_A), feat(W_B), feat(HI_B * DI_B), feat(W_KV_B), feat(SUBLANES)],
        out_shape=[jax.ShapeDtypeStruct((n, N_Z), F32), jax.ShapeDtypeStruct((n, W_A), BF16),
                   jax.ShapeDtypeStruct((n, W_KV_B), BF16), jax.ShapeDtypeStruct((n, DI_B), BF16),
                   feat_shape(W_A, BF16), feat_shape(W_A, BF16), feat_shape(W_B, BF16),
                   feat_shape(HI_B * DI_B, BF16), feat_shape(W_KV_B, BF16), feat_shape(SUBLANES, F32)],
        compiler_params=_params(("parallel",)),
        name="proj",
    )(xb, w, b, wt, bt)


def _cumsum_kernel(x_ref, hi_ref, mid_ref, lo_ref):
    c = x_ref[0]
    length = c.shape[-1]
    lane = lax.broadcasted_iota(jnp.int32, c.shape, 1)
    s = 1
    while s < length:
        c = c + jnp.where(lane >= s, pltpu.roll(c, s, 1), 0.0)
        s *= 2
    hi, mid, lo = _split3(c)
    hi_ref[0] = hi.astype(F32)
    mid_ref[0] = mid.astype(F32)
    lo_ref[0] = lo.astype(F32)


def _cumsum(x):
    b, r, length = x.shape
    spec = pl.BlockSpec((1, r, length), lambda i: (i, 0, 0))
    return pl.pallas_call(
        _cumsum_kernel,
        grid=(b,),
        in_specs=[spec],
        out_specs=[spec, spec, spec],
        out_shape=[jax.ShapeDtypeStruct(x.shape, F32)] * 3,
        compiler_params=_params(("parallel",)),
        name="cumsum",
    )(x)


FOX_HPS = 2
FOX_CK_TERMS = 3


def _fox_kernel(qt_ref, k_ref, ckx_ref, vt_ref, o_ref, m_ref, l_ref, acc_ref, *, tq, tk, q_off):
    hps = FOX_HPS
    i = pl.program_id(2)
    q0 = q_off + i * tq
    nk = (q0 + tq + tk - 1) // tk
    n_full = (q0 + 1) // tk
    ck_row = lax.broadcasted_iota(jnp.int32, (LANES, tq), 0)
    zeros = jnp.zeros((HEAD_DIM, tq), BF16)
    q_cols = []
    for h in range(hps):
        d = ck_row - FOX_CK_TERMS * (pl.program_id(1) * hps + h)
        minus_one = jnp.where(d >= 0, jnp.where(d < FOX_CK_TERMS, -1.0, 0.0), 0.0).astype(BF16)
        q_parts = [zeros] * hps
        q_parts[h] = qt_ref[0, h]
        q_cols.append(jnp.concatenate(q_parts + [minus_one], axis=0))
    qt = jnp.concatenate(q_cols, axis=1)
    krow = lax.broadcasted_iota(jnp.int32, (tk, hps * tq), 0)
    qcol = q0 + (lax.broadcasted_iota(jnp.int32, (tk, hps * tq), 1) & (tq - 1))
    m_ref[...] = jnp.full((1, hps * tq), NEG_INF, F32)
    l_ref[...] = jnp.zeros((1, hps * tq), F32)
    acc_ref[...] = jnp.zeros((HEAD_DIM, hps * tq), F32)

    def tile(j, masked):
        ks = pl.multiple_of(j * tk, tk)
        kk = jnp.concatenate([k_ref[0, pl.ds(ks, tk), :], ckx_ref[0, pl.ds(ks, tk), :]], axis=1)
        s = _dot(kk, qt)
        if masked:
            s = jnp.where(krow + ks <= qcol, s, NEG_INF)
        m_old = m_ref[...]
        m_new = jnp.maximum(m_old, jnp.max(s, axis=0, keepdims=True))
        p = jnp.exp(s - m_new)
        alpha = jnp.exp(m_old - m_new)
        l_ref[...] = alpha * l_ref[...] + jnp.sum(p, axis=0, keepdims=True)
        p = p.astype(BF16)
        pv = [_dot(vt_ref[0, h, :, pl.ds(ks, tk)], p[:, h * tq:(h + 1) * tq]) for h in range(hps)]
        acc_ref[...] = alpha * acc_ref[...] + jnp.concatenate(pv, axis=1)
        m_ref[...] = m_new

    def full_tile(j, carry):
        tile(j, False)
        return carry

    def edge_tile(j, carry):
        tile(j, True)
        return carry

    lax.fori_loop(0, n_full, full_tile, 0)
    lax.fori_loop(n_full, nk, edge_tile, 0)
    o = acc_ref[...] / l_ref[...]
    o_ref[0] = jnp.concatenate([o[:, h * tq:(h + 1) * tq].T for h in range(hps)], axis=-1).astype(o_ref.dtype)


def _fox(qt, k, ckx, vt, *, tq, tk, q_off):
    b, h, dh, t = qt.shape
    length = k.shape[1]
    hps = FOX_HPS
    assert hps * dh == LANES and FOX_CK_TERMS * h <= LANES and length % tk == 0 and tq & (tq - 1) == 0
    kern = functools.partial(_fox_kernel, tq=tq, tk=tk, q_off=q_off)
    return pl.pallas_call(
        kern,
        grid=(b, h // hps, t // tq),
        in_specs=[
            pl.BlockSpec((1, hps, dh, tq), lambda bi, hi, i: (bi, hi, 0, i)),
            pl.BlockSpec((1, length, LANES), lambda bi, hi, i: (bi, 0, hi)),
            pl.BlockSpec((1, length, LANES), lambda bi, hi, i: (bi, 0, 0)),
            pl.BlockSpec((1, hps, dh, length), lambda bi, hi, i: (bi, hi, 0, 0)),
        ],
        out_specs=pl.BlockSpec((1, tq, hps * dh), lambda bi, hi, i: (bi, i, hi)),
        out_shape=jax.ShapeDtypeStruct((b, t, h * dh), BF16),
        scratch_shapes=[
            pltpu.VMEM((1, hps * tq), F32),
            pltpu.VMEM((1, hps * tq), F32),
            pltpu.VMEM((dh, hps * tq), F32),
        ],
        compiler_params=_params(("parallel", "parallel", "arbitrary")),
        name="fox",
    )(qt, k, ckx, vt)


def _ck_table(cum3):
    b, _, length = cum3[0].shape
    ck = jnp.stack(cum3, axis=-1).transpose(0, 2, 1, 3).reshape(b, length, H_A * FOX_CK_TERMS)
    return jnp.concatenate([ck, jnp.zeros((b, length, LANES - H_A * FOX_CK_TERMS), F32)], axis=-1).astype(BF16)


def _with_cache_tok(cache, new, length):
    if cache is not None:
        new = jnp.concatenate([cache.reshape(cache.shape[0], cache.shape[1], -1).astype(BF16), new], axis=1)
    return _pad_time(new, length)


def _with_cache_feat(cache, new, n_heads, length):
    b, _, t = new.shape
    new = new.reshape(b, n_heads, -1, t)
    if cache is not None:
        new = jnp.concatenate([cache.transpose(0, 2, 3, 1).astype(BF16), new], axis=3)
    pad = length - new.shape[3]
    if pad:
        new = jnp.concatenate([new, jnp.zeros(new.shape[:3] + (pad,), BF16)], axis=3)
    return new


INT_MIN = -(2 ** 31)
KEY_NEG_INF = int(np.int32(np.uint32(0xFF800000) ^ np.uint32(0x7FFFFFFF)))
DSA_TK = 256


def _order_key(x):
    bits = lax.bitcast_convert_type(x, jnp.int32)
    return bits ^ ((bits >> 31) & jnp.int32(0x7FFFFFFF))


def _dsa_kernel(qit_ref, wit_ref, ki_ref, qt_ref, k_ref, vt_ref, nbt_ref, o_ref,
                sc_ref, m_ref, l_ref, acc_ref, *, tq, q_off, topk):
    tk = DSA_TK
    i = pl.program_id(1)
    q0 = q_off + i * tq
    i_abs = q0 // tk
    q_tiles = -(-tq // tk)
    nk = i_abs + q_tiles
    cols = G_B * tq

    wit = wit_ref[0]
    wrows = [wit[hh:hh + 1, :] for hh in range(HI_B)]
    q_chunk = (q0 + lax.broadcasted_iota(jnp.int32, (tk, tq), 1)) >> 6
    krow = lax.broadcasted_iota(jnp.int32, (tk, tq), 0)

    def score_tile(j, carry):
        ks = pl.multiple_of(j * tk, tk)
        ki = ki_ref[0, pl.ds(ks, tk), :]
        tot = jnp.zeros((tk, tq), F32)
        for hh in range(HI_B):
            tot = tot + wrows[hh] * jnp.maximum(_dot(ki, qit_ref[0, hh]), 0.0)
        tot = jnp.where(((krow + ks) >> 6) <= q_chunk, tot, NEG_INF)
        sc_ref[j] = _order_key(tot)
        return carry

    lax.fori_loop(0, nk, score_tile, 0)

    kf = float(topk)

    def count(pred_fn):
        def tile_hits(j):
            hit = jnp.where(pred_fn(sc_ref[j]), 1.0, 0.0)
            parts = [hit[r:r + SUBLANES] for r in range(0, tk, SUBLANES)]
            while len(parts) > 1:
                parts = [parts[a] + parts[a + 1] for a in range(0, len(parts), 2)]
            return parts[0]

        acc8 = lax.fori_loop(0, nk, lambda j, acc8: acc8 + tile_hits(j), jnp.zeros((SUBLANES, tq), F32))
        return jnp.sum(acc8, axis=0, keepdims=True)

    c0 = count(lambda key: key >= 0)
    thr = jnp.where(c0 >= kf, jnp.int32(0), jnp.int32(INT_MIN))
    for bit in range(30, -1, -1):
        cand = thr + jnp.int32(1 << bit)
        c = count(lambda key, cand=cand: key >= cand)
        thr = jnp.where(c >= kf, cand, thr)

    cnt_gt = count(lambda key: key > thr)
    short = thr <= jnp.int32(KEY_NEG_INF)
    thr_b = jnp.broadcast_to(jnp.maximum(thr, jnp.int32(KEY_NEG_INF)), (tk, tq))
    need = jnp.where(short, 0.0, kf - cnt_gt)

    tri = jnp.where(lax.broadcasted_iota(jnp.int32, (tk, tk), 1) <= lax.broadcasted_iota(jnp.int32, (tk, tk), 0),
                    1.0, 0.0).astype(BF16)
    m_ref[...] = jnp.full((HKV_B, 1, cols), NEG_INF, F32)
    l_ref[...] = jnp.zeros((HKV_B, 1, cols), F32)
    acc_ref[...] = jnp.zeros((HKV_B, HEAD_DIM, cols), F32)
    q_zero = jnp.zeros((HEAD_DIM, cols), BF16)
    q_ext = []
    for g in range(HKV_B):
        parts = [q_zero] * HKV_B
        parts[g] = qt_ref[0, g]
        q_ext.append(jnp.concatenate(parts, axis=0))

    n_far = jnp.maximum(i_abs - 1, 0)

    def attend_tile(j, eqc, near):
        ks = pl.multiple_of(j * tk, tk)
        key = sc_ref[j]
        gt = key > thr_b
        eq = key == thr_b
        eqf = jnp.where(eq, 1.0, 0.0)
        rank = _dot(tri, eqf.astype(BF16)) + eqc
        pen = jnp.where(gt, 0.0, jnp.where(eq, rank, 1e9))
        madd = jnp.where(pen <= need, 0.0, NEG_INF)
        madd = jnp.concatenate([madd] * G_B, axis=1)
        k_tile = k_ref[0, pl.ds(ks, tk), :]
        for g in range(HKV_B):
            s = _dot(k_tile, q_ext[g]) + madd
            if near:
                s = s + nbt_ref[j - i_abs + 1, g]
            m_old = m_ref[g]
            m_new = jnp.maximum(m_old, jnp.max(s, axis=0, keepdims=True))
            m_safe = jnp.where(m_new == NEG_INF, 0.0, m_new)
            p = jnp.exp(s - m_safe)
            alpha = jnp.exp(m_old - m_safe)
            l_ref[g] = alpha * l_ref[g] + jnp.sum(p, axis=0, keepdims=True)
            acc_ref[g] = alpha * acc_ref[g] + _dot(vt_ref[0, g, :, pl.ds(ks, tk)], p.astype(BF16))
            m_ref[g] = m_new
        return eqc + jnp.sum(eqf, axis=0, keepdims=True)

    eqc = lax.fori_loop(0, n_far, functools.partial(attend_tile, near=False), jnp.zeros((1, tq), F32))
    lax.fori_loop(n_far, nk, functools.partial(attend_tile, near=True), eqc)
    for g in range(HKV_B):
        o_ref[0, g] = (acc_ref[g] / l_ref[g]).astype(o_ref.dtype)


def _dsa(qit, wit, ki, qt, k, vt, nbt, *, tq, q_off, topk):
    b, _, _, t = qit.shape
    length = ki.shape[1]
    cols = G_B * tq
    n_bias = nbt.shape[0]
    tk = DSA_TK
    assert tk >= MAX_DISTANCE and q_off % tk == 0 and length % tk == 0 and length >= q_off + t
    assert HKV_B * HEAD_DIM == LANES and tq % CHUNK == 0 and (tq % tk == 0 or t == tq) and n_bias == 1 + -(-tq // tk)
    kern = functools.partial(_dsa_kernel, tq=tq, q_off=q_off, topk=topk)
    return pl.pallas_call(
        kern,
        grid=(b, t // tq),
        in_specs=[
            pl.BlockSpec((1, HI_B, DI_B, tq), lambda bi, i: (bi, 0, 0, i)),
            pl.BlockSpec((1, SUBLANES, tq), lambda bi, i: (bi, 0, i)),
            pl.BlockSpec((1, length, DI_B), lambda bi, i: (bi, 0, 0)),
            pl.BlockSpec((1, HKV_B, HEAD_DIM, cols), lambda bi, i: (bi, 0, 0, i)),
            pl.BlockSpec((1, length, LANES), lambda bi, i: (bi, 0, 0)),
            pl.BlockSpec((1, HKV_B, HEAD_DIM, length), lambda bi, i: (bi, 0, 0, 0)),
            pl.BlockSpec((n_bias, HKV_B, tk, cols), lambda bi, i: (0, 0, 0, 0), pipeline_mode=pl.Buffered(1)),
        ],
        out_specs=pl.BlockSpec((1, HKV_B, HEAD_DIM, cols), lambda bi, i: (bi, 0, 0, i)),
        out_shape=jax.ShapeDtypeStruct(qt.shape, BF16),
        scratch_shapes=[
            pltpu.VMEM((length // tk, tk, tq), jnp.int32),
            pltpu.VMEM((HKV_B, 1, cols), F32),
            pltpu.VMEM((HKV_B, 1, cols), F32),
            pltpu.VMEM((HKV_B, HEAD_DIM, cols), F32),
        ],
        compiler_params=_params(("parallel", "arbitrary")),
        name="dsa",
    )(qit, wit, ki, qt, k, vt, nbt)


HG_SUB = SUBLANES


def _hgrn_kernel(f_ref, i_ref, q_ref, g_ref, la_ref, l1_ref, oml_ref, ng_ref, s0_ref,
                 o_ref, sfin_ref, st_ref):
    c = pl.program_id(1)
    n_c = pl.num_programs(1)
    ct = CHUNK

    @pl.when(c == 0)
    def _():
        for hh in range(H_C):
            st_ref[hh] = s0_ref[0, hh].T

    zf = f_ref[0]
    a = la_ref[...]
    cc = l1_ref[...] + _log_sigmoid(zf)
    logf = jnp.maximum(a, cc) + jnp.log1p(jnp.exp(-jnp.abs(a - cc)))
    kk = oml_ref[...] * _sigmoid(-zf)

    r_i = lax.broadcasted_iota(jnp.int32, (ct, ct), 0)
    c_i = lax.broadcasted_iota(jnp.int32, (ct, ct), 1)
    low = jnp.where(c_i <= r_i, 1.0, 0.0).astype(BF16)
    hi, mid, lo = _split3(logf)
    bcum = _dot(low, hi) + _dot(low, mid) + _dot(low, lo)

    sub_row = lax.broadcasted_iota(jnp.int32, (HG_SUB, DK_C), 0)
    a_col = lax.broadcasted_iota(jnp.int32, (HG_SUB, ct), 1)
    n_sub = ct // HG_SUB

    for hh in range(H_C):
        cs = slice(hh * DK_C, (hh + 1) * DK_C)
        qh = q_ref[0, :, cs]
        kh = kk[:, cs]
        vh = i_ref[0, :, cs]
        bh = bcum[:, cs]
        a_rows = []
        for bi in range(n_sub):
            r0 = bi * HG_SUB
            q_blk = qh[r0:r0 + HG_SUB]
            b_blk = bh[r0:r0 + HG_SUB]
            k_blk = kh[r0:r0 + HG_SUB]
            if bi > 0:
                ref = bh[r0:r0 + 1]
                qp = q_blk * jnp.exp(b_blk - ref)
                kp = kh[:r0] * jnp.exp(ref - bh[:r0])
                kp = jnp.concatenate([kp, jnp.zeros((ct - r0, DK_C), F32)], axis=0)
                a_row = _dot_nt(qp.astype(BF16), kp.astype(BF16))
            else:
                a_row = jnp.zeros((HG_SUB, ct), F32)
            for s in range(HG_SUB):
                d = jnp.where(sub_row >= s, b_blk - b_blk[s:s + 1], NEG_INF)
                x = q_blk * jnp.exp(d) * k_blk[s:s + 1]
                col = jnp.sum(x, axis=-1, keepdims=True)
                a_row = jnp.where(a_col == r0 + s, col, a_row)
            a_rows.append(a_row)
        a_mat = jnp.concatenate(a_rows, axis=0)
        st = st_ref[hh]
        o = _dot(a_mat.astype(BF16), vh.astype(BF16)) + _dot_nt((qh * jnp.exp(bh)).astype(BF16), st.astype(BF16))
        bl = bh[ct - 1:ct]
        kdec = kh * jnp.exp(bl - bh)
        st_ref[hh] = st * jnp.exp(bl) + _dot_tn(vh.astype(BF16), kdec.astype(BF16))
        o = o * lax.rsqrt(jnp.mean(o * o, axis=-1, keepdims=True) + LN_EPS) * ng_ref[:, cs]
        gh = g_ref[0, :, cs]
        o_ref[0, :, cs] = (o * (gh * _sigmoid(gh))).astype(o_ref.dtype)

    @pl.when(c == n_c - 1)
    def _():
        for hh in range(H_C):
            sfin_ref[0, hh] = st_ref[hh].T


def _hgrn(z3, la, l1, oml, ng, s0):
    b, t, _ = z3.shape
    n_c = t // CHUNK

    def zspec(col):
        blk = col // W_C
        return pl.BlockSpec((1, CHUNK, W_C), lambda bi, ci: (bi, ci, blk))

    vec = pl.BlockSpec((1, W_C), lambda bi, ci: (0, 0))
    return pl.pallas_call(
        _hgrn_kernel,
        grid=(b, n_c),
        in_specs=[zspec(Z_FC), zspec(Z_IC), zspec(Z_QC), zspec(Z_GC), vec, vec, vec, vec,
                  pl.BlockSpec((1, H_C, DK_C, DK_C), lambda bi, ci: (bi, 0, 0, 0))],
        out_specs=[pl.BlockSpec((1, CHUNK, W_C), lambda bi, ci: (bi, ci, 0)),
                   pl.BlockSpec((1, H_C, DK_C, DK_C), lambda bi, ci: (bi, 0, 0, 0))],
        out_shape=[jax.ShapeDtypeStruct((b, t, W_C), BF16),
                   jax.ShapeDtypeStruct((b, H_C, DK_C, DK_C), F32)],
        scratch_shapes=[pltpu.VMEM((H_C, DK_C, DK_C), F32)],
        compiler_params=_params(("parallel", "arbitrary")),
        name="hgrn",
    )(z3, z3, z3, z3, la, l1, oml, ng, s0)


def _merge_kernel(x_ref, xb_ref, oa_ref, ob_ref, oc_ref, wg_ref, bg_ref, wa_ref, wb_ref, wc_ref, wo_ref,
                  g_ref, b_ref, y_ref, yb_ref):
    xb = xb_ref[...]
    d = D_MODEL
    mix = None
    for br, (o_ref, w_ref) in enumerate(((oa_ref, wa_ref), (ob_ref, wb_ref), (oc_ref, wc_ref))):
        gate = _sigmoid(_dot(xb, wg_ref[:, br * d:(br + 1) * d]) + bg_ref[:, br * d:(br + 1) * d])
        term = gate * _dot(o_ref[...], w_ref[...])
        mix = term if mix is None else mix + term
    out = _dot(mix.astype(BF16), wo_ref[...])
    y = _layer_norm_rows(DEEPNORM_ALPHA * x_ref[...] + out, g_ref[...], b_ref[...])
    y_ref[...] = y
    yb_ref[...] = y.astype(BF16)


def _merge(x, xb, oa, ob, oc, wg, bg, wa, wb, wc, wo, g, b, tm):
    n, d = x.shape

    def row(width):
        return pl.BlockSpec((tm, width), lambda i: (i, 0))

    def full(arr):
        return pl.BlockSpec(arr.shape, lambda i: (0, 0))

    return pl.pallas_call(
        _merge_kernel,
        grid=(n // tm,),
        in_specs=[row(d), row(d), row(oa.shape[1]), row(ob.shape[1]), row(oc.shape[1]),
                  full(wg), full(bg), full(wa), full(wb), full(wc), full(wo), full(g), full(b)],
        out_specs=[row(d), row(d)],
        out_shape=[jax.ShapeDtypeStruct((n, d), F32), jax.ShapeDtypeStruct((n, d), BF16)],
        compiler_params=_params(("parallel",)),
        name="merge",
    )(x, xb, oa, ob, oc, wg, bg, wa, wb, wc, wo, g, b)


def _router_kernel(x_ref, w_ref, o_ref):
    x = x_ref[...]
    xh = x.astype(BF16)
    xl = (x - xh.astype(F32)).astype(BF16)
    w = w_ref[...]
    wh = w.astype(BF16)
    wl = (w - wh.astype(F32)).astype(BF16)
    logits = _dot(xh, wh) + _dot(xh, wl) + _dot(xl, wh)
    lane = lax.broadcasted_iota(jnp.int32, logits.shape, 1)
    logits = jnp.where(lane < N_EXPERTS, logits, NEG_INF)
    big = jnp.int32(LANES)
    m1 = jnp.max(logits, axis=-1, keepdims=True)
    i1 = jnp.min(jnp.where(logits == m1, lane, big), axis=-1, keepdims=True)
    rest = jnp.where(lane == i1, NEG_INF, logits)
    m2 = jnp.max(rest, axis=-1, keepdims=True)
    i2 = jnp.min(jnp.where(rest == m2, lane, big), axis=-1, keepdims=True)
    e2 = jnp.exp(m2 - m1)
    g1 = 1.0 / (1.0 + e2)
    g2 = e2 / (1.0 + e2)
    o_ref[...] = jnp.where(lane == i1, g1, 0.0) + jnp.where(lane == i2, g2, 0.0)


def _router(x, w, tm):
    n, d = x.shape
    return pl.pallas_call(
        _router_kernel,
        grid=(n // tm,),
        in_specs=[pl.BlockSpec((tm, d), lambda i: (i, 0)),
                  pl.BlockSpec(w.shape, lambda i: (0, 0))],
        out_specs=pl.BlockSpec((tm, LANES), lambda i: (i, 0)),
        out_shape=jax.ShapeDtypeStruct((n, LANES), F32),
        compiler_params=_params(("parallel",)),
        name="router",
    )(x, w)


def _ffn_kernel(x_ref, xb_ref, w1_ref, w3_ref, w2_ref, g_ref, b_ref, y_ref, yb_ref, acc_ref):
    f = pl.program_id(1)

    @pl.when(f == 0)
    def _():
        acc_ref[...] = jnp.zeros_like(acc_ref)

    xb = xb_ref[...]
    h1 = _dot(xb, w1_ref[...])
    h = (h1 * _sigmoid(h1)) * _dot(xb, w3_ref[...])
    acc_ref[...] += _dot(h.astype(BF16), w2_ref[...])

    @pl.when(f == pl.num_programs(1) - 1)
    def _():
        y = _layer_norm_rows(DEEPNORM_ALPHA * x_ref[...] + acc_ref[...], g_ref[...], b_ref[...])
        y_ref[...] = y
        yb_ref[...] = y.astype(BF16)


def _ffn(x, xb, w1, w3, w2, g, b, *, tm, tf):
    n, d = x.shape
    ff = w1.shape[1]
    return pl.pallas_call(
        _ffn_kernel,
        grid=(n // tm, ff // tf),
        in_specs=[
            pl.BlockSpec((tm, d), lambda i, f: (i, 0)),
            pl.BlockSpec((tm, d), lambda i, f: (i, 0)),
            pl.BlockSpec((d, tf), lambda i, f: (0, f)),
            pl.BlockSpec((d, tf), lambda i, f: (0, f)),
            pl.BlockSpec((tf, d), lambda i, f: (f, 0)),
            pl.BlockSpec((1, d), lambda i, f: (0, 0)),
            pl.BlockSpec((1, d), lambda i, f: (0, 0)),
        ],
        out_specs=[pl.BlockSpec((tm, d), lambda i, f: (i, 0)),
                   pl.BlockSpec((tm, d), lambda i, f: (i, 0))],
        out_shape=[jax.ShapeDtypeStruct((n, d), F32), jax.ShapeDtypeStruct((n, d), BF16)],
        scratch_shapes=[pltpu.VMEM((tm, d), F32)],
        compiler_params=_params(("parallel", "arbitrary")),
        name="ffn",
    )(x, xb, w1, w3, w2, g, b)


MOE_BS = 128


def _moe_kernel(x_ref, xb_ref, comb_ref, w1_ref, w3_ref, w2_ref, g_ref, b_ref, y_ref, yb_ref,
                rank_ref, rankt_ref, xg_ref, yacc_ref, *, tm):
    e = pl.program_id(1)
    f = pl.program_id(2)
    n_f = pl.num_programs(2)
    bs = MOE_BS

    @pl.when((e == 0) & (f == 0))
    def _():
        comb = comb_ref[...]
        routed = comb > 0.0
        r_i = lax.broadcasted_iota(jnp.int32, (tm, tm), 0)
        c_i = lax.broadcasted_iota(jnp.int32, (tm, tm), 1)
        strict_low = jnp.where(c_i < r_i, 1.0, 0.0).astype(BF16)
        rank = _dot(strict_low, jnp.where(routed, 1.0, 0.0).astype(BF16))
        rank = jnp.where(routed, rank, -1.0)
        rank_ref[...] = rank
        rankt_ref[...] = rank.T[:N_EXPERTS]
        y_ref[...] = jnp.zeros_like(y_ref)

    rrow = rankt_ref[pl.ds(e, 1), :]
    n_e = jnp.sum(jnp.where(rrow >= 0.0, 1.0, 0.0)).astype(jnp.int32)
    n_blk = (n_e + bs - 1) // bs

    @pl.when(f == 0)
    def _():
        slot = lax.broadcasted_iota(jnp.int32, (bs, tm), 0).astype(F32)

        def gather(blk, carry):
            r0 = pl.multiple_of(blk * bs, bs)
            pick = jnp.where(rrow - (blk * bs).astype(F32) == slot, 1.0, 0.0).astype(BF16)
            xg_ref[pl.ds(r0, bs), :] = _dot(pick, xb_ref[...]).astype(BF16)
            yacc_ref[pl.ds(r0, bs), :] = jnp.zeros((bs, yacc_ref.shape[1]), F32)
            return carry

        lax.fori_loop(0, n_blk, gather, 0)

    def expert_rows(r0, rows):
        xg = xg_ref[pl.ds(r0, rows), :]
        h1 = _dot(xg, w1_ref[0])
        h = (h1 * _sigmoid(h1)) * _dot(xg, w3_ref[0])
        yacc_ref[pl.ds(r0, rows), :] += _dot(h.astype(BF16), w2_ref[0])

    def expert_pair(pair, carry):
        expert_rows(pl.multiple_of(pair * (2 * bs), 2 * bs), 2 * bs)
        return carry

    n_pair = n_blk // 2
    lax.fori_loop(0, n_pair, expert_pair, 0)

    @pl.when(n_blk % 2 == 1)
    def _():
        expert_rows(pl.multiple_of(n_pair * (2 * bs), bs), bs)

    @pl.when(f == n_f - 1)
    def _():
        lane = lax.broadcasted_iota(jnp.int32, (tm, LANES), 1)
        rcol = jnp.sum(jnp.where(lane == e, rank_ref[...], 0.0), axis=-1, keepdims=True)
        gcol = jnp.sum(jnp.where(lane == e, comb_ref[...], 0.0), axis=-1, keepdims=True)
        slot = lax.broadcasted_iota(jnp.int32, (tm, bs), 1).astype(F32)

        def scatter(blk, carry):
            r0 = pl.multiple_of(blk * bs, bs)
            put = jnp.where(rcol - (blk * bs).astype(F32) == slot, 1.0, 0.0).astype(BF16)
            y_ref[...] += gcol * _dot(put, yacc_ref[pl.ds(r0, bs), :].astype(BF16))
            return carry

        lax.fori_loop(0, n_blk, scatter, 0)

    @pl.when((e == pl.num_programs(1) - 1) & (f == n_f - 1))
    def _():
        y = _layer_norm_rows(DEEPNORM_ALPHA * x_ref[...] + y_ref[...], g_ref[...], b_ref[...])
        y_ref[...] = y
        yb_ref[...] = y.astype(BF16)


def _moe(x, xb, comb, w1, w3, w2, g, b, *, tm, tf):
    n, d = x.shape
    n_e, _, ff = w1.shape
    assert tm % MOE_BS == 0 and n % tm == 0 and ff % tf == 0 and n_e == N_EXPERTS
    kern = functools.partial(_moe_kernel, tm=tm)
    one = pl.Buffered(1)
    return pl.pallas_call(
        kern,
        grid=(n // tm, n_e, ff // tf),
        in_specs=[
            pl.BlockSpec((tm, d), lambda i, e, f: (i, 0), pipeline_mode=one),
            pl.BlockSpec((tm, d), lambda i, e, f: (i, 0), pipeline_mode=one),
            pl.BlockSpec((tm, LANES), lambda i, e, f: (i, 0), pipeline_mode=one),
            pl.BlockSpec((1, d, tf), lambda i, e, f: (e, 0, f)),
            pl.BlockSpec((1, d, tf), lambda i, e, f: (e, 0, f)),
            pl.BlockSpec((1, tf, d), lambda i, e, f: (e, f, 0)),
            pl.BlockSpec((1, d), lambda i, e, f: (0, 0)),
            pl.BlockSpec((1, d), lambda i, e, f: (0, 0)),
        ],
        out_specs=[pl.BlockSpec((tm, d), lambda i, e, f: (i, 0)),
                   pl.BlockSpec((tm, d), lambda i, e, f: (i, 0))],
        out_shape=[jax.ShapeDtypeStruct((n, d), F32), jax.ShapeDtypeStruct((n, d), BF16)],
        scratch_shapes=[pltpu.VMEM((tm, LANES), F32), pltpu.VMEM((N_EXPERTS, tm), F32),
                        pltpu.VMEM((tm, d), BF16), pltpu.VMEM((tm, d), F32)],
        compiler_params=_params(("parallel", "arbitrary", "arbitrary")),
        name="moe",
    )(x, xb, comb, w1, w3, w2, g, b)


def _rel_bucket(rel):
    nb = N_BUCKETS // 2
    max_exact = nb // 2
    ret = jnp.where(rel > 0, nb, 0)
    n = jnp.abs(rel)
    nf = jnp.maximum(n, 1).astype(jnp.float32)
    large = max_exact + (jnp.log(nf / max_exact) / math.log(MAX_DISTANCE / max_exact) * (nb - max_exact)).astype(jnp.int32)
    large = jnp.minimum(large, nb - 1)
    return ret + jnp.where(n < max_exact, n, large)


def _bias_tiles(rel_bias, tq):
    tk = DSA_TK
    r = jnp.arange(tq, dtype=jnp.int32)[:, None]
    c = jnp.arange(tk, dtype=jnp.int32)[None, :]
    n = 1 + -(-tq // tk)
    rel = jnp.stack([c - r + (s - 1) * tk for s in range(n)], axis=0)
    table = rel_bias.astype(F32)
    onehot = _rel_bucket(rel)[..., None] == jnp.arange(N_BUCKETS, dtype=jnp.int32)
    vals = jnp.sum(jnp.where(onehot[..., None], table, 0.0), axis=-2)
    far = table[_rel_bucket(jnp.int32(-(2 * MAX_DISTANCE)))]
    vals = (vals - far).transpose(0, 3, 2, 1).reshape(n, HKV_B, G_B, tk, tq)
    return vals.transpose(0, 1, 3, 2, 4).reshape(n, HKV_B, tk, G_B * tq)


def _prep_w_in(w_in_l, b_in_l):
    def tok_cols(a):
        parts = [a[..., IN_KA:IN_FA], a[..., IN_FC:IN_END], a[..., IN_KB:IN_QI], a[..., IN_KI:IN_WI], a[..., IN_FA:IN_QB]]
        width = sum(p.shape[-1] for p in parts)
        return jnp.concatenate(parts + [jnp.zeros(a.shape[:-1] + (N_Z - width,), a.dtype)], axis=-1)

    def feat_cols(a):
        parts = [a[..., IN_QA:IN_KA] * HEAD_DIM ** -0.5, a[..., IN_VA:IN_FA], a[..., IN_QB:IN_KB] * HEAD_DIM ** -0.5,
                 a[..., IN_QI:IN_KI] * DI_B ** -0.5, a[..., IN_VB:IN_QI], a[..., IN_WI:IN_FC] * HI_B ** -0.5]
        width = sum(p.shape[-1] for p in parts)
        return jnp.concatenate(parts + [jnp.zeros(a.shape[:-1] + (T_ROWS - width,), a.dtype)], axis=-1)

    w = w_in_l.astype(F32)
    bias = b_in_l.astype(F32)[None, :]
    return (tok_cols(w).astype(BF16), tok_cols(bias),
            feat_cols(w).T.astype(BF16), feat_cols(bias).T)


def _pad_time(a, length):
    pad = length - a.shape[1]
    if pad == 0:
        return a
    return jnp.concatenate([a, jnp.zeros((a.shape[0], pad) + a.shape[2:], a.dtype)], axis=1)


def _heads_first(a, n_heads, scale=None):
    b, t, _ = a.shape
    a = a.reshape(b, t, n_heads, -1)
    if scale is not None:
        a = a * scale
    return a.transpose(0, 2, 1, 3).astype(BF16)


def _round_up(x, m):
    return (x + m - 1) // m * m


def _run_group(x, caches, wts):
    b, t, d = x.shape
    n = b * t
    p_len = 0 if caches is None else caches[0].shape[2]
    tm = min(512, n)
    xf = x.reshape(n, d).astype(F32)
    xb = xf.astype(BF16)

    fox_tq = min(512, t)
    fox_tk = 512
    l_fox = _round_up(p_len + t, fox_tk)
    dsa_tq = min(2 * LANES, t)
    l_dsa = _round_up(p_len + t, DSA_TK)
    topk = min(TOPK_MAX, (p_len + t) // 4)
    nb = _bias_tiles(wts["rel_bias"], dsa_tq)

    lb_cum = jnp.cumsum(jax.nn.softmax(wts["hgrn_lb"].astype(F32), axis=0), axis=0)

    states = []
    for l in range(DEPTH):
        w_tok, b_tok, w_feat, b_feat = _prep_w_in(wts["w_in"][l], wts["b_in"][l])
        z, ka_bf, kb_bf, ki_bf, qa_t, va_t, qb_t, qi_t, vb_t, wi_t = _proj(xb, w_tok, b_tok, w_feat, b_feat, b, min(512, t))
        z3 = z.reshape(b, t, N_Z)

        k_a = z3[..., Z_KA:Z_KA + W_A]
        v_a = z3[..., Z_VA:Z_VA + W_A]
        logf_a = z3[..., Z_FA:Z_FA + H_A]
        k_b = z3[..., Z_KB:Z_KB + W_KV_B]
        v_b = z3[..., Z_VB:Z_VB + W_KV_B]
        ki_b = z3[..., Z_KI:Z_KI + DI_B]
        if caches is None:
            c_ka = c_va = c_kb = c_vb = c_ki = None
            lf_all = logf_a
            s0 = jnp.zeros((b, H_C, DK_C, DK_C), F32)
        else:
            c_ka, c_va, c_lf, c_kb, c_vb, c_ki, c_s = [c[l] for c in caches]
            lf_all = jnp.concatenate([c_lf.astype(F32), logf_a], axis=1)
            s0 = c_s.astype(F32)

        cum3 = _cumsum(_pad_time(lf_all, l_fox).transpose(0, 2, 1))
        o_a = _fox(qa_t.reshape(b, H_A, HEAD_DIM, t), _with_cache_tok(c_ka, ka_bf.reshape(b, t, W_A), l_fox),
                   _ck_table(cum3), _with_cache_feat(c_va, va_t, H_A, l_fox), tq=fox_tq, tk=fox_tk, q_off=p_len)

        n_t = t // dsa_tq
        qt_b = qb_t.reshape(b, HKV_B, G_B, HEAD_DIM, n_t, dsa_tq).transpose(0, 1, 3, 4, 2, 5)
        qt_b = qt_b.reshape(b, HKV_B, HEAD_DIM, n_t * G_B * dsa_tq)
        o_b = _dsa(qi_t.reshape(b, HI_B, DI_B, t), wi_t, _with_cache_tok(c_ki, ki_bf.reshape(b, t, DI_B), l_dsa),
                   qt_b, _with_cache_tok(c_kb, kb_bf.reshape(b, t, W_KV_B), l_dsa),
                   _with_cache_feat(c_vb, vb_t, HKV_B, l_dsa), nb, tq=dsa_tq, q_off=p_len, topk=topk)
        o_b = o_b.reshape(b, HKV_B, HEAD_DIM, n_t, G_B, dsa_tq).transpose(0, 3, 5, 1, 4, 2).reshape(n, H_B * HEAD_DIM)

        lb = (lb_cum[l] - lb_cum[0]).reshape(1, W_C)
        o_c, s_new = _hgrn(z3, jnp.log(lb), jnp.log1p(-lb), 1.0 - lb,
                           wts["hgrn_norm_g"][l].reshape(1, W_C).astype(F32), s0)

        x1, x1b = _merge(xf, xb, o_a.reshape(n, -1), o_b, o_c.reshape(n, -1),
                         wts["w_gate"][l].astype(BF16), wts["b_gate"][l][None, :].astype(F32),
                         wts["w_o_fox"][l].astype(BF16), wts["w_o_dsa"][l].astype(BF16), wts["w_o_hgrn"][l].astype(BF16),
                         wts["w_out"][l].astype(BF16), wts["ln1_g"][l][None, :], wts["ln1_b"][l][None, :], min(256, n))

        g2, b2 = wts["ln2_g"][l][None, :], wts["ln2_b"][l][None, :]
        if l % 2 == 0:
            j = l // 2
            xf, xb = _ffn(x1, x1b, wts["ffn_w1"][j].astype(BF16), wts["ffn_w3"][j].astype(BF16),
                          wts["ffn_w2"][j].astype(BF16), g2, b2, tm=tm, tf=1408)
        else:
            j = l // 2
            rw = jnp.zeros((d, LANES), F32).at[:, :N_EXPERTS].set(wts["moe_router"][j].astype(F32))
            comb = _router(x1, rw, tm)
            xf, xb = _moe(x1, x1b, comb, wts["moe_w1"][j].astype(BF16), wts["moe_w3"][j].astype(BF16),
                          wts["moe_w2"][j].astype(BF16), g2, b2, tm=min(1024, n), tf=1792)

        states.append((k_a.reshape(b, t, H_A, HEAD_DIM), v_a.reshape(b, t, H_A, HEAD_DIM), logf_a,
                       k_b.reshape(b, t, HKV_B, HEAD_DIM), v_b.reshape(b, t, HKV_B, HEAD_DIM), ki_b, s_new))
    stacked = [jnp.stack([s[i] for s in states], axis=0) for i in range(7)]
    return xf.reshape(b, t, d), stacked


def kernel(x_prompt, x_sample, cache_fox_k, cache_fox_v, cache_fox_logf, cache_dsa_k, cache_dsa_v, cache_dsa_kidx, state_hgrn,
           w_in, b_in, w_gate, b_gate, w_o_fox, w_o_dsa, w_o_hgrn, w_out, hgrn_lb, hgrn_norm_g, rel_bias,
           ln1_g, ln1_b, ln2_g, ln2_b, ffn_w1, ffn_w3, ffn_w2, moe_router, moe_w1, moe_w3, moe_w2):
    wts = dict(w_in=w_in, b_in=b_in, w_gate=w_gate, b_gate=b_gate, w_o_fox=w_o_fox, w_o_dsa=w_o_dsa, w_o_hgrn=w_o_hgrn,
               w_out=w_out, hgrn_lb=hgrn_lb, hgrn_norm_g=hgrn_norm_g, rel_bias=rel_bias, ln1_g=ln1_g, ln1_b=ln1_b,
               ln2_g=ln2_g, ln2_b=ln2_b, ffn_w1=ffn_w1, ffn_w3=ffn_w3, ffn_w2=ffn_w2, moe_router=moe_router,
               moe_w1=moe_w1, moe_w3=moe_w3, moe_w2=moe_w2)
    caches = (cache_fox_k, cache_fox_v, cache_fox_logf, cache_dsa_k, cache_dsa_v, cache_dsa_kidx, state_hgrn)
    y_prompt, sp = _run_group(x_prompt, None, wts)
    y_sample, ss = _run_group(x_sample, caches, wts)
    return (y_prompt, y_sample, sp[0], sp[1], sp[2], sp[3], sp[4], sp[5], sp[6],
            ss[0], ss[1], ss[2], ss[3], ss[4], ss[5], ss[6])
```

```python
import functools
import math

import jax
import jax.numpy as jnp
import numpy as np
from jax import lax
from jax.experimental import pallas as pl
from jax.experimental.pallas import tpu as pltpu

F32 = jnp.float32
BF16 = jnp.bfloat16
NEG_INF = float("-inf")

D_MODEL = 1024
DEPTH = 2
CHUNK = 64
HEAD_DIM = 64
H_A = 8
H_B = 8
HKV_B = 2
G_B = H_B // HKV_B
HI_B = 4
DI_B = 64
TOPK_MAX = 256
H_C = 4
DK_C = 128
W_C = H_C * DK_C
N_BUCKETS = 32
MAX_DISTANCE = 128
N_EXPERTS = 8
DEEPNORM_ALPHA = (2.0 * DEPTH) ** 0.25
LN_EPS = 1e-5

LANES = 128
SUBLANES = 8
VMEM_LIMIT = 56 * 1024 * 1024

W_A = H_A * HEAD_DIM
W_B = H_B * HEAD_DIM
W_KV_B = HKV_B * HEAD_DIM

IN_QA, IN_KA, IN_VA, IN_FA = 0, 512, 1024, 1536
IN_QB, IN_KB, IN_VB, IN_QI, IN_KI, IN_WI = 1544, 2056, 2184, 2312, 2568, 2632
IN_FC, IN_END = 2636, 4684
Z_KA, Z_VA = 0, 512
Z_FC, Z_IC, Z_QC, Z_GC = 1024, 1536, 2048, 2560
Z_KB, Z_VB, Z_KI, Z_FA = 3072, 3200, 3328, 3392
N_Z = 3456
T_QA, T_VA, T_QB, T_QI, T_VB, T_WI = 0, 512, 1024, 1536, 1792, 1920
T_ROWS = 1928
V_ROWS = 80


def _params(sem, vmem=VMEM_LIMIT):
    return pltpu.CompilerParams(dimension_semantics=sem, vmem_limit_bytes=vmem)


def _log_sigmoid(z):
    return jnp.minimum(z, 0.0) - jnp.log1p(jnp.exp(-jnp.abs(z)))


def _sigmoid(z):
    return 1.0 / (1.0 + jnp.exp(-z))


def _dot_nt(a, b):
    return lax.dot_general(a, b, (((1,), (1,)), ((), ())), preferred_element_type=F32)


def _dot_tn(a, b):
    return lax.dot_general(a, b, (((0,), (0,)), ((), ())), preferred_element_type=F32)


def _dot(a, b):
    return jnp.dot(a, b, preferred_element_type=F32)


def _split3(x):
    hi = x.astype(BF16)
    r1 = x - hi.astype(F32)
    mid = r1.astype(BF16)
    lo = (r1 - mid.astype(F32)).astype(BF16)
    return hi, mid, lo


def _layer_norm_rows(r, g, b):
    mu = jnp.mean(r, axis=-1, keepdims=True)
    rc = r - mu
    var = jnp.mean(rc * rc, axis=-1, keepdims=True)
    return rc * lax.rsqrt(var + LN_EPS) * g + b


def _proj_kernel(x_ref, w_ref, b_ref, wt_ref, bt_ref,
                 z_ref, ka_ref, kb_ref, ki_ref, qa_ref, va_ref, qb_ref, qi_ref, vb_ref, wi_ref):
    x = x_ref[...]
    z_ref[:, :Z_KI] = _dot(x, w_ref[:, :Z_KI]) + b_ref[:, :Z_KI]
    tail = _dot(x, w_ref[:, Z_KI:]) + b_ref[:, Z_KI:]
    col = lax.broadcasted_iota(jnp.int32, tail.shape, 1) + Z_KI
    tail = jnp.where(col >= Z_FA, jnp.where(col < Z_FA + H_A, _log_sigmoid(tail), tail), tail)
    z_ref[:, Z_KI:] = tail
    ka_ref[...] = z_ref[:, Z_KA:Z_KA + W_A].astype(BF16)
    kb_ref[...] = z_ref[:, Z_KB:Z_KB + W_KV_B].astype(BF16)
    ki_ref[...] = tail[:, :DI_B].astype(BF16)
    zt = _dot_nt(wt_ref[...], x) + bt_ref[...]
    qa_ref[0] = zt[T_QA:T_QA + W_A].astype(BF16)
    qb_ref[0] = zt[T_QB:T_QB + W_B].astype(BF16)
    qi_ref[0] = zt[T_QI:T_QI + HI_B * DI_B].astype(BF16)
    wi_ref[0] = zt[T_WI:T_WI + SUBLANES]
    ones = jnp.ones((V_ROWS - HEAD_DIM, zt.shape[1]), BF16)
    for v_ref, row0, heads in ((va_ref, T_VA, H_A), (vb_ref, T_VB, HKV_B)):
        for h in range(heads):
            v_ref[0, h * V_ROWS:h * V_ROWS + HEAD_DIM] = zt[row0 + h * HEAD_DIM:row0 + (h + 1) * HEAD_DIM].astype(BF16)
            v_ref[0, h * V_ROWS + HEAD_DIM:(h + 1) * V_ROWS] = ones


def _proj(xb, w, b, wt, bt, batch, tm):
    n, k = xb.shape
    t = n // batch
    assert t % tm == 0
    tpb = t // tm
    full = pl.Buffered(1)

    def tok(width):
        return pl.BlockSpec((tm, width), lambda i: (i, 0))

    def feat(rows):
        return pl.BlockSpec((1, rows, tm), lambda i: (i // tpb, 0, i % tpb))

    def feat_shape(rows, dtype):
        return jax.ShapeDtypeStruct((batch, rows, t), dtype)

    return pl.pallas_call(
        _proj_kernel,
        grid=(n // tm,),
        in_specs=[
            tok(k),
            pl.BlockSpec(w.shape, lambda i: (0, 0), pipeline_mode=full),
            pl.BlockSpec(b.shape, lambda i: (0, 0), pipeline_mode=full),
            pl.BlockSpec(wt.shape, lambda i: (0, 0), pipeline_mode=full),
            pl.BlockSpec(bt.shape, lambda i: (0, 0), pipeline_mode=full),
        ],
        out_specs=[tok(N_Z), tok(W_A), tok(W_KV_B), tok(DI_B),
                   feat(W_A), feat(H_A * V_ROWS), feat(W_B), feat(HI_B * DI_B), feat(HKV_B * V_ROWS), feat(SUBLANES)],
        out_shape=[jax.ShapeDtypeStruct((n, N_Z), F32), jax.ShapeDtypeStruct((n, W_A), BF16),
                   jax.ShapeDtypeStruct((n, W_KV_B), BF16), jax.ShapeDtypeStruct((n, DI_B), BF16),
                   feat_shape(W_A, BF16), feat_shape(H_A * V_ROWS, BF16), feat_shape(W_B, BF16),
                   feat_shape(HI_B * DI_B, BF16), feat_shape(HKV_B * V_ROWS, BF16), feat_shape(SUBLANES, F32)],
        compiler_params=_params(("parallel",)),
        name="proj",
    )(xb, w, b, wt, bt)


def _cumsum_kernel(x_ref, hi_ref, mid_ref, lo_ref):
    c = x_ref[0]
    length = c.shape[-1]
    lane = lax.broadcasted_iota(jnp.int32, c.shape, 1)
    s = 1
    while s < length:
        c = c + jnp.where(lane >= s, pltpu.roll(c, s, 1), 0.0)
        s *= 2
    hi, mid, lo = _split3(c)
    hi_ref[0] = hi.astype(F32)
    mid_ref[0] = mid.astype(F32)
    lo_ref[0] = lo.astype(F32)


def _cumsum(x):
    b, r, length = x.shape
    spec = pl.BlockSpec((1, r, length), lambda i: (i, 0, 0))
    return pl.pallas_call(
        _cumsum_kernel,
        grid=(b,),
        in_specs=[spec],
        out_specs=[spec, spec, spec],
        out_shape=[jax.ShapeDtypeStruct(x.shape, F32)] * 3,
        compiler_params=_params(("parallel",)),
        name="cumsum",
    )(x)


FOX_HPS = 2
FOX_CK_TERMS = 3


def _fox_kernel(qt_ref, k_ref, ckx_ref, vt_ref, o_ref, m_ref, acc_ref, *, tq, tk, q_off):
    hps = FOX_HPS
    i = pl.program_id(2)
    q0 = q_off + i * tq
    nk = (q0 + tq + tk - 1) // tk
    n_full = (q0 + 1) // tk
    ck_row = lax.broadcasted_iota(jnp.int32, (LANES, tq), 0)
    zeros = jnp.zeros((HEAD_DIM, tq), BF16)
    q_cols = []
    for h in range(hps):
        d = ck_row - FOX_CK_TERMS * (pl.program_id(1) * hps + h)
        minus_one = jnp.where(d >= 0, jnp.where(d < FOX_CK_TERMS, -1.0, 0.0), 0.0).astype(BF16)
        q_parts = [zeros] * hps
        q_parts[h] = qt_ref[0, h]
        q_cols.append(jnp.concatenate(q_parts + [minus_one], axis=0))
    qt = jnp.concatenate(q_cols, axis=1)
    krow = lax.broadcasted_iota(jnp.int32, (tk, hps * tq), 0)
    qcol = q0 + (lax.broadcasted_iota(jnp.int32, (tk, hps * tq), 1) & (tq - 1))
    m_ref[...] = jnp.full((1, hps * tq), NEG_INF, F32)
    acc_ref[...] = jnp.zeros((V_ROWS, hps * tq), F32)

    def tile(j, masked):
        ks = pl.multiple_of(j * tk, tk)
        kk = jnp.concatenate([k_ref[0, pl.ds(ks, tk), :], ckx_ref[0, pl.ds(ks, tk), :]], axis=1)
        s = _dot(kk, qt)
        if masked:
            s = jnp.where(krow + ks <= qcol, s, NEG_INF)
        m_old = m_ref[...]
        m_new = jnp.maximum(m_old, jnp.max(s, axis=0, keepdims=True))
        p = jnp.exp(s - m_new).astype(BF16)
        pv = [_dot(vt_ref[0, h, :, pl.ds(ks, tk)], p[:, h * tq:(h + 1) * tq]) for h in range(hps)]
        acc_ref[...] = jnp.exp(m_old - m_new) * acc_ref[...] + jnp.concatenate(pv, axis=1)
        m_ref[...] = m_new

    def full_tile(j, carry):
        tile(j, False)
        return carry

    def edge_tile(j, carry):
        tile(j, True)
        return carry

    lax.fori_loop(0, n_full, full_tile, 0)
    lax.fori_loop(n_full, nk, edge_tile, 0)
    acc = acc_ref[...]
    o = acc[:HEAD_DIM] / acc[HEAD_DIM:HEAD_DIM + 1]
    o_ref[0] = jnp.concatenate([o[:, h * tq:(h + 1) * tq].T for h in range(hps)], axis=-1).astype(o_ref.dtype)


def _fox(qt, k, ckx, vt, *, tq, tk, q_off):
    b, h, dh, t = qt.shape
    length = k.shape[1]
    hps = FOX_HPS
    assert hps * dh == LANES and FOX_CK_TERMS * h <= LANES and length % tk == 0 and tq & (tq - 1) == 0
    kern = functools.partial(_fox_kernel, tq=tq, tk=tk, q_off=q_off)
    return pl.pallas_call(
        kern,
        grid=(b, h // hps, t // tq),
        in_specs=[
            pl.BlockSpec((1, hps, dh, tq), lambda bi, hi, i: (bi, hi, 0, i)),
            pl.BlockSpec((1, length, LANES), lambda bi, hi, i: (bi, 0, hi)),
            pl.BlockSpec((1, length, LANES), lambda bi, hi, i: (bi, 0, 0)),
            pl.BlockSpec((1, hps, V_ROWS, length), lambda bi, hi, i: (bi, hi, 0, 0)),
        ],
        out_specs=pl.BlockSpec((1, tq, hps * dh), lambda bi, hi, i: (bi, i, hi)),
        out_shape=jax.ShapeDtypeStruct((b, t, h * dh), BF16),
        scratch_shapes=[
            pltpu.VMEM((1, hps * tq), F32),
            pltpu.VMEM((V_ROWS, hps * tq), F32),
        ],
        compiler_params=_params(("parallel", "parallel", "arbitrary")),
        name="fox",
    )(qt, k, ckx, vt)


def _ck_table(cum3):
    b, _, length = cum3[0].shape
    ck = jnp.stack(cum3, axis=-1).transpose(0, 2, 1, 3).reshape(b, length, H_A * FOX_CK_TERMS)
    return jnp.concatenate([ck, jnp.zeros((b, length, LANES - H_A * FOX_CK_TERMS), F32)], axis=-1).astype(BF16)


def _with_cache_tok(cache, new, length):
    if cache is not None:
        new = jnp.concatenate([cache.reshape(cache.shape[0], cache.shape[1], -1).astype(BF16), new], axis=1)
    return _pad_time(new, length)


def _with_cache_feat(cache, new, n_heads, length):
    b, _, t = new.shape
    new = new.reshape(b, n_heads, V_ROWS, t)
    if cache is not None:
        old = cache.transpose(0, 2, 3, 1).astype(BF16)
        old = jnp.concatenate([old, jnp.ones(old.shape[:2] + (V_ROWS - HEAD_DIM, old.shape[3]), BF16)], axis=2)
        new = jnp.concatenate([old, new], axis=3)
    pad = length - new.shape[3]
    if pad:
        new = jnp.concatenate([new, jnp.zeros(new.shape[:3] + (pad,), BF16)], axis=3)
    return new


INT_MIN = -(2 ** 31)
KEY_NEG_INF = int(np.int32(np.uint32(0xFF800000) ^ np.uint32(0x7FFFFFFF)))
DSA_TK = 256


def _order_key(x):
    bits = lax.bitcast_convert_type(x, jnp.int32)
    return bits ^ ((bits >> 31) & jnp.int32(0x7FFFFFFF))


def _dsa_kernel(qit_ref, wit_ref, ki_ref, qt_ref, k_ref, vt_ref, nbt_ref, o_ref,
                sc_ref, m_ref, acc_ref, *, tq, q_off, topk):
    tk = DSA_TK
    i = pl.program_id(1)
    q0 = q_off + i * tq
    i_abs = q0 // tk
    q_tiles = -(-tq // tk)
    nk = i_abs + q_tiles
    cols = G_B * tq

    wit = wit_ref[0]
    wrows = [wit[hh:hh + 1, :] for hh in range(HI_B)]
    q_chunk = (q0 + lax.broadcasted_iota(jnp.int32, (tk, tq), 1)) >> 6
    krow = lax.broadcasted_iota(jnp.int32, (tk, tq), 0)

    def score_tile(j, carry):
        ks = pl.multiple_of(j * tk, tk)
        ki = ki_ref[0, pl.ds(ks, tk), :]
        tot = jnp.zeros((tk, tq), F32)
        for hh in range(HI_B):
            tot = tot + wrows[hh] * jnp.maximum(_dot(ki, qit_ref[0, hh]), 0.0)
        tot = jnp.where(((krow + ks) >> 6) <= q_chunk, tot, NEG_INF)
        sc_ref[j] = _order_key(tot)
        return carry

    lax.fori_loop(0, nk, score_tile, 0)

    kf = float(topk)

    def count(pred_fn):
        def tile_hits(j):
            hit = jnp.where(pred_fn(sc_ref[j]), 1.0, 0.0)
            parts = [hit[r:r + SUBLANES] for r in range(0, tk, SUBLANES)]
            while len(parts) > 1:
                parts = [parts[a] + parts[a + 1] for a in range(0, len(parts), 2)]
            return parts[0]

        acc8 = lax.fori_loop(0, nk, lambda j, acc8: acc8 + tile_hits(j), jnp.zeros((SUBLANES, tq), F32))
        return jnp.sum(acc8, axis=0, keepdims=True)

    c0 = count(lambda key: key >= 0)
    thr = jnp.where(c0 >= kf, jnp.int32(0), jnp.int32(INT_MIN))
    for bit in range(30, -1, -1):
        cand = thr + jnp.int32(1 << bit)
        c = count(lambda key, cand=cand: key >= cand)
        thr = jnp.where(c >= kf, cand, thr)

    cnt_gt = count(lambda key: key > thr)
    short = thr <= jnp.int32(KEY_NEG_INF)
    thr_b = jnp.broadcast_to(jnp.maximum(thr, jnp.int32(KEY_NEG_INF)), (tk, tq))
    need = jnp.where(short, 0.0, kf - cnt_gt)

    tri = jnp.where(lax.broadcasted_iota(jnp.int32, (tk, tk), 1) <= lax.broadcasted_iota(jnp.int32, (tk, tk), 0),
                    1.0, 0.0).astype(BF16)
    m_ref[...] = jnp.full((HKV_B, 1, cols), NEG_INF, F32)
    acc_ref[...] = jnp.zeros((HKV_B, V_ROWS, cols), F32)
    q_zero = jnp.zeros((HEAD_DIM, cols), BF16)
    q_ext = []
    for g in range(HKV_B):
        parts = [q_zero] * HKV_B
        parts[g] = qt_ref[0, g]
        q_ext.append(jnp.concatenate(parts, axis=0))

    n_far = jnp.maximum(i_abs - 1, 0)

    def attend_tile(j, eqc, near):
        ks = pl.multiple_of(j * tk, tk)
        key = sc_ref[j]
        gt = key > thr_b
        eq = key == thr_b
        eqf = jnp.where(eq, 1.0, 0.0)
        rank = _dot(tri, eqf.astype(BF16)) + eqc
        pen = jnp.where(gt, 0.0, jnp.where(eq, rank, 1e9))
        madd = jnp.where(pen <= need, 0.0, NEG_INF)
        madd = jnp.concatenate([madd] * G_B, axis=1)
        k_tile = k_ref[0, pl.ds(ks, tk), :]
        for g in range(HKV_B):
            s = _dot(k_tile, q_ext[g]) + madd
            if near:
                s = s + nbt_ref[j - i_abs + 1, g]
            m_old = m_ref[g]
            m_new = jnp.maximum(m_old, jnp.max(s, axis=0, keepdims=True))
            m_safe = jnp.where(m_new == NEG_INF, 0.0, m_new)
            p = jnp.exp(s - m_safe)
            acc_ref[g] = jnp.exp(m_old - m_safe) * acc_ref[g] + _dot(vt_ref[0, g, :, pl.ds(ks, tk)], p.astype(BF16))
            m_ref[g] = m_new
        return eqc + jnp.sum(eqf, axis=0, keepdims=True)

    eqc = lax.fori_loop(0, n_far, functools.partial(attend_tile, near=False), jnp.zeros((1, tq), F32))
    lax.fori_loop(n_far, nk, functools.partial(attend_tile, near=True), eqc)
    for g in range(HKV_B):
        acc = acc_ref[g]
        o_ref[0, g] = (acc[:HEAD_DIM] / acc[HEAD_DIM:HEAD_DIM + 1]).astype(o_ref.dtype)


def _dsa(qit, wit, ki, qt, k, vt, nbt, *, tq, q_off, topk):
    b, _, _, t = qit.shape
    length = ki.shape[1]
    cols = G_B * tq
    n_bias = nbt.shape[0]
    tk = DSA_TK
    assert tk >= MAX_DISTANCE and q_off % tk == 0 and length % tk == 0 and length >= q_off + t
    assert HKV_B * HEAD_DIM == LANES and tq % CHUNK == 0 and (tq % tk == 0 or t == tq) and n_bias == 1 + -(-tq // tk)
    kern = functools.partial(_dsa_kernel, tq=tq, q_off=q_off, topk=topk)
    return pl.pallas_call(
        kern,
        grid=(b, t // tq),
        in_specs=[
            pl.BlockSpec((1, HI_B, DI_B, tq), lambda bi, i: (bi, 0, 0, i)),
            pl.BlockSpec((1, SUBLANES, tq), lambda bi, i: (bi, 0, i)),
            pl.BlockSpec((1, length, DI_B), lambda bi, i: (bi, 0, 0)),
            pl.BlockSpec((1, HKV_B, HEAD_DIM, cols), lambda bi, i: (bi, 0, 0, i)),
            pl.BlockSpec((1, length, LANES), lambda bi, i: (bi, 0, 0)),
            pl.BlockSpec((1, HKV_B, V_ROWS, length), lambda bi, i: (bi, 0, 0, 0)),
            pl.BlockSpec((n_bias, HKV_B, tk, cols), lambda bi, i: (0, 0, 0, 0), pipeline_mode=pl.Buffered(1)),
        ],
        out_specs=pl.BlockSpec((1, HKV_B, HEAD_DIM, cols), lambda bi, i: (bi, 0, 0, i)),
        out_shape=jax.ShapeDtypeStruct(qt.shape, BF16),
        scratch_shapes=[
            pltpu.VMEM((length // tk, tk, tq), jnp.int32),
            pltpu.VMEM((HKV_B, 1, cols), F32),
            pltpu.VMEM((HKV_B, V_ROWS, cols), F32),
        ],
        compiler_params=_params(("parallel", "arbitrary")),
        name="dsa",
    )(qit, wit, ki, qt, k, vt, nbt)


HG_SUB = SUBLANES


def _hgrn_kernel(f_ref, i_ref, q_ref, g_ref, la_ref, l1_ref, oml_ref, ng_ref, s0_ref,
                 o_ref, sfin_ref, st_ref):
    c = pl.program_id(1)
    n_c = pl.num_programs(1)
    ct = CHUNK

    @pl.when(c == 0)
    def _():
        for hh in range(H_C):
            st_ref[hh] = s0_ref[0, hh].T

    zf = f_ref[0]
    a = la_ref[...]
    cc = l1_ref[...] + _log_sigmoid(zf)
    logf = jnp.maximum(a, cc) + jnp.log1p(jnp.exp(-jnp.abs(a - cc)))
    kk = oml_ref[...] * _sigmoid(-zf)

    r_i = lax.broadcasted_iota(jnp.int32, (ct, ct), 0)
    c_i = lax.broadcasted_iota(jnp.int32, (ct, ct), 1)
    low = jnp.where(c_i <= r_i, 1.0, 0.0).astype(BF16)
    hi, mid, lo = _split3(logf)
    bcum = _dot(low, hi) + _dot(low, mid) + _dot(low, lo)

    sub_row = lax.broadcasted_iota(jnp.int32, (HG_SUB, DK_C), 0)
    a_col = lax.broadcasted_iota(jnp.int32, (HG_SUB, ct), 1)
    n_sub = ct // HG_SUB

    for hh in range(H_C):
        cs = slice(hh * DK_C, (hh + 1) * DK_C)
        qh = q_ref[0, :, cs]
        kh = kk[:, cs]
        vh = i_ref[0, :, cs]
        bh = bcum[:, cs]
        a_rows = []
        for bi in range(n_sub):
            r0 = bi * HG_SUB
            q_blk = qh[r0:r0 + HG_SUB]
            b_blk = bh[r0:r0 + HG_SUB]
            k_blk = kh[r0:r0 + HG_SUB]
            if bi > 0:
                ref = bh[r0:r0 + 1]
                qp = q_blk * jnp.exp(b_blk - ref)
                kp = kh[:r0] * jnp.exp(ref - bh[:r0])
                kp = jnp.concatenate([kp, jnp.zeros((ct - r0, DK_C), F32)], axis=0)
                a_row = _dot_nt(qp.astype(BF16), kp.astype(BF16))
            else:
                a_row = jnp.zeros((HG_SUB, ct), F32)
            for s in range(HG_SUB):
                d = jnp.where(sub_row >= s, b_blk - b_blk[s:s + 1], NEG_INF)
                x = q_blk * jnp.exp(d) * k_blk[s:s + 1]
                col = jnp.sum(x, axis=-1, keepdims=True)
                a_row = jnp.where(a_col == r0 + s, col, a_row)
            a_rows.append(a_row)
        a_mat = jnp.concatenate(a_rows, axis=0)
        st = st_ref[hh]
        o = _dot(a_mat.astype(BF16), vh.astype(BF16)) + _dot_nt((qh * jnp.exp(bh)).astype(BF16), st.astype(BF16))
        bl = bh[ct - 1:ct]
        kdec = kh * jnp.exp(bl - bh)
        st_ref[hh] = st * jnp.exp(bl) + _dot_tn(vh.astype(BF16), kdec.astype(BF16))
        o = o * lax.rsqrt(jnp.mean(o * o, axis=-1, keepdims=True) + LN_EPS) * ng_ref[:, cs]
        gh = g_ref[0, :, cs]
        o_ref[0, :, cs] = (o * (gh * _sigmoid(gh))).astype(o_ref.dtype)

    @pl.when(c == n_c - 1)
    def _():
        for hh in range(H_C):
            sfin_ref[0, hh] = st_ref[hh].T


def _hgrn(z3, la, l1, oml, ng, s0):
    b, t, _ = z3.shape
    n_c = t // CHUNK

    def zspec(col):
        blk = col // W_C
        return pl.BlockSpec((1, CHUNK, W_C), lambda bi, ci: (bi, ci, blk))

    vec = pl.BlockSpec((1, W_C), lambda bi, ci: (0, 0))
    return pl.pallas_call(
        _hgrn_kernel,
        grid=(b, n_c),
        in_specs=[zspec(Z_FC), zspec(Z_IC), zspec(Z_QC), zspec(Z_GC), vec, vec, vec, vec,
                  pl.BlockSpec((1, H_C, DK_C, DK_C), lambda bi, ci: (bi, 0, 0, 0))],
        out_specs=[pl.BlockSpec((1, CHUNK, W_C), lambda bi, ci: (bi, ci, 0)),
                   pl.BlockSpec((1, H_C, DK_C, DK_C), lambda bi, ci: (bi, 0, 0, 0))],
        out_shape=[jax.ShapeDtypeStruct((b, t, W_C), BF16),
                   jax.ShapeDtypeStruct((b, H_C, DK_C, DK_C), F32)],
        scratch_shapes=[pltpu.VMEM((H_C, DK_C, DK_C), F32)],
        compiler_params=_params(("parallel", "arbitrary")),
        name="hgrn",
    )(z3, z3, z3, z3, la, l1, oml, ng, s0)


def _merge_kernel(x_ref, xb_ref, oa_ref, ob_ref, oc_ref, wg_ref, bg_ref, wa_ref, wb_ref, wc_ref, wo_ref,
                  g_ref, b_ref, y_ref, yb_ref):
    xb = xb_ref[...]
    d = D_MODEL
    mix = None
    for br, (o_ref, w_ref) in enumerate(((oa_ref, wa_ref), (ob_ref, wb_ref), (oc_ref, wc_ref))):
        gate = _sigmoid(_dot(xb, wg_ref[:, br * d:(br + 1) * d]) + bg_ref[:, br * d:(br + 1) * d])
        term = gate * _dot(o_ref[...], w_ref[...])
        mix = term if mix is None else mix + term
    out = _dot(mix.astype(BF16), wo_ref[...])
    y = _layer_norm_rows(DEEPNORM_ALPHA * x_ref[...] + out, g_ref[...], b_ref[...])
    y_ref[...] = y
    yb_ref[...] = y.astype(BF16)


def _merge(x, xb, oa, ob, oc, wg, bg, wa, wb, wc, wo, g, b, tm):
    n, d = x.shape

    def row(width):
        return pl.BlockSpec((tm, width), lambda i: (i, 0))

    def full(arr):
        return pl.BlockSpec(arr.shape, lambda i: (0, 0))

    return pl.pallas_call(
        _merge_kernel,
        grid=(n // tm,),
        in_specs=[row(d), row(d), row(oa.shape[1]), row(ob.shape[1]), row(oc.shape[1]),
                  full(wg), full(bg), full(wa), full(wb), full(wc), full(wo), full(g), full(b)],
        out_specs=[row(d), row(d)],
        out_shape=[jax.ShapeDtypeStruct((n, d), F32), jax.ShapeDtypeStruct((n, d), BF16)],
        compiler_params=_params(("parallel",)),
        name="merge",
    )(x, xb, oa, ob, oc, wg, bg, wa, wb, wc, wo, g, b)


def _router_kernel(x_ref, w_ref, o_ref):
    x = x_ref[...]
    xh = x.astype(BF16)
    xl = (x - xh.astype(F32)).astype(BF16)
    w = w_ref[...]
    wh = w.astype(BF16)
    wl = (w - wh.astype(F32)).astype(BF16)
    logits = _dot(xh, wh) + _dot(xh, wl) + _dot(xl, wh)
    lane = lax.broadcasted_iota(jnp.int32, logits.shape, 1)
    logits = jnp.where(lane < N_EXPERTS, logits, NEG_INF)
    big = jnp.int32(LANES)
    m1 = jnp.max(logits, axis=-1, keepdims=True)
    i1 = jnp.min(jnp.where(logits == m1, lane, big), axis=-1, keepdims=True)
    rest = jnp.where(lane == i1, NEG_INF, logits)
    m2 = jnp.max(rest, axis=-1, keepdims=True)
    i2 = jnp.min(jnp.where(rest == m2, lane, big), axis=-1, keepdims=True)
    e2 = jnp.exp(m2 - m1)
    g1 = 1.0 / (1.0 + e2)
    g2 = e2 / (1.0 + e2)
    o_ref[...] = jnp.where(lane == i1, g1, 0.0) + jnp.where(lane == i2, g2, 0.0)


def _router(x, w, tm):
    n, d = x.shape
    return pl.pallas_call(
        _router_kernel,
        grid=(n // tm,),
        in_specs=[pl.BlockSpec((tm, d), lambda i: (i, 0)),
                  pl.BlockSpec(w.shape, lambda i: (0, 0))],
        out_specs=pl.BlockSpec((tm, LANES), lambda i: (i, 0)),
        out_shape=jax.ShapeDtypeStruct((n, LANES), F32),
        compiler_params=_params(("parallel",)),
        name="router",
    )(x, w)


def _ffn_kernel(x_ref, xb_ref, w1_ref, w3_ref, w2_ref, g_ref, b_ref, y_ref, yb_ref, acc_ref):
    f = pl.program_id(1)

    @pl.when(f == 0)
    def _():
        acc_ref[...] = jnp.zeros_like(acc_ref)

    xb = xb_ref[...]
    h1 = _dot(xb, w1_ref[...])
    h = (h1 * _sigmoid(h1)) * _dot(xb, w3_ref[...])
    acc_ref[...] += _dot(h.astype(BF16), w2_ref[...])

    @pl.when(f == pl.num_programs(1) - 1)
    def _():
        y = _layer_norm_rows(DEEPNORM_ALPHA * x_ref[...] + acc_ref[...], g_ref[...], b_ref[...])
        y_ref[...] = y
        yb_ref[...] = y.astype(BF16)


def _ffn(x, xb, w1, w3, w2, g, b, *, tm, tf):
    n, d = x.shape
    ff = w1.shape[1]
    return pl.pallas_call(
        _ffn_kernel,
        grid=(n // tm, ff // tf),
        in_specs=[
            pl.BlockSpec((tm, d), lambda i, f: (i, 0)),
            pl.BlockSpec((tm, d), lambda i, f: (i, 0)),
            pl.BlockSpec((d, tf), lambda i, f: (0, f)),
            pl.BlockSpec((d, tf), lambda i, f: (0, f)),
            pl.BlockSpec((tf, d), lambda i, f: (f, 0)),
            pl.BlockSpec((1, d), lambda i, f: (0, 0)),
            pl.BlockSpec((1, d), lambda i, f: (0, 0)),
        ],
        out_specs=[pl.BlockSpec((tm, d), lambda i, f: (i, 0)),
                   pl.BlockSpec((tm, d), lambda i, f: (i, 0))],
        out_shape=[jax.ShapeDtypeStruct((n, d), F32), jax.ShapeDtypeStruct((n, d), BF16)],
        scratch_shapes=[pltpu.VMEM((tm, d), F32)],
        compiler_params=_params(("parallel", "arbitrary")),
        name="ffn",
    )(x, xb, w1, w3, w2, g, b)


MOE_BS = 128


def _moe_kernel(x_ref, xb_ref, comb_ref, w1_ref, w3_ref, w2_ref, g_ref, b_ref, y_ref, yb_ref,
                rank_ref, rankt_ref, xg_ref, yacc_ref, *, tm):
    e = pl.program_id(1)
    f = pl.program_id(2)
    n_f = pl.num_programs(2)
    bs = MOE_BS

    @pl.when((e == 0) & (f == 0))
    def _():
        comb = comb_ref[...]
        routed = comb > 0.0
        r_i = lax.broadcasted_iota(jnp.int32, (tm, tm), 0)
        c_i = lax.broadcasted_iota(jnp.int32, (tm, tm), 1)
        strict_low = jnp.where(c_i < r_i, 1.0, 0.0).astype(BF16)
        rank = _dot(strict_low, jnp.where(routed, 1.0, 0.0).astype(BF16))
        rank = jnp.where(routed, rank, -1.0)
        rank_ref[...] = rank
        rankt_ref[...] = rank.T[:N_EXPERTS]
        y_ref[...] = jnp.zeros_like(y_ref)

    rrow = rankt_ref[pl.ds(e, 1), :]
    n_e = jnp.sum(jnp.where(rrow >= 0.0, 1.0, 0.0)).astype(jnp.int32)
    n_blk = (n_e + bs - 1) // bs

    @pl.when(f == 0)
    def _():
        slot = lax.broadcasted_iota(jnp.int32, (bs, tm), 0).astype(F32)

        def gather(blk, carry):
            r0 = pl.multiple_of(blk * bs, bs)
            pick = jnp.where(rrow - (blk * bs).astype(F32) == slot, 1.0, 0.0).astype(BF16)
            xg_ref[pl.ds(r0, bs), :] = _dot(pick, xb_ref[...]).astype(BF16)
            yacc_ref[pl.ds(r0, bs), :] = jnp.zeros((bs, yacc_ref.shape[1]), F32)
            return carry

        lax.fori_loop(0, n_blk, gather, 0)

    def expert_rows(r0, rows):
        xg = xg_ref[pl.ds(r0, rows), :]
        h1 = _dot(xg, w1_ref[0])
        h = (h1 * _sigmoid(h1)) * _dot(xg, w3_ref[0])
        yacc_ref[pl.ds(r0, rows), :] += _dot(h.astype(BF16), w2_ref[0])

    def expert_pair(pair, carry):
        expert_rows(pl.multiple_of(pair * (2 * bs), 2 * bs), 2 * bs)
        return carry

    n_pair = n_blk // 2
    lax.fori_loop(0, n_pair, expert_pair, 0)

    @pl.when(n_blk % 2 == 1)
    def _():
        expert_rows(pl.multiple_of(n_pair * (2 * bs), bs), bs)

    @pl.when(f == n_f - 1)
    def _():
        lane = lax.broadcasted_iota(jnp.int32, (tm, LANES), 1)
        rcol = jnp.sum(jnp.where(lane == e, rank_ref[...], 0.0), axis=-1, keepdims=True)
        gcol = jnp.sum(jnp.where(lane == e, comb_ref[...], 0.0), axis=-1, keepdims=True)
        slot = lax.broadcasted_iota(jnp.int32, (tm, bs), 1).astype(F32)

        def scatter(blk, carry):
            r0 = pl.multiple_of(blk * bs, bs)
            put = jnp.where(rcol - (blk * bs).astype(F32) == slot, 1.0, 0.0).astype(BF16)
            y_ref[...] += gcol * _dot(put, yacc_ref[pl.ds(r0, bs), :].astype(BF16))
            return carry

        lax.fori_loop(0, n_blk, scatter, 0)

    @pl.when((e == pl.num_programs(1) - 1) & (f == n_f - 1))
    def _():
        y = _layer_norm_rows(DEEPNORM_ALPHA * x_ref[...] + y_ref[...], g_ref[...], b_ref[...])
        y_ref[...] = y
        yb_ref[...] = y.astype(BF16)


def _moe(x, xb, comb, w1, w3, w2, g, b, *, tm, tf):
    n, d = x.shape
    n_e, _, ff = w1.shape
    assert tm % MOE_BS == 0 and n % tm == 0 and ff % tf == 0 and n_e == N_EXPERTS
    kern = functools.partial(_moe_kernel, tm=tm)
    one = pl.Buffered(1)
    return pl.pallas_call(
        kern,
        grid=(n // tm, n_e, ff // tf),
        in_specs=[
            pl.BlockSpec((tm, d), lambda i, e, f: (i, 0), pipeline_mode=one),
            pl.BlockSpec((tm, d), lambda i, e, f: (i, 0), pipeline_mode=one),
            pl.BlockSpec((tm, LANES), lambda i, e, f: (i, 0), pipeline_mode=one),
            pl.BlockSpec((1, d, tf), lambda i, e, f: (e, 0, f)),
            pl.BlockSpec((1, d, tf), lambda i, e, f: (e, 0, f)),
            pl.BlockSpec((1, tf, d), lambda i, e, f: (e, f, 0)),
            pl.BlockSpec((1, d), lambda i, e, f: (0, 0)),
            pl.BlockSpec((1, d), lambda i, e, f: (0, 0)),
        ],
        out_specs=[pl.BlockSpec((tm, d), lambda i, e, f: (i, 0)),
                   pl.BlockSpec((tm, d), lambda i, e, f: (i, 0))],
        out_shape=[jax.ShapeDtypeStruct((n, d), F32), jax.ShapeDtypeStruct((n, d), BF16)],
        scratch_shapes=[pltpu.VMEM((tm, LANES), F32), pltpu.VMEM((N_EXPERTS, tm), F32),
                        pltpu.VMEM((tm, d), BF16), pltpu.VMEM((tm, d), F32)],
        compiler_params=_params(("parallel", "arbitrary", "arbitrary")),
        name="moe",
    )(x, xb, comb, w1, w3, w2, g, b)


def _rel_bucket(rel):
    nb = N_BUCKETS // 2
    max_exact = nb // 2
    ret = jnp.where(rel > 0, nb, 0)
    n = jnp.abs(rel)
    nf = jnp.maximum(n, 1).astype(jnp.float32)
    large = max_exact + (jnp.log(nf / max_exact) / math.log(MAX_DISTANCE / max_exact) * (nb - max_exact)).astype(jnp.int32)
    large = jnp.minimum(large, nb - 1)
    return ret + jnp.where(n < max_exact, n, large)


def _bias_tiles(rel_bias, tq):
    tk = DSA_TK
    r = jnp.arange(tq, dtype=jnp.int32)[:, None]
    c = jnp.arange(tk, dtype=jnp.int32)[None, :]
    n = 1 + -(-tq // tk)
    rel = jnp.stack([c - r + (s - 1) * tk for s in range(n)], axis=0)
    table = rel_bias.astype(F32)
    onehot = _rel_bucket(rel)[..., None] == jnp.arange(N_BUCKETS, dtype=jnp.int32)
    vals = jnp.sum(jnp.where(onehot[..., None], table, 0.0), axis=-2)
    far = table[_rel_bucket(jnp.int32(-(2 * MAX_DISTANCE)))]
    vals = (vals - far).transpose(0, 3, 2, 1).reshape(n, HKV_B, G_B, tk, tq)
    return vals.transpose(0, 1, 3, 2, 4).reshape(n, HKV_B, tk, G_B * tq)


def _prep_w_in(w_in_l, b_in_l):
    def tok_cols(a):
        parts = [a[..., IN_KA:IN_FA], a[..., IN_FC:IN_END], a[..., IN_KB:IN_QI], a[..., IN_KI:IN_WI], a[..., IN_FA:IN_QB]]
        width = sum(p.shape[-1] for p in parts)
        return jnp.concatenate(parts + [jnp.zeros(a.shape[:-1] + (N_Z - width,), a.dtype)], axis=-1)

    def feat_cols(a):
        parts = [a[..., IN_QA:IN_KA] * HEAD_DIM ** -0.5, a[..., IN_VA:IN_FA], a[..., IN_QB:IN_KB] * HEAD_DIM ** -0.5,
                 a[..., IN_QI:IN_KI] * DI_B ** -0.5, a[..., IN_VB:IN_QI], a[..., IN_WI:IN_FC] * HI_B ** -0.5]
        width = sum(p.shape[-1] for p in parts)
        return jnp.concatenate(parts + [jnp.zeros(a.shape[:-1] + (T_ROWS - width,), a.dtype)], axis=-1)

    w = w_in_l.astype(F32)
    bias = b_in_l.astype(F32)[None, :]
    return (tok_cols(w).astype(BF16), tok_cols(bias),
            feat_cols(w).T.astype(BF16), feat_cols(bias).T)


def _pad_time(a, length):
    pad = length - a.shape[1]
    if pad == 0:
        return a
    return jnp.concatenate([a, jnp.zeros((a.shape[0], pad) + a.shape[2:], a.dtype)], axis=1)


def _heads_first(a, n_heads, scale=None):
    b, t, _ = a.shape
    a = a.reshape(b, t, n_heads, -1)
    if scale is not None:
        a = a * scale
    return a.transpose(0, 2, 1, 3).astype(BF16)


def _round_up(x, m):
    return (x + m - 1) // m * m


def _run_group(x, caches, wts):
    b, t, d = x.shape
    n = b * t
    p_len = 0 if caches is None else caches[0].shape[2]
    tm = min(512, n)
    xf = x.reshape(n, d).astype(F32)
    xb = xf.astype(BF16)

    fox_tq = min(512, t)
    fox_tk = 512
    l_fox = _round_up(p_len + t, fox_tk)
    dsa_tq = min(256, t)
    l_dsa = _round_up(p_len + t, DSA_TK)
    topk = min(TOPK_MAX, (p_len + t) // 4)
    nb = _bias_tiles(wts["rel_bias"], dsa_tq)

    lb_cum = jnp.cumsum(jax.nn.softmax(wts["hgrn_lb"].astype(F32), axis=0), axis=0)

    states = []
    for l in range(DEPTH):
        w_tok, b_tok, w_feat, b_feat = _prep_w_in(wts["w_in"][l], wts["b_in"][l])
        z, ka_bf, kb_bf, ki_bf, qa_t, va_t, qb_t, qi_t, vb_t, wi_t = _proj(xb, w_tok, b_tok, w_feat, b_feat, b, min(512, t))
        z3 = z.reshape(b, t, N_Z)

        k_a = z3[..., Z_KA:Z_KA + W_A]
        v_a = z3[..., Z_VA:Z_VA + W_A]
        logf_a = z3[..., Z_FA:Z_FA + H_A]
        k_b = z3[..., Z_KB:Z_KB + W_KV_B]
        v_b = z3[..., Z_VB:Z_VB + W_KV_B]
        ki_b = z3[..., Z_KI:Z_KI + DI_B]
        if caches is None:
            c_ka = c_va = c_kb = c_vb = c_ki = None
            lf_all = logf_a
            s0 = jnp.zeros((b, H_C, DK_C, DK_C), F32)
        else:
            c_ka, c_va, c_lf, c_kb, c_vb, c_ki, c_s = [c[l] for c in caches]
            lf_all = jnp.concatenate([c_lf.astype(F32), logf_a], axis=1)
            s0 = c_s.astype(F32)

        cum3 = _cumsum(_pad_time(lf_all, l_fox).transpose(0, 2, 1))
        o_a = _fox(qa_t.reshape(b, H_A, HEAD_DIM, t), _with_cache_tok(c_ka, ka_bf.reshape(b, t, W_A), l_fox),
                   _ck_table(cum3), _with_cache_feat(c_va, va_t, H_A, l_fox), tq=fox_tq, tk=fox_tk, q_off=p_len)

        n_t = t // dsa_tq
        qt_b = qb_t.reshape(b, HKV_B, G_B, HEAD_DIM, n_t, dsa_tq).transpose(0, 1, 3, 4, 2, 5)
        qt_b = qt_b.reshape(b, HKV_B, HEAD_DIM, n_t * G_B * dsa_tq)
        o_b = _dsa(qi_t.reshape(b, HI_B, DI_B, t), wi_t, _with_cache_tok(c_ki, ki_bf.reshape(b, t, DI_B), l_dsa),
                   qt_b, _with_cache_tok(c_kb, kb_bf.reshape(b, t, W_KV_B), l_dsa),
                   _with_cache_feat(c_vb, vb_t, HKV_B, l_dsa), nb, tq=dsa_tq, q_off=p_len, topk=topk)
        o_b = o_b.reshape(b, HKV_B, HEAD_DIM, n_t, G_B, dsa_tq).transpose(0, 3, 5, 1, 4, 2).reshape(n, H_B * HEAD_DIM)

        lb = (lb_cum[l] - lb_cum[0]).reshape(1, W_C)
        o_c, s_new = _hgrn(z3, jnp.log(lb), jnp.log1p(-lb), 1.0 - lb,
                           wts["hgrn_norm_g"][l].reshape(1, W_C).astype(F32), s0)

        x1, x1b = _merge(xf, xb, o_a.reshape(n, -1), o_b, o_c.reshape(n, -1),
                         wts["w_gate"][l].astype(BF16), wts["b_gate"][l][None, :].astype(F32),
                         wts["w_o_fox"][l].astype(BF16), wts["w_o_dsa"][l].astype(BF16), wts["w_o_hgrn"][l].astype(BF16),
                         wts["w_out"][l].astype(BF16), wts["ln1_g"][l][None, :], wts["ln1_b"][l][None, :], min(256, n))

        g2, b2 = wts["ln2_g"][l][None, :], wts["ln2_b"][l][None, :]
        if l % 2 == 0:
            j = l // 2
            xf, xb = _ffn(x1, x1b, wts["ffn_w1"][j].astype(BF16), wts["ffn_w3"][j].astype(BF16),
                          wts["ffn_w2"][j].astype(BF16), g2, b2, tm=tm, tf=1408)
        else:
            j = l // 2
            rw = jnp.zeros((d, LANES), F32).at[:, :N_EXPERTS].set(wts["moe_router"][j].astype(F32))
            comb = _router(x1, rw, tm)
            xf, xb = _moe(x1, x1b, comb, wts["moe_w1"][j].astype(BF16), wts["moe_w3"][j].astype(BF16),
                          wts["moe_w2"][j].astype(BF16), g2, b2, tm=min(1024, n), tf=1792)

        states.append((k_a.reshape(b, t, H_A, HEAD_DIM), v_a.reshape(b, t, H_A, HEAD_DIM), logf_a,
                       k_b.reshape(b, t, HKV_B, HEAD_DIM), v_b.reshape(b, t, HKV_B, HEAD_DIM), ki_b, s_new))
    stacked = [jnp.stack([s[i] for s in states], axis=0) for i in range(7)]
    return xf.reshape(b, t, d), stacked


def kernel(x_prompt, x_sample, cache_fox_k, cache_fox_v, cache_fox_logf, cache_dsa_k, cache_dsa_v, cache_dsa_kidx, state_hgrn,
           w_in, b_in, w_gate, b_gate, w_o_fox, w_o_dsa, w_o_hgrn, w_out, hgrn_lb, hgrn_norm_g, rel_bias,
           ln1_g, ln1_b, ln2_g, ln2_b, ffn_w1, ffn_w3, ffn_w2, moe_router, moe_w1, moe_w3, moe_w2):
    wts = dict(w_in=w_in, b_in=b_in, w_gate=w_gate, b_gate=b_gate, w_o_fox=w_o_fox, w_o_dsa=w_o_dsa, w_o_hgrn=w_o_hgrn,
               w_out=w_out, hgrn_lb=hgrn_lb, hgrn_norm_g=hgrn_norm_g, rel_bias=rel_bias, ln1_g=ln1_g, ln1_b=ln1_b,
               ln2_g=ln2_g, ln2_b=ln2_b, ffn_w1=ffn_w1, ffn_w3=ffn_w3, ffn_w2=ffn_w2, moe_router=moe_router,
               moe_w1=moe_w1, moe_w3=moe_w3, moe_w2=moe_w2)
    caches = (cache_fox_k, cache_fox_v, cache_fox_logf, cache_dsa_k, cache_dsa_v, cache_dsa_kidx, state_hgrn)
    y_prompt, sp = _run_group(x_prompt, None, wts)
    y_sample, ss = _run_group(x_sample, caches, wts)
    return (y_prompt, y_sample, sp[0], sp[1], sp[2], sp[3], sp[4], sp[5], sp[6],
            ss[0], ss[1], ss[2], ss[3], ss[4], ss[5], ss[6])
```

```python
import functools
import math

import jax
import jax.numpy as jnp
import numpy as np
from jax import lax
from jax.experimental import pallas as pl
from jax.experimental.pallas import tpu as pltpu

F32 = jnp.float32
BF16 = jnp.bfloat16
NEG_INF = float("-inf")

D_MODEL = 1024
DEPTH = 2
CHUNK = 64
HEAD_DIM = 64
H_A = 8
H_B = 8
HKV_B = 2
G_B = H_B // HKV_B
HI_B = 4
DI_B = 64
TOPK_MAX = 256
H_C = 4
DK_C = 128
W_C = H_C * DK_C
N_BUCKETS = 32
MAX_DISTANCE = 128
N_EXPERTS = 8
DEEPNORM_ALPHA = (2.0 * DEPTH) ** 0.25
LN_EPS = 1e-5

LANES = 128
SUBLANES = 8
VMEM_LIMIT = 56 * 1024 * 1024

W_A = H_A * HEAD_DIM
W_B = H_B * HEAD_DIM
W_KV_B = HKV_B * HEAD_DIM

IN_QA, IN_KA, IN_VA, IN_FA = 0, 512, 1024, 1536
IN_QB, IN_KB, IN_VB, IN_QI, IN_KI, IN_WI = 1544, 2056, 2184, 2312, 2568, 2632
IN_FC, IN_END = 2636, 4684
Z_KA, Z_VA = 0, 512
Z_FC, Z_IC, Z_QC, Z_GC = 1024, 1536, 2048, 2560
Z_KB, Z_VB, Z_KI, Z_FA = 3072, 3200, 3328, 3392
N_Z = 3456
T_QA, T_VA, T_QB, T_QI, T_VB, T_WI = 0, 512, 1024, 1536, 1792, 1920
T_ROWS = 1928
V_ROWS = 80


def _params(sem, vmem=VMEM_LIMIT):
    return pltpu.CompilerParams(dimension_semantics=sem, vmem_limit_bytes=vmem)


def _log_sigmoid(z):
    return jnp.minimum(z, 0.0) - jnp.log1p(jnp.exp(-jnp.abs(z)))


def _sigmoid(z):
    return 1.0 / (1.0 + jnp.exp(-z))


def _dot_nt(a, b):
    return lax.dot_general(a, b, (((1,), (1,)), ((), ())), preferred_element_type=F32)


def _dot_tn(a, b):
    return lax.dot_general(a, b, (((0,), (0,)), ((), ())), preferred_element_type=F32)


def _dot(a, b):
    return jnp.dot(a, b, preferred_element_type=F32)


def _split3(x):
    hi = x.astype(BF16)
    r1 = x - hi.astype(F32)
    mid = r1.astype(BF16)
    lo = (r1 - mid.astype(F32)).astype(BF16)
    return hi, mid, lo


def _layer_norm_rows(r, g, b):
    mu = jnp.mean(r, axis=-1, keepdims=True)
    rc = r - mu
    var = jnp.mean(rc * rc, axis=-1, keepdims=True)
    return rc * lax.rsqrt(var + LN_EPS) * g + b


def _proj_kernel(x_ref, w_ref, b_ref, wt_ref, bt_ref,
                 z_ref, ka_ref, kb_ref, ki_ref, qa_ref, va_ref, qb_ref, qi_ref, vb_ref, wi_ref):
    x = x_ref[...]
    z_ref[:, :Z_KI] = _dot(x, w_ref[:, :Z_KI]) + b_ref[:, :Z_KI]
    tail = _dot(x, w_ref[:, Z_KI:]) + b_ref[:, Z_KI:]
    col = lax.broadcasted_iota(jnp.int32, tail.shape, 1) + Z_KI
    tail = jnp.where(col >= Z_FA, jnp.where(col < Z_FA + H_A, _log_sigmoid(tail), tail), tail)
    z_ref[:, Z_KI:] = tail
    ka_ref[...] = z_ref[:, Z_KA:Z_KA + W_A].astype(BF16)
    kb_ref[...] = z_ref[:, Z_KB:Z_KB + W_KV_B].astype(BF16)
    ki_ref[...] = tail[:, :DI_B].astype(BF16)
    zt = _dot_nt(wt_ref[...], x) + bt_ref[...]
    qa_ref[0] = zt[T_QA:T_QA + W_A].astype(BF16)
    qb_ref[0] = zt[T_QB:T_QB + W_B].astype(BF16)
    qi_ref[0] = zt[T_QI:T_QI + HI_B * DI_B].astype(BF16)
    wi_ref[0] = zt[T_WI:T_WI + SUBLANES]
    ones = jnp.ones((V_ROWS - HEAD_DIM, zt.shape[1]), BF16)
    for v_ref, row0, heads in ((va_ref, T_VA, H_A), (vb_ref, T_VB, HKV_B)):
        for h in range(heads):
            v_ref[0, h * V_ROWS:h * V_ROWS + HEAD_DIM] = zt[row0 + h * HEAD_DIM:row0 + (h + 1) * HEAD_DIM].astype(BF16)
            v_ref[0, h * V_ROWS + HEAD_DIM:(h + 1) * V_ROWS] = ones


def _proj(xb, w, b, wt, bt, batch, tm):
    n, k = xb.shape
    t = n // batch
    assert t % tm == 0
    tpb = t // tm
    full = pl.Buffered(1)

    def tok(width):
        return pl.BlockSpec((tm, width), lambda i: (i, 0))

    def feat(rows):
        return pl.BlockSpec((1, rows, tm), lambda i: (i // tpb, 0, i % tpb))

    def feat_shape(rows, dtype):
        return jax.ShapeDtypeStruct((batch, rows, t), dtype)

    return pl.pallas_call(
        _proj_kernel,
        grid=(n // tm,),
        in_specs=[
            tok(k),
            pl.BlockSpec(w.shape, lambda i: (0, 0), pipeline_mode=full),
            pl.BlockSpec(b.shape, lambda i: (0, 0), pipeline_mode=full),
            pl.BlockSpec(wt.shape, lambda i: (0, 0), pipeline_mode=full),
            pl.BlockSpec(bt.shape, lambda i: (0, 0), pipeline_mode=full),
        ],
        out_specs=[tok(N_Z), tok(W_A), tok(W_KV_B), tok(DI_B),
                   feat(W_A), feat(H_A * V_ROWS), feat(W_B), feat(HI_B * DI_B), feat(HKV_B * V_ROWS), feat(SUBLANES)],
        out_shape=[jax.ShapeDtypeStruct((n, N_Z), F32), jax.ShapeDtypeStruct((n, W_A), BF16),
                   jax.ShapeDtypeStruct((n, W_KV_B), BF16), jax.ShapeDtypeStruct((n, DI_B), BF16),
                   feat_shape(W_A, BF16), feat_shape(H_A * V_ROWS, BF16), feat_shape(W_B, BF16),
                   feat_shape(HI_B * DI_B, BF16), feat_shape(HKV_B * V_ROWS, BF16), feat_shape(SUBLANES, F32)],
        compiler_params=_params(("parallel",)),
        name="proj",
    )(xb, w, b, wt, bt)


def _cumsum_kernel(x_ref, hi_ref, mid_ref, lo_ref):
    c = x_ref[0]
    length = c.shape[-1]
    lane = lax.broadcasted_iota(jnp.int32, c.shape, 1)
    s = 1
    while s < length:
        c = c + jnp.where(lane >= s, pltpu.roll(c, s, 1), 0.0)
        s *= 2
    hi, mid, lo = _split3(c)
    hi_ref[0] = hi.astype(F32)
    mid_ref[0] = mid.astype(F32)
    lo_ref[0] = lo.astype(F32)


def _cumsum(x):
    b, r, length = x.shape
    spec = pl.BlockSpec((1, r, length), lambda i: (i, 0, 0))
    return pl.pallas_call(
        _cumsum_kernel,
        grid=(b,),
        in_specs=[spec],
        out_specs=[spec, spec, spec],
        out_shape=[jax.ShapeDtypeStruct(x.shape, F32)] * 3,
        compiler_params=_params(("parallel",)),
        name="cumsum",
    )(x)


FOX_HPS = 2
FOX_CK_TERMS = 3


def _fox_kernel(qt_ref, k_ref, ckx_ref, vt_ref, o_ref, m_ref, acc_ref, *, tq, tk, q_off):
    hps = FOX_HPS
    i = pl.program_id(2)
    q0 = q_off + i * tq
    nk = (q0 + tq + tk - 1) // tk
    n_full = (q0 + 1) // tk
    ck_row = lax.broadcasted_iota(jnp.int32, (LANES, tq), 0)
    zeros = jnp.zeros((HEAD_DIM, tq), BF16)
    q_cols = []
    for h in range(hps):
        d = ck_row - FOX_CK_TERMS * (pl.program_id(1) * hps + h)
        minus_one = jnp.where(d >= 0, jnp.where(d < FOX_CK_TERMS, -1.0, 0.0), 0.0).astype(BF16)
        q_parts = [zeros] * hps
        q_parts[h] = qt_ref[0, h]
        q_cols.append(jnp.concatenate(q_parts + [minus_one], axis=0))
    qt = jnp.concatenate(q_cols, axis=1)
    krow = lax.broadcasted_iota(jnp.int32, (tk, hps * tq), 0)
    qcol = q0 + (lax.broadcasted_iota(jnp.int32, (tk, hps * tq), 1) & (tq - 1))
    m_ref[...] = jnp.full((1, hps * tq), NEG_INF, F32)
    acc_ref[...] = jnp.zeros((V_ROWS, hps * tq), F32)

    def tile(j, masked):
        ks = pl.multiple_of(j * tk, tk)
        kk = jnp.concatenate([k_ref[0, pl.ds(ks, tk), :], ckx_ref[0, pl.ds(ks, tk), :]], axis=1)
        s = _dot(kk, qt)
        if masked:
            s = jnp.where(krow + ks <= qcol, s, NEG_INF)
        m_old = m_ref[...]
        m_new = jnp.maximum(m_old, jnp.max(s, axis=0, keepdims=True))
        p = jnp.exp(s - m_new).astype(BF16)
        pv = [_dot(vt_ref[0, h, :, pl.ds(ks, tk)], p[:, h * tq:(h + 1) * tq]) for h in range(hps)]
        acc_ref[...] = jnp.exp(m_old - m_new) * acc_ref[...] + jnp.concatenate(pv, axis=1)
        m_ref[...] = m_new

    def full_tile(j, carry):
        tile(j, False)
        return carry

    def edge_tile(j, carry):
        tile(j, True)
        return carry

    lax.fori_loop(0, n_full, full_tile, 0)
    lax.fori_loop(n_full, nk, edge_tile, 0)
    acc = acc_ref[...]
    o = acc[:HEAD_DIM] / acc[HEAD_DIM:HEAD_DIM + 1]
    o_ref[0] = jnp.concatenate([o[:, h * tq:(h + 1) * tq].T for h in range(hps)], axis=-1).astype(o_ref.dtype)


def _fox(qt, k, ckx, vt, *, tq, tk, q_off):
    b, h, dh, t = qt.shape
    length = k.shape[1]
    hps = FOX_HPS
    assert hps * dh == LANES and FOX_CK_TERMS * h <= LANES and length % tk == 0 and tq & (tq - 1) == 0
    kern = functools.partial(_fox_kernel, tq=tq, tk=tk, q_off=q_off)
    return pl.pallas_call(
        kern,
        grid=(b, h // hps, t // tq),
        in_specs=[
            pl.BlockSpec((1, hps, dh, tq), lambda bi, hi, i: (bi, hi, 0, i)),
            pl.BlockSpec((1, length, LANES), lambda bi, hi, i: (bi, 0, hi)),
            pl.BlockSpec((1, length, LANES), lambda bi, hi, i: (bi, 0, 0)),
            pl.BlockSpec((1, hps, V_ROWS, length), lambda bi, hi, i: (bi, hi, 0, 0)),
        ],
        out_specs=pl.BlockSpec((1, tq, hps * dh), lambda bi, hi, i: (bi, i, hi)),
        out_shape=jax.ShapeDtypeStruct((b, t, h * dh), BF16),
        scratch_shapes=[
            pltpu.VMEM((1, hps * tq), F32),
            pltpu.VMEM((V_ROWS, hps * tq), F32),
        ],
        compiler_params=_params(("parallel", "parallel", "arbitrary")),
        name="fox",
    )(qt, k, ckx, vt)


def _ck_table(cum3):
    b, _, length = cum3[0].shape
    ck = jnp.stack(cum3, axis=-1).transpose(0, 2, 1, 3).reshape(b, length, H_A * FOX_CK_TERMS)
    return jnp.concatenate([ck, jnp.zeros((b, length, LANES - H_A * FOX_CK_TERMS), F32)], axis=-1).astype(BF16)


def _with_cache_tok(cache, new, length):
    if cache is not None:
        new = jnp.concatenate([cache.reshape(cache.shape[0], cache.shape[1], -1).astype(BF16), new], axis=1)
    return _pad_time(new, length)


def _with_cache_feat(cache, new, n_heads, length):
    b, _, t = new.shape
    new = new.reshape(b, n_heads, V_ROWS, t)
    if cache is not None:
        old = cache.transpose(0, 2, 3, 1).astype(BF16)
        old = jnp.concatenate([old, jnp.ones(old.shape[:2] + (V_ROWS - HEAD_DIM, old.shape[3]), BF16)], axis=2)
        new = jnp.concatenate([old, new], axis=3)
    pad = length - new.shape[3]
    if pad:
        new = jnp.concatenate([new, jnp.zeros(new.shape[:3] + (pad,), BF16)], axis=3)
    return new


INT_MIN = -(2 ** 31)
KEY_NEG_INF = int(np.int32(np.uint32(0xFF800000) ^ np.uint32(0x7FFFFFFF)))
DSA_TK = 256


def _order_key(x):
    bits = lax.bitcast_convert_type(x, jnp.int32)
    return bits ^ ((bits >> 31) & jnp.int32(0x7FFFFFFF))


def _dsa_kernel(qit_ref, wit_ref, ki_ref, qt_ref, k_ref, vt_ref, nbt_ref, o_ref,
                sc_ref, m_ref, acc_ref, *, tq, q_off, topk):
    tk = DSA_TK
    i = pl.program_id(1)
    q0 = q_off + i * tq
    i_abs = q0 // tk
    q_tiles = -(-tq // tk)
    nk = i_abs + q_tiles
    cols = G_B * tq

    wit = wit_ref[0]
    wrows = [wit[hh:hh + 1, :] for hh in range(HI_B)]
    q_chunk = (q0 + lax.broadcasted_iota(jnp.int32, (tk, tq), 1)) >> 6
    krow = lax.broadcasted_iota(jnp.int32, (tk, tq), 0)

    def score_tile(j, carry):
        ks = pl.multiple_of(j * tk, tk)
        ki = ki_ref[0, pl.ds(ks, tk), :]
        tot = jnp.zeros((tk, tq), F32)
        for hh in range(HI_B):
            tot = tot + wrows[hh] * jnp.maximum(_dot(ki, qit_ref[0, hh]), 0.0)
        tot = jnp.where(((krow + ks) >> 6) <= q_chunk, tot, NEG_INF)
        sc_ref[j] = _order_key(tot)
        return carry

    lax.fori_loop(0, nk, score_tile, 0)

    kf = float(topk)

    def count(pred_fn):
        def tile_hits(j):
            hit = jnp.where(pred_fn(sc_ref[j]), 1.0, 0.0)
            parts = [hit[r:r + SUBLANES] for r in range(0, tk, SUBLANES)]
            while len(parts) > 1:
                parts = [parts[a] + parts[a + 1] for a in range(0, len(parts), 2)]
            return parts[0]

        acc8 = lax.fori_loop(0, nk, lambda j, acc8: acc8 + tile_hits(j), jnp.zeros((SUBLANES, tq), F32))
        return jnp.sum(acc8, axis=0, keepdims=True)

    c0 = count(lambda key: key >= 0)
    thr = jnp.where(c0 >= kf, jnp.int32(0), jnp.int32(INT_MIN))
    for bit in range(30, -1, -1):
        cand = thr + jnp.int32(1 << bit)
        c = count(lambda key, cand=cand: key >= cand)
        thr = jnp.where(c >= kf, cand, thr)

    cnt_gt = count(lambda key: key > thr)
    short = thr <= jnp.int32(KEY_NEG_INF)
    thr_b = jnp.broadcast_to(jnp.maximum(thr, jnp.int32(KEY_NEG_INF)), (tk, tq))
    need = jnp.where(short, 0.0, kf - cnt_gt)

    tri = jnp.where(lax.broadcasted_iota(jnp.int32, (tk, tk), 1) <= lax.broadcasted_iota(jnp.int32, (tk, tk), 0),
                    1.0, 0.0).astype(BF16)
    m_ref[...] = jnp.full((HKV_B, 1, cols), NEG_INF, F32)
    acc_ref[...] = jnp.zeros((HKV_B, V_ROWS, cols), F32)
    q_zero = jnp.zeros((HEAD_DIM, cols), BF16)
    q_ext = []
    for g in range(HKV_B):
        parts = [q_zero] * HKV_B
        parts[g] = qt_ref[0, g]
        q_ext.append(jnp.concatenate(parts, axis=0))

    n_far = jnp.maximum(i_abs - 1, 0)

    def attend_tile(j, eqc, near):
        ks = pl.multiple_of(j * tk, tk)
        key = sc_ref[j]
        gt = key > thr_b
        eq = key == thr_b
        eqf = jnp.where(eq, 1.0, 0.0)
        rank = _dot(tri, eqf.astype(BF16)) + eqc
        pen = jnp.where(gt, 0.0, jnp.where(eq, rank, 1e9))
        madd = jnp.where(pen <= need, 0.0, NEG_INF)
        madd = jnp.concatenate([madd] * G_B, axis=1)
        k_tile = k_ref[0, pl.ds(ks, tk), :]
        for g in range(HKV_B):
            s = _dot(k_tile, q_ext[g]) + madd
            if near:
                s = s + nbt_ref[j - i_abs + 1, g]
            m_old = m_ref[g]
            m_new = jnp.maximum(m_old, jnp.max(s, axis=0, keepdims=True))
            m_safe = jnp.where(m_new == NEG_INF, 0.0, m_new)
            p = jnp.exp(s - m_safe)
            acc_ref[g] = jnp.exp(m_old - m_safe) * acc_ref[g] + _dot(vt_ref[0, g, :, pl.ds(ks, tk)], p.astype(BF16))
            m_ref[g] = m_new
        return eqc + jnp.sum(eqf, axis=0, keepdims=True)

    eqc = lax.fori_loop(0, n_far, functools.partial(attend_tile, near=False), jnp.zeros((1, tq), F32))
    lax.fori_loop(n_far, nk, functools.partial(attend_tile, near=True), eqc)
    for g in range(HKV_B):
        acc = acc_ref[g]
        o_ref[0, g] = (acc[:HEAD_DIM] / acc[HEAD_DIM:HEAD_DIM + 1]).astype(o_ref.dtype)


def _dsa(qit, wit, ki, qt, k, vt, nbt, *, tq, q_off, topk):
    b, _, _, t = qit.shape
    length = ki.shape[1]
    cols = G_B * tq
    n_bias = nbt.shape[0]
    tk = DSA_TK
    assert tk >= MAX_DISTANCE and q_off % tk == 0 and length % tk == 0 and length >= q_off + t
    assert HKV_B * HEAD_DIM == LANES and tq % CHUNK == 0 and (tq % tk == 0 or t == tq) and n_bias == 1 + -(-tq // tk)
    kern = functools.partial(_dsa_kernel, tq=tq, q_off=q_off, topk=topk)
    return pl.pallas_call(
        kern,
        grid=(b, t // tq),
        in_specs=[
            pl.BlockSpec((1, HI_B, DI_B, tq), lambda bi, i: (bi, 0, 0, i)),
            pl.BlockSpec((1, SUBLANES, tq), lambda bi, i: (bi, 0, i)),
            pl.BlockSpec((1, length, DI_B), lambda bi, i: (bi, 0, 0)),
            pl.BlockSpec((1, HKV_B, HEAD_DIM, cols), lambda bi, i: (bi, 0, 0, i)),
            pl.BlockSpec((1, length, LANES), lambda bi, i: (bi, 0, 0)),
            pl.BlockSpec((1, HKV_B, V_ROWS, length), lambda bi, i: (bi, 0, 0, 0)),
            pl.BlockSpec((n_bias, HKV_B, tk, cols), lambda bi, i: (0, 0, 0, 0), pipeline_mode=pl.Buffered(1)),
        ],
        out_specs=pl.BlockSpec((1, HKV_B, HEAD_DIM, cols), lambda bi, i: (bi, 0, 0, i)),
        out_shape=jax.ShapeDtypeStruct(qt.shape, BF16),
        scratch_shapes=[
            pltpu.VMEM((length // tk, tk, tq), jnp.int32),
            pltpu.VMEM((HKV_B, 1, cols), F32),
            pltpu.VMEM((HKV_B, V_ROWS, cols), F32),
        ],
        compiler_params=_params(("parallel", "arbitrary")),
        name="dsa",
    )(qit, wit, ki, qt, k, vt, nbt)


HG_SUB = SUBLANES


def _hgrn_kernel(f_ref, i_ref, q_ref, g_ref, la_ref, l1_ref, oml_ref, ng_ref, s0_ref,
                 o_ref, sfin_ref, st_ref):
    c = pl.program_id(1)
    n_c = pl.num_programs(1)
    ct = CHUNK

    @pl.when(c == 0)
    def _():
        for hh in range(H_C):
            st_ref[hh] = s0_ref[0, hh].T

    zf = f_ref[0]
    a = la_ref[...]
    cc = l1_ref[...] + _log_sigmoid(zf)
    logf = jnp.maximum(a, cc) + jnp.log1p(jnp.exp(-jnp.abs(a - cc)))
    kk = oml_ref[...] * _sigmoid(-zf)

    r_i = lax.broadcasted_iota(jnp.int32, (ct, ct), 0)
    c_i = lax.broadcasted_iota(jnp.int32, (ct, ct), 1)
    low = jnp.where(c_i <= r_i, 1.0, 0.0).astype(BF16)
    hi, mid, lo = _split3(logf)
    bcum = _dot(low, hi) + _dot(low, mid) + _dot(low, lo)

    sub_row = lax.broadcasted_iota(jnp.int32, (HG_SUB, DK_C), 0)
    a_col = lax.broadcasted_iota(jnp.int32, (HG_SUB, ct), 1)
    n_sub = ct // HG_SUB

    for hh in range(H_C):
        cs = slice(hh * DK_C, (hh + 1) * DK_C)
        qh = q_ref[0, :, cs]
        kh = kk[:, cs]
        vh = i_ref[0, :, cs]
        bh = bcum[:, cs]
        a_rows = []
        for bi in range(n_sub):
            r0 = bi * HG_SUB
            q_blk = qh[r0:r0 + HG_SUB]
            b_blk = bh[r0:r0 + HG_SUB]
            k_blk = kh[r0:r0 + HG_SUB]
            if bi > 0:
                ref = bh[r0:r0 + 1]
                qp = q_blk * jnp.exp(b_blk - ref)
                kp = kh[:r0] * jnp.exp(ref - bh[:r0])
                kp = jnp.concatenate([kp, jnp.zeros((ct - r0, DK_C), F32)], axis=0)
                a_row = _dot_nt(qp.astype(BF16), kp.astype(BF16))
            else:
                a_row = jnp.zeros((HG_SUB, ct), F32)
            for s in range(HG_SUB):
                d = jnp.where(sub_row >= s, b_blk - b_blk[s:s + 1], NEG_INF)
                x = q_blk * jnp.exp(d) * k_blk[s:s + 1]
                col = jnp.sum(x, axis=-1, keepdims=True)
                a_row = jnp.where(a_col == r0 + s, col, a_row)
            a_rows.append(a_row)
        a_mat = jnp.concatenate(a_rows, axis=0)
        st = st_ref[hh]
        o = _dot(a_mat.astype(BF16), vh.astype(BF16)) + _dot_nt((qh * jnp.exp(bh)).astype(BF16), st.astype(BF16))
        bl = bh[ct - 1:ct]
        kdec = kh * jnp.exp(bl - bh)
        st_ref[hh] = st * jnp.exp(bl) + _dot_tn(vh.astype(BF16), kdec.astype(BF16))
        o = o * lax.rsqrt(jnp.mean(o * o, axis=-1, keepdims=True) + LN_EPS) * ng_ref[:, cs]
        gh = g_ref[0, :, cs]
        o_ref[0, :, cs] = (o * (gh * _sigmoid(gh))).astype(o_ref.dtype)

    @pl.when(c == n_c - 1)
    def _():
        for hh in range(H_C):
            sfin_ref[0, hh] = st_ref[hh].T


def _hgrn(z3, la, l1, oml, ng, s0):
    b, t, _ = z3.shape
    n_c = t // CHUNK

    def zspec(col):
        blk = col // W_C
        return pl.BlockSpec((1, CHUNK, W_C), lambda bi, ci: (bi, ci, blk))

    vec = pl.BlockSpec((1, W_C), lambda bi, ci: (0, 0))
    return pl.pallas_call(
        _hgrn_kernel,
        grid=(b, n_c),
        in_specs=[zspec(Z_FC), zspec(Z_IC), zspec(Z_QC), zspec(Z_GC), vec, vec, vec, vec,
                  pl.BlockSpec((1, H_C, DK_C, DK_C), lambda bi, ci: (bi, 0, 0, 0))],
        out_specs=[pl.BlockSpec((1, CHUNK, W_C), lambda bi, ci: (bi, ci, 0)),
                   pl.BlockSpec((1, H_C, DK_C, DK_C), lambda bi, ci: (bi, 0, 0, 0))],
        out_shape=[jax.ShapeDtypeStruct((b, t, W_C), BF16),
                   jax.ShapeDtypeStruct((b, H_C, DK_C, DK_C), F32)],
        scratch_shapes=[pltpu.VMEM((H_C, DK_C, DK_C), F32)],
        compiler_params=_params(("parallel", "arbitrary")),
        name="hgrn",
    )(z3, z3, z3, z3, la, l1, oml, ng, s0)


def _merge_kernel(x_ref, xb_ref, oa_ref, ob_ref, oc_ref, wg_ref, bg_ref, wa_ref, wb_ref, wc_ref, wo_ref,
                  g_ref, b_ref, y_ref, yb_ref):
    xb = xb_ref[...]
    d = D_MODEL
    mix = None
    for br, (o_ref, w_ref) in enumerate(((oa_ref, wa_ref), (ob_ref, wb_ref), (oc_ref, wc_ref))):
        gate = _sigmoid(_dot(xb, wg_ref[:, br * d:(br + 1) * d]) + bg_ref[:, br * d:(br + 1) * d])
        term = gate * _dot(o_ref[...], w_ref[...])
        mix = term if mix is None else mix + term
    out = _dot(mix.astype(BF16), wo_ref[...])
    y = _layer_norm_rows(DEEPNORM_ALPHA * x_ref[...] + out, g_ref[...], b_ref[...])
    y_ref[...] = y
    yb_ref[...] = y.astype(BF16)


def _merge(x, xb, oa, ob, oc, wg, bg, wa, wb, wc, wo, g, b, tm):
    n, d = x.shape

    def row(width):
        return pl.BlockSpec((tm, width), lambda i: (i, 0))

    def full(arr):
        return pl.BlockSpec(arr.shape, lambda i: (0, 0))

    return pl.pallas_call(
        _merge_kernel,
        grid=(n // tm,),
        in_specs=[row(d), row(d), row(oa.shape[1]), row(ob.shape[1]), row(oc.shape[1]),
                  full(wg), full(bg), full(wa), full(wb), full(wc), full(wo), full(g), full(b)],
        out_specs=[row(d), row(d)],
        out_shape=[jax.ShapeDtypeStruct((n, d), F32), jax.ShapeDtypeStruct((n, d), BF16)],
        compiler_params=_params(("parallel",)),
        name="merge",
    )(x, xb, oa, ob, oc, wg, bg, wa, wb, wc, wo, g, b)


def _router_kernel(x_ref, w_ref, o_ref):
    x = x_ref[...]
    xh = x.astype(BF16)
    xl = (x - xh.astype(F32)).astype(BF16)
    w = w_ref[...]
    wh = w.astype(BF16)
    wl = (w - wh.astype(F32)).astype(BF16)
    logits = _dot(xh, wh) + _dot(xh, wl) + _dot(xl, wh)
    lane = lax.broadcasted_iota(jnp.int32, logits.shape, 1)
    logits = jnp.where(lane < N_EXPERTS, logits, NEG_INF)
    big = jnp.int32(LANES)
    m1 = jnp.max(logits, axis=-1, keepdims=True)
    i1 = jnp.min(jnp.where(logits == m1, lane, big), axis=-1, keepdims=True)
    rest = jnp.where(lane == i1, NEG_INF, logits)
    m2 = jnp.max(rest, axis=-1, keepdims=True)
    i2 = jnp.min(jnp.where(rest == m2, lane, big), axis=-1, keepdims=True)
    e2 = jnp.exp(m2 - m1)
    g1 = 1.0 / (1.0 + e2)
    g2 = e2 / (1.0 + e2)
    o_ref[...] = jnp.where(lane == i1, g1, 0.0) + jnp.where(lane == i2, g2, 0.0)


def _router(x, w, tm):
    n, d = x.shape
    return pl.pallas_call(
        _router_kernel,
        grid=(n // tm,),
        in_specs=[pl.BlockSpec((tm, d), lambda i: (i, 0)),
                  pl.BlockSpec(w.shape, lambda i: (0, 0))],
        out_specs=pl.BlockSpec((tm, LANES), lambda i: (i, 0)),
        out_shape=jax.ShapeDtypeStruct((n, LANES), F32),
        compiler_params=_params(("parallel",)),
        name="router",
    )(x, w)


def _ffn_kernel(x_ref, xb_ref, w1_ref, w3_ref, w2_ref, g_ref, b_ref, y_ref, yb_ref, acc_ref):
    f = pl.program_id(1)

    @pl.when(f == 0)
    def _():
        acc_ref[...] = jnp.zeros_like(acc_ref)

    xb = xb_ref[...]
    h1 = _dot(xb, w1_ref[...])
    h = (h1 * _sigmoid(h1)) * _dot(xb, w3_ref[...])
    acc_ref[...] += _dot(h.astype(BF16), w2_ref[...])

    @pl.when(f == pl.num_programs(1) - 1)
    def _():
        y = _layer_norm_rows(DEEPNORM_ALPHA * x_ref[...] + acc_ref[...], g_ref[...], b_ref[...])
        y_ref[...] = y
        yb_ref[...] = y.astype(BF16)


def _ffn(x, xb, w1, w3, w2, g, b, *, tm, tf):
    n, d = x.shape
    ff = w1.shape[1]
    return pl.pallas_call(
        _ffn_kernel,
        grid=(n // tm, ff // tf),
        in_specs=[
            pl.BlockSpec((tm, d), lambda i, f: (i, 0)),
            pl.BlockSpec((tm, d), lambda i, f: (i, 0)),
            pl.BlockSpec((d, tf), lambda i, f: (0, f)),
            pl.BlockSpec((d, tf), lambda i, f: (0, f)),
            pl.BlockSpec((tf, d), lambda i, f: (f, 0)),
            pl.BlockSpec((1, d), lambda i, f: (0, 0)),
            pl.BlockSpec((1, d), lambda i, f: (0, 0)),
        ],
        out_specs=[pl.BlockSpec((tm, d), lambda i, f: (i, 0)),
                   pl.BlockSpec((tm, d), lambda i, f: (i, 0))],
        out_shape=[jax.ShapeDtypeStruct((n, d), F32), jax.ShapeDtypeStruct((n, d), BF16)],
        scratch_shapes=[pltpu.VMEM((tm, d), F32)],
        compiler_params=_params(("parallel", "arbitrary")),
        name="ffn",
    )(x, xb, w1, w3, w2, g, b)


MOE_BS = 144


def _moe_slots(tm):
    return -(-tm // MOE_BS) * MOE_BS


def _moe_kernel(x_ref, xb_ref, comb_ref, w1_ref, w3_ref, w2_ref, g_ref, b_ref, y_ref, yb_ref,
                rank_ref, rankt_ref, xg_ref, yacc_ref, *, tm):
    e = pl.program_id(1)
    f = pl.program_id(2)
    n_f = pl.num_programs(2)
    bs = MOE_BS

    @pl.when((e == 0) & (f == 0))
    def _():
        comb = comb_ref[...]
        routed = comb > 0.0
        r_i = lax.broadcasted_iota(jnp.int32, (tm, tm), 0)
        c_i = lax.broadcasted_iota(jnp.int32, (tm, tm), 1)
        strict_low = jnp.where(c_i < r_i, 1.0, 0.0).astype(BF16)
        rank = _dot(strict_low, jnp.where(routed, 1.0, 0.0).astype(BF16))
        rank = jnp.where(routed, rank, -1.0)
        rank_ref[...] = rank
        rankt_ref[...] = rank.T[:N_EXPERTS]
        y_ref[...] = jnp.zeros_like(y_ref)

    rrow = rankt_ref[pl.ds(e, 1), :]
    n_e = jnp.sum(jnp.where(rrow >= 0.0, 1.0, 0.0)).astype(jnp.int32)
    n_blk = (n_e + bs - 1) // bs

    @pl.when(f == 0)
    def _():
        slot = lax.broadcasted_iota(jnp.int32, (bs, tm), 0).astype(F32)

        def gather(blk, carry):
            r0 = pl.multiple_of(blk * bs, bs)
            pick = jnp.where(rrow - (blk * bs).astype(F32) == slot, 1.0, 0.0).astype(BF16)
            xg_ref[pl.ds(r0, bs), :] = _dot(pick, xb_ref[...]).astype(BF16)
            yacc_ref[pl.ds(r0, bs), :] = jnp.zeros((bs, yacc_ref.shape[1]), F32)
            return carry

        lax.fori_loop(0, n_blk, gather, 0)

    def expert_rows(r0, rows):
        xg = xg_ref[pl.ds(r0, rows), :]
        h1 = _dot(xg, w1_ref[0])
        h = (h1 * _sigmoid(h1)) * _dot(xg, w3_ref[0])
        yacc_ref[pl.ds(r0, rows), :] += _dot(h.astype(BF16), w2_ref[0])

    def expert_pair(pair, carry):
        expert_rows(pl.multiple_of(pair * (2 * bs), 2 * bs), 2 * bs)
        return carry

    n_pair = n_blk // 2
    lax.fori_loop(0, n_pair, expert_pair, 0)

    @pl.when(n_blk % 2 == 1)
    def _():
        expert_rows(pl.multiple_of(n_pair * (2 * bs), bs), bs)

    @pl.when(f == n_f - 1)
    def _():
        lane = lax.broadcasted_iota(jnp.int32, (tm, LANES), 1)
        rcol = jnp.sum(jnp.where(lane == e, rank_ref[...], 0.0), axis=-1, keepdims=True)
        gcol = jnp.sum(jnp.where(lane == e, comb_ref[...], 0.0), axis=-1, keepdims=True)
        def scatter_rows(r0, rows):
            slot = lax.broadcasted_iota(jnp.int32, (tm, rows), 1).astype(F32)
            put = jnp.where(rcol - r0.astype(F32) == slot, 1.0, 0.0).astype(BF16)
            y_ref[...] += gcol * _dot(put, yacc_ref[pl.ds(r0, rows), :].astype(BF16))

        def scatter_pair(pair, carry):
            scatter_rows(pl.multiple_of(pair * (2 * bs), 2 * bs), 2 * bs)
            return carry

        lax.fori_loop(0, n_pair, scatter_pair, 0)

        @pl.when(n_blk % 2 == 1)
        def _():
            scatter_rows(pl.multiple_of(n_pair * (2 * bs), bs), bs)

    @pl.when((e == pl.num_programs(1) - 1) & (f == n_f - 1))
    def _():
        y = _layer_norm_rows(DEEPNORM_ALPHA * x_ref[...] + y_ref[...], g_ref[...], b_ref[...])
        y_ref[...] = y
        yb_ref[...] = y.astype(BF16)


def _moe(x, xb, comb, w1, w3, w2, g, b, *, tm, tf):
    n, d = x.shape
    n_e, _, ff = w1.shape
    assert n % tm == 0 and ff % tf == 0 and n_e == N_EXPERTS
    slots = _moe_slots(tm)
    kern = functools.partial(_moe_kernel, tm=tm)
    one = pl.Buffered(1)
    return pl.pallas_call(
        kern,
        grid=(n // tm, n_e, ff // tf),
        in_specs=[
            pl.BlockSpec((tm, d), lambda i, e, f: (i, 0), pipeline_mode=one),
            pl.BlockSpec((tm, d), lambda i, e, f: (i, 0), pipeline_mode=one),
            pl.BlockSpec((tm, LANES), lambda i, e, f: (i, 0), pipeline_mode=one),
            pl.BlockSpec((1, d, tf), lambda i, e, f: (e, 0, f)),
            pl.BlockSpec((1, d, tf), lambda i, e, f: (e, 0, f)),
            pl.BlockSpec((1, tf, d), lambda i, e, f: (e, f, 0)),
            pl.BlockSpec((1, d), lambda i, e, f: (0, 0)),
            pl.BlockSpec((1, d), lambda i, e, f: (0, 0)),
        ],
        out_specs=[pl.BlockSpec((tm, d), lambda i, e, f: (i, 0)),
                   pl.BlockSpec((tm, d), lambda i, e, f: (i, 0))],
        out_shape=[jax.ShapeDtypeStruct((n, d), F32), jax.ShapeDtypeStruct((n, d), BF16)],
        scratch_shapes=[pltpu.VMEM((tm, LANES), F32), pltpu.VMEM((N_EXPERTS, tm), F32),
                        pltpu.VMEM((slots, d), BF16), pltpu.VMEM((slots, d), F32)],
        compiler_params=_params(("parallel", "arbitrary", "arbitrary")),
        name="moe",
    )(x, xb, comb, w1, w3, w2, g, b)


def _rel_bucket(rel):
    nb = N_BUCKETS // 2
    max_exact = nb // 2
    ret = jnp.where(rel > 0, nb, 0)
    n = jnp.abs(rel)
    nf = jnp.maximum(n, 1).astype(jnp.float32)
    large = max_exact + (jnp.log(nf / max_exact) / math.log(MAX_DISTANCE / max_exact) * (nb - max_exact)).astype(jnp.int32)
    large = jnp.minimum(large, nb - 1)
    return ret + jnp.where(n < max_exact, n, large)


def _bias_tiles(rel_bias, tq):
    tk = DSA_TK
    r = jnp.arange(tq, dtype=jnp.int32)[:, None]
    c = jnp.arange(tk, dtype=jnp.int32)[None, :]
    n = 1 + -(-tq // tk)
    rel = jnp.stack([c - r + (s - 1) * tk for s in range(n)], axis=0)
    table = rel_bias.astype(F32)
    onehot = _rel_bucket(rel)[..., None] == jnp.arange(N_BUCKETS, dtype=jnp.int32)
    vals = jnp.sum(jnp.where(onehot[..., None], table, 0.0), axis=-2)
    far = table[_rel_bucket(jnp.int32(-(2 * MAX_DISTANCE)))]
    vals = (vals - far).transpose(0, 3, 2, 1).reshape(n, HKV_B, G_B, tk, tq)
    return vals.transpose(0, 1, 3, 2, 4).reshape(n, HKV_B, tk, G_B * tq)


def _prep_w_in(w_in_l, b_in_l):
    def tok_cols(a):
        parts = [a[..., IN_KA:IN_FA], a[..., IN_FC:IN_END], a[..., IN_KB:IN_QI], a[..., IN_KI:IN_WI], a[..., IN_FA:IN_QB]]
        width = sum(p.shape[-1] for p in parts)
        return jnp.concatenate(parts + [jnp.zeros(a.shape[:-1] + (N_Z - width,), a.dtype)], axis=-1)

    def feat_cols(a):
        parts = [a[..., IN_QA:IN_KA] * HEAD_DIM ** -0.5, a[..., IN_VA:IN_FA], a[..., IN_QB:IN_KB] * HEAD_DIM ** -0.5,
                 a[..., IN_QI:IN_KI] * DI_B ** -0.5, a[..., IN_VB:IN_QI], a[..., IN_WI:IN_FC] * HI_B ** -0.5]
        width = sum(p.shape[-1] for p in parts)
        return jnp.concatenate(parts + [jnp.zeros(a.shape[:-1] + (T_ROWS - width,), a.dtype)], axis=-1)

    w = w_in_l.astype(F32)
    bias = b_in_l.astype(F32)[None, :]
    return (tok_cols(w).astype(BF16), tok_cols(bias),
            feat_cols(w).T.astype(BF16), feat_cols(bias).T)


def _pad_time(a, length):
    pad = length - a.shape[1]
    if pad == 0:
        return a
    return jnp.concatenate([a, jnp.zeros((a.shape[0], pad) + a.shape[2:], a.dtype)], axis=1)


def _round_up(x, m):
    return (x + m - 1) // m * m


def _run_group(x, caches, wts):
    b, t, d = x.shape
    n = b * t
    p_len = 0 if caches is None else caches[0].shape[2]
    tm = min(512, n)
    xf = x.reshape(n, d).astype(F32)
    xb = xf.astype(BF16)

    fox_tq = min(512, t)
    fox_tk = 512
    l_fox = _round_up(p_len + t, fox_tk)
    dsa_tq = min(256, t)
    l_dsa = _round_up(p_len + t, DSA_TK)
    topk = min(TOPK_MAX, (p_len + t) // 4)
    nb = _bias_tiles(wts["rel_bias"], dsa_tq)

    lb_cum = jnp.cumsum(jax.nn.softmax(wts["hgrn_lb"].astype(F32), axis=0), axis=0)

    states = []
    for l in range(DEPTH):
        w_tok, b_tok, w_feat, b_feat = _prep_w_in(wts["w_in"][l], wts["b_in"][l])
        z, ka_bf, kb_bf, ki_bf, qa_t, va_t, qb_t, qi_t, vb_t, wi_t = _proj(xb, w_tok, b_tok, w_feat, b_feat, b, min(512, t))
        z3 = z.reshape(b, t, N_Z)

        k_a = z3[..., Z_KA:Z_KA + W_A]
        v_a = z3[..., Z_VA:Z_VA + W_A]
        logf_a = z3[..., Z_FA:Z_FA + H_A]
        k_b = z3[..., Z_KB:Z_KB + W_KV_B]
        v_b = z3[..., Z_VB:Z_VB + W_KV_B]
        ki_b = z3[..., Z_KI:Z_KI + DI_B]
        if caches is None:
            c_ka = c_va = c_kb = c_vb = c_ki = None
            lf_all = logf_a
            s0 = jnp.zeros((b, H_C, DK_C, DK_C), F32)
        else:
            c_ka, c_va, c_lf, c_kb, c_vb, c_ki, c_s = [c[l] for c in caches]
            lf_all = jnp.concatenate([c_lf.astype(F32), logf_a], axis=1)
            s0 = c_s.astype(F32)

        cum3 = _cumsum(_pad_time(lf_all, l_fox).transpose(0, 2, 1))
        o_a = _fox(qa_t.reshape(b, H_A, HEAD_DIM, t), _with_cache_tok(c_ka, ka_bf.reshape(b, t, W_A), l_fox),
                   _ck_table(cum3), _with_cache_feat(c_va, va_t, H_A, l_fox), tq=fox_tq, tk=fox_tk, q_off=p_len)

        n_t = t // dsa_tq
        qt_b = qb_t.reshape(b, HKV_B, G_B, HEAD_DIM, n_t, dsa_tq).transpose(0, 1, 3, 4, 2, 5)
        qt_b = qt_b.reshape(b, HKV_B, HEAD_DIM, n_t * G_B * dsa_tq)
        o_b = _dsa(qi_t.reshape(b, HI_B, DI_B, t), wi_t, _with_cache_tok(c_ki, ki_bf.reshape(b, t, DI_B), l_dsa),
                   qt_b, _with_cache_tok(c_kb, kb_bf.reshape(b, t, W_KV_B), l_dsa),
                   _with_cache_feat(c_vb, vb_t, HKV_B, l_dsa), nb, tq=dsa_tq, q_off=p_len, topk=topk)
        o_b = o_b.reshape(b, HKV_B, HEAD_DIM, n_t, G_B, dsa_tq).transpose(0, 3, 5, 1, 4, 2).reshape(n, H_B * HEAD_DIM)

        lb = (lb_cum[l] - lb_cum[0]).reshape(1, W_C)
        o_c, s_new = _hgrn(z3, jnp.log(lb), jnp.log1p(-lb), 1.0 - lb,
                           wts["hgrn_norm_g"][l].reshape(1, W_C).astype(F32), s0)

        x1, x1b = _merge(xf, xb, o_a.reshape(n, -1), o_b, o_c.reshape(n, -1),
                         wts["w_gate"][l].astype(BF16), wts["b_gate"][l][None, :].astype(F32),
                         wts["w_o_fox"][l].astype(BF16), wts["w_o_dsa"][l].astype(BF16), wts["w_o_hgrn"][l].astype(BF16),
                         wts["w_out"][l].astype(BF16), wts["ln1_g"][l][None, :], wts["ln1_b"][l][None, :], min(256, n))

        g2, b2 = wts["ln2_g"][l][None, :], wts["ln2_b"][l][None, :]
        if l % 2 == 0:
            j = l // 2
            xf, xb = _ffn(x1, x1b, wts["ffn_w1"][j].astype(BF16), wts["ffn_w3"][j].astype(BF16),
                          wts["ffn_w2"][j].astype(BF16), g2, b2, tm=tm, tf=1408)
        else:
            j = l // 2
            rw = jnp.zeros((d, LANES), F32).at[:, :N_EXPERTS].set(wts["moe_router"][j].astype(F32))
            comb = _router(x1, rw, tm)
            xf, xb = _moe(x1, x1b, comb, wts["moe_w1"][j].astype(BF16), wts["moe_w3"][j].astype(BF16),
                          wts["moe_w2"][j].astype(BF16), g2, b2, tm=min(1024, n), tf=1792)

        states.append((k_a.reshape(b, t, H_A, HEAD_DIM), v_a.reshape(b, t, H_A, HEAD_DIM), logf_a,
                       k_b.reshape(b, t, HKV_B, HEAD_DIM), v_b.reshape(b, t, HKV_B, HEAD_DIM), ki_b, s_new))
    stacked = [jnp.stack([s[i] for s in states], axis=0) for i in range(7)]
    return xf.reshape(b, t, d), stacked


def kernel(x_prompt, x_sample, cache_fox_k, cache_fox_v, cache_fox_logf, cache_dsa_k, cache_dsa_v, cache_dsa_kidx, state_hgrn,
           w_in, b_in, w_gate, b_gate, w_o_fox, w_o_dsa, w_o_hgrn, w_out, hgrn_lb, hgrn_norm_g, rel_bias,
           ln1_g, ln1_b, ln2_g, ln2_b, ffn_w1, ffn_w3, ffn_w2, moe_router, moe_w1, moe_w3, moe_w2):
    wts = dict(w_in=w_in, b_in=b_in, w_gate=w_gate, b_gate=b_gate, w_o_fox=w_o_fox, w_o_dsa=w_o_dsa, w_o_hgrn=w_o_hgrn,
               w_out=w_out, hgrn_lb=hgrn_lb, hgrn_norm_g=hgrn_norm_g, rel_bias=rel_bias, ln1_g=ln1_g, ln1_b=ln1_b,
               ln2_g=ln2_g, ln2_b=ln2_b, ffn_w1=ffn_w1, ffn_w3=ffn_w3, ffn_w2=ffn_w2, moe_router=moe_router,
               moe_w1=moe_w1, moe_w3=moe_w3, moe_w2=moe_w2)
    caches = (cache_fox_k, cache_fox_v, cache_fox_logf, cache_dsa_k, cache_dsa_v, cache_dsa_kidx, state_hgrn)
    y_prompt, sp = _run_group(x_prompt, None, wts)
    y_sample, ss = _run_group(x_sample, caches, wts)
    return (y_prompt, y_sample, sp[0], sp[1], sp[2], sp[3], sp[4], sp[5], sp[6],
            ss[0], ss[1], ss[2], ss[3], ss[4], ss[5], ss[6])
```

```python
import functools
import math

import jax
import jax.numpy as jnp
import numpy as np
from jax import lax
from jax.experimental import pallas as pl
from jax.experimental.pallas import tpu as pltpu

F32 = jnp.float32
BF16 = jnp.bfloat16
NEG_INF = float("-inf")

D_MODEL = 1024
DEPTH = 2
CHUNK = 64
HEAD_DIM = 64
H_A = 8
H_B = 8
HKV_B = 2
G_B = H_B // HKV_B
HI_B = 4
DI_B = 64
TOPK_MAX = 256
H_C = 4
DK_C = 128
W_C = H_C * DK_C
N_BUCKETS = 32
MAX_DISTANCE = 128
N_EXPERTS = 8
DEEPNORM_ALPHA = (2.0 * DEPTH) ** 0.25
LN_EPS = 1e-5

LANES = 128
SUBLANES = 8
VMEM_LIMIT = 56 * 1024 * 1024

W_A = H_A * HEAD_DIM
W_B = H_B * HEAD_DIM
W_KV_B = HKV_B * HEAD_DIM

IN_QA, IN_KA, IN_VA, IN_FA = 0, 512, 1024, 1536
IN_QB, IN_KB, IN_VB, IN_QI, IN_KI, IN_WI = 1544, 2056, 2184, 2312, 2568, 2632
IN_FC, IN_END = 2636, 4684
Z_KA, Z_VA = 0, 512
Z_FC, Z_IC, Z_QC, Z_GC = 1024, 1536, 2048, 2560
Z_KB, Z_VB, Z_KI, Z_FA = 3072, 3200, 3328, 3392
N_Z = 3456
T_QA, T_VA, T_QB, T_QI, T_VB, T_WI = 0, 512, 1024, 1536, 1792, 1920
T_ROWS = 1928
V_ROWS = 80


def _params(sem, vmem=VMEM_LIMIT):
    return pltpu.CompilerParams(dimension_semantics=sem, vmem_limit_bytes=vmem)


def _log_sigmoid(z):
    return jnp.minimum(z, 0.0) - jnp.log1p(jnp.exp(-jnp.abs(z)))


def _sigmoid(z):
    return 1.0 / (1.0 + jnp.exp(-z))


def _dot_nt(a, b):
    return lax.dot_general(a, b, (((1,), (1,)), ((), ())), preferred_element_type=F32)


def _dot_tn(a, b):
    return lax.dot_general(a, b, (((0,), (0,)), ((), ())), preferred_element_type=F32)


def _dot(a, b):
    return jnp.dot(a, b, preferred_element_type=F32)


def _split3(x):
    hi = x.astype(BF16)
    r1 = x - hi.astype(F32)
    mid = r1.astype(BF16)
    lo = (r1 - mid.astype(F32)).astype(BF16)
    return hi, mid, lo


def _layer_norm_rows(r, g, b):
    mu = jnp.mean(r, axis=-1, keepdims=True)
    rc = r - mu
    var = jnp.mean(rc * rc, axis=-1, keepdims=True)
    return rc * lax.rsqrt(var + LN_EPS) * g + b


def _proj_kernel(x_ref, w_ref, b_ref, wt_ref, bt_ref,
                 kaf_ref, vaf_ref, z_ref, kbf_ref, vbf_ref, kif_ref, lf_ref,
                 ka_ref, kb_ref, ki_ref, qa_ref, va_ref, qb_ref, qi_ref, vb_ref, wi_ref):
    x = x_ref[...]

    def cols(c0, width):
        return _dot(x, w_ref[:, c0:c0 + width]) + b_ref[:, c0:c0 + width]

    kaf_ref[...] = cols(Z_KA, W_A)
    vaf_ref[...] = cols(Z_VA, W_A)
    z_ref[...] = cols(Z_FC, 4 * W_C)
    kv_b = cols(Z_KB, 2 * W_KV_B)
    kbf_ref[...] = kv_b[:, :W_KV_B]
    vbf_ref[...] = kv_b[:, W_KV_B:]
    tail = cols(Z_KI, N_Z - Z_KI)
    kif_ref[...] = tail[:, :DI_B]
    lf_ref[...] = _log_sigmoid(tail[:, Z_FA - Z_KI:Z_FA - Z_KI + H_A])
    ka_ref[...] = kaf_ref[...].astype(BF16)
    kb_ref[...] = kv_b[:, :W_KV_B].astype(BF16)
    ki_ref[...] = tail[:, :DI_B].astype(BF16)
    zt = _dot_nt(wt_ref[...], x) + bt_ref[...]
    qa_ref[0] = zt[T_QA:T_QA + W_A].astype(BF16)
    qb_ref[0] = zt[T_QB:T_QB + W_B].astype(BF16)
    qi_ref[0] = zt[T_QI:T_QI + HI_B * DI_B].astype(BF16)
    wi_ref[0] = zt[T_WI:T_WI + SUBLANES]
    ones = jnp.ones((V_ROWS - HEAD_DIM, zt.shape[1]), BF16)
    for v_ref, row0, heads in ((va_ref, T_VA, H_A), (vb_ref, T_VB, HKV_B)):
        for h in range(heads):
            v_ref[0, h * V_ROWS:h * V_ROWS + HEAD_DIM] = zt[row0 + h * HEAD_DIM:row0 + (h + 1) * HEAD_DIM].astype(BF16)
            v_ref[0, h * V_ROWS + HEAD_DIM:(h + 1) * V_ROWS] = ones


def _proj(xb, w, b, wt, bt, batch, tm):
    n, k = xb.shape
    t = n // batch
    assert t % tm == 0
    tpb = t // tm
    full = pl.Buffered(1)

    def tok(width):
        return pl.BlockSpec((tm, width), lambda i: (i, 0))

    def feat(rows):
        return pl.BlockSpec((1, rows, tm), lambda i: (i // tpb, 0, i % tpb))

    def feat_shape(rows, dtype):
        return jax.ShapeDtypeStruct((batch, rows, t), dtype)

    return pl.pallas_call(
        _proj_kernel,
        grid=(n // tm,),
        in_specs=[
            tok(k),
            pl.BlockSpec(w.shape, lambda i: (0, 0), pipeline_mode=full),
            pl.BlockSpec(b.shape, lambda i: (0, 0), pipeline_mode=full),
            pl.BlockSpec(wt.shape, lambda i: (0, 0), pipeline_mode=full),
            pl.BlockSpec(bt.shape, lambda i: (0, 0), pipeline_mode=full),
        ],
        out_specs=[tok(W_A), tok(W_A), tok(4 * W_C), tok(W_KV_B), tok(W_KV_B), tok(DI_B), tok(H_A),
                   tok(W_A), tok(W_KV_B), tok(DI_B),
                   feat(W_A), feat(H_A * V_ROWS), feat(W_B), feat(HI_B * DI_B), feat(HKV_B * V_ROWS), feat(SUBLANES)],
        out_shape=[jax.ShapeDtypeStruct((n, W_A), F32), jax.ShapeDtypeStruct((n, W_A), F32),
                   jax.ShapeDtypeStruct((n, 4 * W_C), F32), jax.ShapeDtypeStruct((n, W_KV_B), F32),
                   jax.ShapeDtypeStruct((n, W_KV_B), F32), jax.ShapeDtypeStruct((n, DI_B), F32),
                   jax.ShapeDtypeStruct((n, H_A), F32),
                   jax.ShapeDtypeStruct((n, W_A), BF16),
                   jax.ShapeDtypeStruct((n, W_KV_B), BF16), jax.ShapeDtypeStruct((n, DI_B), BF16),
                   feat_shape(W_A, BF16), feat_shape(H_A * V_ROWS, BF16), feat_shape(W_B, BF16),
                   feat_shape(HI_B * DI_B, BF16), feat_shape(HKV_B * V_ROWS, BF16), feat_shape(SUBLANES, F32)],
        compiler_params=_params(("parallel",)),
        name="proj",
    )(xb, w, b, wt, bt)


def _cumsum_kernel(x_ref, hi_ref, mid_ref, lo_ref):
    c = x_ref[0]
    length = c.shape[-1]
    lane = lax.broadcasted_iota(jnp.int32, c.shape, 1)
    s = 1
    while s < length:
        c = c + jnp.where(lane >= s, pltpu.roll(c, s, 1), 0.0)
        s *= 2
    hi, mid, lo = _split3(c)
    hi_ref[0] = hi.astype(F32)
    mid_ref[0] = mid.astype(F32)
    lo_ref[0] = lo.astype(F32)


def _cumsum(x):
    b, r, length = x.shape
    spec = pl.BlockSpec((1, r, length), lambda i: (i, 0, 0))
    return pl.pallas_call(
        _cumsum_kernel,
        grid=(b,),
        in_specs=[spec],
        out_specs=[spec, spec, spec],
        out_shape=[jax.ShapeDtypeStruct(x.shape, F32)] * 3,
        compiler_params=_params(("parallel",)),
        name="cumsum",
    )(x)


FOX_HPS = 2
FOX_CK_TERMS = 3


def _fox_kernel(qt_ref, k_ref, ckx_ref, vt_ref, o_ref, m_ref, acc_ref, *, tq, tk, q_off):
    hps = FOX_HPS
    i = pl.program_id(2)
    q0 = q_off + i * tq
    nk = (q0 + tq + tk - 1) // tk
    n_full = (q0 + 1) // tk
    ck_row = lax.broadcasted_iota(jnp.int32, (LANES, tq), 0)
    zeros = jnp.zeros((HEAD_DIM, tq), BF16)
    q_cols = []
    for h in range(hps):
        d = ck_row - FOX_CK_TERMS * (pl.program_id(1) * hps + h)
        minus_one = jnp.where(d >= 0, jnp.where(d < FOX_CK_TERMS, -1.0, 0.0), 0.0).astype(BF16)
        q_parts = [zeros] * hps
        q_parts[h] = qt_ref[0, h]
        q_cols.append(jnp.concatenate(q_parts + [minus_one], axis=0))
    qt = jnp.concatenate(q_cols, axis=1)
    krow = lax.broadcasted_iota(jnp.int32, (tk, hps * tq), 0)
    qcol = q0 + (lax.broadcasted_iota(jnp.int32, (tk, hps * tq), 1) & (tq - 1))
    m_ref[...] = jnp.full((1, hps * tq), NEG_INF, F32)
    acc_ref[...] = jnp.zeros((V_ROWS, hps * tq), F32)

    def tile(j, masked):
        ks = pl.multiple_of(j * tk, tk)
        kk = jnp.concatenate([k_ref[0, pl.ds(ks, tk), :], ckx_ref[0, pl.ds(ks, tk), :]], axis=1)
        s = _dot(kk, qt)
        if masked:
            s = jnp.where(krow + ks <= qcol, s, NEG_INF)
        m_old = m_ref[...]
        m_new = jnp.maximum(m_old, jnp.max(s, axis=0, keepdims=True))
        p = jnp.exp(s - m_new).astype(BF16)
        pv = [_dot(vt_ref[0, h, :, pl.ds(ks, tk)], p[:, h * tq:(h + 1) * tq]) for h in range(hps)]
        acc_ref[...] = jnp.exp(m_old - m_new) * acc_ref[...] + jnp.concatenate(pv, axis=1)
        m_ref[...] = m_new

    def full_tile(j, carry):
        tile(j, False)
        return carry

    def edge_tile(j, carry):
        tile(j, True)
        return carry

    lax.fori_loop(0, n_full, full_tile, 0)
    lax.fori_loop(n_full, nk, edge_tile, 0)
    acc = acc_ref[...]
    o = acc[:HEAD_DIM] / acc[HEAD_DIM:HEAD_DIM + 1]
    o_ref[0] = jnp.concatenate([o[:, h * tq:(h + 1) * tq].T for h in range(hps)], axis=-1).astype(o_ref.dtype)


def _fox(qt, k, ckx, vt, *, tq, tk, q_off):
    b, h, dh, t = qt.shape
    length = k.shape[1]
    hps = FOX_HPS
    assert hps * dh == LANES and FOX_CK_TERMS * h <= LANES and length % tk == 0 and tq & (tq - 1) == 0
    kern = functools.partial(_fox_kernel, tq=tq, tk=tk, q_off=q_off)
    return pl.pallas_call(
        kern,
        grid=(b, h // hps, t // tq),
        in_specs=[
            pl.BlockSpec((1, hps, dh, tq), lambda bi, hi, i: (bi, hi, 0, i)),
            pl.BlockSpec((1, length, LANES), lambda bi, hi, i: (bi, 0, hi)),
            pl.BlockSpec((1, length, LANES), lambda bi, hi, i: (bi, 0, 0)),
            pl.BlockSpec((1, hps, V_ROWS, length), lambda bi, hi, i: (bi, hi, 0, 0)),
        ],
        out_specs=pl.BlockSpec((1, tq, hps * dh), lambda bi, hi, i: (bi, i, hi)),
        out_shape=jax.ShapeDtypeStruct((b, t, h * dh), BF16),
        scratch_shapes=[
            pltpu.VMEM((1, hps * tq), F32),
            pltpu.VMEM((V_ROWS, hps * tq), F32),
        ],
        compiler_params=_params(("parallel", "parallel", "arbitrary")),
        name="fox",
    )(qt, k, ckx, vt)


def _ck_table(cum3):
    b, _, length = cum3[0].shape
    ck = jnp.stack(cum3, axis=-1).transpose(0, 2, 1, 3).reshape(b, length, H_A * FOX_CK_TERMS)
    return jnp.concatenate([ck, jnp.zeros((b, length, LANES - H_A * FOX_CK_TERMS), F32)], axis=-1).astype(BF16)


def _with_cache_tok(cache, new, length):
    if cache is not None:
        new = jnp.concatenate([cache.reshape(cache.shape[0], cache.shape[1], -1).astype(BF16), new], axis=1)
    return _pad_time(new, length)


def _with_cache_feat(cache, new, n_heads, length):
    b, _, t = new.shape
    new = new.reshape(b, n_heads, V_ROWS, t)
    if cache is not None:
        old = cache.transpose(0, 2, 3, 1).astype(BF16)
        old = jnp.concatenate([old, jnp.ones(old.shape[:2] + (V_ROWS - HEAD_DIM, old.shape[3]), BF16)], axis=2)
        new = jnp.concatenate([old, new], axis=3)
    pad = length - new.shape[3]
    if pad:
        new = jnp.concatenate([new, jnp.zeros(new.shape[:3] + (pad,), BF16)], axis=3)
    return new


INT_MIN = -(2 ** 31)
KEY_NEG_INF = int(np.int32(np.uint32(0xFF800000) ^ np.uint32(0x7FFFFFFF)))
DSA_TK = 256


def _order_key(x):
    bits = lax.bitcast_convert_type(x, jnp.int32)
    return bits ^ ((bits >> 31) & jnp.int32(0x7FFFFFFF))


def _dsa_kernel(qit_ref, wit_ref, ki_ref, qt_ref, k_ref, vt_ref, nbt_ref, o_ref,
                sc_ref, m_ref, acc_ref, *, tq, q_off, topk):
    tk = DSA_TK
    i = pl.program_id(1)
    q0 = q_off + i * tq
    i_abs = q0 // tk
    q_tiles = -(-tq // tk)
    nk = i_abs + q_tiles
    cols = G_B * tq

    wit = wit_ref[0]
    wrows = [wit[hh:hh + 1, :] for hh in range(HI_B)]
    q_chunk = (q0 + lax.broadcasted_iota(jnp.int32, (tk, tq), 1)) >> 6
    krow = lax.broadcasted_iota(jnp.int32, (tk, tq), 0)

    def score_tile(j, carry):
        ks = pl.multiple_of(j * tk, tk)
        ki = ki_ref[0, pl.ds(ks, tk), :]
        tot = jnp.zeros((tk, tq), F32)
        for hh in range(HI_B):
            tot = tot + wrows[hh] * jnp.maximum(_dot(ki, qit_ref[0, hh]), 0.0)
        tot = jnp.where(((krow + ks) >> 6) <= q_chunk, tot, NEG_INF)
        sc_ref[j] = _order_key(tot)
        return carry

    lax.fori_loop(0, nk, score_tile, 0)

    kf = float(topk)

    def count(pred_fn):
        def tile_hits(j):
            hit = jnp.where(pred_fn(sc_ref[j]), 1.0, 0.0)
            parts = [hit[r:r + SUBLANES] for r in range(0, tk, SUBLANES)]
            while len(parts) > 1:
                parts = [parts[a] + parts[a + 1] for a in range(0, len(parts), 2)]
            return parts[0]

        acc8 = lax.fori_loop(0, nk, lambda j, acc8: acc8 + tile_hits(j), jnp.zeros((SUBLANES, tq), F32))
        return jnp.sum(acc8, axis=0, keepdims=True)

    c0 = count(lambda key: key >= 0)
    thr = jnp.where(c0 >= kf, jnp.int32(0), jnp.int32(INT_MIN))
    for bit in range(30, -1, -1):
        cand = thr + jnp.int32(1 << bit)
        c = count(lambda key, cand=cand: key >= cand)
        thr = jnp.where(c >= kf, cand, thr)

    cnt_gt = count(lambda key: key > thr)
    short = thr <= jnp.int32(KEY_NEG_INF)
    thr_b = jnp.broadcast_to(jnp.maximum(thr, jnp.int32(KEY_NEG_INF)), (tk, tq))
    need = jnp.where(short, 0.0, kf - cnt_gt)

    tri = jnp.where(lax.broadcasted_iota(jnp.int32, (tk, tk), 1) <= lax.broadcasted_iota(jnp.int32, (tk, tk), 0),
                    1.0, 0.0).astype(BF16)
    m_ref[...] = jnp.full((HKV_B, 1, cols), NEG_INF, F32)
    acc_ref[...] = jnp.zeros((HKV_B, V_ROWS, cols), F32)
    q_zero = jnp.zeros((HEAD_DIM, cols), BF16)
    q_ext = []
    for g in range(HKV_B):
        parts = [q_zero] * HKV_B
        parts[g] = qt_ref[0, g]
        q_ext.append(jnp.concatenate(parts, axis=0))

    n_far = jnp.maximum(i_abs - 1, 0)

    def attend_tile(j, eqc, near):
        ks = pl.multiple_of(j * tk, tk)
        key = sc_ref[j]
        gt = key > thr_b
        eq = key == thr_b
        eqf = jnp.where(eq, 1.0, 0.0)
        rank = _dot(tri, eqf.astype(BF16)) + eqc
        pen = jnp.where(gt, 0.0, jnp.where(eq, rank, 1e9))
        madd = jnp.where(pen <= need, 0.0, NEG_INF)
        madd = jnp.concatenate([madd] * G_B, axis=1)
        k_tile = k_ref[0, pl.ds(ks, tk), :]
        for g in range(HKV_B):
            s = _dot(k_tile, q_ext[g]) + madd
            if near:
                s = s + nbt_ref[j - i_abs + 1, g]
            m_old = m_ref[g]
            m_new = jnp.maximum(m_old, jnp.max(s, axis=0, keepdims=True))
            m_safe = jnp.where(m_new == NEG_INF, 0.0, m_new)
            p = jnp.exp(s - m_safe)
            acc_ref[g] = jnp.exp(m_old - m_safe) * acc_ref[g] + _dot(vt_ref[0, g, :, pl.ds(ks, tk)], p.astype(BF16))
            m_ref[g] = m_new
        return eqc + jnp.sum(eqf, axis=0, keepdims=True)

    eqc = lax.fori_loop(0, n_far, functools.partial(attend_tile, near=False), jnp.zeros((1, tq), F32))
    lax.fori_loop(n_far, nk, functools.partial(attend_tile, near=True), eqc)
    for g in range(HKV_B):
        acc = acc_ref[g]
        o_ref[0, g] = (acc[:HEAD_DIM] / acc[HEAD_DIM:HEAD_DIM + 1]).astype(o_ref.dtype)


def _dsa(qit, wit, ki, qt, k, vt, nbt, *, tq, q_off, topk):
    b, _, _, t = qit.shape
    length = ki.shape[1]
    cols = G_B * tq
    n_bias = nbt.shape[0]
    tk = DSA_TK
    assert tk >= MAX_DISTANCE and q_off % tk == 0 and length % tk == 0 and length >= q_off + t
    assert HKV_B * HEAD_DIM == LANES and tq % CHUNK == 0 and (tq % tk == 0 or t == tq) and n_bias == 1 + -(-tq // tk)
    kern = functools.partial(_dsa_kernel, tq=tq, q_off=q_off, topk=topk)
    return pl.pallas_call(
        kern,
        grid=(b, t // tq),
        in_specs=[
            pl.BlockSpec((1, HI_B, DI_B, tq), lambda bi, i: (bi, 0, 0, i)),
            pl.BlockSpec((1, SUBLANES, tq), lambda bi, i: (bi, 0, i)),
            pl.BlockSpec((1, length, DI_B), lambda bi, i: (bi, 0, 0)),
            pl.BlockSpec((1, HKV_B, HEAD_DIM, cols), lambda bi, i: (bi, 0, 0, i)),
            pl.BlockSpec((1, length, LANES), lambda bi, i: (bi, 0, 0)),
            pl.BlockSpec((1, HKV_B, V_ROWS, length), lambda bi, i: (bi, 0, 0, 0)),
            pl.BlockSpec((n_bias, HKV_B, tk, cols), lambda bi, i: (0, 0, 0, 0), pipeline_mode=pl.Buffered(1)),
        ],
        out_specs=pl.BlockSpec((1, HKV_B, HEAD_DIM, cols), lambda bi, i: (bi, 0, 0, i)),
        out_shape=jax.ShapeDtypeStruct(qt.shape, BF16),
        scratch_shapes=[
            pltpu.VMEM((length // tk, tk, tq), jnp.int32),
            pltpu.VMEM((HKV_B, 1, cols), F32),
            pltpu.VMEM((HKV_B, V_ROWS, cols), F32),
        ],
        compiler_params=_params(("parallel", "arbitrary")),
        name="dsa",
    )(qit, wit, ki, qt, k, vt, nbt)


HG_SUB = SUBLANES


def _hgrn_kernel(f_ref, i_ref, q_ref, g_ref, la_ref, l1_ref, oml_ref, ng_ref, s0_ref,
                 o_ref, sfin_ref, st_ref):
    c = pl.program_id(1)
    n_c = pl.num_programs(1)
    ct = CHUNK

    @pl.when(c == 0)
    def _():
        for hh in range(H_C):
            st_ref[hh] = s0_ref[0, hh].T

    zf = f_ref[0]
    a = la_ref[...]
    cc = l1_ref[...] + _log_sigmoid(zf)
    logf = jnp.maximum(a, cc) + jnp.log1p(jnp.exp(-jnp.abs(a - cc)))
    kk = oml_ref[...] * _sigmoid(-zf)

    r_i = lax.broadcasted_iota(jnp.int32, (ct, ct), 0)
    c_i = lax.broadcasted_iota(jnp.int32, (ct, ct), 1)
    low = jnp.where(c_i <= r_i, 1.0, 0.0).astype(BF16)
    hi, mid, lo = _split3(logf)
    bcum = _dot(low, hi) + _dot(low, mid) + _dot(low, lo)

    sub_row = lax.broadcasted_iota(jnp.int32, (HG_SUB, DK_C), 0)
    a_col = lax.broadcasted_iota(jnp.int32, (HG_SUB, ct), 1)
    n_sub = ct // HG_SUB

    for hh in range(H_C):
        cs = slice(hh * DK_C, (hh + 1) * DK_C)
        qh = q_ref[0, :, cs]
        kh = kk[:, cs]
        vh = i_ref[0, :, cs]
        bh = bcum[:, cs]
        a_rows = []
        for bi in range(n_sub):
            r0 = bi * HG_SUB
            q_blk = qh[r0:r0 + HG_SUB]
            b_blk = bh[r0:r0 + HG_SUB]
            k_blk = kh[r0:r0 + HG_SUB]
            if bi > 0:
                ref = bh[r0:r0 + 1]
                qp = q_blk * jnp.exp(b_blk - ref)
                kp = kh[:r0] * jnp.exp(ref - bh[:r0])
                kp = jnp.concatenate([kp, jnp.zeros((ct - r0, DK_C), F32)], axis=0)
                a_row = _dot_nt(qp.astype(BF16), kp.astype(BF16))
            else:
                a_row = jnp.zeros((HG_SUB, ct), F32)
            for s in range(HG_SUB):
                d = jnp.where(sub_row >= s, b_blk - b_blk[s:s + 1], NEG_INF)
                x = q_blk * jnp.exp(d) * k_blk[s:s + 1]
                col = jnp.sum(x, axis=-1, keepdims=True)
                a_row = jnp.where(a_col == r0 + s, col, a_row)
            a_rows.append(a_row)
        a_mat = jnp.concatenate(a_rows, axis=0)
        st = st_ref[hh]
        o = _dot(a_mat.astype(BF16), vh.astype(BF16)) + _dot_nt((qh * jnp.exp(bh)).astype(BF16), st.astype(BF16))
        bl = bh[ct - 1:ct]
        kdec = kh * jnp.exp(bl - bh)
        st_ref[hh] = st * jnp.exp(bl) + _dot_tn(vh.astype(BF16), kdec.astype(BF16))
        o = o * lax.rsqrt(jnp.mean(o * o, axis=-1, keepdims=True) + LN_EPS) * ng_ref[:, cs]
        gh = g_ref[0, :, cs]
        o_ref[0, :, cs] = (o * (gh * _sigmoid(gh))).astype(o_ref.dtype)

    @pl.when(c == n_c - 1)
    def _():
        for hh in range(H_C):
            sfin_ref[0, hh] = st_ref[hh].T


def _hgrn(z3, la, l1, oml, ng, s0):
    b, t, _ = z3.shape
    n_c = t // CHUNK

    def zspec(col):
        blk = (col - Z_FC) // W_C
        return pl.BlockSpec((1, CHUNK, W_C), lambda bi, ci: (bi, ci, blk))

    vec = pl.BlockSpec((1, W_C), lambda bi, ci: (0, 0))
    return pl.pallas_call(
        _hgrn_kernel,
        grid=(b, n_c),
        in_specs=[zspec(Z_FC), zspec(Z_IC), zspec(Z_QC), zspec(Z_GC), vec, vec, vec, vec,
                  pl.BlockSpec((1, H_C, DK_C, DK_C), lambda bi, ci: (bi, 0, 0, 0))],
        out_specs=[pl.BlockSpec((1, CHUNK, W_C), lambda bi, ci: (bi, ci, 0)),
                   pl.BlockSpec((1, H_C, DK_C, DK_C), lambda bi, ci: (bi, 0, 0, 0))],
        out_shape=[jax.ShapeDtypeStruct((b, t, W_C), BF16),
                   jax.ShapeDtypeStruct((b, H_C, DK_C, DK_C), F32)],
        scratch_shapes=[pltpu.VMEM((H_C, DK_C, DK_C), F32)],
        compiler_params=_params(("parallel", "arbitrary")),
        name="hgrn",
    )(z3, z3, z3, z3, la, l1, oml, ng, s0)


def _merge_kernel(x_ref, xb_ref, oa_ref, ob_ref, oc_ref, wg_ref, bg_ref, wa_ref, wb_ref, wc_ref, wo_ref,
                  g_ref, b_ref, y_ref, yb_ref):
    xb = xb_ref[...]
    d = D_MODEL
    mix = None
    for br, (o_ref, w_ref) in enumerate(((oa_ref, wa_ref), (ob_ref, wb_ref), (oc_ref, wc_ref))):
        gate = _sigmoid(_dot(xb, wg_ref[:, br * d:(br + 1) * d]) + bg_ref[:, br * d:(br + 1) * d])
        term = gate * _dot(o_ref[...], w_ref[...])
        mix = term if mix is None else mix + term
    out = _dot(mix.astype(BF16), wo_ref[...])
    y = _layer_norm_rows(DEEPNORM_ALPHA * x_ref[...] + out, g_ref[...], b_ref[...])
    y_ref[...] = y
    yb_ref[...] = y.astype(BF16)


def _merge(x, xb, oa, ob, oc, wg, bg, wa, wb, wc, wo, g, b, tm):
    n, d = x.shape

    def row(width):
        return pl.BlockSpec((tm, width), lambda i: (i, 0))

    def full(arr):
        return pl.BlockSpec(arr.shape, lambda i: (0, 0))

    return pl.pallas_call(
        _merge_kernel,
        grid=(n // tm,),
        in_specs=[row(d), row(d), row(oa.shape[1]), row(ob.shape[1]), row(oc.shape[1]),
                  full(wg), full(bg), full(wa), full(wb), full(wc), full(wo), full(g), full(b)],
        out_specs=[row(d), row(d)],
        out_shape=[jax.ShapeDtypeStruct((n, d), F32), jax.ShapeDtypeStruct((n, d), BF16)],
        compiler_params=_params(("parallel",)),
        name="merge",
    )(x, xb, oa, ob, oc, wg, bg, wa, wb, wc, wo, g, b)


def _router_kernel(x_ref, w_ref, o_ref):
    x = x_ref[...]
    xh = x.astype(BF16)
    xl = (x - xh.astype(F32)).astype(BF16)
    w = w_ref[...]
    wh = w.astype(BF16)
    wl = (w - wh.astype(F32)).astype(BF16)
    logits = _dot(xh, wh) + _dot(xh, wl) + _dot(xl, wh)
    lane = lax.broadcasted_iota(jnp.int32, logits.shape, 1)
    logits = jnp.where(lane < N_EXPERTS, logits, NEG_INF)
    big = jnp.int32(LANES)
    m1 = jnp.max(logits, axis=-1, keepdims=True)
    i1 = jnp.min(jnp.where(logits == m1, lane, big), axis=-1, keepdims=True)
    rest = jnp.where(lane == i1, NEG_INF, logits)
    m2 = jnp.max(rest, axis=-1, keepdims=True)
    i2 = jnp.min(jnp.where(rest == m2, lane, big), axis=-1, keepdims=True)
    e2 = jnp.exp(m2 - m1)
    g1 = 1.0 / (1.0 + e2)
    g2 = e2 / (1.0 + e2)
    o_ref[...] = jnp.where(lane == i1, g1, 0.0) + jnp.where(lane == i2, g2, 0.0)


def _router(x, w, tm):
    n, d = x.shape
    return pl.pallas_call(
        _router_kernel,
        grid=(n // tm,),
        in_specs=[pl.BlockSpec((tm, d), lambda i: (i, 0)),
                  pl.BlockSpec(w.shape, lambda i: (0, 0))],
        out_specs=pl.BlockSpec((tm, LANES), lambda i: (i, 0)),
        out_shape=jax.ShapeDtypeStruct((n, LANES), F32),
        compiler_params=_params(("parallel",)),
        name="router",
    )(x, w)


def _ffn_kernel(x_ref, xb_ref, w1_ref, w3_ref, w2_ref, g_ref, b_ref, y_ref, yb_ref, acc_ref):
    f = pl.program_id(1)

    @pl.when(f == 0)
    def _():
        acc_ref[...] = jnp.zeros_like(acc_ref)

    xb = xb_ref[...]
    h1 = _dot(xb, w1_ref[...])
    h = (h1 * _sigmoid(h1)) * _dot(xb, w3_ref[...])
    acc_ref[...] += _dot(h.astype(BF16), w2_ref[...])

    @pl.when(f == pl.num_programs(1) - 1)
    def _():
        y = _layer_norm_rows(DEEPNORM_ALPHA * x_ref[...] + acc_ref[...], g_ref[...], b_ref[...])
        y_ref[...] = y
        yb_ref[...] = y.astype(BF16)


def _ffn(x, xb, w1, w3, w2, g, b, *, tm, tf):
    n, d = x.shape
    ff = w1.shape[1]
    return pl.pallas_call(
        _ffn_kernel,
        grid=(n // tm, ff // tf),
        in_specs=[
            pl.BlockSpec((tm, d), lambda i, f: (i, 0)),
            pl.BlockSpec((tm, d), lambda i, f: (i, 0)),
            pl.BlockSpec((d, tf), lambda i, f: (0, f)),
            pl.BlockSpec((d, tf), lambda i, f: (0, f)),
            pl.BlockSpec((tf, d), lambda i, f: (f, 0)),
            pl.BlockSpec((1, d), lambda i, f: (0, 0)),
            pl.BlockSpec((1, d), lambda i, f: (0, 0)),
        ],
        out_specs=[pl.BlockSpec((tm, d), lambda i, f: (i, 0)),
                   pl.BlockSpec((tm, d), lambda i, f: (i, 0))],
        out_shape=[jax.ShapeDtypeStruct((n, d), F32), jax.ShapeDtypeStruct((n, d), BF16)],
        scratch_shapes=[pltpu.VMEM((tm, d), F32)],
        compiler_params=_params(("parallel", "arbitrary")),
        name="ffn",
    )(x, xb, w1, w3, w2, g, b)


MOE_BS = 144


def _moe_slots(tm):
    return -(-tm // MOE_BS) * MOE_BS


def _moe_kernel(x_ref, xb_ref, comb_ref, w1_ref, w3_ref, w2_ref, g_ref, b_ref, y_ref, yb_ref,
                rank_ref, rankt_ref, xg_ref, yacc_ref, *, tm):
    e = pl.program_id(1)
    f = pl.program_id(2)
    n_f = pl.num_programs(2)
    bs = MOE_BS

    @pl.when((e == 0) & (f == 0))
    def _():
        comb = comb_ref[...]
        routed = comb > 0.0
        r_i = lax.broadcasted_iota(jnp.int32, (tm, tm), 0)
        c_i = lax.broadcasted_iota(jnp.int32, (tm, tm), 1)
        strict_low = jnp.where(c_i < r_i, 1.0, 0.0).astype(BF16)
        rank = _dot(strict_low, jnp.where(routed, 1.0, 0.0).astype(BF16))
        rank = jnp.where(routed, rank, -1.0)
        rank_ref[...] = rank
        rankt_ref[...] = rank.T[:N_EXPERTS]
        y_ref[...] = jnp.zeros_like(y_ref)

    rrow = rankt_ref[pl.ds(e, 1), :]
    n_e = jnp.sum(jnp.where(rrow >= 0.0, 1.0, 0.0)).astype(jnp.int32)
    n_blk = (n_e + bs - 1) // bs

    @pl.when(f == 0)
    def _():
        slot = lax.broadcasted_iota(jnp.int32, (bs, tm), 0).astype(F32)

        def gather(blk, carry):
            r0 = pl.multiple_of(blk * bs, bs)
            pick = jnp.where(rrow - (blk * bs).astype(F32) == slot, 1.0, 0.0).astype(BF16)
            xg_ref[pl.ds(r0, bs), :] = _dot(pick, xb_ref[...]).astype(BF16)
            yacc_ref[pl.ds(r0, bs), :] = jnp.zeros((bs, yacc_ref.shape[1]), F32)
            return carry

        lax.fori_loop(0, n_blk, gather, 0)

    def expert_rows(r0, rows):
        xg = xg_ref[pl.ds(r0, rows), :]
        h1 = _dot(xg, w1_ref[0])
        h = (h1 * _sigmoid(h1)) * _dot(xg, w3_ref[0])
        yacc_ref[pl.ds(r0, rows), :] += _dot(h.astype(BF16), w2_ref[0])

    def expert_pair(pair, carry):
        expert_rows(pl.multiple_of(pair * (2 * bs), 2 * bs), 2 * bs)
        return carry

    n_pair = n_blk // 2
    lax.fori_loop(0, n_pair, expert_pair, 0)

    @pl.when(n_blk % 2 == 1)
    def _():
        expert_rows(pl.multiple_of(n_pair * (2 * bs), bs), bs)

    @pl.when(f == n_f - 1)
    def _():
        lane = lax.broadcasted_iota(jnp.int32, (tm, LANES), 1)
        rcol = jnp.sum(jnp.where(lane == e, rank_ref[...], 0.0), axis=-1, keepdims=True)
        gcol = jnp.sum(jnp.where(lane == e, comb_ref[...], 0.0), axis=-1, keepdims=True)
        def scatter_rows(r0, rows):
            slot = lax.broadcasted_iota(jnp.int32, (tm, rows), 1).astype(F32)
            put = jnp.where(rcol - r0.astype(F32) == slot, 1.0, 0.0).astype(BF16)
            y_ref[...] += gcol * _dot(put, yacc_ref[pl.ds(r0, rows), :].astype(BF16))

        def scatter_pair(pair, carry):
            scatter_rows(pl.multiple_of(pair * (2 * bs), 2 * bs), 2 * bs)
            return carry

        lax.fori_loop(0, n_pair, scatter_pair, 0)

        @pl.when(n_blk % 2 == 1)
        def _():
            scatter_rows(pl.multiple_of(n_pair * (2 * bs), bs), bs)

    @pl.when((e == pl.num_programs(1) - 1) & (f == n_f - 1))
    def _():
        y = _layer_norm_rows(DEEPNORM_ALPHA * x_ref[...] + y_ref[...], g_ref[...], b_ref[...])
        y_ref[...] = y
        yb_ref[...] = y.astype(BF16)


def _moe(x, xb, comb, w1, w3, w2, g, b, *, tm, tf):
    n, d = x.shape
    n_e, _, ff = w1.shape
    assert n % tm == 0 and ff % tf == 0 and n_e == N_EXPERTS
    slots = _moe_slots(tm)
    kern = functools.partial(_moe_kernel, tm=tm)
    one = pl.Buffered(1)
    return pl.pallas_call(
        kern,
        grid=(n // tm, n_e, ff // tf),
        in_specs=[
            pl.BlockSpec((tm, d), lambda i, e, f: (i, 0), pipeline_mode=one),
            pl.BlockSpec((tm, d), lambda i, e, f: (i, 0), pipeline_mode=one),
            pl.BlockSpec((tm, LANES), lambda i, e, f: (i, 0), pipeline_mode=one),
            pl.BlockSpec((1, d, tf), lambda i, e, f: (e, 0, f)),
            pl.BlockSpec((1, d, tf), lambda i, e, f: (e, 0, f)),
            pl.BlockSpec((1, tf, d), lambda i, e, f: (e, f, 0)),
            pl.BlockSpec((1, d), lambda i, e, f: (0, 0)),
            pl.BlockSpec((1, d), lambda i, e, f: (0, 0)),
        ],
        out_specs=[pl.BlockSpec((tm, d), lambda i, e, f: (i, 0)),
                   pl.BlockSpec((tm, d), lambda i, e, f: (i, 0))],
        out_shape=[jax.ShapeDtypeStruct((n, d), F32), jax.ShapeDtypeStruct((n, d), BF16)],
        scratch_shapes=[pltpu.VMEM((tm, LANES), F32), pltpu.VMEM((N_EXPERTS, tm), F32),
                        pltpu.VMEM((slots, d), BF16), pltpu.VMEM((slots, d), F32)],
        compiler_params=_params(("parallel", "arbitrary", "arbitrary")),
        name="moe",
    )(x, xb, comb, w1, w3, w2, g, b)


def _rel_bucket(rel):
    nb = N_BUCKETS // 2
    max_exact = nb // 2
    ret = jnp.where(rel > 0, nb, 0)
    n = jnp.abs(rel)
    nf = jnp.maximum(n, 1).astype(jnp.float32)
    large = max_exact + (jnp.log(nf / max_exact) / math.log(MAX_DISTANCE / max_exact) * (nb - max_exact)).astype(jnp.int32)
    large = jnp.minimum(large, nb - 1)
    return ret + jnp.where(n < max_exact, n, large)


def _bias_tiles(rel_bias, tq):
    tk = DSA_TK
    r = jnp.arange(tq, dtype=jnp.int32)[:, None]
    c = jnp.arange(tk, dtype=jnp.int32)[None, :]
    n = 1 + -(-tq // tk)
    rel = jnp.stack([c - r + (s - 1) * tk for s in range(n)], axis=0)
    table = rel_bias.astype(F32)
    onehot = _rel_bucket(rel)[..., None] == jnp.arange(N_BUCKETS, dtype=jnp.int32)
    vals = jnp.sum(jnp.where(onehot[..., None], table, 0.0), axis=-2)
    far = table[_rel_bucket(jnp.int32(-(2 * MAX_DISTANCE)))]
    vals = (vals - far).transpose(0, 3, 2, 1).reshape(n, HKV_B, G_B, tk, tq)
    return vals.transpose(0, 1, 3, 2, 4).reshape(n, HKV_B, tk, G_B * tq)


def _prep_w_in(w_in_l, b_in_l):
    def tok_cols(a):
        parts = [a[..., IN_KA:IN_FA], a[..., IN_FC:IN_END], a[..., IN_KB:IN_QI], a[..., IN_KI:IN_WI], a[..., IN_FA:IN_QB]]
        width = sum(p.shape[-1] for p in parts)
        return jnp.concatenate(parts + [jnp.zeros(a.shape[:-1] + (N_Z - width,), a.dtype)], axis=-1)

    def feat_cols(a):
        parts = [a[..., IN_QA:IN_KA] * HEAD_DIM ** -0.5, a[..., IN_VA:IN_FA], a[..., IN_QB:IN_KB] * HEAD_DIM ** -0.5,
                 a[..., IN_QI:IN_KI] * DI_B ** -0.5, a[..., IN_VB:IN_QI], a[..., IN_WI:IN_FC] * HI_B ** -0.5]
        width = sum(p.shape[-1] for p in parts)
        return jnp.concatenate(parts + [jnp.zeros(a.shape[:-1] + (T_ROWS - width,), a.dtype)], axis=-1)

    w = w_in_l.astype(F32)
    bias = b_in_l.astype(F32)[None, :]
    return (tok_cols(w).astype(BF16), tok_cols(bias),
            feat_cols(w).T.astype(BF16), feat_cols(bias).T)


def _pad_time(a, length):
    pad = length - a.shape[1]
    if pad == 0:
        return a
    return jnp.concatenate([a, jnp.zeros((a.shape[0], pad) + a.shape[2:], a.dtype)], axis=1)


def _round_up(x, m):
    return (x + m - 1) // m * m


def _run_group(x, caches, wts):
    b, t, d = x.shape
    n = b * t
    p_len = 0 if caches is None else caches[0].shape[2]
    tm = min(512, n)
    xf = x.reshape(n, d).astype(F32)
    xb = xf.astype(BF16)

    fox_tq = min(1024, t)
    fox_tk = 512
    l_fox = _round_up(p_len + t, fox_tk)
    dsa_tq = min(256, t)
    l_dsa = _round_up(p_len + t, DSA_TK)
    topk = min(TOPK_MAX, (p_len + t) // 4)
    nb = _bias_tiles(wts["rel_bias"], dsa_tq)

    lb_cum = jnp.cumsum(jax.nn.softmax(wts["hgrn_lb"].astype(F32), axis=0), axis=0)

    states = []
    for l in range(DEPTH):
        w_tok, b_tok, w_feat, b_feat = _prep_w_in(wts["w_in"][l], wts["b_in"][l])
        (k_a, v_a, z, k_b, v_b, ki_b, logf_a, ka_bf, kb_bf, ki_bf,
         qa_t, va_t, qb_t, qi_t, vb_t, wi_t) = _proj(xb, w_tok, b_tok, w_feat, b_feat, b, min(512, t))
        z3 = z.reshape(b, t, 4 * W_C)
        logf_a = logf_a.reshape(b, t, H_A)
        ki_b = ki_b.reshape(b, t, DI_B)
        if caches is None:
            c_ka = c_va = c_kb = c_vb = c_ki = None
            lf_all = logf_a
            s0 = jnp.zeros((b, H_C, DK_C, DK_C), F32)
        else:
            c_ka, c_va, c_lf, c_kb, c_vb, c_ki, c_s = [c[l] for c in caches]
            lf_all = jnp.concatenate([c_lf.astype(F32), logf_a], axis=1)
            s0 = c_s.astype(F32)

        cum3 = _cumsum(_pad_time(lf_all, l_fox).transpose(0, 2, 1))
        o_a = _fox(qa_t.reshape(b, H_A, HEAD_DIM, t), _with_cache_tok(c_ka, ka_bf.reshape(b, t, W_A), l_fox),
                   _ck_table(cum3), _with_cache_feat(c_va, va_t, H_A, l_fox), tq=fox_tq, tk=fox_tk, q_off=p_len)

        n_t = t // dsa_tq
        qt_b = qb_t.reshape(b, HKV_B, G_B, HEAD_DIM, n_t, dsa_tq).transpose(0, 1, 3, 4, 2, 5)
        qt_b = qt_b.reshape(b, HKV_B, HEAD_DIM, n_t * G_B * dsa_tq)
        o_b = _dsa(qi_t.reshape(b, HI_B, DI_B, t), wi_t, _with_cache_tok(c_ki, ki_bf.reshape(b, t, DI_B), l_dsa),
                   qt_b, _with_cache_tok(c_kb, kb_bf.reshape(b, t, W_KV_B), l_dsa),
                   _with_cache_feat(c_vb, vb_t, HKV_B, l_dsa), nb, tq=dsa_tq, q_off=p_len, topk=topk)
        o_b = o_b.reshape(b, HKV_B, HEAD_DIM, n_t, G_B, dsa_tq).transpose(0, 3, 5, 1, 4, 2).reshape(n, H_B * HEAD_DIM)

        lb = (lb_cum[l] - lb_cum[0]).reshape(1, W_C)
        o_c, s_new = _hgrn(z3, jnp.log(lb), jnp.log1p(-lb), 1.0 - lb,
                           wts["hgrn_norm_g"][l].reshape(1, W_C).astype(F32), s0)

        x1, x1b = _merge(xf, xb, o_a.reshape(n, -1), o_b, o_c.reshape(n, -1),
                         wts["w_gate"][l].astype(BF16), wts["b_gate"][l][None, :].astype(F32),
                         wts["w_o_fox"][l].astype(BF16), wts["w_o_dsa"][l].astype(BF16), wts["w_o_hgrn"][l].astype(BF16),
                         wts["w_out"][l].astype(BF16), wts["ln1_g"][l][None, :], wts["ln1_b"][l][None, :], min(256, n))

        g2, b2 = wts["ln2_g"][l][None, :], wts["ln2_b"][l][None, :]
        if l % 2 == 0:
            j = l // 2
            xf, xb = _ffn(x1, x1b, wts["ffn_w1"][j].astype(BF16), wts["ffn_w3"][j].astype(BF16),
                          wts["ffn_w2"][j].astype(BF16), g2, b2, tm=tm, tf=1408)
        else:
            j = l // 2
            rw = jnp.zeros((d, LANES), F32).at[:, :N_EXPERTS].set(wts["moe_router"][j].astype(F32))
            comb = _router(x1, rw, tm)
            xf, xb = _moe(x1, x1b, comb, wts["moe_w1"][j].astype(BF16), wts["moe_w3"][j].astype(BF16),
                          wts["moe_w2"][j].astype(BF16), g2, b2, tm=min(1024, n), tf=1792)

        states.append((k_a.reshape(b, t, H_A, HEAD_DIM), v_a.reshape(b, t, H_A, HEAD_DIM), logf_a,
                       k_b.reshape(b, t, HKV_B, HEAD_DIM), v_b.reshape(b, t, HKV_B, HEAD_DIM), ki_b, s_new))
    stacked = [jnp.stack([s[i] for s in states], axis=0) for i in range(7)]
    return xf.reshape(b, t, d), stacked


def kernel(x_prompt, x_sample, cache_fox_k, cache_fox_v, cache_fox_logf, cache_dsa_k, cache_dsa_v, cache_dsa_kidx, state_hgrn,
           w_in, b_in, w_gate, b_gate, w_o_fox, w_o_dsa, w_o_hgrn, w_out, hgrn_lb, hgrn_norm_g, rel_bias,
           ln1_g, ln1_b, ln2_g, ln2_b, ffn_w1, ffn_w3, ffn_w2, moe_router, moe_w1, moe_w3, moe_w2):
    wts = dict(w_in=w_in, b_in=b_in, w_gate=w_gate, b_gate=b_gate, w_o_fox=w_o_fox, w_o_dsa=w_o_dsa, w_o_hgrn=w_o_hgrn,
               w_out=w_out, hgrn_lb=hgrn_lb, hgrn_norm_g=hgrn_norm_g, rel_bias=rel_bias, ln1_g=ln1_g, ln1_b=ln1_b,
               ln2_g=ln2_g, ln2_b=ln2_b, ffn_w1=ffn_w1, ffn_w3=ffn_w3, ffn_w2=ffn_w2, moe_router=moe_router,
               moe_w1=moe_w1, moe_w3=moe_w3, moe_w2=moe_w2)
    caches = (cache_fox_k, cache_fox_v, cache_fox_logf, cache_dsa_k, cache_dsa_v, cache_dsa_kidx, state_hgrn)
    y_prompt, sp = _run_group(x_prompt, None, wts)
    y_sample, ss = _run_group(x_sample, caches, wts)
    return (y_prompt, y_sample, sp[0], sp[1], sp[2], sp[3], sp[4], sp[5], sp[6],
            ss[0], ss[1], ss[2], ss[3], ss[4], ss[5], ss[6])
```

```python
import functools
import math

import jax
import jax.numpy as jnp
import numpy as np
from jax import lax
from jax.experimental import pallas as pl
from jax.experimental.pallas import tpu as pltpu

F32 = jnp.float32
BF16 = jnp.bfloat16
NEG_INF = float("-inf")

D_MODEL = 1024
DEPTH = 2
CHUNK = 64
HEAD_DIM = 64
H_A = 8
H_B = 8
HKV_B = 2
G_B = H_B // HKV_B
HI_B = 4
DI_B = 64
TOPK_MAX = 256
H_C = 4
DK_C = 128
W_C = H_C * DK_C
N_BUCKETS = 32
MAX_DISTANCE = 128
N_EXPERTS = 8
DEEPNORM_ALPHA = (2.0 * DEPTH) ** 0.25
LN_EPS = 1e-5

LANES = 128
SUBLANES = 8
VMEM_LIMIT = 56 * 1024 * 1024

W_A = H_A * HEAD_DIM
W_B = H_B * HEAD_DIM
W_KV_B = HKV_B * HEAD_DIM

IN_QA, IN_KA, IN_VA, IN_FA = 0, 512, 1024, 1536
IN_QB, IN_KB, IN_VB, IN_QI, IN_KI, IN_WI = 1544, 2056, 2184, 2312, 2568, 2632
IN_FC, IN_END = 2636, 4684
Z_KA, Z_VA = 0, 512
Z_FC, Z_IC, Z_QC, Z_GC = 1024, 1536, 2048, 2560
Z_KB, Z_VB, Z_KI, Z_FA = 3072, 3200, 3328, 3392
N_Z = 3456
T_QA, T_VA, T_QB, T_QI, T_VB, T_WI = 0, 512, 1024, 1536, 1792, 1920
T_ROWS = 1928
V_ROWS = 80


def _params(sem, vmem=VMEM_LIMIT):
    return pltpu.CompilerParams(dimension_semantics=sem, vmem_limit_bytes=vmem)


def _log_sigmoid(z):
    return jnp.minimum(z, 0.0) - jnp.log1p(jnp.exp(-jnp.abs(z)))


def _sigmoid(z):
    return 1.0 / (1.0 + jnp.exp(-z))


def _dot_nt(a, b):
    return lax.dot_general(a, b, (((1,), (1,)), ((), ())), preferred_element_type=F32)


def _dot_tn(a, b):
    return lax.dot_general(a, b, (((0,), (0,)), ((), ())), preferred_element_type=F32)


def _dot(a, b):
    return jnp.dot(a, b, preferred_element_type=F32)


def _split3(x):
    hi = x.astype(BF16)
    r1 = x - hi.astype(F32)
    mid = r1.astype(BF16)
    lo = (r1 - mid.astype(F32)).astype(BF16)
    return hi, mid, lo


def _layer_norm_rows(r, g, b):
    mu = jnp.mean(r, axis=-1, keepdims=True)
    rc = r - mu
    var = jnp.mean(rc * rc, axis=-1, keepdims=True)
    return rc * lax.rsqrt(var + LN_EPS) * g + b


def _proj_kernel(x_ref, w_ref, b_ref, wt_ref, bt_ref,
                 kaf_ref, vaf_ref, z_ref, kbf_ref, vbf_ref, kif_ref, lf_ref,
                 ka_ref, kb_ref, ki_ref, qa_ref, va_ref, qb_ref, qi_ref, vb_ref, wi_ref):
    x = x_ref[...]

    def cols(c0, width):
        return _dot(x, w_ref[:, c0:c0 + width]) + b_ref[:, c0:c0 + width]

    kaf_ref[...] = cols(Z_KA, W_A)
    vaf_ref[...] = cols(Z_VA, W_A)
    z_ref[...] = cols(Z_FC, 4 * W_C)
    kv_b = cols(Z_KB, 2 * W_KV_B)
    kbf_ref[...] = kv_b[:, :W_KV_B]
    vbf_ref[...] = kv_b[:, W_KV_B:]
    tail = cols(Z_KI, N_Z - Z_KI)
    kif_ref[...] = tail[:, :DI_B]
    lf_ref[...] = _log_sigmoid(tail[:, Z_FA - Z_KI:Z_FA - Z_KI + H_A])
    ka_ref[...] = kaf_ref[...].astype(BF16)
    kb_ref[...] = kv_b[:, :W_KV_B].astype(BF16)
    ki_ref[...] = tail[:, :DI_B].astype(BF16)
    zt = _dot_nt(wt_ref[...], x) + bt_ref[...]
    qa_ref[0] = zt[T_QA:T_QA + W_A].astype(BF16)
    qb_ref[0] = zt[T_QB:T_QB + W_B].astype(BF16)
    qi_ref[0] = zt[T_QI:T_QI + HI_B * DI_B].astype(BF16)
    wi_ref[0] = zt[T_WI:T_WI + SUBLANES]
    ones = jnp.ones((V_ROWS - HEAD_DIM, zt.shape[1]), BF16)
    for v_ref, row0, heads in ((va_ref, T_VA, H_A), (vb_ref, T_VB, HKV_B)):
        for h in range(heads):
            v_ref[0, h * V_ROWS:h * V_ROWS + HEAD_DIM] = zt[row0 + h * HEAD_DIM:row0 + (h + 1) * HEAD_DIM].astype(BF16)
            v_ref[0, h * V_ROWS + HEAD_DIM:(h + 1) * V_ROWS] = ones


def _proj(xb, w, b, wt, bt, batch, tm):
    n, k = xb.shape
    t = n // batch
    assert t % tm == 0
    tpb = t // tm
    full = pl.Buffered(1)

    def tok(width):
        return pl.BlockSpec((tm, width), lambda i: (i, 0))

    def feat(rows):
        return pl.BlockSpec((1, rows, tm), lambda i: (i // tpb, 0, i % tpb))

    def feat_shape(rows, dtype):
        return jax.ShapeDtypeStruct((batch, rows, t), dtype)

    return pl.pallas_call(
        _proj_kernel,
        grid=(n // tm,),
        in_specs=[
            tok(k),
            pl.BlockSpec(w.shape, lambda i: (0, 0), pipeline_mode=full),
            pl.BlockSpec(b.shape, lambda i: (0, 0), pipeline_mode=full),
            pl.BlockSpec(wt.shape, lambda i: (0, 0), pipeline_mode=full),
            pl.BlockSpec(bt.shape, lambda i: (0, 0), pipeline_mode=full),
        ],
        out_specs=[tok(W_A), tok(W_A), tok(4 * W_C), tok(W_KV_B), tok(W_KV_B), tok(DI_B), tok(H_A),
                   tok(W_A), tok(W_KV_B), tok(DI_B),
                   feat(W_A), feat(H_A * V_ROWS), feat(W_B), feat(HI_B * DI_B), feat(HKV_B * V_ROWS), feat(SUBLANES)],
        out_shape=[jax.ShapeDtypeStruct((n, W_A), F32), jax.ShapeDtypeStruct((n, W_A), F32),
                   jax.ShapeDtypeStruct((n, 4 * W_C), F32), jax.ShapeDtypeStruct((n, W_KV_B), F32),
                   jax.ShapeDtypeStruct((n, W_KV_B), F32), jax.ShapeDtypeStruct((n, DI_B), F32),
                   jax.ShapeDtypeStruct((n, H_A), F32),
                   jax.ShapeDtypeStruct((n, W_A), BF16),
                   jax.ShapeDtypeStruct((n, W_KV_B), BF16), jax.ShapeDtypeStruct((n, DI_B), BF16),
                   feat_shape(W_A, BF16), feat_shape(H_A * V_ROWS, BF16), feat_shape(W_B, BF16),
                   feat_shape(HI_B * DI_B, BF16), feat_shape(HKV_B * V_ROWS, BF16), feat_shape(SUBLANES, F32)],
        compiler_params=_params(("parallel",)),
        name="proj",
    )(xb, w, b, wt, bt)


def _cumsum_kernel(x_ref, hi_ref, mid_ref, lo_ref):
    c = x_ref[0]
    length = c.shape[-1]
    lane = lax.broadcasted_iota(jnp.int32, c.shape, 1)
    s = 1
    while s < length:
        c = c + jnp.where(lane >= s, pltpu.roll(c, s, 1), 0.0)
        s *= 2
    hi, mid, lo = _split3(c)
    hi_ref[0] = hi.astype(F32)
    mid_ref[0] = mid.astype(F32)
    lo_ref[0] = lo.astype(F32)


def _cumsum(x):
    b, r, length = x.shape
    spec = pl.BlockSpec((1, r, length), lambda i: (i, 0, 0))
    return pl.pallas_call(
        _cumsum_kernel,
        grid=(b,),
        in_specs=[spec],
        out_specs=[spec, spec, spec],
        out_shape=[jax.ShapeDtypeStruct(x.shape, F32)] * 3,
        compiler_params=_params(("parallel",)),
        name="cumsum",
    )(x)


FOX_HPS = 2
FOX_CK_TERMS = 3


def _fox_kernel(qt_ref, k_ref, ckx_ref, vt_ref, o_ref, m_ref, acc_ref, *, tq, tk, q_off):
    hps = FOX_HPS
    i = pl.program_id(2)
    q0 = q_off + i * tq
    nk = (q0 + tq + tk - 1) // tk
    n_full = (q0 + 1) // tk
    ck_row = lax.broadcasted_iota(jnp.int32, (LANES, tq), 0)
    zeros = jnp.zeros((HEAD_DIM, tq), BF16)
    q_cols = []
    for h in range(hps):
        d = ck_row - FOX_CK_TERMS * (pl.program_id(1) * hps + h)
        minus_one = jnp.where(d >= 0, jnp.where(d < FOX_CK_TERMS, -1.0, 0.0), 0.0).astype(BF16)
        q_parts = [zeros] * hps
        q_parts[h] = qt_ref[0, h]
        q_cols.append(jnp.concatenate(q_parts + [minus_one], axis=0))
    qt = jnp.concatenate(q_cols, axis=1)
    krow = lax.broadcasted_iota(jnp.int32, (tk, hps * tq), 0)
    qcol = q0 + (lax.broadcasted_iota(jnp.int32, (tk, hps * tq), 1) & (tq - 1))
    m_ref[...] = jnp.full((1, hps * tq), NEG_INF, F32)
    acc_ref[...] = jnp.zeros((V_ROWS, hps * tq), F32)

    def tile(j, masked):
        ks = pl.multiple_of(j * tk, tk)
        kk = jnp.concatenate([k_ref[0, pl.ds(ks, tk), :], ckx_ref[0, pl.ds(ks, tk), :]], axis=1)
        s = _dot(kk, qt)
        if masked:
            s = jnp.where(krow + ks <= qcol, s, NEG_INF)
        m_old = m_ref[...]
        m_new = jnp.maximum(m_old, jnp.max(s, axis=0, keepdims=True))
        p = jnp.exp(s - m_new).astype(BF16)
        pv = [_dot(vt_ref[0, h, :, pl.ds(ks, tk)], p[:, h * tq:(h + 1) * tq]) for h in range(hps)]
        acc_ref[...] = jnp.exp(m_old - m_new) * acc_ref[...] + jnp.concatenate(pv, axis=1)
        m_ref[...] = m_new

    def full_tile(j, carry):
        tile(j, False)
        return carry

    def edge_tile(j, carry):
        tile(j, True)
        return carry

    lax.fori_loop(0, n_full, full_tile, 0)
    lax.fori_loop(n_full, nk, edge_tile, 0)
    acc = acc_ref[...]
    o = acc[:HEAD_DIM] / acc[HEAD_DIM:HEAD_DIM + 1]
    o_ref[0] = jnp.concatenate([o[:, h * tq:(h + 1) * tq].T for h in range(hps)], axis=-1).astype(o_ref.dtype)


def _fox(qt, k, ckx, vt, *, tq, tk, q_off):
    b, h, dh, t = qt.shape
    length = k.shape[1]
    hps = FOX_HPS
    assert hps * dh == LANES and FOX_CK_TERMS * h <= LANES and length % tk == 0 and tq & (tq - 1) == 0
    kern = functools.partial(_fox_kernel, tq=tq, tk=tk, q_off=q_off)
    return pl.pallas_call(
        kern,
        grid=(b, h // hps, t // tq),
        in_specs=[
            pl.BlockSpec((1, hps, dh, tq), lambda bi, hi, i: (bi, hi, 0, i)),
            pl.BlockSpec((1, length, LANES), lambda bi, hi, i: (bi, 0, hi)),
            pl.BlockSpec((1, length, LANES), lambda bi, hi, i: (bi, 0, 0)),
            pl.BlockSpec((1, hps, V_ROWS, length), lambda bi, hi, i: (bi, hi, 0, 0)),
        ],
        out_specs=pl.BlockSpec((1, tq, hps * dh), lambda bi, hi, i: (bi, i, hi)),
        out_shape=jax.ShapeDtypeStruct((b, t, h * dh), BF16),
        scratch_shapes=[
            pltpu.VMEM((1, hps * tq), F32),
            pltpu.VMEM((V_ROWS, hps * tq), F32),
        ],
        compiler_params=_params(("parallel", "parallel", "arbitrary")),
        name="fox",
    )(qt, k, ckx, vt)


def _ck_table(cum3):
    b, _, length = cum3[0].shape
    ck = jnp.stack(cum3, axis=-1).transpose(0, 2, 1, 3).reshape(b, length, H_A * FOX_CK_TERMS)
    return jnp.concatenate([ck, jnp.zeros((b, length, LANES - H_A * FOX_CK_TERMS), F32)], axis=-1).astype(BF16)


def _with_cache_tok(cache, new, length):
    if cache is not None:
        new = jnp.concatenate([cache.reshape(cache.shape[0], cache.shape[1], -1).astype(BF16), new], axis=1)
    return _pad_time(new, length)


def _with_cache_feat(cache, new, n_heads, length):
    b, _, t = new.shape
    new = new.reshape(b, n_heads, V_ROWS, t)
    if cache is not None:
        old = cache.transpose(0, 2, 3, 1).astype(BF16)
        old = jnp.concatenate([old, jnp.ones(old.shape[:2] + (V_ROWS - HEAD_DIM, old.shape[3]), BF16)], axis=2)
        new = jnp.concatenate([old, new], axis=3)
    pad = length - new.shape[3]
    if pad:
        new = jnp.concatenate([new, jnp.zeros(new.shape[:3] + (pad,), BF16)], axis=3)
    return new


INT_MIN = -(2 ** 31)
KEY_NEG_INF = int(np.int32(np.uint32(0xFF800000) ^ np.uint32(0x7FFFFFFF)))
DSA_TK = 256


def _order_key(x):
    bits = lax.bitcast_convert_type(x, jnp.int32)
    return bits ^ ((bits >> 31) & jnp.int32(0x7FFFFFFF))


def _dsa_kernel(qit_ref, wit_ref, ki_ref, qt_ref, k_ref, vt_ref, nbt_ref, o_ref,
                sc_ref, m_ref, acc_ref, *, tq, q_off, topk):
    tk = DSA_TK
    i = pl.program_id(1)
    q0 = q_off + i * tq
    i_abs = q0 // tk
    q_tiles = -(-tq // tk)
    nk = i_abs + q_tiles
    cols = G_B * tq

    wit = wit_ref[0]
    wrows = [wit[hh:hh + 1, :] for hh in range(HI_B)]
    q_chunk = (q0 + lax.broadcasted_iota(jnp.int32, (tk, tq), 1)) >> 6
    krow = lax.broadcasted_iota(jnp.int32, (tk, tq), 0)

    def score_tile(j, carry):
        ks = pl.multiple_of(j * tk, tk)
        ki = ki_ref[0, pl.ds(ks, tk), :]
        tot = jnp.zeros((tk, tq), F32)
        for hh in range(HI_B):
            tot = tot + wrows[hh] * jnp.maximum(_dot(ki, qit_ref[0, hh]), 0.0)
        tot = jnp.where(((krow + ks) >> 6) <= q_chunk, tot, NEG_INF)
        sc_ref[j] = _order_key(tot)
        return carry

    lax.fori_loop(0, nk, score_tile, 0)

    kf = float(topk)

    def count(pred_fn):
        def tile_hits(j):
            hit = jnp.where(pred_fn(sc_ref[j]), 1.0, 0.0)
            parts = [hit[r:r + SUBLANES] for r in range(0, tk, SUBLANES)]
            while len(parts) > 1:
                parts = [parts[a] + parts[a + 1] for a in range(0, len(parts), 2)]
            return parts[0]

        acc8 = lax.fori_loop(0, nk, lambda j, acc8: acc8 + tile_hits(j), jnp.zeros((SUBLANES, tq), F32))
        return jnp.sum(acc8, axis=0, keepdims=True)

    c0 = count(lambda key: key >= 0)
    thr = jnp.where(c0 >= kf, jnp.int32(0), jnp.int32(INT_MIN))
    for bit in range(30, -1, -1):
        cand = thr + jnp.int32(1 << bit)
        c = count(lambda key, cand=cand: key >= cand)
        thr = jnp.where(c >= kf, cand, thr)

    cnt_gt = count(lambda key: key > thr)
    short = thr <= jnp.int32(KEY_NEG_INF)
    thr_b = jnp.broadcast_to(jnp.maximum(thr, jnp.int32(KEY_NEG_INF)), (tk, tq))
    need = jnp.where(short, 0.0, kf - cnt_gt)

    tri = jnp.where(lax.broadcasted_iota(jnp.int32, (tk, tk), 1) <= lax.broadcasted_iota(jnp.int32, (tk, tk), 0),
                    1.0, 0.0).astype(BF16)
    m_ref[...] = jnp.full((HKV_B, 1, cols), NEG_INF, F32)
    acc_ref[...] = jnp.zeros((HKV_B, V_ROWS, cols), F32)
    q_zero = jnp.zeros((HEAD_DIM, cols), BF16)
    q_ext = []
    for g in range(HKV_B):
        parts = [q_zero] * HKV_B
        parts[g] = qt_ref[0, g]
        q_ext.append(jnp.concatenate(parts, axis=0))

    n_far = jnp.maximum(i_abs - 1, 0)

    def attend_tile(j, eqc, near):
        ks = pl.multiple_of(j * tk, tk)
        key = sc_ref[j]
        gt = key > thr_b
        eq = key == thr_b
        eqf = jnp.where(eq, 1.0, 0.0)
        rank = _dot(tri, eqf.astype(BF16)) + eqc
        pen = jnp.where(gt, 0.0, jnp.where(eq, rank, 1e9))
        madd = jnp.where(pen <= need, 0.0, NEG_INF)
        madd = jnp.concatenate([madd] * G_B, axis=1)
        k_tile = k_ref[0, pl.ds(ks, tk), :]
        for g in range(HKV_B):
            s = _dot(k_tile, q_ext[g]) + madd
            if near:
                s = s + nbt_ref[j - i_abs + 1, g]
            m_old = m_ref[g]
            m_new = jnp.maximum(m_old, jnp.max(s, axis=0, keepdims=True))
            m_safe = jnp.where(m_new == NEG_INF, 0.0, m_new)
            p = jnp.exp(s - m_safe)
            acc_ref[g] = jnp.exp(m_old - m_safe) * acc_ref[g] + _dot(vt_ref[0, g, :, pl.ds(ks, tk)], p.astype(BF16))
            m_ref[g] = m_new
        return eqc + jnp.sum(eqf, axis=0, keepdims=True)

    eqc = lax.fori_loop(0, n_far, functools.partial(attend_tile, near=False), jnp.zeros((1, tq), F32))
    lax.fori_loop(n_far, nk, functools.partial(attend_tile, near=True), eqc)
    for g in range(HKV_B):
        acc = acc_ref[g]
        o_ref[0, g] = (acc[:HEAD_DIM] / acc[HEAD_DIM:HEAD_DIM + 1]).astype(o_ref.dtype)


def _dsa(qit, wit, ki, qt, k, vt, nbt, *, tq, q_off, topk):
    b, _, _, t = qit.shape
    length = ki.shape[1]
    cols = G_B * tq
    n_bias = nbt.shape[0]
    tk = DSA_TK
    assert tk >= MAX_DISTANCE and q_off % tk == 0 and length % tk == 0 and length >= q_off + t
    assert HKV_B * HEAD_DIM == LANES and tq % CHUNK == 0 and (tq % tk == 0 or t == tq) and n_bias == 1 + -(-tq // tk)
    kern = functools.partial(_dsa_kernel, tq=tq, q_off=q_off, topk=topk)
    return pl.pallas_call(
        kern,
        grid=(b, t // tq),
        in_specs=[
            pl.BlockSpec((1, HI_B, DI_B, tq), lambda bi, i: (bi, 0, 0, i)),
            pl.BlockSpec((1, SUBLANES, tq), lambda bi, i: (bi, 0, i)),
            pl.BlockSpec((1, length, DI_B), lambda bi, i: (bi, 0, 0)),
            pl.BlockSpec((1, HKV_B, HEAD_DIM, cols), lambda bi, i: (bi, 0, 0, i)),
            pl.BlockSpec((1, length, LANES), lambda bi, i: (bi, 0, 0)),
            pl.BlockSpec((1, HKV_B, V_ROWS, length), lambda bi, i: (bi, 0, 0, 0)),
            pl.BlockSpec((n_bias, HKV_B, tk, cols), lambda bi, i: (0, 0, 0, 0), pipeline_mode=pl.Buffered(1)),
        ],
        out_specs=pl.BlockSpec((1, HKV_B, HEAD_DIM, cols), lambda bi, i: (bi, 0, 0, i)),
        out_shape=jax.ShapeDtypeStruct(qt.shape, BF16),
        scratch_shapes=[
            pltpu.VMEM((length // tk, tk, tq), jnp.int32),
            pltpu.VMEM((HKV_B, 1, cols), F32),
            pltpu.VMEM((HKV_B, V_ROWS, cols), F32),
        ],
        compiler_params=_params(("parallel", "arbitrary")),
        name="dsa",
    )(qit, wit, ki, qt, k, vt, nbt)


HG_SUB = SUBLANES
HG_NB = 4


def _hgrn_kernel(f_ref, i_ref, q_ref, g_ref, la_ref, l1_ref, oml_ref, ng_ref, s0_ref,
                 o_ref, sfin_ref, st_ref):
    c = pl.program_id(1)
    n_c = pl.num_programs(1)

    @pl.when(c == 0)
    def _():
        for bb in range(HG_NB):
            for hh in range(H_C):
                st_ref[bb, hh] = s0_ref[bb, hh].T

    for bb in range(HG_NB):
        _hgrn_chunk(bb, f_ref, i_ref, q_ref, g_ref, la_ref, l1_ref, oml_ref, ng_ref, o_ref, st_ref)

    @pl.when(c == n_c - 1)
    def _():
        for bb in range(HG_NB):
            for hh in range(H_C):
                sfin_ref[bb, hh] = st_ref[bb, hh].T


def _hgrn_chunk(bb, f_ref, i_ref, q_ref, g_ref, la_ref, l1_ref, oml_ref, ng_ref, o_ref, st_ref):
    ct = CHUNK
    zf = f_ref[bb]
    a = la_ref[...]
    cc = l1_ref[...] + _log_sigmoid(zf)
    logf = jnp.maximum(a, cc) + jnp.log1p(jnp.exp(-jnp.abs(a - cc)))
    kk = oml_ref[...] * _sigmoid(-zf)

    r_i = lax.broadcasted_iota(jnp.int32, (ct, ct), 0)
    c_i = lax.broadcasted_iota(jnp.int32, (ct, ct), 1)
    low = jnp.where(c_i <= r_i, 1.0, 0.0).astype(BF16)
    hi, mid, lo = _split3(logf)
    bcum = _dot(low, hi) + _dot(low, mid) + _dot(low, lo)

    sub_row = lax.broadcasted_iota(jnp.int32, (HG_SUB, DK_C), 0)
    a_col = lax.broadcasted_iota(jnp.int32, (HG_SUB, ct), 1)
    n_sub = ct // HG_SUB

    for hh in range(H_C):
        cs = slice(hh * DK_C, (hh + 1) * DK_C)
        qh = q_ref[bb, :, cs]
        kh = kk[:, cs]
        vh = i_ref[bb, :, cs]
        bh = bcum[:, cs]
        a_rows = []
        for bi in range(n_sub):
            r0 = bi * HG_SUB
            q_blk = qh[r0:r0 + HG_SUB]
            b_blk = bh[r0:r0 + HG_SUB]
            k_blk = kh[r0:r0 + HG_SUB]
            if bi > 0:
                ref = bh[r0:r0 + 1]
                qp = q_blk * jnp.exp(b_blk - ref)
                kp = kh[:r0] * jnp.exp(ref - bh[:r0])
                kp = jnp.concatenate([kp, jnp.zeros((ct - r0, DK_C), F32)], axis=0)
                a_row = _dot_nt(qp.astype(BF16), kp.astype(BF16))
            else:
                a_row = jnp.zeros((HG_SUB, ct), F32)
            for s in range(HG_SUB):
                d = jnp.where(sub_row >= s, b_blk - b_blk[s:s + 1], NEG_INF)
                x = q_blk * jnp.exp(d) * k_blk[s:s + 1]
                col = jnp.sum(x, axis=-1, keepdims=True)
                a_row = jnp.where(a_col == r0 + s, col, a_row)
            a_rows.append(a_row)
        a_mat = jnp.concatenate(a_rows, axis=0)
        st = st_ref[bb, hh]
        o = _dot(a_mat.astype(BF16), vh.astype(BF16)) + _dot_nt((qh * jnp.exp(bh)).astype(BF16), st.astype(BF16))
        bl = bh[ct - 1:ct]
        kdec = kh * jnp.exp(bl - bh)
        st_ref[bb, hh] = st * jnp.exp(bl) + _dot_tn(vh.astype(BF16), kdec.astype(BF16))
        o = o * lax.rsqrt(jnp.mean(o * o, axis=-1, keepdims=True) + LN_EPS) * ng_ref[:, cs]
        gh = g_ref[bb, :, cs]
        o_ref[bb, :, cs] = (o * (gh * _sigmoid(gh))).astype(o_ref.dtype)


def _hgrn(z3, la, l1, oml, ng, s0):
    b, t, _ = z3.shape
    assert b % HG_NB == 0
    n_c = t // CHUNK

    def zspec(col):
        blk = (col - Z_FC) // W_C
        return pl.BlockSpec((HG_NB, CHUNK, W_C), lambda bi, ci: (bi, ci, blk))

    vec = pl.BlockSpec((1, W_C), lambda bi, ci: (0, 0))
    return pl.pallas_call(
        _hgrn_kernel,
        grid=(b // HG_NB, n_c),
        in_specs=[zspec(Z_FC), zspec(Z_IC), zspec(Z_QC), zspec(Z_GC), vec, vec, vec, vec,
                  pl.BlockSpec((HG_NB, H_C, DK_C, DK_C), lambda bi, ci: (bi, 0, 0, 0))],
        out_specs=[pl.BlockSpec((HG_NB, CHUNK, W_C), lambda bi, ci: (bi, ci, 0)),
                   pl.BlockSpec((HG_NB, H_C, DK_C, DK_C), lambda bi, ci: (bi, 0, 0, 0))],
        out_shape=[jax.ShapeDtypeStruct((b, t, W_C), BF16),
                   jax.ShapeDtypeStruct((b, H_C, DK_C, DK_C), F32)],
        scratch_shapes=[pltpu.VMEM((HG_NB, H_C, DK_C, DK_C), F32)],
        compiler_params=_params(("parallel", "arbitrary")),
        name="hgrn",
    )(z3, z3, z3, z3, la, l1, oml, ng, s0)


def _merge_kernel(x_ref, xb_ref, oa_ref, ob_ref, oc_ref, wg_ref, bg_ref, wa_ref, wb_ref, wc_ref, wo_ref,
                  g_ref, b_ref, y_ref, yb_ref):
    xb = xb_ref[...]
    d = D_MODEL
    mix = None
    for br, (o_ref, w_ref) in enumerate(((oa_ref, wa_ref), (ob_ref, wb_ref), (oc_ref, wc_ref))):
        gate = _sigmoid(_dot(xb, wg_ref[:, br * d:(br + 1) * d]) + bg_ref[:, br * d:(br + 1) * d])
        term = gate * _dot(o_ref[...], w_ref[...])
        mix = term if mix is None else mix + term
    out = _dot(mix.astype(BF16), wo_ref[...])
    y = _layer_norm_rows(DEEPNORM_ALPHA * x_ref[...] + out, g_ref[...], b_ref[...])
    y_ref[...] = y
    yb_ref[...] = y.astype(BF16)


def _merge(x, xb, oa, ob, oc, wg, bg, wa, wb, wc, wo, g, b, tm):
    n, d = x.shape

    def row(width):
        return pl.BlockSpec((tm, width), lambda i: (i, 0))

    def full(arr):
        return pl.BlockSpec(arr.shape, lambda i: (0, 0))

    return pl.pallas_call(
        _merge_kernel,
        grid=(n // tm,),
        in_specs=[row(d), row(d), row(oa.shape[1]), row(ob.shape[1]), row(oc.shape[1]),
                  full(wg), full(bg), full(wa), full(wb), full(wc), full(wo), full(g), full(b)],
        out_specs=[row(d), row(d)],
        out_shape=[jax.ShapeDtypeStruct((n, d), F32), jax.ShapeDtypeStruct((n, d), BF16)],
        compiler_params=_params(("parallel",)),
        name="merge",
    )(x, xb, oa, ob, oc, wg, bg, wa, wb, wc, wo, g, b)


def _router_kernel(x_ref, w_ref, o_ref):
    x = x_ref[...]
    xh = x.astype(BF16)
    xl = (x - xh.astype(F32)).astype(BF16)
    w = w_ref[...]
    wh = w.astype(BF16)
    wl = (w - wh.astype(F32)).astype(BF16)
    logits = _dot(xh, wh) + _dot(xh, wl) + _dot(xl, wh)
    lane = lax.broadcasted_iota(jnp.int32, logits.shape, 1)
    logits = jnp.where(lane < N_EXPERTS, logits, NEG_INF)
    big = jnp.int32(LANES)
    m1 = jnp.max(logits, axis=-1, keepdims=True)
    i1 = jnp.min(jnp.where(logits == m1, lane, big), axis=-1, keepdims=True)
    rest = jnp.where(lane == i1, NEG_INF, logits)
    m2 = jnp.max(rest, axis=-1, keepdims=True)
    i2 = jnp.min(jnp.where(rest == m2, lane, big), axis=-1, keepdims=True)
    e2 = jnp.exp(m2 - m1)
    g1 = 1.0 / (1.0 + e2)
    g2 = e2 / (1.0 + e2)
    o_ref[...] = jnp.where(lane == i1, g1, 0.0) + jnp.where(lane == i2, g2, 0.0)


def _router(x, w, tm):
    n, d = x.shape
    return pl.pallas_call(
        _router_kernel,
        grid=(n // tm,),
        in_specs=[pl.BlockSpec((tm, d), lambda i: (i, 0)),
                  pl.BlockSpec(w.shape, lambda i: (0, 0))],
        out_specs=pl.BlockSpec((tm, LANES), lambda i: (i, 0)),
        out_shape=jax.ShapeDtypeStruct((n, LANES), F32),
        compiler_params=_params(("parallel",)),
        name="router",
    )(x, w)


def _ffn_kernel(x_ref, xb_ref, w1_ref, w3_ref, w2_ref, g_ref, b_ref, y_ref, yb_ref, acc_ref):
    f = pl.program_id(1)

    @pl.when(f == 0)
    def _():
        acc_ref[...] = jnp.zeros_like(acc_ref)

    xb = xb_ref[...]
    h1 = _dot(xb, w1_ref[...])
    h = (h1 * _sigmoid(h1)) * _dot(xb, w3_ref[...])
    acc_ref[...] += _dot(h.astype(BF16), w2_ref[...])

    @pl.when(f == pl.num_programs(1) - 1)
    def _():
        y = _layer_norm_rows(DEEPNORM_ALPHA * x_ref[...] + acc_ref[...], g_ref[...], b_ref[...])
        y_ref[...] = y
        yb_ref[...] = y.astype(BF16)


def _ffn(x, xb, w1, w3, w2, g, b, *, tm, tf):
    n, d = x.shape
    ff = w1.shape[1]
    return pl.pallas_call(
        _ffn_kernel,
        grid=(n // tm, ff // tf),
        in_specs=[
            pl.BlockSpec((tm, d), lambda i, f: (i, 0)),
            pl.BlockSpec((tm, d), lambda i, f: (i, 0)),
            pl.BlockSpec((d, tf), lambda i, f: (0, f)),
            pl.BlockSpec((d, tf), lambda i, f: (0, f)),
            pl.BlockSpec((tf, d), lambda i, f: (f, 0)),
            pl.BlockSpec((1, d), lambda i, f: (0, 0)),
            pl.BlockSpec((1, d), lambda i, f: (0, 0)),
        ],
        out_specs=[pl.BlockSpec((tm, d), lambda i, f: (i, 0)),
                   pl.BlockSpec((tm, d), lambda i, f: (i, 0))],
        out_shape=[jax.ShapeDtypeStruct((n, d), F32), jax.ShapeDtypeStruct((n, d), BF16)],
        scratch_shapes=[pltpu.VMEM((tm, d), F32)],
        compiler_params=_params(("parallel", "arbitrary")),
        name="ffn",
    )(x, xb, w1, w3, w2, g, b)


MOE_BS = 144


def _moe_slots(tm):
    return -(-tm // MOE_BS) * MOE_BS


def _moe_kernel(x_ref, xb_ref, comb_ref, w1_ref, w3_ref, w2_ref, g_ref, b_ref, y_ref, yb_ref,
                rank_ref, rankt_ref, xg_ref, yacc_ref, *, tm):
    e = pl.program_id(1)
    f = pl.program_id(2)
    n_f = pl.num_programs(2)
    bs = MOE_BS

    @pl.when((e == 0) & (f == 0))
    def _():
        comb = comb_ref[...]
        routed = comb > 0.0
        r_i = lax.broadcasted_iota(jnp.int32, (tm, tm), 0)
        c_i = lax.broadcasted_iota(jnp.int32, (tm, tm), 1)
        strict_low = jnp.where(c_i < r_i, 1.0, 0.0).astype(BF16)
        rank = _dot(strict_low, jnp.where(routed, 1.0, 0.0).astype(BF16))
        rank = jnp.where(routed, rank, -1.0)
        rank_ref[...] = rank
        rankt_ref[...] = rank.T[:N_EXPERTS]
        y_ref[...] = jnp.zeros_like(y_ref)

    rrow = rankt_ref[pl.ds(e, 1), :]
    n_e = jnp.sum(jnp.where(rrow >= 0.0, 1.0, 0.0)).astype(jnp.int32)
    n_blk = (n_e + bs - 1) // bs

    @pl.when(f == 0)
    def _():
        slot = lax.broadcasted_iota(jnp.int32, (bs, tm), 0).astype(F32)

        def gather(blk, carry):
            r0 = pl.multiple_of(blk * bs, bs)
            pick = jnp.where(rrow - (blk * bs).astype(F32) == slot, 1.0, 0.0).astype(BF16)
            xg_ref[pl.ds(r0, bs), :] = _dot(pick, xb_ref[...]).astype(BF16)
            yacc_ref[pl.ds(r0, bs), :] = jnp.zeros((bs, yacc_ref.shape[1]), F32)
            return carry

        lax.fori_loop(0, n_blk, gather, 0)

    def expert_rows(r0, rows):
        xg = xg_ref[pl.ds(r0, rows), :]
        h1 = _dot(xg, w1_ref[0])
        h = (h1 * _sigmoid(h1)) * _dot(xg, w3_ref[0])
        yacc_ref[pl.ds(r0, rows), :] += _dot(h.astype(BF16), w2_ref[0])

    def expert_pair(pair, carry):
        expert_rows(pl.multiple_of(pair * (2 * bs), 2 * bs), 2 * bs)
        return carry

    n_pair = n_blk // 2
    lax.fori_loop(0, n_pair, expert_pair, 0)

    @pl.when(n_blk % 2 == 1)
    def _():
        expert_rows(pl.multiple_of(n_pair * (2 * bs), bs), bs)

    @pl.when(f == n_f - 1)
    def _():
        lane = lax.broadcasted_iota(jnp.int32, (tm, LANES), 1)
        rcol = jnp.sum(jnp.where(lane == e, rank_ref[...], 0.0), axis=-1, keepdims=True)
        gcol = jnp.sum(jnp.where(lane == e, comb_ref[...], 0.0), axis=-1, keepdims=True)
        def scatter_rows(r0, rows):
            slot = lax.broadcasted_iota(jnp.int32, (tm, rows), 1).astype(F32)
            put = jnp.where(rcol - r0.astype(F32) == slot, 1.0, 0.0).astype(BF16)
            y_ref[...] += gcol * _dot(put, yacc_ref[pl.ds(r0, rows), :].astype(BF16))

        def scatter_pair(pair, carry):
            scatter_rows(pl.multiple_of(pair * (2 * bs), 2 * bs), 2 * bs)
            return carry

        lax.fori_loop(0, n_pair, scatter_pair, 0)

        @pl.when(n_blk % 2 == 1)
        def _():
            scatter_rows(pl.multiple_of(n_pair * (2 * bs), bs), bs)

    @pl.when((e == pl.num_programs(1) - 1) & (f == n_f - 1))
    def _():
        y = _layer_norm_rows(DEEPNORM_ALPHA * x_ref[...] + y_ref[...], g_ref[...], b_ref[...])
        y_ref[...] = y
        yb_ref[...] = y.astype(BF16)


def _moe(x, xb, comb, w1, w3, w2, g, b, *, tm, tf):
    n, d = x.shape
    n_e, _, ff = w1.shape
    assert n % tm == 0 and ff % tf == 0 and n_e == N_EXPERTS
    slots = _moe_slots(tm)
    kern = functools.partial(_moe_kernel, tm=tm)
    one = pl.Buffered(1)
    return pl.pallas_call(
        kern,
        grid=(n // tm, n_e, ff // tf),
        in_specs=[
            pl.BlockSpec((tm, d), lambda i, e, f: (i, 0), pipeline_mode=one),
            pl.BlockSpec((tm, d), lambda i, e, f: (i, 0), pipeline_mode=one),
            pl.BlockSpec((tm, LANES), lambda i, e, f: (i, 0), pipeline_mode=one),
            pl.BlockSpec((1, d, tf), lambda i, e, f: (e, 0, f)),
            pl.BlockSpec((1, d, tf), lambda i, e, f: (e, 0, f)),
            pl.BlockSpec((1, tf, d), lambda i, e, f: (e, f, 0)),
            pl.BlockSpec((1, d), lambda i, e, f: (0, 0)),
            pl.BlockSpec((1, d), lambda i, e, f: (0, 0)),
        ],
        out_specs=[pl.BlockSpec((tm, d), lambda i, e, f: (i, 0)),
                   pl.BlockSpec((tm, d), lambda i, e, f: (i, 0))],
        out_shape=[jax.ShapeDtypeStruct((n, d), F32), jax.ShapeDtypeStruct((n, d), BF16)],
        scratch_shapes=[pltpu.VMEM((tm, LANES), F32), pltpu.VMEM((N_EXPERTS, tm), F32),
                        pltpu.VMEM((slots, d), BF16), pltpu.VMEM((slots, d), F32)],
        compiler_params=_params(("parallel", "arbitrary", "arbitrary")),
        name="moe",
    )(x, xb, comb, w1, w3, w2, g, b)


def _rel_bucket(rel):
    nb = N_BUCKETS // 2
    max_exact = nb // 2
    ret = jnp.where(rel > 0, nb, 0)
    n = jnp.abs(rel)
    nf = jnp.maximum(n, 1).astype(jnp.float32)
    large = max_exact + (jnp.log(nf / max_exact) / math.log(MAX_DISTANCE / max_exact) * (nb - max_exact)).astype(jnp.int32)
    large = jnp.minimum(large, nb - 1)
    return ret + jnp.where(n < max_exact, n, large)


def _bias_tiles(rel_bias, tq):
    tk = DSA_TK
    r = jnp.arange(tq, dtype=jnp.int32)[:, None]
    c = jnp.arange(tk, dtype=jnp.int32)[None, :]
    n = 1 + -(-tq // tk)
    rel = jnp.stack([c - r + (s - 1) * tk for s in range(n)], axis=0)
    table = rel_bias.astype(F32)
    onehot = _rel_bucket(rel)[..., None] == jnp.arange(N_BUCKETS, dtype=jnp.int32)
    vals = jnp.sum(jnp.where(onehot[..., None], table, 0.0), axis=-2)
    far = table[_rel_bucket(jnp.int32(-(2 * MAX_DISTANCE)))]
    vals = (vals - far).transpose(0, 3, 2, 1).reshape(n, HKV_B, G_B, tk, tq)
    return vals.transpose(0, 1, 3, 2, 4).reshape(n, HKV_B, tk, G_B * tq)


def _prep_w_in(w_in_l, b_in_l):
    def tok_cols(a):
        parts = [a[..., IN_KA:IN_FA], a[..., IN_FC:IN_END], a[..., IN_KB:IN_QI], a[..., IN_KI:IN_WI], a[..., IN_FA:IN_QB]]
        width = sum(p.shape[-1] for p in parts)
        return jnp.concatenate(parts + [jnp.zeros(a.shape[:-1] + (N_Z - width,), a.dtype)], axis=-1)

    def feat_cols(a):
        parts = [a[..., IN_QA:IN_KA] * HEAD_DIM ** -0.5, a[..., IN_VA:IN_FA], a[..., IN_QB:IN_KB] * HEAD_DIM ** -0.5,
                 a[..., IN_QI:IN_KI] * DI_B ** -0.5, a[..., IN_VB:IN_QI], a[..., IN_WI:IN_FC] * HI_B ** -0.5]
        width = sum(p.shape[-1] for p in parts)
        return jnp.concatenate(parts + [jnp.zeros(a.shape[:-1] + (T_ROWS - width,), a.dtype)], axis=-1)

    w = w_in_l.astype(F32)
    bias = b_in_l.astype(F32)[None, :]
    return (tok_cols(w).astype(BF16), tok_cols(bias),
            feat_cols(w).T.astype(BF16), feat_cols(bias).T)


def _pad_time(a, length):
    pad = length - a.shape[1]
    if pad == 0:
        return a
    return jnp.concatenate([a, jnp.zeros((a.shape[0], pad) + a.shape[2:], a.dtype)], axis=1)


def _round_up(x, m):
    return (x + m - 1) // m * m


def _run_group(x, caches, wts):
    b, t, d = x.shape
    n = b * t
    p_len = 0 if caches is None else caches[0].shape[2]
    tm = min(512, n)
    xf = x.reshape(n, d).astype(F32)
    xb = xf.astype(BF16)

    fox_tq = min(1024, t)
    fox_tk = 512
    l_fox = _round_up(p_len + t, fox_tk)
    dsa_tq = min(256, t)
    l_dsa = _round_up(p_len + t, DSA_TK)
    topk = min(TOPK_MAX, (p_len + t) // 4)
    nb = _bias_tiles(wts["rel_bias"], dsa_tq)

    lb_cum = jnp.cumsum(jax.nn.softmax(wts["hgrn_lb"].astype(F32), axis=0), axis=0)

    states = []
    for l in range(DEPTH):
        w_tok, b_tok, w_feat, b_feat = _prep_w_in(wts["w_in"][l], wts["b_in"][l])
        (k_a, v_a, z, k_b, v_b, ki_b, logf_a, ka_bf, kb_bf, ki_bf,
         qa_t, va_t, qb_t, qi_t, vb_t, wi_t) = _proj(xb, w_tok, b_tok, w_feat, b_feat, b, min(512, t))
        z3 = z.reshape(b, t, 4 * W_C)
        logf_a = logf_a.reshape(b, t, H_A)
        ki_b = ki_b.reshape(b, t, DI_B)
        if caches is None:
            c_ka = c_va = c_kb = c_vb = c_ki = None
            lf_all = logf_a
            s0 = jnp.zeros((b, H_C, DK_C, DK_C), F32)
        else:
            c_ka, c_va, c_lf, c_kb, c_vb, c_ki, c_s = [c[l] for c in caches]
            lf_all = jnp.concatenate([c_lf.astype(F32), logf_a], axis=1)
            s0 = c_s.astype(F32)

        cum3 = _cumsum(_pad_time(lf_all, l_fox).transpose(0, 2, 1))
        o_a = _fox(qa_t.reshape(b, H_A, HEAD_DIM, t), _with_cache_tok(c_ka, ka_bf.reshape(b, t, W_A), l_fox),
                   _ck_table(cum3), _with_cache_feat(c_va, va_t, H_A, l_fox), tq=fox_tq, tk=fox_tk, q_off=p_len)

        n_t = t // dsa_tq
        qt_b = qb_t.reshape(b, HKV_B, G_B, HEAD_DIM, n_t, dsa_tq).transpose(0, 1, 3, 4, 2, 5)
        qt_b = qt_b.reshape(b, HKV_B, HEAD_DIM, n_t * G_B * dsa_tq)
        o_b = _dsa(qi_t.reshape(b, HI_B, DI_B, t), wi_t, _with_cache_tok(c_ki, ki_bf.reshape(b, t, DI_B), l_dsa),
                   qt_b, _with_cache_tok(c_kb, kb_bf.reshape(b, t, W_KV_B), l_dsa),
                   _with_cache_feat(c_vb, vb_t, HKV_B, l_dsa), nb, tq=dsa_tq, q_off=p_len, topk=topk)
        o_b = o_b.reshape(b, HKV_B, HEAD_DIM, n_t, G_B, dsa_tq).transpose(0, 3, 5, 1, 4, 2).reshape(n, H_B * HEAD_DIM)

        lb = (lb_cum[l] - lb_cum[0]).reshape(1, W_C)
        o_c, s_new = _hgrn(z3, jnp.log(lb), jnp.log1p(-lb), 1.0 - lb,
                           wts["hgrn_norm_g"][l].reshape(1, W_C).astype(F32), s0)

        x1, x1b = _merge(xf, xb, o_a.reshape(n, -1), o_b, o_c.reshape(n, -1),
                         wts["w_gate"][l].astype(BF16), wts["b_gate"][l][None, :].astype(F32),
                         wts["w_o_fox"][l].astype(BF16), wts["w_o_dsa"][l].astype(BF16), wts["w_o_hgrn"][l].astype(BF16),
                         wts["w_out"][l].astype(BF16), wts["ln1_g"][l][None, :], wts["ln1_b"][l][None, :], min(256, n))

        g2, b2 = wts["ln2_g"][l][None, :], wts["ln2_b"][l][None, :]
        if l % 2 == 0:
            j = l // 2
            xf, xb = _ffn(x1, x1b, wts["ffn_w1"][j].astype(BF16), wts["ffn_w3"][j].astype(BF16),
                          wts["ffn_w2"][j].astype(BF16), g2, b2, tm=tm, tf=1408)
        else:
            j = l // 2
            rw = jnp.zeros((d, LANES), F32).at[:, :N_EXPERTS].set(wts["moe_router"][j].astype(F32))
            comb = _router(x1, rw, tm)
            xf, xb = _moe(x1, x1b, comb, wts["moe_w1"][j].astype(BF16), wts["moe_w3"][j].astype(BF16),
                          wts["moe_w2"][j].astype(BF16), g2, b2, tm=min(1024, n), tf=1792)

        states.append((k_a.reshape(b, t, H_A, HEAD_DIM), v_a.reshape(b, t, H_A, HEAD_DIM), logf_a,
                       k_b.reshape(b, t, HKV_B, HEAD_DIM), v_b.reshape(b, t, HKV_B, HEAD_DIM), ki_b, s_new))
    stacked = [jnp.stack([s[i] for s in states], axis=0) for i in range(7)]
    return xf.reshape(b, t, d), stacked


def kernel(x_prompt, x_sample, cache_fox_k, cache_fox_v, cache_fox_logf, cache_dsa_k, cache_dsa_v, cache_dsa_kidx, state_hgrn,
           w_in, b_in, w_gate, b_gate, w_o_fox, w_o_dsa, w_o_hgrn, w_out, hgrn_lb, hgrn_norm_g, rel_bias,
           ln1_g, ln1_b, ln2_g, ln2_b, ffn_w1, ffn_w3, ffn_w2, moe_router, moe_w1, moe_w3, moe_w2):
    wts = dict(w_in=w_in, b_in=b_in, w_gate=w_gate, b_gate=b_gate, w_o_fox=w_o_fox, w_o_dsa=w_o_dsa, w_o_hgrn=w_o_hgrn,
               w_out=w_out, hgrn_lb=hgrn_lb, hgrn_norm_g=hgrn_norm_g, rel_bias=rel_bias, ln1_g=ln1_g, ln1_b=ln1_b,
               ln2_g=ln2_g, ln2_b=ln2_b, ffn_w1=ffn_w1, ffn_w3=ffn_w3, ffn_w2=ffn_w2, moe_router=moe_router,
               moe_w1=moe_w1, moe_w3=moe_w3, moe_w2=moe_w2)
    caches = (cache_fox_k, cache_fox_v, cache_fox_logf, cache_dsa_k, cache_dsa_v, cache_dsa_kidx, state_hgrn)
    y_prompt, sp = _run_group(x_prompt, None, wts)
    y_sample, ss = _run_group(x_sample, caches, wts)
    return (y_prompt, y_sample, sp[0], sp[1], sp[2], sp[3], sp[4], sp[5], sp[6],
            ss[0], ss[1], ss[2], ss[3], ss[4], ss[5], ss[6])
```

```python
import functools
import math

import jax
import jax.numpy as jnp
import numpy as np
from jax import lax
from jax.experimental import pallas as pl
from jax.experimental.pallas import tpu as pltpu

F32 = jnp.float32
BF16 = jnp.bfloat16
NEG_INF = float("-inf")

D_MODEL = 1024
DEPTH = 2
CHUNK = 64
HEAD_DIM = 64
H_A = 8
H_B = 8
HKV_B = 2
G_B = H_B // HKV_B
HI_B = 4
DI_B = 64
TOPK_MAX = 256
H_C = 4
DK_C = 128
W_C = H_C * DK_C
N_BUCKETS = 32
MAX_DISTANCE = 128
N_EXPERTS = 8
DEEPNORM_ALPHA = (2.0 * DEPTH) ** 0.25
LN_EPS = 1e-5

LANES = 128
SUBLANES = 8
VMEM_LIMIT = 56 * 1024 * 1024

W_A = H_A * HEAD_DIM
W_B = H_B * HEAD_DIM
W_KV_B = HKV_B * HEAD_DIM

IN_QA, IN_KA, IN_VA, IN_FA = 0, 512, 1024, 1536
IN_QB, IN_KB, IN_VB, IN_QI, IN_KI, IN_WI = 1544, 2056, 2184, 2312, 2568, 2632
IN_FC, IN_END = 2636, 4684
Z_KA, Z_VA = 0, 512
Z_FC, Z_IC, Z_QC, Z_GC = 1024, 1536, 2048, 2560
Z_KB, Z_VB, Z_KI, Z_FA = 3072, 3200, 3328, 3392
N_Z = 3456
T_QA, T_VA, T_QB, T_QI, T_VB, T_WI = 0, 512, 1024, 1536, 1792, 1920
T_ROWS = 1928
V_ROWS = 80


def _params(sem, vmem=VMEM_LIMIT):
    return pltpu.CompilerParams(dimension_semantics=sem, vmem_limit_bytes=vmem)


def _log_sigmoid(z):
    return jnp.minimum(z, 0.0) - jnp.log1p(jnp.exp(-jnp.abs(z)))


def _sigmoid(z):
    return 1.0 / (1.0 + jnp.exp(-z))


def _dot_nt(a, b):
    return lax.dot_general(a, b, (((1,), (1,)), ((), ())), preferred_element_type=F32)


def _dot_tn(a, b):
    return lax.dot_general(a, b, (((0,), (0,)), ((), ())), preferred_element_type=F32)


def _dot(a, b):
    return jnp.dot(a, b, preferred_element_type=F32)


def _split3(x):
    hi = x.astype(BF16)
    r1 = x - hi.astype(F32)
    mid = r1.astype(BF16)
    lo = (r1 - mid.astype(F32)).astype(BF16)
    return hi, mid, lo


def _layer_norm_rows(r, g, b):
    mu = jnp.mean(r, axis=-1, keepdims=True)
    rc = r - mu
    var = jnp.mean(rc * rc, axis=-1, keepdims=True)
    return rc * lax.rsqrt(var + LN_EPS) * g + b


def _proj_kernel(x_ref, w_ref, b_ref, wt_ref, bt_ref,
                 kaf_ref, vaf_ref, z_ref, kbf_ref, vbf_ref, kif_ref, lf_ref,
                 ka_ref, kb_ref, ki_ref, qa_ref, va_ref, qb_ref, qi_ref, vb_ref, wi_ref):
    x = x_ref[...]

    def cols(c0, width):
        return _dot(x, w_ref[:, c0:c0 + width]) + b_ref[:, c0:c0 + width]

    kaf_ref[...] = cols(Z_KA, W_A)
    vaf_ref[...] = cols(Z_VA, W_A)
    z_ref[...] = cols(Z_FC, 4 * W_C)
    kv_b = cols(Z_KB, 2 * W_KV_B)
    kbf_ref[...] = kv_b[:, :W_KV_B]
    vbf_ref[...] = kv_b[:, W_KV_B:]
    tail = cols(Z_KI, N_Z - Z_KI)
    kif_ref[...] = tail[:, :DI_B]
    lf_ref[...] = _log_sigmoid(tail[:, Z_FA - Z_KI:Z_FA - Z_KI + H_A])
    ka_ref[...] = kaf_ref[...].astype(BF16)
    kb_ref[...] = kv_b[:, :W_KV_B].astype(BF16)
    ki_ref[...] = tail[:, :DI_B].astype(BF16)
    zt = _dot_nt(wt_ref[...], x) + bt_ref[...]
    qa_ref[0] = zt[T_QA:T_QA + W_A].astype(BF16)
    qb_ref[0] = zt[T_QB:T_QB + W_B].astype(BF16)
    qi_ref[0] = zt[T_QI:T_QI + HI_B * DI_B].astype(BF16)
    wi_ref[0] = zt[T_WI:T_WI + SUBLANES]
    ones = jnp.ones((V_ROWS - HEAD_DIM, zt.shape[1]), BF16)
    for v_ref, row0, heads in ((va_ref, T_VA, H_A), (vb_ref, T_VB, HKV_B)):
        for h in range(heads):
            v_ref[0, h * V_ROWS:h * V_ROWS + HEAD_DIM] = zt[row0 + h * HEAD_DIM:row0 + (h + 1) * HEAD_DIM].astype(BF16)
            v_ref[0, h * V_ROWS + HEAD_DIM:(h + 1) * V_ROWS] = ones


def _proj(xb, w, b, wt, bt, batch, tm):
    n, k = xb.shape
    t = n // batch
    assert t % tm == 0
    tpb = t // tm
    full = pl.Buffered(1)

    def tok(width):
        return pl.BlockSpec((tm, width), lambda i: (i, 0))

    def feat(rows):
        return pl.BlockSpec((1, rows, tm), lambda i: (i // tpb, 0, i % tpb))

    def feat_shape(rows, dtype):
        return jax.ShapeDtypeStruct((batch, rows, t), dtype)

    return pl.pallas_call(
        _proj_kernel,
        grid=(n // tm,),
        in_specs=[
            tok(k),
            pl.BlockSpec(w.shape, lambda i: (0, 0), pipeline_mode=full),
            pl.BlockSpec(b.shape, lambda i: (0, 0), pipeline_mode=full),
            pl.BlockSpec(wt.shape, lambda i: (0, 0), pipeline_mode=full),
            pl.BlockSpec(bt.shape, lambda i: (0, 0), pipeline_mode=full),
        ],
        out_specs=[tok(W_A), tok(W_A), tok(4 * W_C), tok(W_KV_B), tok(W_KV_B), tok(DI_B), tok(H_A),
                   tok(W_A), tok(W_KV_B), tok(DI_B),
                   feat(W_A), feat(H_A * V_ROWS), feat(W_B), feat(HI_B * DI_B), feat(HKV_B * V_ROWS), feat(SUBLANES)],
        out_shape=[jax.ShapeDtypeStruct((n, W_A), F32), jax.ShapeDtypeStruct((n, W_A), F32),
                   jax.ShapeDtypeStruct((n, 4 * W_C), F32), jax.ShapeDtypeStruct((n, W_KV_B), F32),
                   jax.ShapeDtypeStruct((n, W_KV_B), F32), jax.ShapeDtypeStruct((n, DI_B), F32),
                   jax.ShapeDtypeStruct((n, H_A), F32),
                   jax.ShapeDtypeStruct((n, W_A), BF16),
                   jax.ShapeDtypeStruct((n, W_KV_B), BF16), jax.ShapeDtypeStruct((n, DI_B), BF16),
                   feat_shape(W_A, BF16), feat_shape(H_A * V_ROWS, BF16), feat_shape(W_B, BF16),
                   feat_shape(HI_B * DI_B, BF16), feat_shape(HKV_B * V_ROWS, BF16), feat_shape(SUBLANES, F32)],
        compiler_params=_params(("parallel",)),
        name="proj",
    )(xb, w, b, wt, bt)


def _cumsum_kernel(x_ref, hi_ref, mid_ref, lo_ref):
    c = x_ref[0]
    length = c.shape[-1]
    lane = lax.broadcasted_iota(jnp.int32, c.shape, 1)
    s = 1
    while s < length:
        c = c + jnp.where(lane >= s, pltpu.roll(c, s, 1), 0.0)
        s *= 2
    hi, mid, lo = _split3(c)
    hi_ref[0] = hi.astype(F32)
    mid_ref[0] = mid.astype(F32)
    lo_ref[0] = lo.astype(F32)


def _cumsum(x):
    b, r, length = x.shape
    spec = pl.BlockSpec((1, r, length), lambda i: (i, 0, 0))
    return pl.pallas_call(
        _cumsum_kernel,
        grid=(b,),
        in_specs=[spec],
        out_specs=[spec, spec, spec],
        out_shape=[jax.ShapeDtypeStruct(x.shape, F32)] * 3,
        compiler_params=_params(("parallel",)),
        name="cumsum",
    )(x)


FOX_HPS = 2
FOX_CK_TERMS = 3


def _fox_kernel(qt_ref, k_ref, ckx_ref, vt_ref, o_ref, m_ref, acc_ref, *, tq, tk, q_off):
    hps = FOX_HPS
    i = pl.program_id(2)
    q0 = q_off + i * tq
    nk = (q0 + tq + tk - 1) // tk
    n_full = (q0 + 1) // tk
    ck_row = lax.broadcasted_iota(jnp.int32, (LANES, tq), 0)
    zeros = jnp.zeros((HEAD_DIM, tq), BF16)
    q_cols = []
    for h in range(hps):
        d = ck_row - FOX_CK_TERMS * (pl.program_id(1) * hps + h)
        minus_one = jnp.where(d >= 0, jnp.where(d < FOX_CK_TERMS, -1.0, 0.0), 0.0).astype(BF16)
        q_parts = [zeros] * hps
        q_parts[h] = qt_ref[0, h]
        q_cols.append(jnp.concatenate(q_parts + [minus_one], axis=0))
    qt = jnp.concatenate(q_cols, axis=1)
    krow = lax.broadcasted_iota(jnp.int32, (tk, hps * tq), 0)
    qcol = q0 + (lax.broadcasted_iota(jnp.int32, (tk, hps * tq), 1) & (tq - 1))
    m_ref[...] = jnp.full((1, hps * tq), NEG_INF, F32)
    acc_ref[...] = jnp.zeros((V_ROWS, hps * tq), F32)

    def tile(j, masked):
        ks = pl.multiple_of(j * tk, tk)
        kk = jnp.concatenate([k_ref[0, pl.ds(ks, tk), :], ckx_ref[0, pl.ds(ks, tk), :]], axis=1)
        s = _dot(kk, qt)
        if masked:
            s = jnp.where(krow + ks <= qcol, s, NEG_INF)
        m_old = m_ref[...]
        m_new = jnp.maximum(m_old, jnp.max(s, axis=0, keepdims=True))
        p = jnp.exp(s - m_new).astype(BF16)
        pv = [_dot(vt_ref[0, h, :, pl.ds(ks, tk)], p[:, h * tq:(h + 1) * tq]) for h in range(hps)]
        acc_ref[...] = jnp.exp(m_old - m_new) * acc_ref[...] + jnp.concatenate(pv, axis=1)
        m_ref[...] = m_new

    def full_tile(j, carry):
        tile(j, False)
        return carry

    def edge_tile(j, carry):
        tile(j, True)
        return carry

    lax.fori_loop(0, n_full, full_tile, 0)
    lax.fori_loop(n_full, nk, edge_tile, 0)
    acc = acc_ref[...]
    o = acc[:HEAD_DIM] / acc[HEAD_DIM:HEAD_DIM + 1]
    o_ref[0] = jnp.concatenate([o[:, h * tq:(h + 1) * tq].T for h in range(hps)], axis=-1).astype(o_ref.dtype)


def _fox(qt, k, ckx, vt, *, tq, tk, q_off):
    b, h, dh, t = qt.shape
    length = k.shape[1]
    hps = FOX_HPS
    assert hps * dh == LANES and FOX_CK_TERMS * h <= LANES and length % tk == 0 and tq & (tq - 1) == 0
    kern = functools.partial(_fox_kernel, tq=tq, tk=tk, q_off=q_off)
    return pl.pallas_call(
        kern,
        grid=(b, h // hps, t // tq),
        in_specs=[
            pl.BlockSpec((1, hps, dh, tq), lambda bi, hi, i: (bi, hi, 0, i)),
            pl.BlockSpec((1, length, LANES), lambda bi, hi, i: (bi, 0, hi)),
            pl.BlockSpec((1, length, LANES), lambda bi, hi, i: (bi, 0, 0)),
            pl.BlockSpec((1, hps, V_ROWS, length), lambda bi, hi, i: (bi, hi, 0, 0)),
        ],
        out_specs=pl.BlockSpec((1, tq, hps * dh), lambda bi, hi, i: (bi, i, hi)),
        out_shape=jax.ShapeDtypeStruct((b, t, h * dh), BF16),
        scratch_shapes=[
            pltpu.VMEM((1, hps * tq), F32),
            pltpu.VMEM((V_ROWS, hps * tq), F32),
        ],
        compiler_params=_params(("parallel", "parallel", "arbitrary")),
        name="fox",
    )(qt, k, ckx, vt)


def _ck_table(cum3):
    b, _, length = cum3[0].shape
    ck = jnp.stack(cum3, axis=-1).transpose(0, 2, 1, 3).reshape(b, length, H_A * FOX_CK_TERMS)
    return jnp.concatenate([ck, jnp.zeros((b, length, LANES - H_A * FOX_CK_TERMS), F32)], axis=-1).astype(BF16)


def _with_cache_tok(cache, new, length):
    if cache is not None:
        new = jnp.concatenate([cache.reshape(cache.shape[0], cache.shape[1], -1).astype(BF16), new], axis=1)
    return _pad_time(new, length)


def _with_cache_feat(cache, new, n_heads, length):
    b, _, t = new.shape
    new = new.reshape(b, n_heads, V_ROWS, t)
    if cache is not None:
        old = cache.transpose(0, 2, 3, 1).astype(BF16)
        old = jnp.concatenate([old, jnp.ones(old.shape[:2] + (V_ROWS - HEAD_DIM, old.shape[3]), BF16)], axis=2)
        new = jnp.concatenate([old, new], axis=3)
    pad = length - new.shape[3]
    if pad:
        new = jnp.concatenate([new, jnp.zeros(new.shape[:3] + (pad,), BF16)], axis=3)
    return new


INT_MIN = -(2 ** 31)
KEY_NEG_INF = int(np.int32(np.uint32(0xFF800000) ^ np.uint32(0x7FFFFFFF)))
DSA_TK = 256


def _order_key(x):
    bits = lax.bitcast_convert_type(x, jnp.int32)
    return bits ^ ((bits >> 31) & jnp.int32(0x7FFFFFFF))


def _dsa_kernel(qit_ref, wit_ref, ki_ref, qt_ref, k_ref, vt_ref, nbt_ref, o_ref,
                sc_ref, m_ref, acc_ref, *, tq, q_off, topk):
    tk = DSA_TK
    i = pl.program_id(1)
    q0 = q_off + i * tq
    i_abs = q0 // tk
    q_tiles = -(-tq // tk)
    nk = i_abs + q_tiles
    cols = G_B * tq

    wit = wit_ref[0]
    wrows = [wit[hh:hh + 1, :] for hh in range(HI_B)]
    q_chunk = (q0 + lax.broadcasted_iota(jnp.int32, (tk, tq), 1)) >> 6
    krow = lax.broadcasted_iota(jnp.int32, (tk, tq), 0)

    def score_tile(j, carry):
        ks = pl.multiple_of(j * tk, tk)
        ki = ki_ref[0, pl.ds(ks, tk), :]
        tot = jnp.zeros((tk, tq), F32)
        for hh in range(HI_B):
            tot = tot + wrows[hh] * jnp.maximum(_dot(ki, qit_ref[0, hh]), 0.0)
        tot = jnp.where(((krow + ks) >> 6) <= q_chunk, tot, NEG_INF)
        sc_ref[j] = _order_key(tot)
        return carry

    lax.fori_loop(0, nk, score_tile, 0)

    kf = float(topk)

    def count(pred_fn):
        def tile_hits(j):
            hit = jnp.where(pred_fn(sc_ref[j]), 1.0, 0.0)
            parts = [hit[r:r + SUBLANES] for r in range(0, tk, SUBLANES)]
            while len(parts) > 1:
                parts = [parts[a] + parts[a + 1] for a in range(0, len(parts), 2)]
            return parts[0]

        acc8 = lax.fori_loop(0, nk, lambda j, acc8: acc8 + tile_hits(j), jnp.zeros((SUBLANES, tq), F32))
        return jnp.sum(acc8, axis=0, keepdims=True)

    c0 = count(lambda key: key >= 0)
    thr = jnp.where(c0 >= kf, jnp.int32(0), jnp.int32(INT_MIN))
    for bit in range(30, -1, -1):
        cand = thr + jnp.int32(1 << bit)
        c = count(lambda key, cand=cand: key >= cand)
        thr = jnp.where(c >= kf, cand, thr)

    cnt_gt = count(lambda key: key > thr)
    short = thr <= jnp.int32(KEY_NEG_INF)
    thr_b = jnp.broadcast_to(jnp.maximum(thr, jnp.int32(KEY_NEG_INF)), (tk, tq))
    need = jnp.where(short, 0.0, kf - cnt_gt)

    tri = jnp.where(lax.broadcasted_iota(jnp.int32, (tk, tk), 1) <= lax.broadcasted_iota(jnp.int32, (tk, tk), 0),
                    1.0, 0.0).astype(BF16)
    m_ref[...] = jnp.full((HKV_B, 1, cols), NEG_INF, F32)
    acc_ref[...] = jnp.zeros((HKV_B, V_ROWS, cols), F32)
    q_zero = jnp.zeros((HEAD_DIM, cols), BF16)
    q_ext = []
    for g in range(HKV_B):
        parts = [q_zero] * HKV_B
        parts[g] = qt_ref[0, g]
        q_ext.append(jnp.concatenate(parts, axis=0))

    n_far = jnp.maximum(i_abs - 1, 0)

    def attend_tile(j, eqc, near):
        ks = pl.multiple_of(j * tk, tk)
        key = sc_ref[j]
        gt = key > thr_b
        eq = key == thr_b
        eqf = jnp.where(eq, 1.0, 0.0)
        rank = _dot(tri, eqf.astype(BF16)) + eqc
        pen = jnp.where(gt, 0.0, jnp.where(eq, rank, 1e9))
        madd = jnp.where(pen <= need, 0.0, NEG_INF)
        madd = jnp.concatenate([madd] * G_B, axis=1)
        k_tile = k_ref[0, pl.ds(ks, tk), :]
        for g in range(HKV_B):
            s = _dot(k_tile, q_ext[g]) + madd
            if near:
                s = s + nbt_ref[j - i_abs + 1, g]
            m_old = m_ref[g]
            m_new = jnp.maximum(m_old, jnp.max(s, axis=0, keepdims=True))
            m_safe = jnp.where(m_new == NEG_INF, 0.0, m_new)
            p = jnp.exp(s - m_safe)
            acc_ref[g] = jnp.exp(m_old - m_safe) * acc_ref[g] + _dot(vt_ref[0, g, :, pl.ds(ks, tk)], p.astype(BF16))
            m_ref[g] = m_new
        return eqc + jnp.sum(eqf, axis=0, keepdims=True)

    eqc = lax.fori_loop(0, n_far, functools.partial(attend_tile, near=False), jnp.zeros((1, tq), F32))
    lax.fori_loop(n_far, nk, functools.partial(attend_tile, near=True), eqc)
    for g in range(HKV_B):
        acc = acc_ref[g]
        o_ref[0, g] = (acc[:HEAD_DIM] / acc[HEAD_DIM:HEAD_DIM + 1]).astype(o_ref.dtype)


def _dsa(qit, wit, ki, qt, k, vt, nbt, *, tq, q_off, topk):
    b, _, _, t = qit.shape
    length = ki.shape[1]
    cols = G_B * tq
    n_bias = nbt.shape[0]
    tk = DSA_TK
    assert tk >= MAX_DISTANCE and q_off % tk == 0 and length % tk == 0 and length >= q_off + t
    assert HKV_B * HEAD_DIM == LANES and tq % CHUNK == 0 and (tq % tk == 0 or t == tq) and n_bias == 1 + -(-tq // tk)
    kern = functools.partial(_dsa_kernel, tq=tq, q_off=q_off, topk=topk)
    return pl.pallas_call(
        kern,
        grid=(b, t // tq),
        in_specs=[
            pl.BlockSpec((1, HI_B, DI_B, tq), lambda bi, i: (bi, 0, 0, i)),
            pl.BlockSpec((1, SUBLANES, tq), lambda bi, i: (bi, 0, i)),
            pl.BlockSpec((1, length, DI_B), lambda bi, i: (bi, 0, 0)),
            pl.BlockSpec((1, HKV_B, HEAD_DIM, cols), lambda bi, i: (bi, 0, 0, i)),
            pl.BlockSpec((1, length, LANES), lambda bi, i: (bi, 0, 0)),
            pl.BlockSpec((1, HKV_B, V_ROWS, length), lambda bi, i: (bi, 0, 0, 0)),
            pl.BlockSpec((n_bias, HKV_B, tk, cols), lambda bi, i: (0, 0, 0, 0), pipeline_mode=pl.Buffered(1)),
        ],
        out_specs=pl.BlockSpec((1, HKV_B, HEAD_DIM, cols), lambda bi, i: (bi, 0, 0, i)),
        out_shape=jax.ShapeDtypeStruct(qt.shape, BF16),
        scratch_shapes=[
            pltpu.VMEM((length // tk, tk, tq), jnp.int32),
            pltpu.VMEM((HKV_B, 1, cols), F32),
            pltpu.VMEM((HKV_B, V_ROWS, cols), F32),
        ],
        compiler_params=_params(("parallel", "arbitrary")),
        name="dsa",
    )(qit, wit, ki, qt, k, vt, nbt)


HG_SUB = SUBLANES
HG_NB = 4


def _hgrn_kernel(f_ref, i_ref, q_ref, g_ref, la_ref, l1_ref, oml_ref, ng_ref, s0_ref,
                 o_ref, sfin_ref, st_ref):
    c = pl.program_id(1)
    n_c = pl.num_programs(1)

    @pl.when(c == 0)
    def _():
        for bb in range(HG_NB):
            for hh in range(H_C):
                st_ref[bb, hh] = s0_ref[bb, hh].T

    for bb in range(HG_NB):
        _hgrn_chunk(bb, f_ref, i_ref, q_ref, g_ref, la_ref, l1_ref, oml_ref, ng_ref, o_ref, st_ref)

    @pl.when(c == n_c - 1)
    def _():
        for bb in range(HG_NB):
            for hh in range(H_C):
                sfin_ref[bb, hh] = st_ref[bb, hh].T


def _hgrn_chunk(bb, f_ref, i_ref, q_ref, g_ref, la_ref, l1_ref, oml_ref, ng_ref, o_ref, st_ref):
    ct = CHUNK
    zf = f_ref[bb]
    a = la_ref[...]
    cc = l1_ref[...] + _log_sigmoid(zf)
    logf = jnp.maximum(a, cc) + jnp.log1p(jnp.exp(-jnp.abs(a - cc)))
    kk = oml_ref[...] * _sigmoid(-zf)

    r_i = lax.broadcasted_iota(jnp.int32, (ct, ct), 0)
    c_i = lax.broadcasted_iota(jnp.int32, (ct, ct), 1)
    low = jnp.where(c_i <= r_i, 1.0, 0.0).astype(BF16)
    hi, mid, lo = _split3(logf)
    bcum = _dot(low, hi) + _dot(low, mid) + _dot(low, lo)

    sub_row = lax.broadcasted_iota(jnp.int32, (HG_SUB, DK_C), 0)
    a_col = lax.broadcasted_iota(jnp.int32, (HG_SUB, ct), 1)
    n_sub = ct // HG_SUB

    for hh in range(H_C):
        cs = slice(hh * DK_C, (hh + 1) * DK_C)
        qh = q_ref[bb, :, cs]
        kh = kk[:, cs]
        vh = i_ref[bb, :, cs]
        bh = bcum[:, cs]
        a_rows = []
        for bi in range(n_sub):
            r0 = bi * HG_SUB
            q_blk = qh[r0:r0 + HG_SUB]
            b_blk = bh[r0:r0 + HG_SUB]
            k_blk = kh[r0:r0 + HG_SUB]
            if bi > 0:
                ref = bh[r0:r0 + 1]
                qp = q_blk * jnp.exp(b_blk - ref)
                kp = kh[:r0] * jnp.exp(ref - bh[:r0])
                kp = jnp.concatenate([kp, jnp.zeros((ct - r0, DK_C), F32)], axis=0)
                a_row = _dot_nt(qp.astype(BF16), kp.astype(BF16))
            else:
                a_row = jnp.zeros((HG_SUB, ct), F32)
            for s in range(HG_SUB):
                d = jnp.where(sub_row >= s, b_blk - b_blk[s:s + 1], NEG_INF)
                x = q_blk * jnp.exp(d) * k_blk[s:s + 1]
                col = jnp.sum(x, axis=-1, keepdims=True)
                a_row = jnp.where(a_col == r0 + s, col, a_row)
            a_rows.append(a_row)
        a_mat = jnp.concatenate(a_rows, axis=0)
        st = st_ref[bb, hh]
        o = _dot(a_mat.astype(BF16), vh.astype(BF16)) + _dot_nt((qh * jnp.exp(bh)).astype(BF16), st.astype(BF16))
        bl = bh[ct - 1:ct]
        kdec = kh * jnp.exp(bl - bh)
        st_ref[bb, hh] = st * jnp.exp(bl) + _dot_tn(vh.astype(BF16), kdec.astype(BF16))
        o = o * lax.rsqrt(jnp.mean(o * o, axis=-1, keepdims=True) + LN_EPS) * ng_ref[:, cs]
        gh = g_ref[bb, :, cs]
        o_ref[bb, :, cs] = (o * (gh * _sigmoid(gh))).astype(o_ref.dtype)


def _hgrn(z3, la, l1, oml, ng, s0):
    b, t, _ = z3.shape
    assert b % HG_NB == 0
    n_c = t // CHUNK

    def zspec(col):
        blk = (col - Z_FC) // W_C
        return pl.BlockSpec((HG_NB, CHUNK, W_C), lambda bi, ci: (bi, ci, blk))

    vec = pl.BlockSpec((1, W_C), lambda bi, ci: (0, 0))
    return pl.pallas_call(
        _hgrn_kernel,
        grid=(b // HG_NB, n_c),
        in_specs=[zspec(Z_FC), zspec(Z_IC), zspec(Z_QC), zspec(Z_GC), vec, vec, vec, vec,
                  pl.BlockSpec((HG_NB, H_C, DK_C, DK_C), lambda bi, ci: (bi, 0, 0, 0))],
        out_specs=[pl.BlockSpec((HG_NB, CHUNK, W_C), lambda bi, ci: (bi, ci, 0)),
                   pl.BlockSpec((HG_NB, H_C, DK_C, DK_C), lambda bi, ci: (bi, 0, 0, 0))],
        out_shape=[jax.ShapeDtypeStruct((b, t, W_C), BF16),
                   jax.ShapeDtypeStruct((b, H_C, DK_C, DK_C), F32)],
        scratch_shapes=[pltpu.VMEM((HG_NB, H_C, DK_C, DK_C), F32)],
        compiler_params=_params(("parallel", "arbitrary")),
        name="hgrn",
    )(z3, z3, z3, z3, la, l1, oml, ng, s0)


def _merge_kernel(x_ref, xb_ref, oa_ref, ob_ref, oc_ref, wg_ref, bg_ref, wa_ref, wb_ref, wc_ref, wo_ref,
                  g_ref, b_ref, y_ref, yb_ref):
    xb = xb_ref[...]
    d = D_MODEL
    mix = None
    for br, (o_ref, w_ref) in enumerate(((oa_ref, wa_ref), (ob_ref, wb_ref), (oc_ref, wc_ref))):
        gate = _sigmoid(_dot(xb, wg_ref[:, br * d:(br + 1) * d]) + bg_ref[:, br * d:(br + 1) * d])
        term = gate * _dot(o_ref[...], w_ref[...])
        mix = term if mix is None else mix + term
    out = _dot(mix.astype(BF16), wo_ref[...])
    y = _layer_norm_rows(DEEPNORM_ALPHA * x_ref[...] + out, g_ref[...], b_ref[...])
    y_ref[...] = y
    yb_ref[...] = y.astype(BF16)


def _merge(x, xb, oa, ob, oc, wg, bg, wa, wb, wc, wo, g, b, tm):
    n, d = x.shape

    def row(width):
        return pl.BlockSpec((tm, width), lambda i: (i, 0))

    def full(arr):
        return pl.BlockSpec(arr.shape, lambda i: (0, 0), pipeline_mode=pl.Buffered(1))

    return pl.pallas_call(
        _merge_kernel,
        grid=(n // tm,),
        in_specs=[row(d), row(d), row(oa.shape[1]), row(ob.shape[1]), row(oc.shape[1]),
                  full(wg), full(bg), full(wa), full(wb), full(wc), full(wo), full(g), full(b)],
        out_specs=[row(d), row(d)],
        out_shape=[jax.ShapeDtypeStruct((n, d), F32), jax.ShapeDtypeStruct((n, d), BF16)],
        compiler_params=_params(("parallel",)),
        name="merge",
    )(x, xb, oa, ob, oc, wg, bg, wa, wb, wc, wo, g, b)


def _router_kernel(x_ref, w_ref, o_ref):
    x = x_ref[...]
    xh = x.astype(BF16)
    xl = (x - xh.astype(F32)).astype(BF16)
    w = w_ref[...]
    wh = w.astype(BF16)
    wl = (w - wh.astype(F32)).astype(BF16)
    logits = _dot(xh, wh) + _dot(xh, wl) + _dot(xl, wh)
    lane = lax.broadcasted_iota(jnp.int32, logits.shape, 1)
    logits = jnp.where(lane < N_EXPERTS, logits, NEG_INF)
    big = jnp.int32(LANES)
    m1 = jnp.max(logits, axis=-1, keepdims=True)
    i1 = jnp.min(jnp.where(logits == m1, lane, big), axis=-1, keepdims=True)
    rest = jnp.where(lane == i1, NEG_INF, logits)
    m2 = jnp.max(rest, axis=-1, keepdims=True)
    i2 = jnp.min(jnp.where(rest == m2, lane, big), axis=-1, keepdims=True)
    e2 = jnp.exp(m2 - m1)
    g1 = 1.0 / (1.0 + e2)
    g2 = e2 / (1.0 + e2)
    o_ref[...] = jnp.where(lane == i1, g1, 0.0) + jnp.where(lane == i2, g2, 0.0)


def _router(x, w, tm):
    n, d = x.shape
    return pl.pallas_call(
        _router_kernel,
        grid=(n // tm,),
        in_specs=[pl.BlockSpec((tm, d), lambda i: (i, 0)),
                  pl.BlockSpec(w.shape, lambda i: (0, 0))],
        out_specs=pl.BlockSpec((tm, LANES), lambda i: (i, 0)),
        out_shape=jax.ShapeDtypeStruct((n, LANES), F32),
        compiler_params=_params(("parallel",)),
        name="router",
    )(x, w)


def _ffn_kernel(x_ref, xb_ref, w1_ref, w3_ref, w2_ref, g_ref, b_ref, y_ref, yb_ref, acc_ref):
    f = pl.program_id(1)

    @pl.when(f == 0)
    def _():
        acc_ref[...] = jnp.zeros_like(acc_ref)

    xb = xb_ref[...]
    h1 = _dot(xb, w1_ref[...])
    h = (h1 * _sigmoid(h1)) * _dot(xb, w3_ref[...])
    acc_ref[...] += _dot(h.astype(BF16), w2_ref[...])

    @pl.when(f == pl.num_programs(1) - 1)
    def _():
        y = _layer_norm_rows(DEEPNORM_ALPHA * x_ref[...] + acc_ref[...], g_ref[...], b_ref[...])
        y_ref[...] = y
        yb_ref[...] = y.astype(BF16)


def _ffn(x, xb, w1, w3, w2, g, b, *, tm, tf):
    n, d = x.shape
    ff = w1.shape[1]
    return pl.pallas_call(
        _ffn_kernel,
        grid=(n // tm, ff // tf),
        in_specs=[
            pl.BlockSpec((tm, d), lambda i, f: (i, 0)),
            pl.BlockSpec((tm, d), lambda i, f: (i, 0)),
            pl.BlockSpec((d, tf), lambda i, f: (0, f)),
            pl.BlockSpec((d, tf), lambda i, f: (0, f)),
            pl.BlockSpec((tf, d), lambda i, f: (f, 0)),
            pl.BlockSpec((1, d), lambda i, f: (0, 0)),
            pl.BlockSpec((1, d), lambda i, f: (0, 0)),
        ],
        out_specs=[pl.BlockSpec((tm, d), lambda i, f: (i, 0)),
                   pl.BlockSpec((tm, d), lambda i, f: (i, 0))],
        out_shape=[jax.ShapeDtypeStruct((n, d), F32), jax.ShapeDtypeStruct((n, d), BF16)],
        scratch_shapes=[pltpu.VMEM((tm, d), F32)],
        compiler_params=_params(("parallel", "arbitrary")),
        name="ffn",
    )(x, xb, w1, w3, w2, g, b)


MOE_BS = 144


def _moe_slots(tm):
    return -(-tm // MOE_BS) * MOE_BS


def _moe_kernel(x_ref, xb_ref, comb_ref, w1_ref, w3_ref, w2_ref, g_ref, b_ref, y_ref, yb_ref,
                rank_ref, rankt_ref, xg_ref, yacc_ref, *, tm):
    e = pl.program_id(1)
    f = pl.program_id(2)
    n_f = pl.num_programs(2)
    bs = MOE_BS

    @pl.when((e == 0) & (f == 0))
    def _():
        comb = comb_ref[...]
        routed = comb > 0.0
        r_i = lax.broadcasted_iota(jnp.int32, (tm, tm), 0)
        c_i = lax.broadcasted_iota(jnp.int32, (tm, tm), 1)
        strict_low = jnp.where(c_i < r_i, 1.0, 0.0).astype(BF16)
        rank = _dot(strict_low, jnp.where(routed, 1.0, 0.0).astype(BF16))
        rank = jnp.where(routed, rank, -1.0)
        rank_ref[...] = rank
        rankt_ref[...] = rank.T[:N_EXPERTS]
        y_ref[...] = jnp.zeros_like(y_ref)

    rrow = rankt_ref[pl.ds(e, 1), :]
    n_e = jnp.sum(jnp.where(rrow >= 0.0, 1.0, 0.0)).astype(jnp.int32)
    n_blk = (n_e + bs - 1) // bs

    @pl.when(f == 0)
    def _():
        slot = lax.broadcasted_iota(jnp.int32, (bs, tm), 0).astype(F32)

        def gather(blk, carry):
            r0 = pl.multiple_of(blk * bs, bs)
            pick = jnp.where(rrow - (blk * bs).astype(F32) == slot, 1.0, 0.0).astype(BF16)
            xg_ref[pl.ds(r0, bs), :] = _dot(pick, xb_ref[...]).astype(BF16)
            yacc_ref[pl.ds(r0, bs), :] = jnp.zeros((bs, yacc_ref.shape[1]), F32)
            return carry

        lax.fori_loop(0, n_blk, gather, 0)

    def expert_rows(r0, rows):
        xg = xg_ref[pl.ds(r0, rows), :]
        h1 = _dot(xg, w1_ref[0])
        h = (h1 * _sigmoid(h1)) * _dot(xg, w3_ref[0])
        yacc_ref[pl.ds(r0, rows), :] += _dot(h.astype(BF16), w2_ref[0])

    def expert_pair(pair, carry):
        expert_rows(pl.multiple_of(pair * (2 * bs), 2 * bs), 2 * bs)
        return carry

    n_pair = n_blk // 2
    lax.fori_loop(0, n_pair, expert_pair, 0)

    @pl.when(n_blk % 2 == 1)
    def _():
        expert_rows(pl.multiple_of(n_pair * (2 * bs), bs), bs)

    @pl.when(f == n_f - 1)
    def _():
        lane = lax.broadcasted_iota(jnp.int32, (tm, LANES), 1)
        rcol = jnp.sum(jnp.where(lane == e, rank_ref[...], 0.0), axis=-1, keepdims=True)
        gcol = jnp.sum(jnp.where(lane == e, comb_ref[...], 0.0), axis=-1, keepdims=True)
        def scatter_rows(r0, rows):
            slot = lax.broadcasted_iota(jnp.int32, (tm, rows), 1).astype(F32)
            put = jnp.where(rcol - r0.astype(F32) == slot, 1.0, 0.0).astype(BF16)
            y_ref[...] += gcol * _dot(put, yacc_ref[pl.ds(r0, rows), :].astype(BF16))

        def scatter_pair(pair, carry):
            scatter_rows(pl.multiple_of(pair * (2 * bs), 2 * bs), 2 * bs)
            return carry

        lax.fori_loop(0, n_pair, scatter_pair, 0)

        @pl.when(n_blk % 2 == 1)
        def _():
            scatter_rows(pl.multiple_of(n_pair * (2 * bs), bs), bs)

    @pl.when((e == pl.num_programs(1) - 1) & (f == n_f - 1))
    def _():
        y = _layer_norm_rows(DEEPNORM_ALPHA * x_ref[...] + y_ref[...], g_ref[...], b_ref[...])
        y_ref[...] = y
        yb_ref[...] = y.astype(BF16)


def _moe(x, xb, comb, w1, w3, w2, g, b, *, tm, tf):
    n, d = x.shape
    n_e, _, ff = w1.shape
    assert n % tm == 0 and ff % tf == 0 and n_e == N_EXPERTS
    slots = _moe_slots(tm)
    kern = functools.partial(_moe_kernel, tm=tm)
    one = pl.Buffered(1)
    return pl.pallas_call(
        kern,
        grid=(n // tm, n_e, ff // tf),
        in_specs=[
            pl.BlockSpec((tm, d), lambda i, e, f: (i, 0), pipeline_mode=one),
            pl.BlockSpec((tm, d), lambda i, e, f: (i, 0), pipeline_mode=one),
            pl.BlockSpec((tm, LANES), lambda i, e, f: (i, 0), pipeline_mode=one),
            pl.BlockSpec((1, d, tf), lambda i, e, f: (e, 0, f)),
            pl.BlockSpec((1, d, tf), lambda i, e, f: (e, 0, f)),
            pl.BlockSpec((1, tf, d), lambda i, e, f: (e, f, 0)),
            pl.BlockSpec((1, d), lambda i, e, f: (0, 0)),
            pl.BlockSpec((1, d), lambda i, e, f: (0, 0)),
        ],
        out_specs=[pl.BlockSpec((tm, d), lambda i, e, f: (i, 0)),
                   pl.BlockSpec((tm, d), lambda i, e, f: (i, 0))],
        out_shape=[jax.ShapeDtypeStruct((n, d), F32), jax.ShapeDtypeStruct((n, d), BF16)],
        scratch_shapes=[pltpu.VMEM((tm, LANES), F32), pltpu.VMEM((N_EXPERTS, tm), F32),
                        pltpu.VMEM((slots, d), BF16), pltpu.VMEM((slots, d), F32)],
        compiler_params=_params(("parallel", "arbitrary", "arbitrary")),
        name="moe",
    )(x, xb, comb, w1, w3, w2, g, b)


def _rel_bucket(rel):
    nb = N_BUCKETS // 2
    max_exact = nb // 2
    ret = jnp.where(rel > 0, nb, 0)
    n = jnp.abs(rel)
    nf = jnp.maximum(n, 1).astype(jnp.float32)
    large = max_exact + (jnp.log(nf / max_exact) / math.log(MAX_DISTANCE / max_exact) * (nb - max_exact)).astype(jnp.int32)
    large = jnp.minimum(large, nb - 1)
    return ret + jnp.where(n < max_exact, n, large)


def _bias_tiles(rel_bias, tq):
    tk = DSA_TK
    r = jnp.arange(tq, dtype=jnp.int32)[:, None]
    c = jnp.arange(tk, dtype=jnp.int32)[None, :]
    n = 1 + -(-tq // tk)
    rel = jnp.stack([c - r + (s - 1) * tk for s in range(n)], axis=0)
    table = rel_bias.astype(F32)
    onehot = _rel_bucket(rel)[..., None] == jnp.arange(N_BUCKETS, dtype=jnp.int32)
    vals = jnp.sum(jnp.where(onehot[..., None], table, 0.0), axis=-2)
    far = table[_rel_bucket(jnp.int32(-(2 * MAX_DISTANCE)))]
    vals = (vals - far).transpose(0, 3, 2, 1).reshape(n, HKV_B, G_B, tk, tq)
    return vals.transpose(0, 1, 3, 2, 4).reshape(n, HKV_B, tk, G_B * tq)


def _prep_w_in(w_in_l, b_in_l):
    def tok_cols(a):
        parts = [a[..., IN_KA:IN_FA], a[..., IN_FC:IN_END], a[..., IN_KB:IN_QI], a[..., IN_KI:IN_WI], a[..., IN_FA:IN_QB]]
        width = sum(p.shape[-1] for p in parts)
        return jnp.concatenate(parts + [jnp.zeros(a.shape[:-1] + (N_Z - width,), a.dtype)], axis=-1)

    def feat_cols(a):
        parts = [a[..., IN_QA:IN_KA] * HEAD_DIM ** -0.5, a[..., IN_VA:IN_FA], a[..., IN_QB:IN_KB] * HEAD_DIM ** -0.5,
                 a[..., IN_QI:IN_KI] * DI_B ** -0.5, a[..., IN_VB:IN_QI], a[..., IN_WI:IN_FC] * HI_B ** -0.5]
        width = sum(p.shape[-1] for p in parts)
        return jnp.concatenate(parts + [jnp.zeros(a.shape[:-1] + (T_ROWS - width,), a.dtype)], axis=-1)

    w = w_in_l.astype(F32)
    bias = b_in_l.astype(F32)[None, :]
    return (tok_cols(w).astype(BF16), tok_cols(bias),
            feat_cols(w).T.astype(BF16), feat_cols(bias).T)


def _pad_time(a, length):
    pad = length - a.shape[1]
    if pad == 0:
        return a
    return jnp.concatenate([a, jnp.zeros((a.shape[0], pad) + a.shape[2:], a.dtype)], axis=1)


def _round_up(x, m):
    return (x + m - 1) // m * m


def _run_group(x, caches, wts):
    b, t, d = x.shape
    n = b * t
    p_len = 0 if caches is None else caches[0].shape[2]
    tm = min(512, n)
    xf = x.reshape(n, d).astype(F32)
    xb = xf.astype(BF16)

    fox_tq = min(1024, t)
    fox_tk = 512
    l_fox = _round_up(p_len + t, fox_tk)
    dsa_tq = min(256, t)
    l_dsa = _round_up(p_len + t, DSA_TK)
    topk = min(TOPK_MAX, (p_len + t) // 4)
    nb = _bias_tiles(wts["rel_bias"], dsa_tq)

    lb_cum = jnp.cumsum(jax.nn.softmax(wts["hgrn_lb"].astype(F32), axis=0), axis=0)

    states = []
    for l in range(DEPTH):
        w_tok, b_tok, w_feat, b_feat = _prep_w_in(wts["w_in"][l], wts["b_in"][l])
        (k_a, v_a, z, k_b, v_b, ki_b, logf_a, ka_bf, kb_bf, ki_bf,
         qa_t, va_t, qb_t, qi_t, vb_t, wi_t) = _proj(xb, w_tok, b_tok, w_feat, b_feat, b, min(512, t))
        z3 = z.reshape(b, t, 4 * W_C)
        logf_a = logf_a.reshape(b, t, H_A)
        ki_b = ki_b.reshape(b, t, DI_B)
        if caches is None:
            c_ka = c_va = c_kb = c_vb = c_ki = None
            lf_all = logf_a
            s0 = jnp.zeros((b, H_C, DK_C, DK_C), F32)
        else:
            c_ka, c_va, c_lf, c_kb, c_vb, c_ki, c_s = [c[l] for c in caches]
            lf_all = jnp.concatenate([c_lf.astype(F32), logf_a], axis=1)
            s0 = c_s.astype(F32)

        cum3 = _cumsum(_pad_time(lf_all, l_fox).transpose(0, 2, 1))
        o_a = _fox(qa_t.reshape(b, H_A, HEAD_DIM, t), _with_cache_tok(c_ka, ka_bf.reshape(b, t, W_A), l_fox),
                   _ck_table(cum3), _with_cache_feat(c_va, va_t, H_A, l_fox), tq=fox_tq, tk=fox_tk, q_off=p_len)

        n_t = t // dsa_tq
        qt_b = qb_t.reshape(b, HKV_B, G_B, HEAD_DIM, n_t, dsa_tq).transpose(0, 1, 3, 4, 2, 5)
        qt_b = qt_b.reshape(b, HKV_B, HEAD_DIM, n_t * G_B * dsa_tq)
        o_b = _dsa(qi_t.reshape(b, HI_B, DI_B, t), wi_t, _with_cache_tok(c_ki, ki_bf.reshape(b, t, DI_B), l_dsa),
                   qt_b, _with_cache_tok(c_kb, kb_bf.reshape(b, t, W_KV_B), l_dsa),
                   _with_cache_feat(c_vb, vb_t, HKV_B, l_dsa), nb, tq=dsa_tq, q_off=p_len, topk=topk)
        o_b = o_b.reshape(b, HKV_B, HEAD_DIM, n_t, G_B, dsa_tq).transpose(0, 3, 5, 1, 4, 2).reshape(n, H_B * HEAD_DIM)

        lb = (lb_cum[l] - lb_cum[0]).reshape(1, W_C)
        o_c, s_new = _hgrn(z3, jnp.log(lb), jnp.log1p(-lb), 1.0 - lb,
                           wts["hgrn_norm_g"][l].reshape(1, W_C).astype(F32), s0)

        x1, x1b = _merge(xf, xb, o_a.reshape(n, -1), o_b, o_c.reshape(n, -1),
                         wts["w_gate"][l].astype(BF16), wts["b_gate"][l][None, :].astype(F32),
                         wts["w_o_fox"][l].astype(BF16), wts["w_o_dsa"][l].astype(BF16), wts["w_o_hgrn"][l].astype(BF16),
                         wts["w_out"][l].astype(BF16), wts["ln1_g"][l][None, :], wts["ln1_b"][l][None, :], min(512, n))

        g2, b2 = wts["ln2_g"][l][None, :], wts["ln2_b"][l][None, :]
        if l % 2 == 0:
            j = l // 2
            xf, xb = _ffn(x1, x1b, wts["ffn_w1"][j].astype(BF16), wts["ffn_w3"][j].astype(BF16),
                          wts["ffn_w2"][j].astype(BF16), g2, b2, tm=tm, tf=1408)
        else:
            j = l // 2
            rw = jnp.zeros((d, LANES), F32).at[:, :N_EXPERTS].set(wts["moe_router"][j].astype(F32))
            comb = _router(x1, rw, tm)
            xf, xb = _moe(x1, x1b, comb, wts["moe_w1"][j].astype(BF16), wts["moe_w3"][j].astype(BF16),
                          wts["moe_w2"][j].astype(BF16), g2, b2, tm=min(1024, n), tf=1792)

        states.append((k_a.reshape(b, t, H_A, HEAD_DIM), v_a.reshape(b, t, H_A, HEAD_DIM), logf_a,
                       k_b.reshape(b, t, HKV_B, HEAD_DIM), v_b.reshape(b, t, HKV_B, HEAD_DIM), ki_b, s_new))
    stacked = [jnp.stack([s[i] for s in states], axis=0) for i in range(7)]
    return xf.reshape(b, t, d), stacked


def kernel(x_prompt, x_sample, cache_fox_k, cache_fox_v, cache_fox_logf, cache_dsa_k, cache_dsa_v, cache_dsa_kidx, state_hgrn,
           w_in, b_in, w_gate, b_gate, w_o_fox, w_o_dsa, w_o_hgrn, w_out, hgrn_lb, hgrn_norm_g, rel_bias,
           ln1_g, ln1_b, ln2_g, ln2_b, ffn_w1, ffn_w3, ffn_w2, moe_router, moe_w1, moe_w3, moe_w2):
    wts = dict(w_in=w_in, b_in=b_in, w_gate=w_gate, b_gate=b_gate, w_o_fox=w_o_fox, w_o_dsa=w_o_dsa, w_o_hgrn=w_o_hgrn,
               w_out=w_out, hgrn_lb=hgrn_lb, hgrn_norm_g=hgrn_norm_g, rel_bias=rel_bias, ln1_g=ln1_g, ln1_b=ln1_b,
               ln2_g=ln2_g, ln2_b=ln2_b, ffn_w1=ffn_w1, ffn_w3=ffn_w3, ffn_w2=ffn_w2, moe_router=moe_router,
               moe_w1=moe_w1, moe_w3=moe_w3, moe_w2=moe_w2)
    caches = (cache_fox_k, cache_fox_v, cache_fox_logf, cache_dsa_k, cache_dsa_v, cache_dsa_kidx, state_hgrn)
    y_prompt, sp = _run_group(x_prompt, None, wts)
    y_sample, ss = _run_group(x_sample, caches, wts)
    return (y_prompt, y_sample, sp[0], sp[1], sp[2], sp[3], sp[4], sp[5], sp[6],
            ss[0], ss[1], ss[2], ss[3], ss[4], ss[5], ss[6])
```

```python
import functools
import math

import jax
import jax.numpy as jnp
import numpy as np
from jax import lax
from jax.experimental import pallas as pl
from jax.experimental.pallas import tpu as pltpu

F32 = jnp.float32
BF16 = jnp.bfloat16
NEG_INF = float("-inf")

D_MODEL = 1024
DEPTH = 2
CHUNK = 64
HEAD_DIM = 64
H_A = 8
H_B = 8
HKV_B = 2
G_B = H_B // HKV_B
HI_B = 4
DI_B = 64
TOPK_MAX = 256
H_C = 4
DK_C = 128
W_C = H_C * DK_C
N_BUCKETS = 32
MAX_DISTANCE = 128
N_EXPERTS = 8
DEEPNORM_ALPHA = (2.0 * DEPTH) ** 0.25
LN_EPS = 1e-5

LANES = 128
SUBLANES = 8
VMEM_LIMIT = 56 * 1024 * 1024

W_A = H_A * HEAD_DIM
W_B = H_B * HEAD_DIM
W_KV_B = HKV_B * HEAD_DIM

IN_QA, IN_KA, IN_VA, IN_FA = 0, 512, 1024, 1536
IN_QB, IN_KB, IN_VB, IN_QI, IN_KI, IN_WI = 1544, 2056, 2184, 2312, 2568, 2632
IN_FC, IN_END = 2636, 4684
Z_KA, Z_VA = 0, 512
Z_FC, Z_IC, Z_QC, Z_GC = 1024, 1536, 2048, 2560
Z_KB, Z_VB, Z_KI, Z_FA = 3072, 3200, 3328, 3392
N_Z = 3456
T_QA, T_VA, T_QB, T_QI, T_VB, T_WI = 0, 512, 1024, 1536, 1792, 1920
T_ROWS = 1928
V_ROWS = 80


def _params(sem, vmem=VMEM_LIMIT):
    return pltpu.CompilerParams(dimension_semantics=sem, vmem_limit_bytes=vmem)


def _log_sigmoid(z):
    return jnp.minimum(z, 0.0) - jnp.log1p(jnp.exp(-jnp.abs(z)))


def _sigmoid(z):
    return 1.0 / (1.0 + jnp.exp(-z))


def _dot_nt(a, b):
    return lax.dot_general(a, b, (((1,), (1,)), ((), ())), preferred_element_type=F32)


def _dot_tn(a, b):
    return lax.dot_general(a, b, (((0,), (0,)), ((), ())), preferred_element_type=F32)


def _dot(a, b):
    return jnp.dot(a, b, preferred_element_type=F32)


def _split3(x):
    hi = x.astype(BF16)
    r1 = x - hi.astype(F32)
    mid = r1.astype(BF16)
    lo = (r1 - mid.astype(F32)).astype(BF16)
    return hi, mid, lo


def _layer_norm_rows(r, g, b):
    mu = jnp.mean(r, axis=-1, keepdims=True)
    rc = r - mu
    var = jnp.mean(rc * rc, axis=-1, keepdims=True)
    return rc * lax.rsqrt(var + LN_EPS) * g + b


def _proj_kernel(x_ref, w_ref, b_ref, wt_ref, bt_ref,
                 kaf_ref, vaf_ref, z_ref, kbf_ref, vbf_ref, kif_ref, lf_ref,
                 ka_ref, kb_ref, ki_ref, qa_ref, va_ref, qb_ref, qi_ref, vb_ref, wi_ref):
    x = x_ref[...]

    def cols(c0, width):
        return _dot(x, w_ref[:, c0:c0 + width]) + b_ref[:, c0:c0 + width]

    kaf_ref[...] = cols(Z_KA, W_A)
    vaf_ref[...] = cols(Z_VA, W_A)
    z_ref[...] = cols(Z_FC, 4 * W_C)
    kv_b = cols(Z_KB, 2 * W_KV_B)
    kbf_ref[...] = kv_b[:, :W_KV_B]
    vbf_ref[...] = kv_b[:, W_KV_B:]
    tail = cols(Z_KI, N_Z - Z_KI)
    kif_ref[...] = tail[:, :DI_B]
    lf_ref[...] = _log_sigmoid(tail[:, Z_FA - Z_KI:Z_FA - Z_KI + H_A])
    ka_ref[...] = kaf_ref[...].astype(BF16)
    kb_ref[...] = kv_b[:, :W_KV_B].astype(BF16)
    ki_ref[...] = tail[:, :DI_B].astype(BF16)
    zt = _dot_nt(wt_ref[...], x) + bt_ref[...]
    qa_ref[0] = zt[T_QA:T_QA + W_A].astype(BF16)
    qb_ref[0] = zt[T_QB:T_QB + W_B].astype(BF16)
    qi_ref[0] = zt[T_QI:T_QI + HI_B * DI_B].astype(BF16)
    wi_ref[0] = zt[T_WI:T_WI + SUBLANES]
    ones = jnp.ones((V_ROWS - HEAD_DIM, zt.shape[1]), BF16)
    for v_ref, row0, heads in ((va_ref, T_VA, H_A), (vb_ref, T_VB, HKV_B)):
        for h in range(heads):
            v_ref[0, h * V_ROWS:h * V_ROWS + HEAD_DIM] = zt[row0 + h * HEAD_DIM:row0 + (h + 1) * HEAD_DIM].astype(BF16)
            v_ref[0, h * V_ROWS + HEAD_DIM:(h + 1) * V_ROWS] = ones


def _proj(xb, w, b, wt, bt, batch, tm):
    n, k = xb.shape
    t = n // batch
    assert t % tm == 0
    tpb = t // tm
    full = pl.Buffered(1)

    def tok(width):
        return pl.BlockSpec((tm, width), lambda i: (i, 0))

    def feat(rows):
        return pl.BlockSpec((1, rows, tm), lambda i: (i // tpb, 0, i % tpb))

    def feat_shape(rows, dtype):
        return jax.ShapeDtypeStruct((batch, rows, t), dtype)

    return pl.pallas_call(
        _proj_kernel,
        grid=(n // tm,),
        in_specs=[
            tok(k),
            pl.BlockSpec(w.shape, lambda i: (0, 0), pipeline_mode=full),
            pl.BlockSpec(b.shape, lambda i: (0, 0), pipeline_mode=full),
            pl.BlockSpec(wt.shape, lambda i: (0, 0), pipeline_mode=full),
            pl.BlockSpec(bt.shape, lambda i: (0, 0), pipeline_mode=full),
        ],
        out_specs=[tok(W_A), tok(W_A), tok(4 * W_C), tok(W_KV_B), tok(W_KV_B), tok(DI_B), tok(H_A),
                   tok(W_A), tok(W_KV_B), tok(DI_B),
                   feat(W_A), feat(H_A * V_ROWS), feat(W_B), feat(HI_B * DI_B), feat(HKV_B * V_ROWS), feat(SUBLANES)],
        out_shape=[jax.ShapeDtypeStruct((n, W_A), F32), jax.ShapeDtypeStruct((n, W_A), F32),
                   jax.ShapeDtypeStruct((n, 4 * W_C), F32), jax.ShapeDtypeStruct((n, W_KV_B), F32),
                   jax.ShapeDtypeStruct((n, W_KV_B), F32), jax.ShapeDtypeStruct((n, DI_B), F32),
                   jax.ShapeDtypeStruct((n, H_A), F32),
                   jax.ShapeDtypeStruct((n, W_A), BF16),
                   jax.ShapeDtypeStruct((n, W_KV_B), BF16), jax.ShapeDtypeStruct((n, DI_B), BF16),
                   feat_shape(W_A, BF16), feat_shape(H_A * V_ROWS, BF16), feat_shape(W_B, BF16),
                   feat_shape(HI_B * DI_B, BF16), feat_shape(HKV_B * V_ROWS, BF16), feat_shape(SUBLANES, F32)],
        compiler_params=_params(("parallel",)),
        name="proj",
    )(xb, w, b, wt, bt)


def _cumsum_kernel(x_ref, hi_ref, mid_ref, lo_ref):
    c = x_ref[0]
    length = c.shape[-1]
    lane = lax.broadcasted_iota(jnp.int32, c.shape, 1)
    s = 1
    while s < length:
        c = c + jnp.where(lane >= s, pltpu.roll(c, s, 1), 0.0)
        s *= 2
    hi, mid, lo = _split3(c)
    hi_ref[0] = hi.astype(F32)
    mid_ref[0] = mid.astype(F32)
    lo_ref[0] = lo.astype(F32)


def _cumsum(x):
    b, r, length = x.shape
    spec = pl.BlockSpec((1, r, length), lambda i: (i, 0, 0))
    return pl.pallas_call(
        _cumsum_kernel,
        grid=(b,),
        in_specs=[spec],
        out_specs=[spec, spec, spec],
        out_shape=[jax.ShapeDtypeStruct(x.shape, F32)] * 3,
        compiler_params=_params(("parallel",)),
        name="cumsum",
    )(x)


FOX_HPS = 2
FOX_CK_TERMS = 3


def _fox_kernel(qt_ref, k_ref, ckx_ref, vt_ref, o_ref, m_ref, acc_ref, *, tq, tk, q_off):
    hps = FOX_HPS
    i = pl.program_id(2)
    q0 = q_off + i * tq
    nk = (q0 + tq + tk - 1) // tk
    n_full = (q0 + 1) // tk
    ck_row = lax.broadcasted_iota(jnp.int32, (LANES, tq), 0)
    zeros = jnp.zeros((HEAD_DIM, tq), BF16)
    q_cols = []
    for h in range(hps):
        d = ck_row - FOX_CK_TERMS * (pl.program_id(1) * hps + h)
        minus_one = jnp.where(d >= 0, jnp.where(d < FOX_CK_TERMS, -1.0, 0.0), 0.0).astype(BF16)
        q_parts = [zeros] * hps
        q_parts[h] = qt_ref[0, h]
        q_cols.append(jnp.concatenate(q_parts + [minus_one], axis=0))
    qt = jnp.concatenate(q_cols, axis=1)
    krow = lax.broadcasted_iota(jnp.int32, (tk, hps * tq), 0)
    qcol = q0 + (lax.broadcasted_iota(jnp.int32, (tk, hps * tq), 1) & (tq - 1))
    m_ref[...] = jnp.full((1, hps * tq), NEG_INF, F32)
    acc_ref[...] = jnp.zeros((V_ROWS, hps * tq), F32)

    def tile(j, masked):
        ks = pl.multiple_of(j * tk, tk)
        kk = jnp.concatenate([k_ref[0, pl.ds(ks, tk), :], ckx_ref[0, pl.ds(ks, tk), :]], axis=1)
        s = _dot(kk, qt)
        if masked:
            s = jnp.where(krow + ks <= qcol, s, NEG_INF)
        m_old = m_ref[...]
        m_new = jnp.maximum(m_old, jnp.max(s, axis=0, keepdims=True))
        p = jnp.exp(s - m_new).astype(BF16)
        pv = [_dot(vt_ref[0, h, :, pl.ds(ks, tk)], p[:, h * tq:(h + 1) * tq]) for h in range(hps)]
        acc_ref[...] = jnp.exp(m_old - m_new) * acc_ref[...] + jnp.concatenate(pv, axis=1)
        m_ref[...] = m_new

    def full_tile(j, carry):
        tile(j, False)
        return carry

    def edge_tile(j, carry):
        tile(j, True)
        return carry

    lax.fori_loop(0, n_full, full_tile, 0)
    lax.fori_loop(n_full, nk, edge_tile, 0)
    acc = acc_ref[...]
    o = acc[:HEAD_DIM] / acc[HEAD_DIM:HEAD_DIM + 1]
    o_ref[0] = jnp.concatenate([o[:, h * tq:(h + 1) * tq].T for h in range(hps)], axis=-1).astype(o_ref.dtype)


def _fox(qt, k, ckx, vt, *, tq, tk, q_off):
    b, h, dh, t = qt.shape
    length = k.shape[1]
    hps = FOX_HPS
    assert hps * dh == LANES and FOX_CK_TERMS * h <= LANES and length % tk == 0 and tq & (tq - 1) == 0
    kern = functools.partial(_fox_kernel, tq=tq, tk=tk, q_off=q_off)
    return pl.pallas_call(
        kern,
        grid=(b, h // hps, t // tq),
        in_specs=[
            pl.BlockSpec((1, hps, dh, tq), lambda bi, hi, i: (bi, hi, 0, i)),
            pl.BlockSpec((1, length, LANES), lambda bi, hi, i: (bi, 0, hi)),
            pl.BlockSpec((1, length, LANES), lambda bi, hi, i: (bi, 0, 0)),
            pl.BlockSpec((1, hps, V_ROWS, length), lambda bi, hi, i: (bi, hi, 0, 0)),
        ],
        out_specs=pl.BlockSpec((1, tq, hps * dh), lambda bi, hi, i: (bi, i, hi)),
        out_shape=jax.ShapeDtypeStruct((b, t, h * dh), BF16),
        scratch_shapes=[
            pltpu.VMEM((1, hps * tq), F32),
            pltpu.VMEM((V_ROWS, hps * tq), F32),
        ],
        compiler_params=_params(("parallel", "parallel", "arbitrary")),
        name="fox",
    )(qt, k, ckx, vt)


def _ck_table(cum3):
    b, _, length = cum3[0].shape
    ck = jnp.stack(cum3, axis=-1).transpose(0, 2, 1, 3).reshape(b, length, H_A * FOX_CK_TERMS)
    return jnp.concatenate([ck, jnp.zeros((b, length, LANES - H_A * FOX_CK_TERMS), F32)], axis=-1).astype(BF16)


def _with_cache_tok(cache, new, length):
    if cache is not None:
        new = jnp.concatenate([cache.reshape(cache.shape[0], cache.shape[1], -1).astype(BF16), new], axis=1)
    return _pad_time(new, length)


def _with_cache_feat(cache, new, n_heads, length):
    b, _, t = new.shape
    new = new.reshape(b, n_heads, V_ROWS, t)
    if cache is not None:
        old = cache.transpose(0, 2, 3, 1).astype(BF16)
        old = jnp.concatenate([old, jnp.ones(old.shape[:2] + (V_ROWS - HEAD_DIM, old.shape[3]), BF16)], axis=2)
        new = jnp.concatenate([old, new], axis=3)
    pad = length - new.shape[3]
    if pad:
        new = jnp.concatenate([new, jnp.zeros(new.shape[:3] + (pad,), BF16)], axis=3)
    return new


INT_MIN = -(2 ** 31)
KEY_NEG_INF = int(np.int32(np.uint32(0xFF800000) ^ np.uint32(0x7FFFFFFF)))
DSA_TK = 256


def _order_key(x):
    bits = lax.bitcast_convert_type(x, jnp.int32)
    return bits ^ ((bits >> 31) & jnp.int32(0x7FFFFFFF))


def _dsa_kernel(qit_ref, wit_ref, ki_ref, qt_ref, k_ref, vt_ref, nbt_ref, o_ref,
                sc_ref, m_ref, acc_ref, *, tq, q_off, topk):
    tk = DSA_TK
    i = pl.program_id(1)
    q0 = q_off + i * tq
    i_abs = q0 // tk
    q_tiles = -(-tq // tk)
    nk = i_abs + q_tiles
    cols = G_B * tq

    wit = wit_ref[0]
    wrows = [wit[hh:hh + 1, :] for hh in range(HI_B)]
    q_chunk = (q0 + lax.broadcasted_iota(jnp.int32, (tk, tq), 1)) >> 6
    krow = lax.broadcasted_iota(jnp.int32, (tk, tq), 0)

    def score_tile(j, carry):
        ks = pl.multiple_of(j * tk, tk)
        ki = ki_ref[0, pl.ds(ks, tk), :]
        tot = jnp.zeros((tk, tq), F32)
        for hh in range(HI_B):
            tot = tot + wrows[hh] * jnp.maximum(_dot(ki, qit_ref[0, hh]), 0.0)
        tot = jnp.where(((krow + ks) >> 6) <= q_chunk, tot, NEG_INF)
        sc_ref[j] = _order_key(tot)
        return carry

    lax.fori_loop(0, nk, score_tile, 0)

    kf = float(topk)

    def count(pred_fn):
        def tile_hits(j):
            hit = jnp.where(pred_fn(sc_ref[j]), 1.0, 0.0)
            parts = [hit[r:r + SUBLANES] for r in range(0, tk, SUBLANES)]
            while len(parts) > 1:
                parts = [parts[a] + parts[a + 1] for a in range(0, len(parts), 2)]
            return parts[0]

        acc8 = lax.fori_loop(0, nk, lambda j, acc8: acc8 + tile_hits(j), jnp.zeros((SUBLANES, tq), F32))
        return jnp.sum(acc8, axis=0, keepdims=True)

    c0 = count(lambda key: key >= 0)
    thr = jnp.where(c0 >= kf, jnp.int32(0), jnp.int32(INT_MIN))
    for bit in range(30, -1, -1):
        cand = thr + jnp.int32(1 << bit)
        c = count(lambda key, cand=cand: key >= cand)
        thr = jnp.where(c >= kf, cand, thr)

    cnt_gt = count(lambda key: key > thr)
    short = thr <= jnp.int32(KEY_NEG_INF)
    thr_b = jnp.broadcast_to(jnp.maximum(thr, jnp.int32(KEY_NEG_INF)), (tk, tq))
    need = jnp.where(short, 0.0, kf - cnt_gt)

    tri = jnp.where(lax.broadcasted_iota(jnp.int32, (tk, tk), 1) <= lax.broadcasted_iota(jnp.int32, (tk, tk), 0),
                    1.0, 0.0).astype(BF16)
    m_ref[...] = jnp.full((HKV_B, 1, cols), NEG_INF, F32)
    acc_ref[...] = jnp.zeros((HKV_B, V_ROWS, cols), F32)
    q_zero = jnp.zeros((HEAD_DIM, cols), BF16)
    q_ext = []
    for g in range(HKV_B):
        parts = [q_zero] * HKV_B
        parts[g] = qt_ref[0, g]
        q_ext.append(jnp.concatenate(parts, axis=0))

    n_far = jnp.maximum(i_abs - 1, 0)

    def attend_tile(j, eqc, near):
        ks = pl.multiple_of(j * tk, tk)
        key = sc_ref[j]
        gt = key > thr_b
        eq = key == thr_b
        eqf = jnp.where(eq, 1.0, 0.0)
        rank = _dot(tri, eqf.astype(BF16)) + eqc
        pen = jnp.where(gt, 0.0, jnp.where(eq, rank, 1e9))
        madd = jnp.where(pen <= need, 0.0, NEG_INF)
        madd = jnp.concatenate([madd] * G_B, axis=1)
        k_tile = k_ref[0, pl.ds(ks, tk), :]
        for g in range(HKV_B):
            s = _dot(k_tile, q_ext[g]) + madd
            if near:
                s = s + nbt_ref[j - i_abs + 1, g]
            m_old = m_ref[g]
            m_new = jnp.maximum(m_old, jnp.max(s, axis=0, keepdims=True))
            m_safe = jnp.where(m_new == NEG_INF, 0.0, m_new)
            p = jnp.exp(s - m_safe)
            acc_ref[g] = jnp.exp(m_old - m_safe) * acc_ref[g] + _dot(vt_ref[0, g, :, pl.ds(ks, tk)], p.astype(BF16))
            m_ref[g] = m_new
        return eqc + jnp.sum(eqf, axis=0, keepdims=True)

    eqc = lax.fori_loop(0, n_far, functools.partial(attend_tile, near=False), jnp.zeros((1, tq), F32))
    lax.fori_loop(n_far, nk, functools.partial(attend_tile, near=True), eqc)
    for g in range(HKV_B):
        acc = acc_ref[g]
        o_ref[0, g] = (acc[:HEAD_DIM] / acc[HEAD_DIM:HEAD_DIM + 1]).astype(o_ref.dtype)


def _dsa(qit, wit, ki, qt, k, vt, nbt, *, tq, q_off, topk):
    b, _, _, t = qit.shape
    length = ki.shape[1]
    cols = G_B * tq
    n_bias = nbt.shape[0]
    tk = DSA_TK
    assert tk >= MAX_DISTANCE and q_off % tk == 0 and length % tk == 0 and length >= q_off + t
    assert HKV_B * HEAD_DIM == LANES and tq % CHUNK == 0 and (tq % tk == 0 or t == tq) and n_bias == 1 + -(-tq // tk)
    kern = functools.partial(_dsa_kernel, tq=tq, q_off=q_off, topk=topk)
    return pl.pallas_call(
        kern,
        grid=(b, t // tq),
        in_specs=[
            pl.BlockSpec((1, HI_B, DI_B, tq), lambda bi, i: (bi, 0, 0, i)),
            pl.BlockSpec((1, SUBLANES, tq), lambda bi, i: (bi, 0, i)),
            pl.BlockSpec((1, length, DI_B), lambda bi, i: (bi, 0, 0)),
            pl.BlockSpec((1, HKV_B, HEAD_DIM, cols), lambda bi, i: (bi, 0, 0, i)),
            pl.BlockSpec((1, length, LANES), lambda bi, i: (bi, 0, 0)),
            pl.BlockSpec((1, HKV_B, V_ROWS, length), lambda bi, i: (bi, 0, 0, 0)),
            pl.BlockSpec((n_bias, HKV_B, tk, cols), lambda bi, i: (0, 0, 0, 0), pipeline_mode=pl.Buffered(1)),
        ],
        out_specs=pl.BlockSpec((1, HKV_B, HEAD_DIM, cols), lambda bi, i: (bi, 0, 0, i)),
        out_shape=jax.ShapeDtypeStruct(qt.shape, BF16),
        scratch_shapes=[
            pltpu.VMEM((length // tk, tk, tq), jnp.int32),
            pltpu.VMEM((HKV_B, 1, cols), F32),
            pltpu.VMEM((HKV_B, V_ROWS, cols), F32),
        ],
        compiler_params=_params(("parallel", "arbitrary")),
        name="dsa",
    )(qit, wit, ki, qt, k, vt, nbt)


HG_SUB = SUBLANES
HG_NB = 4


def _hgrn_kernel(f_ref, i_ref, q_ref, g_ref, la_ref, l1_ref, oml_ref, ng_ref, s0_ref,
                 o_ref, sfin_ref, st_ref):
    c = pl.program_id(1)
    n_c = pl.num_programs(1)

    @pl.when(c == 0)
    def _():
        for bb in range(HG_NB):
            for hh in range(H_C):
                st_ref[bb, hh] = s0_ref[bb, hh].T

    for bb in range(HG_NB):
        _hgrn_chunk(bb, f_ref, i_ref, q_ref, g_ref, la_ref, l1_ref, oml_ref, ng_ref, o_ref, st_ref)

    @pl.when(c == n_c - 1)
    def _():
        for bb in range(HG_NB):
            for hh in range(H_C):
                sfin_ref[bb, hh] = st_ref[bb, hh].T


def _hgrn_chunk(bb, f_ref, i_ref, q_ref, g_ref, la_ref, l1_ref, oml_ref, ng_ref, o_ref, st_ref):
    ct = CHUNK
    zf = f_ref[bb]
    a = la_ref[...]
    cc = l1_ref[...] + _log_sigmoid(zf)
    logf = jnp.maximum(a, cc) + jnp.log1p(jnp.exp(-jnp.abs(a - cc)))
    kk = oml_ref[...] * _sigmoid(-zf)

    r_i = lax.broadcasted_iota(jnp.int32, (ct, ct), 0)
    c_i = lax.broadcasted_iota(jnp.int32, (ct, ct), 1)
    low = jnp.where(c_i <= r_i, 1.0, 0.0).astype(BF16)
    hi, mid, lo = _split3(logf)
    bcum = _dot(low, hi) + _dot(low, mid) + _dot(low, lo)

    sub_row = lax.broadcasted_iota(jnp.int32, (HG_SUB, DK_C), 0)
    a_col = lax.broadcasted_iota(jnp.int32, (HG_SUB, ct), 1)
    n_sub = ct // HG_SUB

    for hh in range(H_C):
        cs = slice(hh * DK_C, (hh + 1) * DK_C)
        qh = q_ref[bb, :, cs]
        kh = kk[:, cs]
        vh = i_ref[bb, :, cs]
        bh = bcum[:, cs]
        a_rows = []
        for bi in range(n_sub):
            r0 = bi * HG_SUB
            q_blk = qh[r0:r0 + HG_SUB]
            b_blk = bh[r0:r0 + HG_SUB]
            k_blk = kh[r0:r0 + HG_SUB]
            if bi > 0:
                ref = bh[r0:r0 + 1]
                qp = q_blk * jnp.exp(b_blk - ref)
                kp = kh[:r0] * jnp.exp(ref - bh[:r0])
                kp = jnp.concatenate([kp, jnp.zeros((ct - r0, DK_C), F32)], axis=0)
                a_row = _dot_nt(qp.astype(BF16), kp.astype(BF16))
            else:
                a_row = jnp.zeros((HG_SUB, ct), F32)
            for s in range(HG_SUB):
                d = jnp.where(sub_row >= s, b_blk - b_blk[s:s + 1], NEG_INF)
                x = q_blk * jnp.exp(d) * k_blk[s:s + 1]
                col = jnp.sum(x, axis=-1, keepdims=True)
                a_row = jnp.where(a_col == r0 + s, col, a_row)
            a_rows.append(a_row)
        a_mat = jnp.concatenate(a_rows, axis=0)
        st = st_ref[bb, hh]
        o = _dot(a_mat.astype(BF16), vh.astype(BF16)) + _dot_nt((qh * jnp.exp(bh)).astype(BF16), st.astype(BF16))
        bl = bh[ct - 1:ct]
        kdec = kh * jnp.exp(bl - bh)
        st_ref[bb, hh] = st * jnp.exp(bl) + _dot_tn(vh.astype(BF16), kdec.astype(BF16))
        o = o * lax.rsqrt(jnp.mean(o * o, axis=-1, keepdims=True) + LN_EPS) * ng_ref[:, cs]
        gh = g_ref[bb, :, cs]
        o_ref[bb, :, cs] = (o * (gh * _sigmoid(gh))).astype(o_ref.dtype)


def _hgrn(z3, la, l1, oml, ng, s0):
    b, t, _ = z3.shape
    assert b % HG_NB == 0
    n_c = t // CHUNK

    def zspec(col):
        blk = (col - Z_FC) // W_C
        return pl.BlockSpec((HG_NB, CHUNK, W_C), lambda bi, ci: (bi, ci, blk))

    vec = pl.BlockSpec((1, W_C), lambda bi, ci: (0, 0))
    return pl.pallas_call(
        _hgrn_kernel,
        grid=(b // HG_NB, n_c),
        in_specs=[zspec(Z_FC), zspec(Z_IC), zspec(Z_QC), zspec(Z_GC), vec, vec, vec, vec,
                  pl.BlockSpec((HG_NB, H_C, DK_C, DK_C), lambda bi, ci: (bi, 0, 0, 0))],
        out_specs=[pl.BlockSpec((HG_NB, CHUNK, W_C), lambda bi, ci: (bi, ci, 0)),
                   pl.BlockSpec((HG_NB, H_C, DK_C, DK_C), lambda bi, ci: (bi, 0, 0, 0))],
        out_shape=[jax.ShapeDtypeStruct((b, t, W_C), BF16),
                   jax.ShapeDtypeStruct((b, H_C, DK_C, DK_C), F32)],
        scratch_shapes=[pltpu.VMEM((HG_NB, H_C, DK_C, DK_C), F32)],
        compiler_params=_params(("parallel", "arbitrary")),
        name="hgrn",
    )(z3, z3, z3, z3, la, l1, oml, ng, s0)


def _merge_kernel(x_ref, xb_ref, oa_ref, ob_ref, oc_ref, wg_ref, bg_ref, wa_ref, wb_ref, wc_ref, wo_ref,
                  g_ref, b_ref, y_ref, yb_ref):
    xb = xb_ref[...]
    d = D_MODEL
    mix = None
    for br, (o_ref, w_ref) in enumerate(((oa_ref, wa_ref), (ob_ref, wb_ref), (oc_ref, wc_ref))):
        gate = _sigmoid(_dot(xb, wg_ref[:, br * d:(br + 1) * d]) + bg_ref[:, br * d:(br + 1) * d])
        term = gate * _dot(o_ref[...], w_ref[...])
        mix = term if mix is None else mix + term
    out = _dot(mix.astype(BF16), wo_ref[...])
    y = _layer_norm_rows(DEEPNORM_ALPHA * x_ref[...] + out, g_ref[...], b_ref[...])
    y_ref[...] = y
    yb_ref[...] = y.astype(BF16)


def _merge(x, xb, oa, ob, oc, wg, bg, wa, wb, wc, wo, g, b, tm):
    n, d = x.shape

    def row(width):
        return pl.BlockSpec((tm, width), lambda i: (i, 0))

    def full(arr):
        return pl.BlockSpec(arr.shape, lambda i: (0, 0))

    return pl.pallas_call(
        _merge_kernel,
        grid=(n // tm,),
        in_specs=[row(d), row(d), row(oa.shape[1]), row(ob.shape[1]), row(oc.shape[1]),
                  full(wg), full(bg), full(wa), full(wb), full(wc), full(wo), full(g), full(b)],
        out_specs=[row(d), row(d)],
        out_shape=[jax.ShapeDtypeStruct((n, d), F32), jax.ShapeDtypeStruct((n, d), BF16)],
        compiler_params=_params(("parallel",)),
        name="merge",
    )(x, xb, oa, ob, oc, wg, bg, wa, wb, wc, wo, g, b)


def _router_kernel(x_ref, w_ref, o_ref):
    x = x_ref[...]
    xh = x.astype(BF16)
    xl = (x - xh.astype(F32)).astype(BF16)
    w = w_ref[...]
    wh = w.astype(BF16)
    wl = (w - wh.astype(F32)).astype(BF16)
    logits = _dot(xh, wh) + _dot(xh, wl) + _dot(xl, wh)
    lane = lax.broadcasted_iota(jnp.int32, logits.shape, 1)
    logits = jnp.where(lane < N_EXPERTS, logits, NEG_INF)
    big = jnp.int32(LANES)
    m1 = jnp.max(logits, axis=-1, keepdims=True)
    i1 = jnp.min(jnp.where(logits == m1, lane, big), axis=-1, keepdims=True)
    rest = jnp.where(lane == i1, NEG_INF, logits)
    m2 = jnp.max(rest, axis=-1, keepdims=True)
    i2 = jnp.min(jnp.where(rest == m2, lane, big), axis=-1, keepdims=True)
    e2 = jnp.exp(m2 - m1)
    g1 = 1.0 / (1.0 + e2)
    g2 = e2 / (1.0 + e2)
    o_ref[...] = jnp.where(lane == i1, g1, 0.0) + jnp.where(lane == i2, g2, 0.0)


def _router(x, w, tm):
    n, d = x.shape
    return pl.pallas_call(
        _router_kernel,
        grid=(n // tm,),
        in_specs=[pl.BlockSpec((tm, d), lambda i: (i, 0)),
                  pl.BlockSpec(w.shape, lambda i: (0, 0))],
        out_specs=pl.BlockSpec((tm, LANES), lambda i: (i, 0)),
        out_shape=jax.ShapeDtypeStruct((n, LANES), F32),
        compiler_params=_params(("parallel",)),
        name="router",
    )(x, w)


def _ffn_kernel(x_ref, xb_ref, w1_ref, w3_ref, w2_ref, g_ref, b_ref, y_ref, yb_ref, acc_ref):
    f = pl.program_id(1)

    @pl.when(f == 0)
    def _():
        acc_ref[...] = jnp.zeros_like(acc_ref)

    xb = xb_ref[...]
    h1 = _dot(xb, w1_ref[...])
    h = (h1 * _sigmoid(h1)) * _dot(xb, w3_ref[...])
    acc_ref[...] += _dot(h.astype(BF16), w2_ref[...])

    @pl.when(f == pl.num_programs(1) - 1)
    def _():
        y = _layer_norm_rows(DEEPNORM_ALPHA * x_ref[...] + acc_ref[...], g_ref[...], b_ref[...])
        y_ref[...] = y
        yb_ref[...] = y.astype(BF16)


def _ffn(x, xb, w1, w3, w2, g, b, *, tm, tf):
    n, d = x.shape
    ff = w1.shape[1]
    return pl.pallas_call(
        _ffn_kernel,
        grid=(n // tm, ff // tf),
        in_specs=[
            pl.BlockSpec((tm, d), lambda i, f: (i, 0)),
            pl.BlockSpec((tm, d), lambda i, f: (i, 0)),
            pl.BlockSpec((d, tf), lambda i, f: (0, f)),
            pl.BlockSpec((d, tf), lambda i, f: (0, f)),
            pl.BlockSpec((tf, d), lambda i, f: (f, 0)),
            pl.BlockSpec((1, d), lambda i, f: (0, 0)),
            pl.BlockSpec((1, d), lambda i, f: (0, 0)),
        ],
        out_specs=[pl.BlockSpec((tm, d), lambda i, f: (i, 0)),
                   pl.BlockSpec((tm, d), lambda i, f: (i, 0))],
        out_shape=[jax.ShapeDtypeStruct((n, d), F32), jax.ShapeDtypeStruct((n, d), BF16)],
        scratch_shapes=[pltpu.VMEM((tm, d), F32)],
        compiler_params=_params(("parallel", "arbitrary")),
        name="ffn",
    )(x, xb, w1, w3, w2, g, b)


MOE_BS = 144


def _moe_slots(tm):
    return -(-tm // MOE_BS) * MOE_BS


def _moe_kernel(x_ref, xb_ref, comb_ref, w1_ref, w3_ref, w2_ref, g_ref, b_ref, y_ref, yb_ref,
                rank_ref, rankt_ref, xg_ref, yacc_ref, *, tm):
    e = pl.program_id(1)
    f = pl.program_id(2)
    n_f = pl.num_programs(2)
    bs = MOE_BS

    @pl.when((e == 0) & (f == 0))
    def _():
        comb = comb_ref[...]
        routed = comb > 0.0
        r_i = lax.broadcasted_iota(jnp.int32, (tm, tm), 0)
        c_i = lax.broadcasted_iota(jnp.int32, (tm, tm), 1)
        strict_low = jnp.where(c_i < r_i, 1.0, 0.0).astype(BF16)
        rank = _dot(strict_low, jnp.where(routed, 1.0, 0.0).astype(BF16))
        rank = jnp.where(routed, rank, -1.0)
        rank_ref[...] = rank
        rankt_ref[...] = rank.T[:N_EXPERTS]
        y_ref[...] = jnp.zeros_like(y_ref)

    rrow = rankt_ref[pl.ds(e, 1), :]
    n_e = jnp.sum(jnp.where(rrow >= 0.0, 1.0, 0.0)).astype(jnp.int32)
    n_blk = (n_e + bs - 1) // bs

    @pl.when(f == 0)
    def _():
        slot = lax.broadcasted_iota(jnp.int32, (bs, tm), 0).astype(F32)

        def gather(blk, carry):
            r0 = pl.multiple_of(blk * bs, bs)
            pick = jnp.where(rrow - (blk * bs).astype(F32) == slot, 1.0, 0.0).astype(BF16)
            xg_ref[pl.ds(r0, bs), :] = _dot(pick, xb_ref[...]).astype(BF16)
            yacc_ref[pl.ds(r0, bs), :] = jnp.zeros((bs, yacc_ref.shape[1]), F32)
            return carry

        lax.fori_loop(0, n_blk, gather, 0)

    def expert_rows(r0, rows):
        xg = xg_ref[pl.ds(r0, rows), :]
        h1 = _dot(xg, w1_ref[0])
        h = (h1 * _sigmoid(h1)) * _dot(xg, w3_ref[0])
        yacc_ref[pl.ds(r0, rows), :] += _dot(h.astype(BF16), w2_ref[0])

    def expert_pair(pair, carry):
        expert_rows(pl.multiple_of(pair * (2 * bs), 2 * bs), 2 * bs)
        return carry

    n_pair = n_blk // 2
    lax.fori_loop(0, n_pair, expert_pair, 0)

    @pl.when(n_blk % 2 == 1)
    def _():
        expert_rows(pl.multiple_of(n_pair * (2 * bs), bs), bs)

    @pl.when(f == n_f - 1)
    def _():
        lane = lax.broadcasted_iota(jnp.int32, (tm, LANES), 1)
        rcol = jnp.sum(jnp.where(lane == e, rank_ref[...], 0.0), axis=-1, keepdims=True)
        gcol = jnp.sum(jnp.where(lane == e, comb_ref[...], 0.0), axis=-1, keepdims=True)
        def scatter_rows(r0, rows):
            slot = lax.broadcasted_iota(jnp.int32, (tm, rows), 1).astype(F32)
            put = jnp.where(rcol - r0.astype(F32) == slot, 1.0, 0.0).astype(BF16)
            y_ref[...] += gcol * _dot(put, yacc_ref[pl.ds(r0, rows), :].astype(BF16))

        def scatter_pair(pair, carry):
            scatter_rows(pl.multiple_of(pair * (2 * bs), 2 * bs), 2 * bs)
            return carry

        lax.fori_loop(0, n_pair, scatter_pair, 0)

        @pl.when(n_blk % 2 == 1)
        def _():
            scatter_rows(pl.multiple_of(n_pair * (2 * bs), bs), bs)

    @pl.when((e == pl.num_programs(1) - 1) & (f == n_f - 1))
    def _():
        y = _layer_norm_rows(DEEPNORM_ALPHA * x_ref[...] + y_ref[...], g_ref[...], b_ref[...])
        y_ref[...] = y
        yb_ref[...] = y.astype(BF16)


def _moe(x, xb, comb, w1, w3, w2, g, b, *, tm, tf):
    n, d = x.shape
    n_e, _, ff = w1.shape
    assert n % tm == 0 and ff % tf == 0 and n_e == N_EXPERTS
    slots = _moe_slots(tm)
    kern = functools.partial(_moe_kernel, tm=tm)
    one = pl.Buffered(1)
    return pl.pallas_call(
        kern,
        grid=(n // tm, n_e, ff // tf),
        in_specs=[
            pl.BlockSpec((tm, d), lambda i, e, f: (i, 0), pipeline_mode=one),
            pl.BlockSpec((tm, d), lambda i, e, f: (i, 0), pipeline_mode=one),
            pl.BlockSpec((tm, LANES), lambda i, e, f: (i, 0), pipeline_mode=one),
            pl.BlockSpec((1, d, tf), lambda i, e, f: (e, 0, f)),
            pl.BlockSpec((1, d, tf), lambda i, e, f: (e, 0, f)),
            pl.BlockSpec((1, tf, d), lambda i, e, f: (e, f, 0)),
            pl.BlockSpec((1, d), lambda i, e, f: (0, 0)),
            pl.BlockSpec((1, d), lambda i, e, f: (0, 0)),
        ],
        out_specs=[pl.BlockSpec((tm, d), lambda i, e, f: (i, 0)),
                   pl.BlockSpec((tm, d), lambda i, e, f: (i, 0))],
        out_shape=[jax.ShapeDtypeStruct((n, d), F32), jax.ShapeDtypeStruct((n, d), BF16)],
        scratch_shapes=[pltpu.VMEM((tm, LANES), F32), pltpu.VMEM((N_EXPERTS, tm), F32),
                        pltpu.VMEM((slots, d), BF16), pltpu.VMEM((slots, d), F32)],
        compiler_params=_params(("parallel", "arbitrary", "arbitrary")),
        name="moe",
    )(x, xb, comb, w1, w3, w2, g, b)


def _rel_bucket(rel):
    nb = N_BUCKETS // 2
    max_exact = nb // 2
    ret = jnp.where(rel > 0, nb, 0)
    n = jnp.abs(rel)
    nf = jnp.maximum(n, 1).astype(jnp.float32)
    large = max_exact + (jnp.log(nf / max_exact) / math.log(MAX_DISTANCE / max_exact) * (nb - max_exact)).astype(jnp.int32)
    large = jnp.minimum(large, nb - 1)
    return ret + jnp.where(n < max_exact, n, large)


def _bias_tiles(rel_bias, tq):
    tk = DSA_TK
    r = jnp.arange(tq, dtype=jnp.int32)[:, None]
    c = jnp.arange(tk, dtype=jnp.int32)[None, :]
    n = 1 + -(-tq // tk)
    rel = jnp.stack([c - r + (s - 1) * tk for s in range(n)], axis=0)
    table = rel_bias.astype(F32)
    onehot = _rel_bucket(rel)[..., None] == jnp.arange(N_BUCKETS, dtype=jnp.int32)
    vals = jnp.sum(jnp.where(onehot[..., None], table, 0.0), axis=-2)
    far = table[_rel_bucket(jnp.int32(-(2 * MAX_DISTANCE)))]
    vals = (vals - far).transpose(0, 3, 2, 1).reshape(n, HKV_B, G_B, tk, tq)
    return vals.transpose(0, 1, 3, 2, 4).reshape(n, HKV_B, tk, G_B * tq)


def _prep_w_in(w_in_l, b_in_l):
    def tok_cols(a):
        parts = [a[..., IN_KA:IN_FA], a[..., IN_FC:IN_END], a[..., IN_KB:IN_QI], a[..., IN_KI:IN_WI], a[..., IN_FA:IN_QB]]
        width = sum(p.shape[-1] for p in parts)
        return jnp.concatenate(parts + [jnp.zeros(a.shape[:-1] + (N_Z - width,), a.dtype)], axis=-1)

    def feat_cols(a):
        parts = [a[..., IN_QA:IN_KA] * HEAD_DIM ** -0.5, a[..., IN_VA:IN_FA], a[..., IN_QB:IN_KB] * HEAD_DIM ** -0.5,
                 a[..., IN_QI:IN_KI] * DI_B ** -0.5, a[..., IN_VB:IN_QI], a[..., IN_WI:IN_FC] * HI_B ** -0.5]
        width = sum(p.shape[-1] for p in parts)
        return jnp.concatenate(parts + [jnp.zeros(a.shape[:-1] + (T_ROWS - width,), a.dtype)], axis=-1)

    w = w_in_l.astype(F32)
    bias = b_in_l.astype(F32)[None, :]
    return (tok_cols(w).astype(BF16), tok_cols(bias),
            feat_cols(w).T.astype(BF16), feat_cols(bias).T)


def _pad_time(a, length):
    pad = length - a.shape[1]
    if pad == 0:
        return a
    return jnp.concatenate([a, jnp.zeros((a.shape[0], pad) + a.shape[2:], a.dtype)], axis=1)


def _round_up(x, m):
    return (x + m - 1) // m * m


def _run_group(x, caches, wts):
    b, t, d = x.shape
    n = b * t
    p_len = 0 if caches is None else caches[0].shape[2]
    tm = min(512, n)
    xf = x.reshape(n, d).astype(F32)
    xb = xf.astype(BF16)

    fox_tq = min(1024, t)
    fox_tk = 1024
    l_fox = _round_up(p_len + t, fox_tk)
    dsa_tq = min(256, t)
    l_dsa = _round_up(p_len + t, DSA_TK)
    topk = min(TOPK_MAX, (p_len + t) // 4)
    nb = _bias_tiles(wts["rel_bias"], dsa_tq)

    lb_cum = jnp.cumsum(jax.nn.softmax(wts["hgrn_lb"].astype(F32), axis=0), axis=0)

    states = []
    for l in range(DEPTH):
        w_tok, b_tok, w_feat, b_feat = _prep_w_in(wts["w_in"][l], wts["b_in"][l])
        (k_a, v_a, z, k_b, v_b, ki_b, logf_a, ka_bf, kb_bf, ki_bf,
         qa_t, va_t, qb_t, qi_t, vb_t, wi_t) = _proj(xb, w_tok, b_tok, w_feat, b_feat, b, min(512, t))
        z3 = z.reshape(b, t, 4 * W_C)
        logf_a = logf_a.reshape(b, t, H_A)
        ki_b = ki_b.reshape(b, t, DI_B)
        if caches is None:
            c_ka = c_va = c_kb = c_vb = c_ki = None
            lf_all = logf_a
            s0 = jnp.zeros((b, H_C, DK_C, DK_C), F32)
        else:
            c_ka, c_va, c_lf, c_kb, c_vb, c_ki, c_s = [c[l] for c in caches]
            lf_all = jnp.concatenate([c_lf.astype(F32), logf_a], axis=1)
            s0 = c_s.astype(F32)

        cum3 = _cumsum(_pad_time(lf_all, l_fox).transpose(0, 2, 1))
        o_a = _fox(qa_t.reshape(b, H_A, HEAD_DIM, t), _with_cache_tok(c_ka, ka_bf.reshape(b, t, W_A), l_fox),
                   _ck_table(cum3), _with_cache_feat(c_va, va_t, H_A, l_fox), tq=fox_tq, tk=fox_tk, q_off=p_len)

        n_t = t // dsa_tq
        qt_b = qb_t.reshape(b, HKV_B, G_B, HEAD_DIM, n_t, dsa_tq).transpose(0, 1, 3, 4, 2, 5)
        qt_b = qt_b.reshape(b, HKV_B, HEAD_DIM, n_t * G_B * dsa_tq)
        o_b = _dsa(qi_t.reshape(b, HI_B, DI_B, t), wi_t, _with_cache_tok(c_ki, ki_bf.reshape(b, t, DI_B), l_dsa),
                   qt_b, _with_cache_tok(c_kb, kb_bf.reshape(b, t, W_KV_B), l_dsa),
                   _with_cache_feat(c_vb, vb_t, HKV_B, l_dsa), nb, tq=dsa_tq, q_off=p_len, topk=topk)
        o_b = o_b.reshape(b, HKV_B, HEAD_DIM, n_t, G_B, dsa_tq).transpose(0, 3, 5, 1, 4, 2).reshape(n, H_B * HEAD_DIM)

        lb = (lb_cum[l] - lb_cum[0]).reshape(1, W_C)
        o_c, s_new = _hgrn(z3, jnp.log(lb), jnp.log1p(-lb), 1.0 - lb,
                           wts["hgrn_norm_g"][l].reshape(1, W_C).astype(F32), s0)

        x1, x1b = _merge(xf, xb, o_a.reshape(n, -1), o_b, o_c.reshape(n, -1),
                         wts["w_gate"][l].astype(BF16), wts["b_gate"][l][None, :].astype(F32),
                         wts["w_o_fox"][l].astype(BF16), wts["w_o_dsa"][l].astype(BF16), wts["w_o_hgrn"][l].astype(BF16),
                         wts["w_out"][l].astype(BF16), wts["ln1_g"][l][None, :], wts["ln1_b"][l][None, :], min(256, n))

        g2, b2 = wts["ln2_g"][l][None, :], wts["ln2_b"][l][None, :]
        if l % 2 == 0:
            j = l // 2
            xf, xb = _ffn(x1, x1b, wts["ffn_w1"][j].astype(BF16), wts["ffn_w3"][j].astype(BF16),
                          wts["ffn_w2"][j].astype(BF16), g2, b2, tm=tm, tf=1408)
        else:
            j = l // 2
            rw = jnp.zeros((d, LANES), F32).at[:, :N_EXPERTS].set(wts["moe_router"][j].astype(F32))
            comb = _router(x1, rw, tm)
            xf, xb = _moe(x1, x1b, comb, wts["moe_w1"][j].astype(BF16), wts["moe_w3"][j].astype(BF16),
                          wts["moe_w2"][j].astype(BF16), g2, b2, tm=min(1024, n), tf=1792)

        states.append((k_a.reshape(b, t, H_A, HEAD_DIM), v_a.reshape(b, t, H_A, HEAD_DIM), logf_a,
                       k_b.reshape(b, t, HKV_B, HEAD_DIM), v_b.reshape(b, t, HKV_B, HEAD_DIM), ki_b, s_new))
    stacked = [jnp.stack([s[i] for s in states], axis=0) for i in range(7)]
    return xf.reshape(b, t, d), stacked


def kernel(x_prompt, x_sample, cache_fox_k, cache_fox_v, cache_fox_logf, cache_dsa_k, cache_dsa_v, cache_dsa_kidx, state_hgrn,
           w_in, b_in, w_gate, b_gate, w_o_fox, w_o_dsa, w_o_hgrn, w_out, hgrn_lb, hgrn_norm_g, rel_bias,
           ln1_g, ln1_b, ln2_g, ln2_b, ffn_w1, ffn_w3, ffn_w2, moe_router, moe_w1, moe_w3, moe_w2):
    wts = dict(w_in=w_in, b_in=b_in, w_gate=w_gate, b_gate=b_gate, w_o_fox=w_o_fox, w_o_dsa=w_o_dsa, w_o_hgrn=w_o_hgrn,
               w_out=w_out, hgrn_lb=hgrn_lb, hgrn_norm_g=hgrn_norm_g, rel_bias=rel_bias, ln1_g=ln1_g, ln1_b=ln1_b,
               ln2_g=ln2_g, ln2_b=ln2_b, ffn_w1=ffn_w1, ffn_w3=ffn_w3, ffn_w2=ffn_w2, moe_router=moe_router,
               moe_w1=moe_w1, moe_w3=moe_w3, moe_w2=moe_w2)
    caches = (cache_fox_k, cache_fox_v, cache_fox_logf, cache_dsa_k, cache_dsa_v, cache_dsa_kidx, state_hgrn)
    y_prompt, sp = _run_group(x_prompt, None, wts)
    y_sample, ss = _run_group(x_sample, caches, wts)
    return (y_prompt, y_sample, sp[0], sp[1], sp[2], sp[3], sp[4], sp[5], sp[6],
            ss[0], ss[1], ss[2], ss[3], ss[4], ss[5], ss[6])
```

```python
import functools
import math

import jax
import jax.numpy as jnp
import numpy as np
from jax import lax
from jax.experimental import pallas as pl
from jax.experimental.pallas import tpu as pltpu

F32 = jnp.float32
BF16 = jnp.bfloat16
NEG_INF = float("-inf")

D_MODEL = 1024
DEPTH = 2
CHUNK = 64
HEAD_DIM = 64
H_A = 8
H_B = 8
HKV_B = 2
G_B = H_B // HKV_B
HI_B = 4
DI_B = 64
TOPK_MAX = 256
H_C = 4
DK_C = 128
W_C = H_C * DK_C
N_BUCKETS = 32
MAX_DISTANCE = 128
N_EXPERTS = 8
DEEPNORM_ALPHA = (2.0 * DEPTH) ** 0.25
LN_EPS = 1e-5

LANES = 128
SUBLANES = 8
VMEM_LIMIT = 56 * 1024 * 1024

W_A = H_A * HEAD_DIM
W_B = H_B * HEAD_DIM
W_KV_B = HKV_B * HEAD_DIM

IN_QA, IN_KA, IN_VA, IN_FA = 0, 512, 1024, 1536
IN_QB, IN_KB, IN_VB, IN_QI, IN_KI, IN_WI = 1544, 2056, 2184, 2312, 2568, 2632
IN_FC, IN_END = 2636, 4684
Z_KA, Z_VA = 0, 512
Z_FC, Z_IC, Z_QC, Z_GC = 1024, 1536, 2048, 2560
Z_KB, Z_VB, Z_KI, Z_FA = 3072, 3200, 3328, 3392
N_Z = 3456
T_QA, T_VA, T_QB, T_QI, T_VB, T_WI = 0, 512, 1024, 1536, 1792, 1920
T_ROWS = 1928
V_ROWS = 80


def _params(sem, vmem=VMEM_LIMIT):
    return pltpu.CompilerParams(dimension_semantics=sem, vmem_limit_bytes=vmem)


def _log_sigmoid(z):
    return jnp.minimum(z, 0.0) - jnp.log1p(jnp.exp(-jnp.abs(z)))


def _sigmoid(z):
    return 1.0 / (1.0 + jnp.exp(-z))


def _dot_nt(a, b):
    return lax.dot_general(a, b, (((1,), (1,)), ((), ())), preferred_element_type=F32)


def _dot_tn(a, b):
    return lax.dot_general(a, b, (((0,), (0,)), ((), ())), preferred_element_type=F32)


def _dot(a, b):
    return jnp.dot(a, b, preferred_element_type=F32)


def _split3(x):
    hi = x.astype(BF16)
    r1 = x - hi.astype(F32)
    mid = r1.astype(BF16)
    lo = (r1 - mid.astype(F32)).astype(BF16)
    return hi, mid, lo


def _layer_norm_rows(r, g, b):
    mu = jnp.mean(r, axis=-1, keepdims=True)
    rc = r - mu
    var = jnp.mean(rc * rc, axis=-1, keepdims=True)
    return rc * lax.rsqrt(var + LN_EPS) * g + b


def _proj_kernel(x_ref, w_ref, b_ref, wt_ref, bt_ref,
                 kaf_ref, vaf_ref, z_ref, kbf_ref, vbf_ref, kif_ref, lf_ref,
                 ka_ref, kb_ref, ki_ref, qa_ref, va_ref, qb_ref, qi_ref, vb_ref, wi_ref):
    x = x_ref[...]

    def cols(c0, width):
        return _dot(x, w_ref[:, c0:c0 + width]) + b_ref[:, c0:c0 + width]

    kaf_ref[...] = cols(Z_KA, W_A)
    vaf_ref[...] = cols(Z_VA, W_A)
    z_ref[...] = cols(Z_FC, 4 * W_C)
    kv_b = cols(Z_KB, 2 * W_KV_B)
    kbf_ref[...] = kv_b[:, :W_KV_B]
    vbf_ref[...] = kv_b[:, W_KV_B:]
    tail = cols(Z_KI, N_Z - Z_KI)
    kif_ref[...] = tail[:, :DI_B]
    lf_ref[...] = _log_sigmoid(tail[:, Z_FA - Z_KI:Z_FA - Z_KI + H_A])
    ka_ref[...] = kaf_ref[...].astype(BF16)
    kb_ref[...] = kv_b[:, :W_KV_B].astype(BF16)
    ki_ref[...] = tail[:, :DI_B].astype(BF16)
    zt = _dot_nt(wt_ref[...], x) + bt_ref[...]
    qa_ref[0] = zt[T_QA:T_QA + W_A].astype(BF16)
    qb_ref[0] = zt[T_QB:T_QB + W_B].astype(BF16)
    qi_ref[0] = zt[T_QI:T_QI + HI_B * DI_B].astype(BF16)
    wi_ref[0] = zt[T_WI:T_WI + SUBLANES]
    ones = jnp.ones((V_ROWS - HEAD_DIM, zt.shape[1]), BF16)
    for v_ref, row0, heads in ((va_ref, T_VA, H_A), (vb_ref, T_VB, HKV_B)):
        for h in range(heads):
            v_ref[0, h * V_ROWS:h * V_ROWS + HEAD_DIM] = zt[row0 + h * HEAD_DIM:row0 + (h + 1) * HEAD_DIM].astype(BF16)
            v_ref[0, h * V_ROWS + HEAD_DIM:(h + 1) * V_ROWS] = ones


def _proj(xb, w, b, wt, bt, batch, tm):
    n, k = xb.shape
    t = n // batch
    assert t % tm == 0
    tpb = t // tm
    full = pl.Buffered(1)

    def tok(width):
        return pl.BlockSpec((tm, width), lambda i: (i, 0))

    def feat(rows):
        return pl.BlockSpec((1, rows, tm), lambda i: (i // tpb, 0, i % tpb))

    def feat_shape(rows, dtype):
        return jax.ShapeDtypeStruct((batch, rows, t), dtype)

    return pl.pallas_call(
        _proj_kernel,
        grid=(n // tm,),
        in_specs=[
            tok(k),
            pl.BlockSpec(w.shape, lambda i: (0, 0), pipeline_mode=full),
            pl.BlockSpec(b.shape, lambda i: (0, 0), pipeline_mode=full),
            pl.BlockSpec(wt.shape, lambda i: (0, 0), pipeline_mode=full),
            pl.BlockSpec(bt.shape, lambda i: (0, 0), pipeline_mode=full),
        ],
        out_specs=[tok(W_A), tok(W_A), tok(4 * W_C), tok(W_KV_B), tok(W_KV_B), tok(DI_B), tok(H_A),
                   tok(W_A), tok(W_KV_B), tok(DI_B),
                   feat(W_A), feat(H_A * V_ROWS), feat(W_B), feat(HI_B * DI_B), feat(HKV_B * V_ROWS), feat(SUBLANES)],
        out_shape=[jax.ShapeDtypeStruct((n, W_A), F32), jax.ShapeDtypeStruct((n, W_A), F32),
                   jax.ShapeDtypeStruct((n, 4 * W_C), F32), jax.ShapeDtypeStruct((n, W_KV_B), F32),
                   jax.ShapeDtypeStruct((n, W_KV_B), F32), jax.ShapeDtypeStruct((n, DI_B), F32),
                   jax.ShapeDtypeStruct((n, H_A), F32),
                   jax.ShapeDtypeStruct((n, W_A), BF16),
                   jax.ShapeDtypeStruct((n, W_KV_B), BF16), jax.ShapeDtypeStruct((n, DI_B), BF16),
                   feat_shape(W_A, BF16), feat_shape(H_A * V_ROWS, BF16), feat_shape(W_B, BF16),
                   feat_shape(HI_B * DI_B, BF16), feat_shape(HKV_B * V_ROWS, BF16), feat_shape(SUBLANES, F32)],
        compiler_params=_params(("parallel",)),
        name="proj",
    )(xb, w, b, wt, bt)


def _cumsum_kernel(x_ref, hi_ref, mid_ref, lo_ref):
    c = x_ref[0]
    length = c.shape[-1]
    lane = lax.broadcasted_iota(jnp.int32, c.shape, 1)
    s = 1
    while s < length:
        c = c + jnp.where(lane >= s, pltpu.roll(c, s, 1), 0.0)
        s *= 2
    hi, mid, lo = _split3(c)
    hi_ref[0] = hi.astype(F32)
    mid_ref[0] = mid.astype(F32)
    lo_ref[0] = lo.astype(F32)


def _cumsum(x):
    b, r, length = x.shape
    spec = pl.BlockSpec((1, r, length), lambda i: (i, 0, 0))
    return pl.pallas_call(
        _cumsum_kernel,
        grid=(b,),
        in_specs=[spec],
        out_specs=[spec, spec, spec],
        out_shape=[jax.ShapeDtypeStruct(x.shape, F32)] * 3,
        compiler_params=_params(("parallel",)),
        name="cumsum",
    )(x)


FOX_HPS = 2
FOX_CK_TERMS = 3


def _fox_kernel(qt_ref, k_ref, ckx_ref, vt_ref, o_ref, m_ref, acc_ref, *, tq, tk, q_off):
    hps = FOX_HPS
    i = pl.program_id(2)
    q0 = q_off + i * tq
    nk = (q0 + tq + tk - 1) // tk
    n_full = (q0 + 1) // tk
    ck_row = lax.broadcasted_iota(jnp.int32, (LANES, tq), 0)
    zeros = jnp.zeros((HEAD_DIM, tq), BF16)
    q_cols = []
    for h in range(hps):
        d = ck_row - FOX_CK_TERMS * (pl.program_id(1) * hps + h)
        minus_one = jnp.where(d >= 0, jnp.where(d < FOX_CK_TERMS, -1.0, 0.0), 0.0).astype(BF16)
        q_parts = [zeros] * hps
        q_parts[h] = qt_ref[0, h]
        q_cols.append(jnp.concatenate(q_parts + [minus_one], axis=0))
    qt = jnp.concatenate(q_cols, axis=1)
    krow = lax.broadcasted_iota(jnp.int32, (tk, hps * tq), 0)
    qcol = q0 + (lax.broadcasted_iota(jnp.int32, (tk, hps * tq), 1) & (tq - 1))
    m_ref[...] = jnp.full((1, hps * tq), NEG_INF, F32)
    acc_ref[...] = jnp.zeros((V_ROWS, hps * tq), F32)

    def tile(j, masked):
        ks = pl.multiple_of(j * tk, tk)
        kk = jnp.concatenate([k_ref[0, pl.ds(ks, tk), :], ckx_ref[0, pl.ds(ks, tk), :]], axis=1)
        s = _dot(kk, qt)
        if masked:
            s = jnp.where(krow + ks <= qcol, s, NEG_INF)
        m_old = m_ref[...]
        m_new = jnp.maximum(m_old, jnp.max(s, axis=0, keepdims=True))
        p = jnp.exp(s - m_new).astype(BF16)
        pv = [_dot(vt_ref[0, h, :, pl.ds(ks, tk)], p[:, h * tq:(h + 1) * tq]) for h in range(hps)]
        acc_ref[...] = jnp.exp(m_old - m_new) * acc_ref[...] + jnp.concatenate(pv, axis=1)
        m_ref[...] = m_new

    def full_tile(j, carry):
        tile(j, False)
        return carry

    def edge_tile(j, carry):
        tile(j, True)
        return carry

    lax.fori_loop(0, n_full, full_tile, 0)
    lax.fori_loop(n_full, nk, edge_tile, 0)
    acc = acc_ref[...]
    o = acc[:HEAD_DIM] / acc[HEAD_DIM:HEAD_DIM + 1]
    o_ref[0] = jnp.concatenate([o[:, h * tq:(h + 1) * tq].T for h in range(hps)], axis=-1).astype(o_ref.dtype)


def _fox(qt, k, ckx, vt, *, tq, tk, q_off):
    b, h, dh, t = qt.shape
    length = k.shape[1]
    hps = FOX_HPS
    assert hps * dh == LANES and FOX_CK_TERMS * h <= LANES and length % tk == 0 and tq & (tq - 1) == 0
    kern = functools.partial(_fox_kernel, tq=tq, tk=tk, q_off=q_off)
    return pl.pallas_call(
        kern,
        grid=(b, h // hps, t // tq),
        in_specs=[
            pl.BlockSpec((1, hps, dh, tq), lambda bi, hi, i: (bi, hi, 0, i)),
            pl.BlockSpec((1, length, LANES), lambda bi, hi, i: (bi, 0, hi)),
            pl.BlockSpec((1, length, LANES), lambda bi, hi, i: (bi, 0, 0)),
            pl.BlockSpec((1, hps, V_ROWS, length), lambda bi, hi, i: (bi, hi, 0, 0)),
        ],
        out_specs=pl.BlockSpec((1, tq, hps * dh), lambda bi, hi, i: (bi, i, hi)),
        out_shape=jax.ShapeDtypeStruct((b, t, h * dh), BF16),
        scratch_shapes=[
            pltpu.VMEM((1, hps * tq), F32),
            pltpu.VMEM((V_ROWS, hps * tq), F32),
        ],
        compiler_params=_params(("parallel", "parallel", "arbitrary")),
        name="fox",
    )(qt, k, ckx, vt)


def _ck_table(cum3):
    b, _, length = cum3[0].shape
    ck = jnp.stack(cum3, axis=-1).transpose(0, 2, 1, 3).reshape(b, length, H_A * FOX_CK_TERMS)
    return jnp.concatenate([ck, jnp.zeros((b, length, LANES - H_A * FOX_CK_TERMS), F32)], axis=-1).astype(BF16)


def _with_cache_tok(cache, new, length):
    if cache is not None:
        new = jnp.concatenate([cache.reshape(cache.shape[0], cache.shape[1], -1).astype(BF16), new], axis=1)
    return _pad_time(new, length)


def _with_cache_feat(cache, new, n_heads, length):
    b, _, t = new.shape
    new = new.reshape(b, n_heads, V_ROWS, t)
    if cache is not None:
        old = cache.transpose(0, 2, 3, 1).astype(BF16)
        old = jnp.concatenate([old, jnp.ones(old.shape[:2] + (V_ROWS - HEAD_DIM, old.shape[3]), BF16)], axis=2)
        new = jnp.concatenate([old, new], axis=3)
    pad = length - new.shape[3]
    if pad:
        new = jnp.concatenate([new, jnp.zeros(new.shape[:3] + (pad,), BF16)], axis=3)
    return new


INT_MIN = -(2 ** 31)
KEY_NEG_INF = int(np.int32(np.uint32(0xFF800000) ^ np.uint32(0x7FFFFFFF)))
DSA_TK = 256


def _order_key(x):
    bits = lax.bitcast_convert_type(x, jnp.int32)
    return bits ^ ((bits >> 31) & jnp.int32(0x7FFFFFFF))


def _dsa_kernel(qit_ref, wit_ref, ki_ref, qt_ref, k_ref, vt_ref, nbt_ref, o_ref,
                sc_ref, m_ref, acc_ref, *, tq, q_off, topk):
    tk = DSA_TK
    i = pl.program_id(1)
    q0 = q_off + i * tq
    i_abs = q0 // tk
    q_tiles = -(-tq // tk)
    nk = i_abs + q_tiles
    cols = G_B * tq

    wit = wit_ref[0]
    wrows = [wit[hh:hh + 1, :] for hh in range(HI_B)]
    q_chunk = (q0 + lax.broadcasted_iota(jnp.int32, (tk, tq), 1)) >> 6
    krow = lax.broadcasted_iota(jnp.int32, (tk, tq), 0)

    def score_tile(j, carry):
        ks = pl.multiple_of(j * tk, tk)
        ki = ki_ref[0, pl.ds(ks, tk), :]
        tot = jnp.zeros((tk, tq), F32)
        for hh in range(HI_B):
            tot = tot + wrows[hh] * jnp.maximum(_dot(ki, qit_ref[0, hh]), 0.0)
        tot = jnp.where(((krow + ks) >> 6) <= q_chunk, tot, NEG_INF)
        sc_ref[j] = _order_key(tot)
        return carry

    lax.fori_loop(0, nk, score_tile, 0)

    kf = float(topk)

    def count(pred_fn):
        def tile_hits(j):
            hit = jnp.where(pred_fn(sc_ref[j]), 1.0, 0.0)
            parts = [hit[r:r + SUBLANES] for r in range(0, tk, SUBLANES)]
            while len(parts) > 1:
                parts = [parts[a] + parts[a + 1] for a in range(0, len(parts), 2)]
            return parts[0]

        acc8 = lax.fori_loop(0, nk, lambda j, acc8: acc8 + tile_hits(j), jnp.zeros((SUBLANES, tq), F32))
        return jnp.sum(acc8, axis=0, keepdims=True)

    c0 = count(lambda key: key >= 0)
    thr = jnp.where(c0 >= kf, jnp.int32(0), jnp.int32(INT_MIN))
    for bit in range(30, -1, -1):
        cand = thr + jnp.int32(1 << bit)
        c = count(lambda key, cand=cand: key >= cand)
        thr = jnp.where(c >= kf, cand, thr)

    cnt_gt = count(lambda key: key > thr)
    short = thr <= jnp.int32(KEY_NEG_INF)
    thr_b = jnp.broadcast_to(jnp.maximum(thr, jnp.int32(KEY_NEG_INF)), (tk, tq))
    need = jnp.where(short, 0.0, kf - cnt_gt)

    tri = jnp.where(lax.broadcasted_iota(jnp.int32, (tk, tk), 1) <= lax.broadcasted_iota(jnp.int32, (tk, tk), 0),
                    1.0, 0.0).astype(BF16)
    m_ref[...] = jnp.full((1, HKV_B * cols), NEG_INF, F32)
    acc_ref[...] = jnp.zeros((V_ROWS, HKV_B * cols), F32)
    q_zero = jnp.zeros((HEAD_DIM, cols), BF16)
    q_groups = []
    for g in range(HKV_B):
        parts = [q_zero] * HKV_B
        parts[g] = qt_ref[0, g]
        q_groups.append(jnp.concatenate(parts, axis=0))
    q_all = jnp.concatenate(q_groups, axis=1)

    n_far = jnp.maximum(i_abs - 1, 0)

    def attend_tile(j, eqc, near):
        ks = pl.multiple_of(j * tk, tk)
        key = sc_ref[j]
        gt = key > thr_b
        eq = key == thr_b
        eqf = jnp.where(eq, 1.0, 0.0)
        rank = _dot(tri, eqf.astype(BF16)) + eqc
        pen = jnp.where(gt, 0.0, jnp.where(eq, rank, 1e9))
        madd = jnp.where(pen <= need, 0.0, NEG_INF)
        madd = jnp.concatenate([madd] * H_B, axis=1)
        s = _dot(k_ref[0, pl.ds(ks, tk), :], q_all) + madd
        if near:
            s = s + jnp.concatenate([nbt_ref[j - i_abs + 1, g] for g in range(HKV_B)], axis=1)
        m_old = m_ref[...]
        m_new = jnp.maximum(m_old, jnp.max(s, axis=0, keepdims=True))
        m_safe = jnp.where(m_new == NEG_INF, 0.0, m_new)
        p = jnp.exp(s - m_safe).astype(BF16)
        pv = [_dot(vt_ref[0, g, :, pl.ds(ks, tk)], p[:, g * cols:(g + 1) * cols]) for g in range(HKV_B)]
        acc_ref[...] = jnp.exp(m_old - m_safe) * acc_ref[...] + jnp.concatenate(pv, axis=1)
        m_ref[...] = m_new
        return eqc + jnp.sum(eqf, axis=0, keepdims=True)

    eqc = lax.fori_loop(0, n_far, functools.partial(attend_tile, near=False), jnp.zeros((1, tq), F32))
    lax.fori_loop(n_far, nk, functools.partial(attend_tile, near=True), eqc)
    acc = acc_ref[...]
    o = (acc[:HEAD_DIM] / acc[HEAD_DIM:HEAD_DIM + 1]).astype(o_ref.dtype)
    for g in range(HKV_B):
        o_ref[0, g] = o[:, g * cols:(g + 1) * cols]


def _dsa(qit, wit, ki, qt, k, vt, nbt, *, tq, q_off, topk):
    b, _, _, t = qit.shape
    length = ki.shape[1]
    cols = G_B * tq
    n_bias = nbt.shape[0]
    tk = DSA_TK
    assert tk >= MAX_DISTANCE and q_off % tk == 0 and length % tk == 0 and length >= q_off + t
    assert HKV_B * HEAD_DIM == LANES and tq % CHUNK == 0 and (tq % tk == 0 or t == tq) and n_bias == 1 + -(-tq // tk)
    kern = functools.partial(_dsa_kernel, tq=tq, q_off=q_off, topk=topk)
    return pl.pallas_call(
        kern,
        grid=(b, t // tq),
        in_specs=[
            pl.BlockSpec((1, HI_B, DI_B, tq), lambda bi, i: (bi, 0, 0, i)),
            pl.BlockSpec((1, SUBLANES, tq), lambda bi, i: (bi, 0, i)),
            pl.BlockSpec((1, length, DI_B), lambda bi, i: (bi, 0, 0)),
            pl.BlockSpec((1, HKV_B, HEAD_DIM, cols), lambda bi, i: (bi, 0, 0, i)),
            pl.BlockSpec((1, length, LANES), lambda bi, i: (bi, 0, 0)),
            pl.BlockSpec((1, HKV_B, V_ROWS, length), lambda bi, i: (bi, 0, 0, 0)),
            pl.BlockSpec((n_bias, HKV_B, tk, cols), lambda bi, i: (0, 0, 0, 0), pipeline_mode=pl.Buffered(1)),
        ],
        out_specs=pl.BlockSpec((1, HKV_B, HEAD_DIM, cols), lambda bi, i: (bi, 0, 0, i)),
        out_shape=jax.ShapeDtypeStruct(qt.shape, BF16),
        scratch_shapes=[
            pltpu.VMEM((length // tk, tk, tq), jnp.int32),
            pltpu.VMEM((1, HKV_B * cols), F32),
            pltpu.VMEM((V_ROWS, HKV_B * cols), F32),
        ],
        compiler_params=_params(("parallel", "arbitrary")),
        name="dsa",
    )(qit, wit, ki, qt, k, vt, nbt)


HG_SUB = SUBLANES
HG_NB = 4


def _hgrn_kernel(f_ref, i_ref, q_ref, g_ref, la_ref, l1_ref, oml_ref, ng_ref, s0_ref,
                 o_ref, sfin_ref, st_ref):
    c = pl.program_id(1)
    n_c = pl.num_programs(1)

    @pl.when(c == 0)
    def _():
        for bb in range(HG_NB):
            for hh in range(H_C):
                st_ref[bb, hh] = s0_ref[bb, hh].T

    for bb in range(HG_NB):
        _hgrn_chunk(bb, f_ref, i_ref, q_ref, g_ref, la_ref, l1_ref, oml_ref, ng_ref, o_ref, st_ref)

    @pl.when(c == n_c - 1)
    def _():
        for bb in range(HG_NB):
            for hh in range(H_C):
                sfin_ref[bb, hh] = st_ref[bb, hh].T


def _hgrn_chunk(bb, f_ref, i_ref, q_ref, g_ref, la_ref, l1_ref, oml_ref, ng_ref, o_ref, st_ref):
    ct = CHUNK
    zf = f_ref[bb]
    a = la_ref[...]
    cc = l1_ref[...] + _log_sigmoid(zf)
    logf = jnp.maximum(a, cc) + jnp.log1p(jnp.exp(-jnp.abs(a - cc)))
    kk = oml_ref[...] * _sigmoid(-zf)

    r_i = lax.broadcasted_iota(jnp.int32, (ct, ct), 0)
    c_i = lax.broadcasted_iota(jnp.int32, (ct, ct), 1)
    low = jnp.where(c_i <= r_i, 1.0, 0.0).astype(BF16)
    hi, mid, lo = _split3(logf)
    bcum = _dot(low, hi) + _dot(low, mid) + _dot(low, lo)

    sub_row = lax.broadcasted_iota(jnp.int32, (HG_SUB, DK_C), 0)
    a_col = lax.broadcasted_iota(jnp.int32, (HG_SUB, ct), 1)
    n_sub = ct // HG_SUB

    for hh in range(H_C):
        cs = slice(hh * DK_C, (hh + 1) * DK_C)
        qh = q_ref[bb, :, cs]
        kh = kk[:, cs]
        vh = i_ref[bb, :, cs]
        bh = bcum[:, cs]
        a_rows = []
        for bi in range(n_sub):
            r0 = bi * HG_SUB
            q_blk = qh[r0:r0 + HG_SUB]
            b_blk = bh[r0:r0 + HG_SUB]
            k_blk = kh[r0:r0 + HG_SUB]
            if bi > 0:
                ref = bh[r0:r0 + 1]
                qp = q_blk * jnp.exp(b_blk - ref)
                kp = kh[:r0] * jnp.exp(ref - bh[:r0])
                kp = jnp.concatenate([kp, jnp.zeros((ct - r0, DK_C), F32)], axis=0)
                a_row = _dot_nt(qp.astype(BF16), kp.astype(BF16))
            else:
                a_row = jnp.zeros((HG_SUB, ct), F32)
            for s in range(HG_SUB):
                d = jnp.where(sub_row >= s, b_blk - b_blk[s:s + 1], NEG_INF)
                x = q_blk * jnp.exp(d) * k_blk[s:s + 1]
                col = jnp.sum(x, axis=-1, keepdims=True)
                a_row = jnp.where(a_col == r0 + s, col, a_row)
            a_rows.append(a_row)
        a_mat = jnp.concatenate(a_rows, axis=0)
        st = st_ref[bb, hh]
        o = _dot(a_mat.astype(BF16), vh.astype(BF16)) + _dot_nt((qh * jnp.exp(bh)).astype(BF16), st.astype(BF16))
        bl = bh[ct - 1:ct]
        kdec = kh * jnp.exp(bl - bh)
        st_ref[bb, hh] = st * jnp.exp(bl) + _dot_tn(vh.astype(BF16), kdec.astype(BF16))
        o = o * lax.rsqrt(jnp.mean(o * o, axis=-1, keepdims=True) + LN_EPS) * ng_ref[:, cs]
        gh = g_ref[bb, :, cs]
        o_ref[bb, :, cs] = (o * (gh * _sigmoid(gh))).astype(o_ref.dtype)


def _hgrn(z3, la, l1, oml, ng, s0):
    b, t, _ = z3.shape
    assert b % HG_NB == 0
    n_c = t // CHUNK

    def zspec(col):
        blk = (col - Z_FC) // W_C
        return pl.BlockSpec((HG_NB, CHUNK, W_C), lambda bi, ci: (bi, ci, blk))

    vec = pl.BlockSpec((1, W_C), lambda bi, ci: (0, 0))
    return pl.pallas_call(
        _hgrn_kernel,
        grid=(b // HG_NB, n_c),
        in_specs=[zspec(Z_FC), zspec(Z_IC), zspec(Z_QC), zspec(Z_GC), vec, vec, vec, vec,
                  pl.BlockSpec((HG_NB, H_C, DK_C, DK_C), lambda bi, ci: (bi, 0, 0, 0))],
        out_specs=[pl.BlockSpec((HG_NB, CHUNK, W_C), lambda bi, ci: (bi, ci, 0)),
                   pl.BlockSpec((HG_NB, H_C, DK_C, DK_C), lambda bi, ci: (bi, 0, 0, 0))],
        out_shape=[jax.ShapeDtypeStruct((b, t, W_C), BF16),
                   jax.ShapeDtypeStruct((b, H_C, DK_C, DK_C), F32)],
        scratch_shapes=[pltpu.VMEM((HG_NB, H_C, DK_C, DK_C), F32)],
        compiler_params=_params(("parallel", "arbitrary")),
        name="hgrn",
    )(z3, z3, z3, z3, la, l1, oml, ng, s0)


def _merge_kernel(x_ref, xb_ref, oa_ref, ob_ref, oc_ref, wg_ref, bg_ref, wa_ref, wb_ref, wc_ref, wo_ref,
                  g_ref, b_ref, y_ref, yb_ref):
    xb = xb_ref[...]
    d = D_MODEL
    mix = None
    for br, (o_ref, w_ref) in enumerate(((oa_ref, wa_ref), (ob_ref, wb_ref), (oc_ref, wc_ref))):
        gate = _sigmoid(_dot(xb, wg_ref[:, br * d:(br + 1) * d]) + bg_ref[:, br * d:(br + 1) * d])
        term = gate * _dot(o_ref[...], w_ref[...])
        mix = term if mix is None else mix + term
    out = _dot(mix.astype(BF16), wo_ref[...])
    y = _layer_norm_rows(DEEPNORM_ALPHA * x_ref[...] + out, g_ref[...], b_ref[...])
    y_ref[...] = y
    yb_ref[...] = y.astype(BF16)


def _merge(x, xb, oa, ob, oc, wg, bg, wa, wb, wc, wo, g, b, tm):
    n, d = x.shape

    def row(width):
        return pl.BlockSpec((tm, width), lambda i: (i, 0))

    def full(arr):
        return pl.BlockSpec(arr.shape, lambda i: (0, 0))

    return pl.pallas_call(
        _merge_kernel,
        grid=(n // tm,),
        in_specs=[row(d), row(d), row(oa.shape[1]), row(ob.shape[1]), row(oc.shape[1]),
                  full(wg), full(bg), full(wa), full(wb), full(wc), full(wo), full(g), full(b)],
        out_specs=[row(d), row(d)],
        out_shape=[jax.ShapeDtypeStruct((n, d), F32), jax.ShapeDtypeStruct((n, d), BF16)],
        compiler_params=_params(("parallel",)),
        name="merge",
    )(x, xb, oa, ob, oc, wg, bg, wa, wb, wc, wo, g, b)


def _router_kernel(x_ref, w_ref, o_ref):
    x = x_ref[...]
    xh = x.astype(BF16)
    xl = (x - xh.astype(F32)).astype(BF16)
    w = w_ref[...]
    wh = w.astype(BF16)
    wl = (w - wh.astype(F32)).astype(BF16)
    logits = _dot(xh, wh) + _dot(xh, wl) + _dot(xl, wh)
    lane = lax.broadcasted_iota(jnp.int32, logits.shape, 1)
    logits = jnp.where(lane < N_EXPERTS, logits, NEG_INF)
    big = jnp.int32(LANES)
    m1 = jnp.max(logits, axis=-1, keepdims=True)
    i1 = jnp.min(jnp.where(logits == m1, lane, big), axis=-1, keepdims=True)
    rest = jnp.where(lane == i1, NEG_INF, logits)
    m2 = jnp.max(rest, axis=-1, keepdims=True)
    i2 = jnp.min(jnp.where(rest == m2, lane, big), axis=-1, keepdims=True)
    e2 = jnp.exp(m2 - m1)
    g1 = 1.0 / (1.0 + e2)
    g2 = e2 / (1.0 + e2)
    o_ref[...] = jnp.where(lane == i1, g1, 0.0) + jnp.where(lane == i2, g2, 0.0)


def _router(x, w, tm):
    n, d = x.shape
    return pl.pallas_call(
        _router_kernel,
        grid=(n // tm,),
        in_specs=[pl.BlockSpec((tm, d), lambda i: (i, 0)),
                  pl.BlockSpec(w.shape, lambda i: (0, 0))],
        out_specs=pl.BlockSpec((tm, LANES), lambda i: (i, 0)),
        out_shape=jax.ShapeDtypeStruct((n, LANES), F32),
        compiler_params=_params(("parallel",)),
        name="router",
    )(x, w)


def _ffn_kernel(x_ref, xb_ref, w1_ref, w3_ref, w2_ref, g_ref, b_ref, y_ref, yb_ref, acc_ref):
    f = pl.program_id(1)

    @pl.when(f == 0)
    def _():
        acc_ref[...] = jnp.zeros_like(acc_ref)

    xb = xb_ref[...]
    h1 = _dot(xb, w1_ref[...])
    h = (h1 * _sigmoid(h1)) * _dot(xb, w3_ref[...])
    acc_ref[...] += _dot(h.astype(BF16), w2_ref[...])

    @pl.when(f == pl.num_programs(1) - 1)
    def _():
        y = _layer_norm_rows(DEEPNORM_ALPHA * x_ref[...] + acc_ref[...], g_ref[...], b_ref[...])
        y_ref[...] = y
        yb_ref[...] = y.astype(BF16)


def _ffn(x, xb, w1, w3, w2, g, b, *, tm, tf):
    n, d = x.shape
    ff = w1.shape[1]
    return pl.pallas_call(
        _ffn_kernel,
        grid=(n // tm, ff // tf),
        in_specs=[
            pl.BlockSpec((tm, d), lambda i, f: (i, 0)),
            pl.BlockSpec((tm, d), lambda i, f: (i, 0)),
            pl.BlockSpec((d, tf), lambda i, f: (0, f)),
            pl.BlockSpec((d, tf), lambda i, f: (0, f)),
            pl.BlockSpec((tf, d), lambda i, f: (f, 0)),
            pl.BlockSpec((1, d), lambda i, f: (0, 0)),
            pl.BlockSpec((1, d), lambda i, f: (0, 0)),
        ],
        out_specs=[pl.BlockSpec((tm, d), lambda i, f: (i, 0)),
                   pl.BlockSpec((tm, d), lambda i, f: (i, 0))],
        out_shape=[jax.ShapeDtypeStruct((n, d), F32), jax.ShapeDtypeStruct((n, d), BF16)],
        scratch_shapes=[pltpu.VMEM((tm, d), F32)],
        compiler_params=_params(("parallel", "arbitrary")),
        name="ffn",
    )(x, xb, w1, w3, w2, g, b)


MOE_BS = 144


def _moe_slots(tm):
    return -(-tm // MOE_BS) * MOE_BS


def _moe_kernel(x_ref, xb_ref, comb_ref, w1_ref, w3_ref, w2_ref, g_ref, b_ref, y_ref, yb_ref,
                rank_ref, rankt_ref, xg_ref, yacc_ref, *, tm):
    e = pl.program_id(1)
    f = pl.program_id(2)
    n_f = pl.num_programs(2)
    bs = MOE_BS

    @pl.when((e == 0) & (f == 0))
    def _():
        comb = comb_ref[...]
        routed = comb > 0.0
        r_i = lax.broadcasted_iota(jnp.int32, (tm, tm), 0)
        c_i = lax.broadcasted_iota(jnp.int32, (tm, tm), 1)
        strict_low = jnp.where(c_i < r_i, 1.0, 0.0).astype(BF16)
        rank = _dot(strict_low, jnp.where(routed, 1.0, 0.0).astype(BF16))
        rank = jnp.where(routed, rank, -1.0)
        rank_ref[...] = rank
        rankt_ref[...] = rank.T[:N_EXPERTS]
        y_ref[...] = jnp.zeros_like(y_ref)

    rrow = rankt_ref[pl.ds(e, 1), :]
    n_e = jnp.sum(jnp.where(rrow >= 0.0, 1.0, 0.0)).astype(jnp.int32)
    n_blk = (n_e + bs - 1) // bs

    @pl.when(f == 0)
    def _():
        slot = lax.broadcasted_iota(jnp.int32, (bs, tm), 0).astype(F32)

        def gather(blk, carry):
            r0 = pl.multiple_of(blk * bs, bs)
            pick = jnp.where(rrow - (blk * bs).astype(F32) == slot, 1.0, 0.0).astype(BF16)
            xg_ref[pl.ds(r0, bs), :] = _dot(pick, xb_ref[...]).astype(BF16)
            yacc_ref[pl.ds(r0, bs), :] = jnp.zeros((bs, yacc_ref.shape[1]), F32)
            return carry

        lax.fori_loop(0, n_blk, gather, 0)

    def expert_rows(r0, rows):
        xg = xg_ref[pl.ds(r0, rows), :]
        h1 = _dot(xg, w1_ref[0])
        h = (h1 * _sigmoid(h1)) * _dot(xg, w3_ref[0])
        yacc_ref[pl.ds(r0, rows), :] += _dot(h.astype(BF16), w2_ref[0])

    def expert_pair(pair, carry):
        expert_rows(pl.multiple_of(pair * (2 * bs), 2 * bs), 2 * bs)
        return carry

    n_pair = n_blk // 2
    lax.fori_loop(0, n_pair, expert_pair, 0)

    @pl.when(n_blk % 2 == 1)
    def _():
        expert_rows(pl.multiple_of(n_pair * (2 * bs), bs), bs)

    @pl.when(f == n_f - 1)
    def _():
        lane = lax.broadcasted_iota(jnp.int32, (tm, LANES), 1)
        rcol = jnp.sum(jnp.where(lane == e, rank_ref[...], 0.0), axis=-1, keepdims=True)
        gcol = jnp.sum(jnp.where(lane == e, comb_ref[...], 0.0), axis=-1, keepdims=True)
        def scatter_rows(r0, rows):
            slot = lax.broadcasted_iota(jnp.int32, (tm, rows), 1).astype(F32)
            put = jnp.where(rcol - r0.astype(F32) == slot, 1.0, 0.0).astype(BF16)
            y_ref[...] += gcol * _dot(put, yacc_ref[pl.ds(r0, rows), :].astype(BF16))

        def scatter_pair(pair, carry):
            scatter_rows(pl.multiple_of(pair * (2 * bs), 2 * bs), 2 * bs)
            return carry

        lax.fori_loop(0, n_pair, scatter_pair, 0)

        @pl.when(n_blk % 2 == 1)
        def _():
            scatter_rows(pl.multiple_of(n_pair * (2 * bs), bs), bs)

    @pl.when((e == pl.num_programs(1) - 1) & (f == n_f - 1))
    def _():
        y = _layer_norm_rows(DEEPNORM_ALPHA * x_ref[...] + y_ref[...], g_ref[...], b_ref[...])
        y_ref[...] = y
        yb_ref[...] = y.astype(BF16)


def _moe(x, xb, comb, w1, w3, w2, g, b, *, tm, tf):
    n, d = x.shape
    n_e, _, ff = w1.shape
    assert n % tm == 0 and ff % tf == 0 and n_e == N_EXPERTS
    slots = _moe_slots(tm)
    kern = functools.partial(_moe_kernel, tm=tm)
    one = pl.Buffered(1)
    return pl.pallas_call(
        kern,
        grid=(n // tm, n_e, ff // tf),
        in_specs=[
            pl.BlockSpec((tm, d), lambda i, e, f: (i, 0), pipeline_mode=one),
            pl.BlockSpec((tm, d), lambda i, e, f: (i, 0), pipeline_mode=one),
            pl.BlockSpec((tm, LANES), lambda i, e, f: (i, 0), pipeline_mode=one),
            pl.BlockSpec((1, d, tf), lambda i, e, f: (e, 0, f)),
            pl.BlockSpec((1, d, tf), lambda i, e, f: (e, 0, f)),
            pl.BlockSpec((1, tf, d), lambda i, e, f: (e, f, 0)),
            pl.BlockSpec((1, d), lambda i, e, f: (0, 0)),
            pl.BlockSpec((1, d), lambda i, e, f: (0, 0)),
        ],
        out_specs=[pl.BlockSpec((tm, d), lambda i, e, f: (i, 0)),
                   pl.BlockSpec((tm, d), lambda i, e, f: (i, 0))],
        out_shape=[jax.ShapeDtypeStruct((n, d), F32), jax.ShapeDtypeStruct((n, d), BF16)],
        scratch_shapes=[pltpu.VMEM((tm, LANES), F32), pltpu.VMEM((N_EXPERTS, tm), F32),
                        pltpu.VMEM((slots, d), BF16), pltpu.VMEM((slots, d), F32)],
        compiler_params=_params(("parallel", "arbitrary", "arbitrary")),
        name="moe",
    )(x, xb, comb, w1, w3, w2, g, b)


def _rel_bucket(rel):
    nb = N_BUCKETS // 2
    max_exact = nb // 2
    ret = jnp.where(rel > 0, nb, 0)
    n = jnp.abs(rel)
    nf = jnp.maximum(n, 1).astype(jnp.float32)
    large = max_exact + (jnp.log(nf / max_exact) / math.log(MAX_DISTANCE / max_exact) * (nb - max_exact)).astype(jnp.int32)
    large = jnp.minimum(large, nb - 1)
    return ret + jnp.where(n < max_exact, n, large)


def _bias_tiles(rel_bias, tq):
    tk = DSA_TK
    r = jnp.arange(tq, dtype=jnp.int32)[:, None]
    c = jnp.arange(tk, dtype=jnp.int32)[None, :]
    n = 1 + -(-tq // tk)
    rel = jnp.stack([c - r + (s - 1) * tk for s in range(n)], axis=0)
    table = rel_bias.astype(F32)
    onehot = _rel_bucket(rel)[..., None] == jnp.arange(N_BUCKETS, dtype=jnp.int32)
    vals = jnp.sum(jnp.where(onehot[..., None], table, 0.0), axis=-2)
    far = table[_rel_bucket(jnp.int32(-(2 * MAX_DISTANCE)))]
    vals = (vals - far).transpose(0, 3, 2, 1).reshape(n, HKV_B, G_B, tk, tq)
    return vals.transpose(0, 1, 3, 2, 4).reshape(n, HKV_B, tk, G_B * tq)


def _prep_w_in(w_in_l, b_in_l):
    def tok_cols(a):
        parts = [a[..., IN_KA:IN_FA], a[..., IN_FC:IN_END], a[..., IN_KB:IN_QI], a[..., IN_KI:IN_WI], a[..., IN_FA:IN_QB]]
        width = sum(p.shape[-1] for p in parts)
        return jnp.concatenate(parts + [jnp.zeros(a.shape[:-1] + (N_Z - width,), a.dtype)], axis=-1)

    def feat_cols(a):
        parts = [a[..., IN_QA:IN_KA] * HEAD_DIM ** -0.5, a[..., IN_VA:IN_FA], a[..., IN_QB:IN_KB] * HEAD_DIM ** -0.5,
                 a[..., IN_QI:IN_KI] * DI_B ** -0.5, a[..., IN_VB:IN_QI], a[..., IN_WI:IN_FC] * HI_B ** -0.5]
        width = sum(p.shape[-1] for p in parts)
        return jnp.concatenate(parts + [jnp.zeros(a.shape[:-1] + (T_ROWS - width,), a.dtype)], axis=-1)

    w = w_in_l.astype(F32)
    bias = b_in_l.astype(F32)[None, :]
    return (tok_cols(w).astype(BF16), tok_cols(bias),
            feat_cols(w).T.astype(BF16), feat_cols(bias).T)


def _pad_time(a, length):
    pad = length - a.shape[1]
    if pad == 0:
        return a
    return jnp.concatenate([a, jnp.zeros((a.shape[0], pad) + a.shape[2:], a.dtype)], axis=1)


def _round_up(x, m):
    return (x + m - 1) // m * m


def _run_group(x, caches, wts):
    b, t, d = x.shape
    n = b * t
    p_len = 0 if caches is None else caches[0].shape[2]
    tm = min(512, n)
    xf = x.reshape(n, d).astype(F32)
    xb = xf.astype(BF16)

    fox_tq = min(1024, t)
    fox_tk = 1024
    l_fox = _round_up(p_len + t, fox_tk)
    dsa_tq = min(256, t)
    l_dsa = _round_up(p_len + t, DSA_TK)
    topk = min(TOPK_MAX, (p_len + t) // 4)
    nb = _bias_tiles(wts["rel_bias"], dsa_tq)

    lb_cum = jnp.cumsum(jax.nn.softmax(wts["hgrn_lb"].astype(F32), axis=0), axis=0)

    states = []
    for l in range(DEPTH):
        w_tok, b_tok, w_feat, b_feat = _prep_w_in(wts["w_in"][l], wts["b_in"][l])
        (k_a, v_a, z, k_b, v_b, ki_b, logf_a, ka_bf, kb_bf, ki_bf,
         qa_t, va_t, qb_t, qi_t, vb_t, wi_t) = _proj(xb, w_tok, b_tok, w_feat, b_feat, b, min(512, t))
        z3 = z.reshape(b, t, 4 * W_C)
        logf_a = logf_a.reshape(b, t, H_A)
        ki_b = ki_b.reshape(b, t, DI_B)
        if caches is None:
            c_ka = c_va = c_kb = c_vb = c_ki = None
            lf_all = logf_a
            s0 = jnp.zeros((b, H_C, DK_C, DK_C), F32)
        else:
            c_ka, c_va, c_lf, c_kb, c_vb, c_ki, c_s = [c[l] for c in caches]
            lf_all = jnp.concatenate([c_lf.astype(F32), logf_a], axis=1)
            s0 = c_s.astype(F32)

        cum3 = _cumsum(_pad_time(lf_all, l_fox).transpose(0, 2, 1))
        o_a = _fox(qa_t.reshape(b, H_A, HEAD_DIM, t), _with_cache_tok(c_ka, ka_bf.reshape(b, t, W_A), l_fox),
                   _ck_table(cum3), _with_cache_feat(c_va, va_t, H_A, l_fox), tq=fox_tq, tk=fox_tk, q_off=p_len)

        n_t = t // dsa_tq
        qt_b = qb_t.reshape(b, HKV_B, G_B, HEAD_DIM, n_t, dsa_tq).transpose(0, 1, 3, 4, 2, 5)
        qt_b = qt_b.reshape(b, HKV_B, HEAD_DIM, n_t * G_B * dsa_tq)
        o_b = _dsa(qi_t.reshape(b, HI_B, DI_B, t), wi_t, _with_cache_tok(c_ki, ki_bf.reshape(b, t, DI_B), l_dsa),
                   qt_b, _with_cache_tok(c_kb, kb_bf.reshape(b, t, W_KV_B), l_dsa),
                   _with_cache_feat(c_vb, vb_t, HKV_B, l_dsa), nb, tq=dsa_tq, q_off=p_len, topk=topk)
        o_b = o_b.reshape(b, HKV_B, HEAD_DIM, n_t, G_B, dsa_tq).transpose(0, 3, 5, 1, 4, 2).reshape(n, H_B * HEAD_DIM)

        lb = (lb_cum[l] - lb_cum[0]).reshape(1, W_C)
        o_c, s_new = _hgrn(z3, jnp.log(lb), jnp.log1p(-lb), 1.0 - lb,
                           wts["hgrn_norm_g"][l].reshape(1, W_C).astype(F32), s0)

        x1, x1b = _merge(xf, xb, o_a.reshape(n, -1), o_b, o_c.reshape(n, -1),
                         wts["w_gate"][l].astype(BF16), wts["b_gate"][l][None, :].astype(F32),
                         wts["w_o_fox"][l].astype(BF16), wts["w_o_dsa"][l].astype(BF16), wts["w_o_hgrn"][l].astype(BF16),
                         wts["w_out"][l].astype(BF16), wts["ln1_g"][l][None, :], wts["ln1_b"][l][None, :], min(256, n))

        g2, b2 = wts["ln2_g"][l][None, :], wts["ln2_b"][l][None, :]
        if l % 2 == 0:
            j = l // 2
            xf, xb = _ffn(x1, x1b, wts["ffn_w1"][j].astype(BF16), wts["ffn_w3"][j].astype(BF16),
                          wts["ffn_w2"][j].astype(BF16), g2, b2, tm=tm, tf=1408)
        else:
            j = l // 2
            rw = jnp.zeros((d, LANES), F32).at[:, :N_EXPERTS].set(wts["moe_router"][j].astype(F32))
            comb = _router(x1, rw, tm)
            xf, xb = _moe(x1, x1b, comb, wts["moe_w1"][j].astype(BF16), wts["moe_w3"][j].astype(BF16),
                          wts["moe_w2"][j].astype(BF16), g2, b2, tm=min(1024, n), tf=1792)

        states.append((k_a.reshape(b, t, H_A, HEAD_DIM), v_a.reshape(b, t, H_A, HEAD_DIM), logf_a,
                       k_b.reshape(b, t, HKV_B, HEAD_DIM), v_b.reshape(b, t, HKV_B, HEAD_DIM), ki_b, s_new))
    stacked = [jnp.stack([s[i] for s in states], axis=0) for i in range(7)]
    return xf.reshape(b, t, d), stacked


def kernel(x_prompt, x_sample, cache_fox_k, cache_fox_v, cache_fox_logf, cache_dsa_k, cache_dsa_v, cache_dsa_kidx, state_hgrn,
           w_in, b_in, w_gate, b_gate, w_o_fox, w_o_dsa, w_o_hgrn, w_out, hgrn_lb, hgrn_norm_g, rel_bias,
           ln1_g, ln1_b, ln2_g, ln2_b, ffn_w1, ffn_w3, ffn_w2, moe_router, moe_w1, moe_w3, moe_w2):
    wts = dict(w_in=w_in, b_in=b_in, w_gate=w_gate, b_gate=b_gate, w_o_fox=w_o_fox, w_o_dsa=w_o_dsa, w_o_hgrn=w_o_hgrn,
               w_out=w_out, hgrn_lb=hgrn_lb, hgrn_norm_g=hgrn_norm_g, rel_bias=rel_bias, ln1_g=ln1_g, ln1_b=ln1_b,
               ln2_g=ln2_g, ln2_b=ln2_b, ffn_w1=ffn_w1, ffn_w3=ffn_w3, ffn_w2=ffn_w2, moe_router=moe_router,
               moe_w1=moe_w1, moe_w3=moe_w3, moe_w2=moe_w2)
    caches = (cache_fox_k, cache_fox_v, cache_fox_logf, cache_dsa_k, cache_dsa_v, cache_dsa_kidx, state_hgrn)
    y_prompt, sp = _run_group(x_prompt, None, wts)
    y_sample, ss = _run_group(x_sample, caches, wts)
    return (y_prompt, y_sample, sp[0], sp[1], sp[2], sp[3], sp[4], sp[5], sp[6],
            ss[0], ss[1], ss[2], ss[3], ss[4], ss[5], ss[6])
```
